```python
import jax, jax.numpy as jnp
from jax import lax
import numpy as np

D_MODEL = 1024
BATCH = 8
SEQ = 2048
DEPTH = 2

D_MIX = D_MODEL
D_POOL = D_MIX // 2
POOL_WINDOWS = (2, 4, 8, 16)
N_POOL_GROUPS = len(POOL_WINDOWS)
POOL_GROUP = D_POOL // N_POOL_GROUPS
HEAD_DIM = 64
D_ATTN = D_MIX - D_POOL
N_HEADS = D_ATTN // HEAD_DIM
N_KV_HEADS = 2
GQA_GROUP = N_HEADS // N_KV_HEADS
D_KV = N_KV_HEADS * HEAD_DIM
WINDOW = 128
BLOCK = 128
IN_WIDTHS = (D_POOL, D_POOL, D_ATTN, D_KV, D_KV, D_ATTN)
D_IN = sum(IN_WIDTHS)
EPS = 1e-6
NEG_INF = -1e30

kernel_name = "hybrid_pool_swa_sink_parallel_heads"


def rmsnorm(x, gain):
    x32 = x.astype(jnp.float32)
    y = x32 * lax.rsqrt(jnp.mean(x32 * x32, axis=-1, keepdims=True) + EPS) * gain.astype(jnp.float32)
    return y.astype(x.dtype)


def alibi_slopes():
    return jnp.exp2(-8.0 * jnp.arange(1, N_HEADS + 1, dtype=jnp.float32) / N_HEADS)


def pool_mixer(u, w_grp, scale):
    B, S, _ = u.shape
    u32 = u.astype(jnp.float32).reshape(B, S, N_POOL_GROUPS, POOL_GROUP)
    csum = jnp.cumsum(u32, axis=1)
    csum = jnp.concatenate([jnp.zeros_like(csum[:, :1]), csum], axis=1)
    pos = jnp.arange(1, S + 1, dtype=jnp.float32)
    means = []
    for g, w in enumerate(POOL_WINDOWS):
        c = csum[:, :, g]
        lo = jnp.concatenate([jnp.zeros_like(c[:, :w - 1]), c[:, :S + 1 - w]], axis=1)
        count = jnp.minimum(pos, float(w))[None, :, None]
        means.append((c[:, 1:] - lo) / count)
    pooled = jnp.stack(means, axis=2) - u32
    mixed = jnp.einsum('bsgc,gcd->bsgd', pooled.astype(u.dtype), w_grp)
    return mixed.reshape(B, S, D_POOL) * scale


def swa_sink_attention(q, k, v, sinks):
    B, S, _ = q.shape
    NB = S // BLOCK
    q = q.reshape(B, NB, BLOCK, N_KV_HEADS, GQA_GROUP, HEAD_DIM)
    k = k.reshape(B, NB, BLOCK, N_KV_HEADS, HEAD_DIM)
    v = v.reshape(B, NB, BLOCK, N_KV_HEADS, HEAD_DIM)

    def with_prev(t):
        prev = jnp.concatenate([jnp.zeros_like(t[:, :1]), t[:, :-1]], axis=1)
        return jnp.concatenate([prev, t], axis=2)

    kb, vb = with_prev(k), with_prev(v)
    scores = jnp.einsum('bnqhgd,bnkhd->bnhgqk', q, kb).astype(jnp.float32) * (HEAD_DIM ** -0.5)
    qi = jnp.arange(BLOCK)[:, None]
    kj = jnp.arange(2 * BLOCK)[None, :]
    dist = qi + BLOCK - kj
    in_win = (dist >= 0) & (dist < WINDOW)
    key_exists = (jnp.arange(NB)[:, None, None] > 0) | (kj >= BLOCK)[None]
    valid = in_win[None] & key_exists
    slopes = alibi_slopes().reshape(N_KV_HEADS, GQA_GROUP)
    bias = -slopes[:, :, None, None] * dist.astype(jnp.float32)
    scores = jnp.where(valid[None, :, None, None], scores + bias, NEG_INF)
    sink = jnp.broadcast_to(sinks.astype(jnp.float32).reshape(N_KV_HEADS, GQA_GROUP, 1, 1),
                            scores.shape[:-1] + (1,))
    probs = jax.nn.softmax(jnp.concatenate([scores, sink], axis=-1), axis=-1)[..., :-1]
    out = jnp.einsum('bnhgqk,bnkhd->bnqhgd', probs.astype(v.dtype), vb)
    return out.reshape(B, S, D_ATTN)


def setup_inputs(seed: int = 0) -> dict:
    key = jax.random.key(seed)
    ks = jax.random.split(key, 9)
    x = jax.random.normal(ks[0], (BATCH, SEQ, D_MODEL), jnp.float32)
    w_in = jax.random.normal(ks[1], (DEPTH, D_MODEL, D_IN), jnp.float32) * D_MODEL ** -0.5
    pool_w = jax.random.normal(ks[2], (DEPTH, N_POOL_GROUPS, POOL_GROUP, POOL_GROUP), jnp.float32) * POOL_GROUP ** -0.5
    pool_scale = 1.0 + 0.1 * jax.random.normal(ks[3], (DEPTH, D_POOL), jnp.float32)
    attn_sinks = 0.5 * jax.random.normal(ks[4], (DEPTH, N_HEADS), jnp.float32)
    w_out = jax.random.normal(ks[5], (DEPTH, D_MIX, D_MODEL), jnp.float32) * D_MIX ** -0.5
    norm_pre = 1.0 + 0.1 * jax.random.normal(ks[6], (DEPTH, D_MODEL), jnp.float32)
    norm_post = 1.0 + 0.1 * jax.random.normal(ks[7], (DEPTH, D_MODEL), jnp.float32)
    return {"x": x, "w_in": w_in, "pool_w": pool_w, "pool_scale": pool_scale,
            "attn_sinks": attn_sinks, "w_out": w_out, "norm_pre": norm_pre, "norm_post": norm_post}


def reference(x, w_in, pool_w, pool_scale, attn_sinks, w_out, norm_pre, norm_post):
    splits = [int(s) for s in np.cumsum(IN_WIDTHS)[:-1]]
    for layer in range(DEPTH):
        h = rmsnorm(x, norm_pre[layer])
        proj = h @ w_in[layer]
        pool_u, pool_gate, q, k, v, attn_gate = jnp.split(proj, splits, axis=-1)
        pool_out = pool_mixer(pool_u, pool_w[layer], pool_scale[layer]) * jax.nn.silu(pool_gate)
        attn_out = swa_sink_attention(q, k, v, attn_sinks[layer]) * jax.nn.silu(attn_gate)
        y = jnp.concatenate([pool_out, attn_out], axis=-1) @ w_out[layer]
        x = x + rmsnorm(y, norm_post[layer])
    return x
```

```python
import functools

import jax
import jax.numpy as jnp
from jax import lax
from jax.experimental import pallas as pl
from jax.experimental.pallas import tpu as pltpu

D_MODEL = 1024
D_POOL = 512
POOL_WINDOWS = (2, 4, 8, 16)
POOL_GROUP = 128
HEAD_DIM = 64
D_ATTN = 512
N_HEADS = 8
N_KV_HEADS = 2
GQA_GROUP = N_HEADS // N_KV_HEADS
D_KV = N_KV_HEADS * HEAD_DIM
WINDOW = 128
BLOCK = 128
D_IN = 2304
EPS = 1e-6
NEG_INF = -1e30

OFF_U, OFF_PG, OFF_Q, OFF_K, OFF_V, OFF_AG = 0, 512, 1024, 1536, 1664, 1792

LANES = 128
POOL_HALO = 16
SEQ_TILE = 512
VMEM_LIMIT_BYTES = 56 * 1024 * 1024


def _silu(x):
    return x * (1.0 / (1.0 + jnp.exp(-x)))


def _layer_kernel(x_ref, win_ref, poolw_ref, pscale_ref, sinks_ref, wout_ref,
                  gpre_ref, gpost_ref, o_ref,
                  u_buf, q_buf, k_buf, kr_buf, v_buf, vr_buf, mix_buf, bias_buf):
    tile = x_ref.shape[0]
    t = pl.program_id(1)

    @pl.when((pl.program_id(0) == 0) & (t == 0))
    def _build_bias():
        qi = lax.broadcasted_iota(jnp.int32, (BLOCK, 2 * BLOCK), 0)
        kj = lax.broadcasted_iota(jnp.int32, (BLOCK, 2 * BLOCK), 1)
        dist = qi + BLOCK - kj
        in_win = (dist >= 0) & (dist < WINDOW)
        distf = dist.astype(jnp.float32)
        for j in range(N_HEADS):
            slope = 2.0 ** (-8.0 * (j + 1) / N_HEADS)
            b = -slope * distf
            bias_buf[0, j] = jnp.where(in_win, b, NEG_INF)
            bias_buf[1, j] = jnp.where(in_win & (kj >= BLOCK), b, NEG_INF)

    @pl.when(t == 0)
    def _zero_halo():
        u_buf[0:POOL_HALO, :] = jnp.zeros((POOL_HALO, D_POOL), jnp.float32)
        z = jnp.zeros((BLOCK, LANES), jnp.bfloat16)
        k_buf[0:BLOCK, :] = z
        kr_buf[0:BLOCK, :] = z
        v_buf[0:BLOCK, :] = z
        vr_buf[0:BLOCK, :] = z

    x = x_ref[...]
    ms = jnp.mean(x * x, axis=-1, keepdims=True)
    h = (x * lax.rsqrt(ms + EPS) * gpre_ref[...]).astype(jnp.bfloat16)

    def proj(lo, hi):
        return jnp.dot(h, win_ref[:, lo:hi], preferred_element_type=jnp.float32)

    u_buf[POOL_HALO:POOL_HALO + tile, :] = proj(OFF_U, OFF_PG)
    q_buf[...] = proj(OFF_Q, OFF_K).astype(jnp.bfloat16)
    kf = proj(OFF_K, OFF_V)
    vf = proj(OFF_V, OFF_AG)
    k_buf[BLOCK:BLOCK + tile, :] = kf.astype(jnp.bfloat16)
    kr_buf[BLOCK:BLOCK + tile, :] = pltpu.roll(kf, HEAD_DIM, axis=1).astype(jnp.bfloat16)
    v_buf[BLOCK:BLOCK + tile, :] = vf.astype(jnp.bfloat16)
    vr_buf[BLOCK:BLOCK + tile, :] = pltpu.roll(vf, HEAD_DIM, axis=1).astype(jnp.bfloat16)

    pool_gate = _silu(proj(OFF_PG, OFF_Q))
    pos1 = (lax.broadcasted_iota(jnp.int32, (tile, POOL_GROUP), 0) + (t * tile + 1)
            ).astype(jnp.float32)
    for g, w in enumerate(POOL_WINDOWS):
        cols = slice(g * POOL_GROUP, (g + 1) * POOL_GROUP)
        cur = u_buf[POOL_HALO:POOL_HALO + tile, cols]
        acc = cur
        for s in range(1, w):
            acc = acc + u_buf[POOL_HALO - s:POOL_HALO - s + tile, cols]
        pooled = acc / jnp.minimum(pos1, float(w)) - cur
        mixed = jnp.dot(pooled.astype(jnp.bfloat16), poolw_ref[g],
                        preferred_element_type=jnp.float32)
        out = mixed * pscale_ref[:, cols] * pool_gate[:, cols]
        mix_buf[:, cols] = out.astype(jnp.bfloat16)

    attn_gate = _silu(proj(OFF_AG, D_IN))
    lane = lax.broadcasted_iota(jnp.int32, (1, LANES), 1)
    scale = HEAD_DIM ** -0.5
    q_lo = jnp.where(lane < HEAD_DIM, scale, 0.0).astype(jnp.bfloat16)
    q_hi = jnp.where(lane >= HEAD_DIM, scale, 0.0).astype(jnp.bfloat16)
    is_lo = lane < HEAD_DIM
    first = (t == 0)
    nt_dims = (((1,), (1,)), ((), ()))
    for n in range(tile // BLOCK):
        rows = slice(n * BLOCK, (n + 1) * BLOCK)
        krows = slice(n * BLOCK, (n + 2) * BLOCK)
        fsel = jnp.where(first, 1, 0) if n == 0 else 0
        for c in range(D_ATTN // LANES):
            qc = q_buf[rows, c * LANES:(c + 1) * LANES]
            halves = []
            for half in range(2):
                j = 2 * c + half
                kv_head = j // GQA_GROUP
                native = (kv_head == half)
                kk = (k_buf if native else kr_buf)[krows, :]
                vv = (v_buf if native else vr_buf)[krows, :]
                qm = qc * (q_lo if half == 0 else q_hi)
                s = lax.dot_general(qm, kk, nt_dims, preferred_element_type=jnp.float32)
                s = s + bias_buf[fsel, j]
                sink = sinks_ref[j]
                m = jnp.maximum(jnp.max(s, axis=-1, keepdims=True), sink)
                p = jnp.exp(s - m)
                l = jnp.sum(p, axis=-1, keepdims=True) + jnp.exp(sink - m)
                o = jnp.dot(p.astype(jnp.bfloat16), vv, preferred_element_type=jnp.float32)
                halves.append(o / l)
            o_c = jnp.where(is_lo, halves[0], halves[1])
            o_c = o_c * attn_gate[rows, c * LANES:(c + 1) * LANES]
            mix_buf[rows, D_POOL + c * LANES:D_POOL + (c + 1) * LANES] = o_c.astype(jnp.bfloat16)

    u_buf[0:POOL_HALO, :] = u_buf[tile:tile + POOL_HALO, :]
    k_buf[0:BLOCK, :] = k_buf[tile:tile + BLOCK, :]
    kr_buf[0:BLOCK, :] = kr_buf[tile:tile + BLOCK, :]
    v_buf[0:BLOCK, :] = v_buf[tile:tile + BLOCK, :]
    vr_buf[0:BLOCK, :] = vr_buf[tile:tile + BLOCK, :]

    y = jnp.dot(mix_buf[...], wout_ref[...], preferred_element_type=jnp.float32)
    ms2 = jnp.mean(y * y, axis=-1, keepdims=True)
    o_ref[...] = x_ref[...] + y * lax.rsqrt(ms2 + EPS) * gpost_ref[...]


def _layer(x, w_in, pool_w, pool_scale, sinks, w_out, g_pre, g_post):
    batch, seq, d = x.shape
    tile = SEQ_TILE
    assert seq % tile == 0 and tile % BLOCK == 0 and d == D_MODEL
    full = lambda shape: pl.BlockSpec(shape, lambda b, t: (0,) * len(shape))
    return pl.pallas_call(
        _layer_kernel,
        out_shape=jax.ShapeDtypeStruct(x.shape, x.dtype),
        grid=(batch, seq // tile),
        in_specs=[
            pl.BlockSpec((None, tile, d), lambda b, t: (b, t, 0)),
            full((D_MODEL, D_IN)),
            full((len(POOL_WINDOWS), POOL_GROUP, POOL_GROUP)),
            full((1, D_POOL)),
            pl.BlockSpec(memory_space=pltpu.SMEM),
            full((D_MODEL, D_MODEL)),
            full((1, D_MODEL)),
            full((1, D_MODEL)),
        ],
        out_specs=pl.BlockSpec((None, tile, d), lambda b, t: (b, t, 0)),
        scratch_shapes=[
            pltpu.VMEM((POOL_HALO + tile, D_POOL), jnp.float32),
            pltpu.VMEM((tile, D_ATTN), jnp.bfloat16),
            pltpu.VMEM((BLOCK + tile, LANES), jnp.bfloat16),
            pltpu.VMEM((BLOCK + tile, LANES), jnp.bfloat16),
            pltpu.VMEM((BLOCK + tile, LANES), jnp.bfloat16),
            pltpu.VMEM((BLOCK + tile, LANES), jnp.bfloat16),
            pltpu.VMEM((tile, D_MODEL), jnp.bfloat16),
            pltpu.VMEM((2, N_HEADS, BLOCK, 2 * BLOCK), jnp.float32),
        ],
        compiler_params=pltpu.CompilerParams(
            dimension_semantics=("arbitrary", "arbitrary"),
            vmem_limit_bytes=VMEM_LIMIT_BYTES),
        name="hybrid_layer",
    )(x, w_in, pool_w, pool_scale, sinks, w_out, g_pre, g_post)


@jax.jit
def kernel(x, w_in, pool_w, pool_scale, attn_sinks, w_out, norm_pre, norm_post):
    depth = w_in.shape[0]
    w_in_b = w_in.astype(jnp.bfloat16)
    pool_w_b = pool_w.astype(jnp.bfloat16)
    w_out_b = w_out.astype(jnp.bfloat16)
    for layer in range(depth):
        x = _layer(x, w_in_b[layer], pool_w_b[layer], pool_scale[layer][None, :],
                   attn_sinks[layer], w_out_b[layer],
                   norm_pre[layer][None, :], norm_post[layer][None, :])
    return x
```

```python
import functools

import jax
import jax.numpy as jnp
from jax import lax
from jax.experimental import pallas as pl
from jax.experimental.pallas import tpu as pltpu

D_MODEL = 1024
D_POOL = 512
POOL_WINDOWS = (2, 4, 8, 16)
POOL_GROUP = 128
HEAD_DIM = 64
D_ATTN = 512
N_HEADS = 8
N_KV_HEADS = 2
GQA_GROUP = N_HEADS // N_KV_HEADS
WINDOW = 128
D_IN = 2304
EPS = 1e-6
NEG_INF = -1e30

OFF_U, OFF_PG, OFF_Q, OFF_K, OFF_V, OFF_AG = 0, 512, 1024, 1536, 1664, 1792

LANES = 128
POOL_HALO = 16
SEQ_TILE = 512
Q_SUB = 64
KEY_SPAN = Q_SUB + WINDOW
VMEM_LIMIT_BYTES = 56 * 1024 * 1024


def _silu(x):
    return x * (1.0 / (1.0 + jnp.exp(-x)))


def _layer_kernel(layer, x_ref, win_ref, poolw_ref, pscale_ref, sinks_ref, wout_ref,
                  gpre_ref, gpost_ref, o_ref,
                  u_buf, q_buf, kz_buf, vt_buf, ag_buf, mix_buf, bias_buf):
    tile = x_ref.shape[0]
    t = pl.program_id(1)

    @pl.when((pl.program_id(0) == 0) & (t == 0))
    def _build_bias():
        r = lax.broadcasted_iota(jnp.int32, (KEY_SPAN, GQA_GROUP * Q_SUB), 0)
        col = lax.broadcasted_iota(jnp.int32, (KEY_SPAN, GQA_GROUP * Q_SUB), 1)
        dist = (col & (Q_SUB - 1)) + WINDOW - r
        in_win = (dist >= 0) & (dist < WINDOW)
        distf = dist.astype(jnp.float32)
        for kv in range(N_KV_HEADS):
            slope = jnp.zeros_like(distf)
            for g in range(GQA_GROUP):
                j = kv * GQA_GROUP + g
                slope = jnp.where(col >= g * Q_SUB, 2.0 ** (-8.0 * (j + 1) / N_HEADS), slope)
            b = -slope * distf
            for f in range(3):
                bias_buf[f, kv * KEY_SPAN:(kv + 1) * KEY_SPAN, :] = jnp.where(
                    in_win & (r >= f * Q_SUB), b, NEG_INF)

    @pl.when(t == 0)
    def _zero_halo():
        u_buf[0:POOL_HALO, :] = jnp.zeros((POOL_HALO, D_POOL), jnp.float32)
        kz_buf[:, 0:WINDOW, :] = jnp.zeros((N_KV_HEADS, WINDOW, LANES), jnp.bfloat16)
        vt_buf[:, 0:WINDOW] = jnp.zeros((LANES, WINDOW), jnp.bfloat16)

    x = x_ref[...]
    ms = jnp.mean(x * x, axis=-1, keepdims=True)
    h = (x * lax.rsqrt(ms + EPS) * gpre_ref[...]).astype(jnp.bfloat16)

    def proj(lo, hi):
        return jnp.dot(h, win_ref[:, lo:hi], preferred_element_type=jnp.float32)

    u_buf[POOL_HALO:POOL_HALO + tile, :] = proj(OFF_U, OFF_PG)
    q_buf[...] = (proj(OFF_Q, OFF_K) * (HEAD_DIM ** -0.5)).astype(jnp.bfloat16)
    kvf = proj(OFF_K, OFF_AG)
    kf, vf = kvf[:, :LANES], kvf[:, LANES:]
    lane = lax.broadcasted_iota(jnp.int32, (1, LANES), 1)
    kz_buf[0, WINDOW:WINDOW + tile, :] = jnp.where(lane < HEAD_DIM, kf, 0.0).astype(jnp.bfloat16)
    kz_buf[1, WINDOW:WINDOW + tile, :] = jnp.where(lane >= HEAD_DIM, kf, 0.0).astype(jnp.bfloat16)
    vt_buf[:, WINDOW:WINDOW + tile] = vf.T.astype(jnp.bfloat16)
    ag_buf[...] = _silu(proj(OFF_AG, D_IN))

    pool_gate = _silu(proj(OFF_PG, OFF_Q))
    pos1 = (lax.broadcasted_iota(jnp.int32, (tile, POOL_GROUP), 0) + (t * tile + 1)
            ).astype(jnp.float32)
    for g, w in enumerate(POOL_WINDOWS):
        cols = slice(g * POOL_GROUP, (g + 1) * POOL_GROUP)
        cur = u_buf[POOL_HALO:POOL_HALO + tile, cols]
        acc = cur
        for s in range(1, w):
            acc = acc + u_buf[POOL_HALO - s:POOL_HALO - s + tile, cols]
        pooled = acc / jnp.minimum(pos1, float(w)) - cur
        mixed = jnp.dot(pooled.astype(jnp.bfloat16), poolw_ref[g],
                        preferred_element_type=jnp.float32)
        out = mixed * pscale_ref[:, cols] * pool_gate[:, cols]
        mix_buf[:, cols] = out.astype(jnp.bfloat16)

    col = lax.broadcasted_iota(jnp.int32, (1, GQA_GROUP * Q_SUB), 1)
    sink_rows = []
    for kv in range(N_KV_HEADS):
        row = jnp.zeros((1, GQA_GROUP * Q_SUB), jnp.float32)
        for g in range(GQA_GROUP):
            row = jnp.where(col >= g * Q_SUB, sinks_ref[layer, kv * GQA_GROUP + g], row)
        sink_rows.append(row)
    first = (t == 0)
    nt_dims = (((1,), (1,)), ((), ()))
    pad = jnp.zeros((Q_SUB, GQA_GROUP * Q_SUB), jnp.bfloat16)
    for sb in range(tile // Q_SUB):
        r0 = sb * Q_SUB
        qrows = slice(r0, r0 + Q_SUB)
        krows = slice(r0, r0 + KEY_SPAN)
        variant = jnp.where(first, 2 - sb, 0) if sb < 2 else 0
        q4 = jnp.concatenate([q_buf[qrows, g * LANES:(g + 1) * LANES] for g in range(GQA_GROUP)],
                             axis=0)
        kk = jnp.concatenate([kz_buf[0, krows, :], kz_buf[1, krows, :]], axis=0)
        s = lax.dot_general(kk, q4, nt_dims, preferred_element_type=jnp.float32)
        s = s + bias_buf[variant]
        outs = []
        for kv in range(N_KV_HEADS):
            sh = s[kv * KEY_SPAN:(kv + 1) * KEY_SPAN]
            m = jnp.maximum(jnp.max(sh, axis=0, keepdims=True), sink_rows[kv])
            p = jnp.exp(sh - m)
            l = jnp.sum(p, axis=0, keepdims=True) + jnp.exp(sink_rows[kv] - m)
            pb = p.astype(jnp.bfloat16)
            if sb % 2 == 0:
                pb = jnp.concatenate([pb, pad], axis=0)
                c0 = r0
            else:
                pb = jnp.concatenate([pad, pb], axis=0)
                c0 = r0 - Q_SUB
            vt = vt_buf[kv * HEAD_DIM:(kv + 1) * HEAD_DIM, c0:c0 + 2 * LANES]
            o = jnp.dot(vt, pb, preferred_element_type=jnp.float32)
            outs.append(o * (1.0 / l))
        for pr in range(GQA_GROUP // 2):
            both = jnp.concatenate([o[:, pr * LANES:(pr + 1) * LANES] for o in outs], axis=0)
            both_t = both.T
            for half in range(2):
                g = 2 * pr + half
                gcols = slice(g * LANES, (g + 1) * LANES)
                val = both_t[half * Q_SUB:(half + 1) * Q_SUB] * ag_buf[qrows, gcols]
                mix_buf[qrows, D_POOL + g * LANES:D_POOL + (g + 1) * LANES] = val.astype(jnp.bfloat16)

    u_buf[0:POOL_HALO, :] = u_buf[tile:tile + POOL_HALO, :]
    kz_buf[:, 0:WINDOW, :] = kz_buf[:, tile:tile + WINDOW, :]
    vt_buf[:, 0:WINDOW] = vt_buf[:, tile:tile + WINDOW]

    y = jnp.dot(mix_buf[...], wout_ref[...], preferred_element_type=jnp.float32)
    ms2 = jnp.mean(y * y, axis=-1, keepdims=True)
    o_ref[...] = x_ref[...] + y * lax.rsqrt(ms2 + EPS) * gpost_ref[...]


def _layer(layer, x, w_in, pool_w, pool_scale, sinks, w_out, g_pre, g_post):
    batch, seq, d = x.shape
    tile = SEQ_TILE
    assert seq % tile == 0 and tile % (2 * Q_SUB) == 0 and d == D_MODEL

    def of_layer(*shape):
        return pl.BlockSpec((None,) + shape, lambda b, t: (layer,) + (0,) * len(shape))

    return pl.pallas_call(
        functools.partial(_layer_kernel, layer),
        out_shape=jax.ShapeDtypeStruct(x.shape, x.dtype),
        grid=(batch, seq // tile),
        in_specs=[
            pl.BlockSpec((None, tile, d), lambda b, t: (b, t, 0)),
            of_layer(D_MODEL, D_IN),
            of_layer(len(POOL_WINDOWS), POOL_GROUP, POOL_GROUP),
            of_layer(1, D_POOL),
            pl.BlockSpec(memory_space=pltpu.SMEM),
            of_layer(D_MODEL, D_MODEL),
            of_layer(1, D_MODEL),
            of_layer(1, D_MODEL),
        ],
        out_specs=pl.BlockSpec((None, tile, d), lambda b, t: (b, t, 0)),
        scratch_shapes=[
            pltpu.VMEM((POOL_HALO + tile, D_POOL), jnp.float32),
            pltpu.VMEM((tile, D_ATTN), jnp.bfloat16),
            pltpu.VMEM((N_KV_HEADS, WINDOW + tile, LANES), jnp.bfloat16),
            pltpu.VMEM((LANES, WINDOW + tile), jnp.bfloat16),
            pltpu.VMEM((tile, D_ATTN), jnp.float32),
            pltpu.VMEM((tile, D_MODEL), jnp.bfloat16),
            pltpu.VMEM((3, N_KV_HEADS * KEY_SPAN, GQA_GROUP * Q_SUB), jnp.float32),
        ],
        compiler_params=pltpu.CompilerParams(
            dimension_semantics=("arbitrary", "arbitrary"),
            vmem_limit_bytes=VMEM_LIMIT_BYTES),
        name="hybrid_layer",
    )(x, w_in, pool_w, pool_scale, sinks, w_out, g_pre, g_post)


def _heads_g_major(w, axis):
    shape = w.shape
    w = w.reshape(shape[:axis] + (N_KV_HEADS, GQA_GROUP, HEAD_DIM) + shape[axis + 1:])
    w = jnp.swapaxes(w, axis, axis + 1)
    return w.reshape(shape)


@jax.jit
def kernel(x, w_in, pool_w, pool_scale, attn_sinks, w_out, norm_pre, norm_post):
    depth = w_in.shape[0]
    w_in_b = jnp.concatenate(
        [w_in[..., :OFF_Q], _heads_g_major(w_in[..., OFF_Q:OFF_K], 2), w_in[..., OFF_K:OFF_AG],
         _heads_g_major(w_in[..., OFF_AG:], 2)], axis=-1).astype(jnp.bfloat16)
    w_out_b = jnp.concatenate(
        [w_out[:, :D_POOL], _heads_g_major(w_out[:, D_POOL:], 1)], axis=1).astype(jnp.bfloat16)
    pool_w_b = pool_w.astype(jnp.bfloat16)
    for layer in range(depth):
        x = _layer(layer, x, w_in_b, pool_w_b, pool_scale[:, None, :], attn_sinks, w_out_b,
                   norm_pre[:, None, :], norm_post[:, None, :])
    return x
```

```python
import functools

import jax
import jax.numpy as jnp
from jax import lax
from jax.experimental import pallas as pl
from jax.experimental.pallas import tpu as pltpu

D_MODEL = 1024
D_POOL = 512
POOL_WINDOWS = (2, 4, 8, 16)
POOL_GROUP = 128
HEAD_DIM = 64
D_ATTN = 512
N_HEADS = 8
N_KV_HEADS = 2
GQA_GROUP = N_HEADS // N_KV_HEADS
WINDOW = 128
D_IN = 2304
EPS = 1e-6
NEG_INF = -1e30

OFF_U, OFF_PG, OFF_Q, OFF_K, OFF_V, OFF_AG = 0, 512, 1024, 1536, 1664, 1792

LANES = 128
MXU_N = 256
POOL_HALO = 16
SEQ_TILE = 512
Q_SUB = 64
KEY_SPAN = Q_SUB + WINDOW
VMEM_LIMIT_BYTES = 56 * 1024 * 1024


def _silu(x):
    return x * (1.0 / (1.0 + jnp.exp(-x)))


def _layer_kernel(layer, x_ref, win_ref, poolw_ref, pscale_ref, sinks_ref, wout_ref,
                  gpre_ref, gpost_ref, o_ref,
                  u_buf, q_buf, kz_buf, vt_buf, ag_buf, pg_buf, mixp_buf, mixa_buf, bias_buf, wqa_buf, wo_buf):
    tile = x_ref.shape[0]
    n_sub = tile // Q_SUB
    t = pl.program_id(1)

    @pl.when((pl.program_id(0) == 0) & (t == 0))
    def _first_step():
        r = lax.broadcasted_iota(jnp.int32, (KEY_SPAN, GQA_GROUP * Q_SUB), 0)
        col = lax.broadcasted_iota(jnp.int32, (KEY_SPAN, GQA_GROUP * Q_SUB), 1)
        dist = (col & (Q_SUB - 1)) + WINDOW - r
        in_win = (dist >= 0) & (dist < WINDOW)
        distf = dist.astype(jnp.float32)
        for kv in range(N_KV_HEADS):
            slope = jnp.zeros_like(distf)
            for g in range(GQA_GROUP):
                j = kv * GQA_GROUP + g
                slope = jnp.where(col >= g * Q_SUB, 2.0 ** (-8.0 * (j + 1) / N_HEADS), slope)
            b = -slope * distf
            for f in range(3):
                bias_buf[f, kv * KEY_SPAN:(kv + 1) * KEY_SPAN, :] = jnp.where(
                    in_win & (r >= f * Q_SUB), b, NEG_INF)

        a0 = lax.broadcasted_iota(jnp.int32, (D_ATTN, D_ATTN), 0)
        a1 = lax.broadcasted_iota(jnp.int32, (D_ATTN, D_ATTN), 1)

        def source(n):
            return ((n >> 6) & 1) * (GQA_GROUP * HEAD_DIM) + (n >> 7) * HEAD_DIM + (n & (HEAD_DIM - 1))

        perm = jnp.where(a0 == source(a1), 1.0, 0.0).astype(jnp.bfloat16)
        perm_t = jnp.where(a1 == source(a0), 1.0, 0.0).astype(jnp.bfloat16)
        wqa_buf[:, 0:D_ATTN] = jnp.dot(win_ref[:, OFF_Q:OFF_K], perm,
                                       preferred_element_type=jnp.float32).astype(jnp.bfloat16)
        wqa_buf[:, D_ATTN:] = jnp.dot(win_ref[:, OFF_AG:D_IN], perm,
                                      preferred_element_type=jnp.float32).astype(jnp.bfloat16)
        wo_buf[...] = jnp.dot(perm_t, wout_ref[D_POOL:, :],
                              preferred_element_type=jnp.float32).astype(jnp.bfloat16)

    @pl.when(t == 0)
    def _zero_halo():
        u_buf[0:POOL_HALO, :] = jnp.zeros((POOL_HALO, D_POOL), jnp.float32)
        kz_buf[:, 0:WINDOW, :] = jnp.zeros((N_KV_HEADS, WINDOW, LANES), jnp.bfloat16)
        vt_buf[:, 0:WINDOW] = jnp.zeros((LANES, WINDOW), jnp.bfloat16)

    x = x_ref[...]
    ms = jnp.mean(x * x, axis=-1, keepdims=True)
    h = (x * lax.rsqrt(ms + EPS) * gpre_ref[...]).astype(jnp.bfloat16)

    def proj(w_ref, lo, hi):
        return jnp.dot(h, w_ref[:, lo:hi], preferred_element_type=jnp.float32)

    kvf = proj(win_ref, OFF_K, OFF_AG)
    q_buf[...] = (proj(wqa_buf, 0, D_ATTN) * (HEAD_DIM ** -0.5)).astype(jnp.bfloat16)
    kf, vf = kvf[:, :LANES], kvf[:, LANES:]
    lane = lax.broadcasted_iota(jnp.int32, (1, LANES), 1)
    kz_buf[0, WINDOW:WINDOW + tile, :] = jnp.where(lane < HEAD_DIM, kf, 0.0).astype(jnp.bfloat16)
    kz_buf[1, WINDOW:WINDOW + tile, :] = jnp.where(lane >= HEAD_DIM, kf, 0.0).astype(jnp.bfloat16)
    vt_buf[:, WINDOW:WINDOW + tile] = vf.T.astype(jnp.bfloat16)

    def attn_gate_item():
        ag_buf[...] = _silu(proj(wqa_buf, D_ATTN, 2 * D_ATTN))

    def pool_u_item():
        u_buf[POOL_HALO:POOL_HALO + tile, :] = proj(win_ref, OFF_U, OFF_PG)

    pos1 = (lax.broadcasted_iota(jnp.int32, (tile, POOL_GROUP), 0) + (t * tile + 1)
            ).astype(jnp.float32)

    def pool_mix_item():
        for g, w in enumerate(POOL_WINDOWS):
            cols = slice(g * POOL_GROUP, (g + 1) * POOL_GROUP)
            ext = u_buf[:, cols]
            acc = ext
            shift = 1
            while shift < w:
                acc = acc + pltpu.roll(acc, shift, axis=0)
                shift *= 2
            cur = ext[POOL_HALO:]
            pooled = acc[POOL_HALO:] / jnp.minimum(pos1, float(w)) - cur
            mixed = jnp.dot(pooled.astype(jnp.bfloat16), poolw_ref[g],
                            preferred_element_type=jnp.float32)
            mixp_buf[:, cols] = (mixed * pscale_ref[:, cols] * pg_buf[:, cols]).astype(jnp.bfloat16)

    def pool_gate_item():
        pg_buf[...] = _silu(proj(win_ref, OFF_PG, OFF_Q))

    col = lax.broadcasted_iota(jnp.int32, (1, GQA_GROUP * Q_SUB), 1)
    sink_rows = []
    for kv in range(N_KV_HEADS):
        row = jnp.zeros((1, GQA_GROUP * Q_SUB), jnp.float32)
        for g in range(GQA_GROUP):
            row = jnp.where(col >= g * Q_SUB, sinks_ref[layer, kv * GQA_GROUP + g], row)
        sink_rows.append(row)
    first = (t == 0)
    nt_dims = (((1,), (1,)), ((), ()))
    pad = jnp.zeros((Q_SUB, GQA_GROUP * Q_SUB), jnp.bfloat16)

    def attn_scores(sb):
        r0 = sb * Q_SUB
        qrows = slice(r0, r0 + Q_SUB)
        krows = slice(r0, r0 + KEY_SPAN)
        variant = jnp.where(first, 2 - sb, 0) if sb < 2 else 0
        q4 = jnp.concatenate([q_buf[qrows, g * LANES:(g + 1) * LANES] for g in range(GQA_GROUP)],
                             axis=0)
        kk = jnp.concatenate([kz_buf[0, krows, :], kz_buf[1, krows, :]], axis=0)
        s = lax.dot_general(kk, q4, nt_dims, preferred_element_type=jnp.float32)
        return s + bias_buf[variant]

    def attn_finish(sb, s):
        r0 = sb * Q_SUB
        qrows = slice(r0, r0 + Q_SUB)
        outs = []
        for kv in range(N_KV_HEADS):
            sh = s[kv * KEY_SPAN:(kv + 1) * KEY_SPAN]
            m = jnp.maximum(jnp.max(sh, axis=0, keepdims=True), sink_rows[kv])
            p = jnp.exp(sh - m)
            l = jnp.sum(p, axis=0, keepdims=True) + jnp.exp(sink_rows[kv] - m)
            pb = p.astype(jnp.bfloat16)
            if sb % 2 == 0:
                pb = jnp.concatenate([pb, pad], axis=0)
                c0 = r0
            else:
                pb = jnp.concatenate([pad, pb], axis=0)
                c0 = r0 - Q_SUB
            vt = vt_buf[kv * HEAD_DIM:(kv + 1) * HEAD_DIM, c0:c0 + 2 * LANES]
            o = jnp.dot(vt, pb, preferred_element_type=jnp.float32)
            outs.append(o * (1.0 / l))
        for pr in range(GQA_GROUP // 2):
            both = jnp.concatenate([o[:, pr * LANES:(pr + 1) * LANES] for o in outs], axis=0)
            both_t = both.T
            for half in range(2):
                g = 2 * pr + half
                val = both_t[half * Q_SUB:(half + 1) * Q_SUB] * ag_buf[qrows, g * LANES:(g + 1) * LANES]
                mixa_buf[qrows, g * LANES:(g + 1) * LANES] = val.astype(jnp.bfloat16)

    y_pool = []

    def out_pool_item():
        y_pool.append(jnp.dot(mixp_buf[...], wout_ref[:D_POOL, :],
                              preferred_element_type=jnp.float32))

    attn_gate_item()
    dense_items = [pool_u_item, pool_gate_item, pool_mix_item, out_pool_item]
    group = n_sub // len(dense_items)
    for p, item in enumerate(dense_items):
        subs = range(p * group, (p + 1) * group)
        scores = [attn_scores(sb) for sb in subs]
        item()
        for sb, s in zip(subs, scores):
            attn_finish(sb, s)

    u_buf[0:POOL_HALO, :] = u_buf[tile:tile + POOL_HALO, :]
    kz_buf[:, 0:WINDOW, :] = kz_buf[:, tile:tile + WINDOW, :]
    vt_buf[:, 0:WINDOW] = vt_buf[:, tile:tile + WINDOW]

    y = y_pool[0] + jnp.dot(mixa_buf[...], wo_buf[...], preferred_element_type=jnp.float32)
    ms2 = jnp.mean(y * y, axis=-1, keepdims=True)
    o_ref[...] = x_ref[...] + y * lax.rsqrt(ms2 + EPS) * gpost_ref[...]


def _layer(layer, x, w_in, pool_w, pool_scale, sinks, w_out, g_pre, g_post):
    batch, seq, d = x.shape
    tile = SEQ_TILE
    assert seq % tile == 0 and tile % (2 * Q_SUB) == 0 and d == D_MODEL

    def of_layer(*shape):
        return pl.BlockSpec((None,) + shape, lambda b, t: (layer,) + (0,) * len(shape))

    return pl.pallas_call(
        functools.partial(_layer_kernel, layer),
        out_shape=jax.ShapeDtypeStruct(x.shape, x.dtype),
        grid=(batch, seq // tile),
        in_specs=[
            pl.BlockSpec((None, tile, d), lambda b, t: (b, t, 0)),
            of_layer(D_MODEL, D_IN),
            of_layer(len(POOL_WINDOWS), POOL_GROUP, POOL_GROUP),
            of_layer(1, D_POOL),
            pl.BlockSpec(memory_space=pltpu.SMEM),
            of_layer(D_MODEL, D_MODEL),
            of_layer(1, D_MODEL),
            of_layer(1, D_MODEL),
        ],
        out_specs=pl.BlockSpec((None, tile, d), lambda b, t: (b, t, 0)),
        scratch_shapes=[
            pltpu.VMEM((POOL_HALO + tile, D_POOL), jnp.float32),
            pltpu.VMEM((tile, D_ATTN), jnp.bfloat16),
            pltpu.VMEM((N_KV_HEADS, WINDOW + tile, LANES), jnp.bfloat16),
            pltpu.VMEM((LANES, WINDOW + tile), jnp.bfloat16),
            pltpu.VMEM((tile, D_ATTN), jnp.float32),
            pltpu.VMEM((tile, D_POOL), jnp.float32),
            pltpu.VMEM((tile, D_POOL), jnp.bfloat16),
            pltpu.VMEM((tile, D_ATTN), jnp.bfloat16),
            pltpu.VMEM((3, N_KV_HEADS * KEY_SPAN, GQA_GROUP * Q_SUB), jnp.float32),
            pltpu.VMEM((D_MODEL, 2 * D_ATTN), jnp.bfloat16),
            pltpu.VMEM((D_ATTN, D_MODEL), jnp.bfloat16),
        ],
        compiler_params=pltpu.CompilerParams(
            dimension_semantics=("arbitrary", "arbitrary"),
            vmem_limit_bytes=VMEM_LIMIT_BYTES),
        name="hybrid_layer",
    )(x, w_in, pool_w, pool_scale, sinks, w_out, g_pre, g_post)


@jax.jit
def kernel(x, w_in, pool_w, pool_scale, attn_sinks, w_out, norm_pre, norm_post):
    depth = w_in.shape[0]
    w_in_b = w_in.astype(jnp.bfloat16)
    w_out_b = w_out.astype(jnp.bfloat16)
    pool_w_b = pool_w.astype(jnp.bfloat16)
    for layer in range(depth):
        x = _layer(layer, x, w_in_b, pool_w_b, pool_scale[:, None, :], attn_sinks, w_out_b,
                   norm_pre[:, None, :], norm_post[:, None, :])
    return x
```

```python
import functools

import jax
import jax.numpy as jnp
from jax import lax
from jax.experimental import pallas as pl
from jax.experimental.pallas import tpu as pltpu

D_MODEL = 1024
D_POOL = 512
POOL_WINDOWS = (2, 4, 8, 16)
POOL_GROUP = 128
HEAD_DIM = 64
D_ATTN = 512
N_HEADS = 8
N_KV_HEADS = 2
GQA_GROUP = N_HEADS // N_KV_HEADS
WINDOW = 128
D_IN = 2304
EPS = 1e-6
NEG_INF = -1e30

OFF_U, OFF_PG, OFF_Q, OFF_K, OFF_V, OFF_AG = 0, 512, 1024, 1536, 1664, 1792

LANES = 128
POOL_HALO = 16
SEQ_TILE = 512
Q_SUB = 64
KEY_SPAN = Q_SUB + WINDOW
VMEM_LIMIT_BYTES = 56 * 1024 * 1024


def _silu(x):
    return x * (1.0 / (1.0 + jnp.exp(-x)))


def _layer_kernel(layer, x_ref, win_ref, poolw_ref, pscale_ref, sinks_ref, wout_ref,
                  gpre_ref, gpost_ref, o_ref,
                  u_buf, q_buf, kz_buf, vt_buf, ag_buf, pg_buf, mixp_buf, mixa_buf, bias_buf, wi_buf, wo_buf):
    tile = x_ref.shape[0]
    n_sub = tile // Q_SUB
    t = pl.program_id(1)

    @pl.when((pl.program_id(0) == 0) & (t == 0))
    def _first_step():
        r = lax.broadcasted_iota(jnp.int32, (KEY_SPAN, GQA_GROUP * Q_SUB), 0)
        col = lax.broadcasted_iota(jnp.int32, (KEY_SPAN, GQA_GROUP * Q_SUB), 1)
        dist = (col & (Q_SUB - 1)) + WINDOW - r
        in_win = (dist >= 0) & (dist < WINDOW)
        distf = dist.astype(jnp.float32)
        for kv in range(N_KV_HEADS):
            slope = jnp.zeros_like(distf)
            for g in range(GQA_GROUP):
                j = kv * GQA_GROUP + g
                slope = jnp.where(col >= g * Q_SUB, 2.0 ** (-8.0 * (j + 1) / N_HEADS), slope)
            b = -slope * distf
            for f in range(3):
                bias_buf[f, kv * KEY_SPAN:(kv + 1) * KEY_SPAN, :] = jnp.where(
                    in_win & (r >= f * Q_SUB), b, NEG_INF)

        a0 = lax.broadcasted_iota(jnp.int32, (D_ATTN, D_ATTN), 0)
        a1 = lax.broadcasted_iota(jnp.int32, (D_ATTN, D_ATTN), 1)

        def source(n):
            return ((n >> 6) & 1) * (GQA_GROUP * HEAD_DIM) + (n >> 7) * HEAD_DIM + (n & (HEAD_DIM - 1))

        perm = jnp.where(a0 == source(a1), 1.0, 0.0).astype(jnp.bfloat16)
        perm_t = jnp.where(a1 == source(a0), 1.0, 0.0).astype(jnp.bfloat16)
        for lo, hi in ((OFF_U, OFF_PG), (OFF_PG, OFF_Q), (OFF_Q, OFF_K), (OFF_K, OFF_AG), (OFF_AG, D_IN)):
            w = win_ref[:, lo:hi].astype(jnp.bfloat16)
            if lo in (OFF_Q, OFF_AG):
                w = jnp.dot(w, perm, preferred_element_type=jnp.float32).astype(jnp.bfloat16)
            wi_buf[:, lo:hi] = w
        wo_buf[:D_POOL, :] = wout_ref[:D_POOL, :].astype(jnp.bfloat16)
        wo_buf[D_POOL:, :] = jnp.dot(perm_t, wout_ref[D_POOL:, :].astype(jnp.bfloat16),
                                     preferred_element_type=jnp.float32).astype(jnp.bfloat16)

    @pl.when(t == 0)
    def _zero_halo():
        u_buf[0:POOL_HALO, :] = jnp.zeros((POOL_HALO, D_POOL), jnp.float32)
        kz_buf[:, 0:WINDOW, :] = jnp.zeros((N_KV_HEADS, WINDOW, LANES), jnp.bfloat16)
        vt_buf[:, 0:WINDOW] = jnp.zeros((LANES, WINDOW), jnp.bfloat16)

    x = x_ref[...]
    ms = jnp.mean(x * x, axis=-1, keepdims=True)
    h = (x * lax.rsqrt(ms + EPS) * gpre_ref[layer:layer + 1, :]).astype(jnp.bfloat16)

    def proj(lo, hi):
        return jnp.dot(h, wi_buf[:, lo:hi], preferred_element_type=jnp.float32)

    kvf = proj(OFF_K, OFF_AG)
    q_buf[...] = (proj(OFF_Q, OFF_K) * (HEAD_DIM ** -0.5)).astype(jnp.bfloat16)
    kf, vf = kvf[:, :LANES], kvf[:, LANES:]
    lane = lax.broadcasted_iota(jnp.int32, (1, LANES), 1)
    kz_buf[0, WINDOW:WINDOW + tile, :] = jnp.where(lane < HEAD_DIM, kf, 0.0).astype(jnp.bfloat16)
    kz_buf[1, WINDOW:WINDOW + tile, :] = jnp.where(lane >= HEAD_DIM, kf, 0.0).astype(jnp.bfloat16)
    vt_buf[:, WINDOW:WINDOW + tile] = vf.T.astype(jnp.bfloat16)

    def attn_gate_item():
        ag_buf[...] = _silu(proj(OFF_AG, D_IN))

    def pool_u_item():
        u_buf[POOL_HALO:POOL_HALO + tile, :] = proj(OFF_U, OFF_PG)

    pos1 = (lax.broadcasted_iota(jnp.int32, (tile, POOL_GROUP), 0) + (t * tile + 1)
            ).astype(jnp.float32)

    def pool_mix_item():
        for g, w in enumerate(POOL_WINDOWS):
            cols = slice(g * POOL_GROUP, (g + 1) * POOL_GROUP)
            ext = u_buf[:, cols]
            acc = ext
            shift = 1
            while shift < w:
                acc = acc + pltpu.roll(acc, shift, axis=0)
                shift *= 2
            cur = ext[POOL_HALO:]
            pooled = acc[POOL_HALO:] / jnp.minimum(pos1, float(w)) - cur
            mixed = jnp.dot(pooled.astype(jnp.bfloat16), poolw_ref[g].astype(jnp.bfloat16),
                            preferred_element_type=jnp.float32)
            scale = pscale_ref[layer:layer + 1, cols]
            mixp_buf[:, cols] = (mixed * scale * pg_buf[:, cols]).astype(jnp.bfloat16)

    def pool_gate_item():
        pg_buf[...] = _silu(proj(OFF_PG, OFF_Q))

    col = lax.broadcasted_iota(jnp.int32, (1, GQA_GROUP * Q_SUB), 1)
    sink_rows = []
    for kv in range(N_KV_HEADS):
        row = jnp.zeros((1, GQA_GROUP * Q_SUB), jnp.float32)
        for g in range(GQA_GROUP):
            row = jnp.where(col >= g * Q_SUB, sinks_ref[layer, kv * GQA_GROUP + g], row)
        sink_rows.append(row)
    first = (t == 0)
    nt_dims = (((1,), (1,)), ((), ()))
    pad = jnp.zeros((Q_SUB, GQA_GROUP * Q_SUB), jnp.bfloat16)

    def attn_scores(sb):
        r0 = sb * Q_SUB
        qrows = slice(r0, r0 + Q_SUB)
        krows = slice(r0, r0 + KEY_SPAN)
        variant = jnp.where(first, 2 - sb, 0) if sb < 2 else 0
        q4 = jnp.concatenate([q_buf[qrows, g * LANES:(g + 1) * LANES] for g in range(GQA_GROUP)],
                             axis=0)
        kk = jnp.concatenate([kz_buf[0, krows, :], kz_buf[1, krows, :]], axis=0)
        s = lax.dot_general(kk, q4, nt_dims, preferred_element_type=jnp.float32)
        return s + bias_buf[variant]

    def attn_finish(sb, s):
        r0 = sb * Q_SUB
        qrows = slice(r0, r0 + Q_SUB)
        outs = []
        for kv in range(N_KV_HEADS):
            sh = s[kv * KEY_SPAN:(kv + 1) * KEY_SPAN]
            m = jnp.maximum(jnp.max(sh, axis=0, keepdims=True), sink_rows[kv])
            p = jnp.exp(sh - m)
            l = jnp.sum(p, axis=0, keepdims=True) + jnp.exp(sink_rows[kv] - m)
            pb = p.astype(jnp.bfloat16)
            if sb % 2 == 0:
                pb = jnp.concatenate([pb, pad], axis=0)
                c0 = r0
            else:
                pb = jnp.concatenate([pad, pb], axis=0)
                c0 = r0 - Q_SUB
            vt = vt_buf[kv * HEAD_DIM:(kv + 1) * HEAD_DIM, c0:c0 + 2 * LANES]
            o = jnp.dot(vt, pb, preferred_element_type=jnp.float32)
            outs.append(o * (1.0 / l))
        for pr in range(GQA_GROUP // 2):
            both = jnp.concatenate([o[:, pr * LANES:(pr + 1) * LANES] for o in outs], axis=0)
            both_t = both.T
            for half in range(2):
                g = 2 * pr + half
                val = both_t[half * Q_SUB:(half + 1) * Q_SUB] * ag_buf[qrows, g * LANES:(g + 1) * LANES]
                mixa_buf[qrows, g * LANES:(g + 1) * LANES] = val.astype(jnp.bfloat16)

    y_pool = []

    def out_pool_item():
        y_pool.append(jnp.dot(mixp_buf[...], wo_buf[:D_POOL, :],
                              preferred_element_type=jnp.float32))

    attn_gate_item()
    dense_items = [pool_u_item, pool_gate_item, pool_mix_item, out_pool_item]
    group = n_sub // len(dense_items)
    for p, item in enumerate(dense_items):
        subs = range(p * group, (p + 1) * group)
        scores = [attn_scores(sb) for sb in subs]
        item()
        for sb, s in zip(subs, scores):
            attn_finish(sb, s)

    u_buf[0:POOL_HALO, :] = u_buf[tile:tile + POOL_HALO, :]
    kz_buf[:, 0:WINDOW, :] = kz_buf[:, tile:tile + WINDOW, :]
    vt_buf[:, 0:WINDOW] = vt_buf[:, tile:tile + WINDOW]

    y = y_pool[0] + jnp.dot(mixa_buf[...], wo_buf[D_POOL:, :], preferred_element_type=jnp.float32)
    ms2 = jnp.mean(y * y, axis=-1, keepdims=True)
    o_ref[...] = x_ref[...] + y * lax.rsqrt(ms2 + EPS) * gpost_ref[layer:layer + 1, :]


def _layer(layer, x, w_in, pool_w, pool_scale, sinks, w_out, g_pre, g_post):
    batch, seq, d = x.shape
    tile = SEQ_TILE
    assert seq % tile == 0 and tile % (2 * Q_SUB) == 0 and d == D_MODEL

    def of_layer(*shape):
        return pl.BlockSpec((None,) + shape, lambda b, t: (layer,) + (0,) * len(shape),
                            pipeline_mode=pl.Buffered(1))

    def whole(a):
        return pl.BlockSpec(a.shape, lambda b, t: (0,) * a.ndim)

    return pl.pallas_call(
        functools.partial(_layer_kernel, layer),
        out_shape=jax.ShapeDtypeStruct(x.shape, x.dtype),
        grid=(batch, seq // tile),
        in_specs=[
            pl.BlockSpec((None, tile, d), lambda b, t: (b, t, 0)),
            of_layer(D_MODEL, D_IN),
            of_layer(len(POOL_WINDOWS), POOL_GROUP, POOL_GROUP),
            whole(pool_scale),
            pl.BlockSpec(memory_space=pltpu.SMEM),
            of_layer(D_MODEL, D_MODEL),
            whole(g_pre),
            whole(g_post),
        ],
        out_specs=pl.BlockSpec((None, tile, d), lambda b, t: (b, t, 0)),
        scratch_shapes=[
            pltpu.VMEM((POOL_HALO + tile, D_POOL), jnp.float32),
            pltpu.VMEM((tile, D_ATTN), jnp.bfloat16),
            pltpu.VMEM((N_KV_HEADS, WINDOW + tile, LANES), jnp.bfloat16),
            pltpu.VMEM((LANES, WINDOW + tile), jnp.bfloat16),
            pltpu.VMEM((tile, D_ATTN), jnp.float32),
            pltpu.VMEM((tile, D_POOL), jnp.float32),
            pltpu.VMEM((tile, D_POOL), jnp.bfloat16),
            pltpu.VMEM((tile, D_ATTN), jnp.bfloat16),
            pltpu.VMEM((3, N_KV_HEADS * KEY_SPAN, GQA_GROUP * Q_SUB), jnp.float32),
            pltpu.VMEM((D_MODEL, D_IN), jnp.bfloat16),
            pltpu.VMEM((D_MODEL, D_MODEL), jnp.bfloat16),
        ],
        compiler_params=pltpu.CompilerParams(
            dimension_semantics=("arbitrary", "arbitrary"),
            vmem_limit_bytes=VMEM_LIMIT_BYTES),
        name="hybrid_layer",
    )(x, w_in, pool_w, pool_scale, sinks, w_out, g_pre, g_post)


@jax.jit
def kernel(x, w_in, pool_w, pool_scale, attn_sinks, w_out, norm_pre, norm_post):
    for layer in range(w_in.shape[0]):
        x = _layer(layer, x, w_in, pool_w, pool_scale, attn_sinks, w_out, norm_pre, norm_post)
    return x
```

```python
import functools

import jax
import jax.numpy as jnp
from jax import lax
from jax.experimental import pallas as pl
from jax.experimental.pallas import tpu as pltpu

D_MODEL = 1024
D_POOL = 512
POOL_WINDOWS = (2, 4, 8, 16)
POOL_GROUP = 128
HEAD_DIM = 64
D_ATTN = 512
N_HEADS = 8
N_KV_HEADS = 2
GQA_GROUP = N_HEADS // N_KV_HEADS
WINDOW = 128
D_IN = 2304
EPS = 1e-6
NEG_INF = -1e30

OFF_U, OFF_PG, OFF_Q, OFF_K, OFF_V, OFF_AG = 0, 512, 1024, 1536, 1664, 1792

LANES = 128
POOL_HALO = 16
SEQ_TILE = 1024
Q_SUB = 64
KEY_SPAN = Q_SUB + WINDOW
VMEM_LIMIT_BYTES = 56 * 1024 * 1024


def _silu(x):
    return x * (1.0 / (1.0 + jnp.exp(-x)))


def _layer_kernel(layer, x_ref, win_ref, poolw_ref, pscale_ref, sinks_ref, wout_ref,
                  gpre_ref, gpost_ref, o_ref,
                  u_buf, q_buf, kz_buf, vt_buf, ag_buf, pg_buf, mixp_buf, mixa_buf, bias_buf, wi_buf, wo_buf):
    tile = x_ref.shape[0]
    n_sub = tile // Q_SUB
    t = pl.program_id(1)

    @pl.when((pl.program_id(0) == 0) & (t == 0))
    def _first_step():
        r = lax.broadcasted_iota(jnp.int32, (KEY_SPAN, GQA_GROUP * Q_SUB), 0)
        col = lax.broadcasted_iota(jnp.int32, (KEY_SPAN, GQA_GROUP * Q_SUB), 1)
        dist = (col & (Q_SUB - 1)) + WINDOW - r
        in_win = (dist >= 0) & (dist < WINDOW)
        distf = dist.astype(jnp.float32)
        for kv in range(N_KV_HEADS):
            slope = jnp.zeros_like(distf)
            for g in range(GQA_GROUP):
                j = kv * GQA_GROUP + g
                slope = jnp.where(col >= g * Q_SUB, 2.0 ** (-8.0 * (j + 1) / N_HEADS), slope)
            b = -slope * distf
            for f in range(3):
                bias_buf[f, kv * KEY_SPAN:(kv + 1) * KEY_SPAN, :] = jnp.where(
                    in_win & (r >= f * Q_SUB), b, NEG_INF)

        a0 = lax.broadcasted_iota(jnp.int32, (D_ATTN, D_ATTN), 0)
        a1 = lax.broadcasted_iota(jnp.int32, (D_ATTN, D_ATTN), 1)

        def source(n):
            return ((n >> 6) & 1) * (GQA_GROUP * HEAD_DIM) + (n >> 7) * HEAD_DIM + (n & (HEAD_DIM - 1))

        perm = jnp.where(a0 == source(a1), 1.0, 0.0).astype(jnp.bfloat16)
        perm_t = jnp.where(a1 == source(a0), 1.0, 0.0).astype(jnp.bfloat16)
        for lo, hi in ((OFF_U, OFF_PG), (OFF_PG, OFF_Q), (OFF_Q, OFF_K), (OFF_K, OFF_AG), (OFF_AG, D_IN)):
            w = win_ref[:, lo:hi].astype(jnp.bfloat16)
            if lo in (OFF_Q, OFF_AG):
                w = jnp.dot(w, perm, preferred_element_type=jnp.float32).astype(jnp.bfloat16)
            wi_buf[:, lo:hi] = w
        wo_buf[:D_POOL, :] = wout_ref[:D_POOL, :].astype(jnp.bfloat16)
        wo_buf[D_POOL:, :] = jnp.dot(perm_t, wout_ref[D_POOL:, :].astype(jnp.bfloat16),
                                     preferred_element_type=jnp.float32).astype(jnp.bfloat16)

    @pl.when(t == 0)
    def _zero_halo():
        u_buf[0:POOL_HALO, :] = jnp.zeros((POOL_HALO, D_POOL), jnp.float32)
        kz_buf[:, 0:WINDOW, :] = jnp.zeros((N_KV_HEADS, WINDOW, LANES), jnp.bfloat16)
        vt_buf[:, 0:WINDOW] = jnp.zeros((LANES, WINDOW), jnp.bfloat16)

    x = x_ref[...]
    ms = jnp.mean(x * x, axis=-1, keepdims=True)
    h = (x * lax.rsqrt(ms + EPS) * gpre_ref[layer:layer + 1, :]).astype(jnp.bfloat16)

    def proj(lo, hi):
        return jnp.dot(h, wi_buf[:, lo:hi], preferred_element_type=jnp.float32)

    kvf = proj(OFF_K, OFF_AG)
    q_buf[...] = (proj(OFF_Q, OFF_K) * (HEAD_DIM ** -0.5)).astype(jnp.bfloat16)
    kf, vf = kvf[:, :LANES], kvf[:, LANES:]
    lane = lax.broadcasted_iota(jnp.int32, (1, LANES), 1)
    kz_buf[0, WINDOW:WINDOW + tile, :] = jnp.where(lane < HEAD_DIM, kf, 0.0).astype(jnp.bfloat16)
    kz_buf[1, WINDOW:WINDOW + tile, :] = jnp.where(lane >= HEAD_DIM, kf, 0.0).astype(jnp.bfloat16)
    vt_buf[:, WINDOW:WINDOW + tile] = vf.T.astype(jnp.bfloat16)

    def attn_gate_item():
        ag_buf[...] = _silu(proj(OFF_AG, D_IN))

    def pool_u_item():
        u_buf[POOL_HALO:POOL_HALO + tile, :] = proj(OFF_U, OFF_PG)

    pos1 = (lax.broadcasted_iota(jnp.int32, (tile, POOL_GROUP), 0) + (t * tile + 1)
            ).astype(jnp.float32)

    def pool_mix_item():
        for g, w in enumerate(POOL_WINDOWS):
            cols = slice(g * POOL_GROUP, (g + 1) * POOL_GROUP)
            ext = u_buf[:, cols]
            acc = ext
            shift = 1
            while shift < w:
                acc = acc + pltpu.roll(acc, shift, axis=0)
                shift *= 2
            cur = ext[POOL_HALO:]
            pooled = acc[POOL_HALO:] / jnp.minimum(pos1, float(w)) - cur
            mixed = jnp.dot(pooled.astype(jnp.bfloat16), poolw_ref[g].astype(jnp.bfloat16),
                            preferred_element_type=jnp.float32)
            scale = pscale_ref[layer:layer + 1, cols]
            mixp_buf[:, cols] = (mixed * scale * pg_buf[:, cols]).astype(jnp.bfloat16)

    def pool_gate_item():
        pg_buf[...] = _silu(proj(OFF_PG, OFF_Q))

    col = lax.broadcasted_iota(jnp.int32, (1, GQA_GROUP * Q_SUB), 1)
    sink_rows = []
    for kv in range(N_KV_HEADS):
        row = jnp.zeros((1, GQA_GROUP * Q_SUB), jnp.float32)
        for g in range(GQA_GROUP):
            row = jnp.where(col >= g * Q_SUB, sinks_ref[layer, kv * GQA_GROUP + g], row)
        sink_rows.append(row)
    first = (t == 0)
    nt_dims = (((1,), (1,)), ((), ()))
    pad = jnp.zeros((Q_SUB, GQA_GROUP * Q_SUB), jnp.bfloat16)

    def attn_scores(sb):
        r0 = sb * Q_SUB
        qrows = slice(r0, r0 + Q_SUB)
        krows = slice(r0, r0 + KEY_SPAN)
        variant = jnp.where(first, 2 - sb, 0) if sb < 2 else 0
        q4 = jnp.concatenate([q_buf[qrows, g * LANES:(g + 1) * LANES] for g in range(GQA_GROUP)],
                             axis=0)
        kk = jnp.concatenate([kz_buf[0, krows, :], kz_buf[1, krows, :]], axis=0)
        s = lax.dot_general(kk, q4, nt_dims, preferred_element_type=jnp.float32)
        return s + bias_buf[variant]

    def attn_finish(sb, s):
        r0 = sb * Q_SUB
        qrows = slice(r0, r0 + Q_SUB)
        outs = []
        for kv in range(N_KV_HEADS):
            sh = s[kv * KEY_SPAN:(kv + 1) * KEY_SPAN]
            m = jnp.maximum(jnp.max(sh, axis=0, keepdims=True), sink_rows[kv])
            p = jnp.exp(sh - m)
            l = jnp.sum(p, axis=0, keepdims=True) + jnp.exp(sink_rows[kv] - m)
            pb = p.astype(jnp.bfloat16)
            if sb % 2 == 0:
                pb = jnp.concatenate([pb, pad], axis=0)
                c0 = r0
            else:
                pb = jnp.concatenate([pad, pb], axis=0)
                c0 = r0 - Q_SUB
            vt = vt_buf[kv * HEAD_DIM:(kv + 1) * HEAD_DIM, c0:c0 + 2 * LANES]
            o = jnp.dot(vt, pb, preferred_element_type=jnp.float32)
            outs.append(o * (1.0 / l))
        for pr in range(GQA_GROUP // 2):
            both = jnp.concatenate([o[:, pr * LANES:(pr + 1) * LANES] for o in outs], axis=0)
            both_t = both.T
            for half in range(2):
                g = 2 * pr + half
                val = both_t[half * Q_SUB:(half + 1) * Q_SUB] * ag_buf[qrows, g * LANES:(g + 1) * LANES]
                mixa_buf[qrows, g * LANES:(g + 1) * LANES] = val.astype(jnp.bfloat16)

    y_pool = []

    def out_pool_item():
        y_pool.append(jnp.dot(mixp_buf[...], wo_buf[:D_POOL, :],
                              preferred_element_type=jnp.float32))

    attn_gate_item()
    dense_items = [pool_u_item, pool_gate_item, pool_mix_item, out_pool_item]
    group = n_sub // len(dense_items)
    for p, item in enumerate(dense_items):
        subs = range(p * group, (p + 1) * group)
        scores = [attn_scores(sb) for sb in subs]
        item()
        for sb, s in zip(subs, scores):
            attn_finish(sb, s)

    u_buf[0:POOL_HALO, :] = u_buf[tile:tile + POOL_HALO, :]
    kz_buf[:, 0:WINDOW, :] = kz_buf[:, tile:tile + WINDOW, :]
    vt_buf[:, 0:WINDOW] = vt_buf[:, tile:tile + WINDOW]

    y = y_pool[0] + jnp.dot(mixa_buf[...], wo_buf[D_POOL:, :], preferred_element_type=jnp.float32)
    ms2 = jnp.mean(y * y, axis=-1, keepdims=True)
    o_ref[...] = x_ref[...] + y * lax.rsqrt(ms2 + EPS) * gpost_ref[layer:layer + 1, :]


def _layer(layer, x, w_in, pool_w, pool_scale, sinks, w_out, g_pre, g_post):
    batch, seq, d = x.shape
    tile = SEQ_TILE
    assert seq % tile == 0 and tile % (2 * Q_SUB) == 0 and d == D_MODEL

    def of_layer(*shape):
        return pl.BlockSpec((None,) + shape, lambda b, t: (layer,) + (0,) * len(shape),
                            pipeline_mode=pl.Buffered(1))

    def whole(a):
        return pl.BlockSpec(a.shape, lambda b, t: (0,) * a.ndim)

    return pl.pallas_call(
        functools.partial(_layer_kernel, layer),
        out_shape=jax.ShapeDtypeStruct(x.shape, x.dtype),
        grid=(batch, seq // tile),
        in_specs=[
            pl.BlockSpec((None, tile, d), lambda b, t: (b, t, 0)),
            of_layer(D_MODEL, D_IN),
            of_layer(len(POOL_WINDOWS), POOL_GROUP, POOL_GROUP),
            whole(pool_scale),
            pl.BlockSpec(memory_space=pltpu.SMEM),
            of_layer(D_MODEL, D_MODEL),
            whole(g_pre),
            whole(g_post),
        ],
        out_specs=pl.BlockSpec((None, tile, d), lambda b, t: (b, t, 0)),
        scratch_shapes=[
            pltpu.VMEM((POOL_HALO + tile, D_POOL), jnp.float32),
            pltpu.VMEM((tile, D_ATTN), jnp.bfloat16),
            pltpu.VMEM((N_KV_HEADS, WINDOW + tile, LANES), jnp.bfloat16),
            pltpu.VMEM((LANES, WINDOW + tile), jnp.bfloat16),
            pltpu.VMEM((tile, D_ATTN), jnp.float32),
            pltpu.VMEM((tile, D_POOL), jnp.float32),
            pltpu.VMEM((tile, D_POOL), jnp.bfloat16),
            pltpu.VMEM((tile, D_ATTN), jnp.bfloat16),
            pltpu.VMEM((3, N_KV_HEADS * KEY_SPAN, GQA_GROUP * Q_SUB), jnp.float32),
            pltpu.VMEM((D_MODEL, D_IN), jnp.bfloat16),
            pltpu.VMEM((D_MODEL, D_MODEL), jnp.bfloat16),
        ],
        compiler_params=pltpu.CompilerParams(
            dimension_semantics=("arbitrary", "arbitrary"),
            vmem_limit_bytes=VMEM_LIMIT_BYTES),
        name="hybrid_layer",
    )(x, w_in, pool_w, pool_scale, sinks, w_out, g_pre, g_post)


@jax.jit
def kernel(x, w_in, pool_w, pool_scale, attn_sinks, w_out, norm_pre, norm_post):
    for layer in range(w_in.shape[0]):
        x = _layer(layer, x, w_in, pool_w, pool_scale, attn_sinks, w_out, norm_pre, norm_post)
    return x
```

```python
import functools

import jax
import jax.numpy as jnp
from jax import lax
from jax.experimental import pallas as pl
from jax.experimental.pallas import tpu as pltpu

D_MODEL = 1024
D_POOL = 512
POOL_WINDOWS = (2, 4, 8, 16)
POOL_GROUP = 128
HEAD_DIM = 64
D_ATTN = 512
N_HEADS = 8
N_KV_HEADS = 2
GQA_GROUP = N_HEADS // N_KV_HEADS
WINDOW = 128
D_IN = 2304
EPS = 1e-6
NEG_INF = -1e30

OFF_U, OFF_PG, OFF_Q, OFF_K, OFF_V, OFF_AG = 0, 512, 1024, 1536, 1664, 1792

LANES = 128
POOL_HALO = 16
SEQ_TILE = 1024
Q_SUB = 64
EDGE_ROWS = 256
KEY_SPAN = Q_SUB + WINDOW
VMEM_LIMIT_BYTES = 56 * 1024 * 1024


def _silu(x):
    return x * (1.0 / (1.0 + jnp.exp(-x)))


def _layer_kernel(layer, x_ref, win_ref, poolw_ref, pscale_ref, sinks_ref, wout_ref,
                  gpre_ref, gpost_ref, o_ref,
                  u_buf, q_buf, kz_buf, vt_buf, ag_buf, pg_buf, pooled_buf, mixp_buf, mixa_buf, bias_buf, wi_buf,
                  wo_buf):
    tile = x_ref.shape[0]
    n_sub = tile // Q_SUB
    t = pl.program_id(1)

    @pl.when((pl.program_id(0) == 0) & (t == 0))
    def _first_step():
        r = lax.broadcasted_iota(jnp.int32, (KEY_SPAN, GQA_GROUP * Q_SUB), 0)
        col = lax.broadcasted_iota(jnp.int32, (KEY_SPAN, GQA_GROUP * Q_SUB), 1)
        dist = (col & (Q_SUB - 1)) + WINDOW - r
        in_win = (dist >= 0) & (dist < WINDOW)
        distf = dist.astype(jnp.float32)
        for kv in range(N_KV_HEADS):
            slope = jnp.zeros_like(distf)
            for g in range(GQA_GROUP):
                j = kv * GQA_GROUP + g
                slope = jnp.where(col >= g * Q_SUB, 2.0 ** (-8.0 * (j + 1) / N_HEADS), slope)
            b = -slope * distf
            for f in range(3):
                bias_buf[f, kv * KEY_SPAN:(kv + 1) * KEY_SPAN, :] = jnp.where(
                    in_win & (r >= f * Q_SUB), b, NEG_INF)

        a0 = lax.broadcasted_iota(jnp.int32, (D_ATTN, D_ATTN), 0)
        a1 = lax.broadcasted_iota(jnp.int32, (D_ATTN, D_ATTN), 1)

        def source(n):
            return ((n >> 6) & 1) * (GQA_GROUP * HEAD_DIM) + (n >> 7) * HEAD_DIM + (n & (HEAD_DIM - 1))

        perm = jnp.where(a0 == source(a1), 1.0, 0.0).astype(jnp.bfloat16)
        perm_t = jnp.where(a1 == source(a0), 1.0, 0.0).astype(jnp.bfloat16)
        for lo, hi in ((OFF_U, OFF_PG), (OFF_PG, OFF_Q), (OFF_Q, OFF_K), (OFF_K, OFF_AG), (OFF_AG, D_IN)):
            w = win_ref[:, lo:hi].astype(jnp.bfloat16)
            if lo in (OFF_Q, OFF_AG):
                w = jnp.dot(w, perm, preferred_element_type=jnp.float32).astype(jnp.bfloat16)
            wi_buf[:, lo:hi] = w
        wo_buf[:D_POOL, :] = wout_ref[:D_POOL, :].astype(jnp.bfloat16)
        wo_buf[D_POOL:, :] = jnp.dot(perm_t, wout_ref[D_POOL:, :].astype(jnp.bfloat16),
                                     preferred_element_type=jnp.float32).astype(jnp.bfloat16)

    @pl.when(t == 0)
    def _zero_halo():
        u_buf[0:POOL_HALO, :] = jnp.zeros((POOL_HALO, D_POOL), jnp.float32)
        kz_buf[:, 0:WINDOW, :] = jnp.zeros((N_KV_HEADS, WINDOW, LANES), jnp.bfloat16)
        vt_buf[:, 0:WINDOW] = jnp.zeros((LANES, WINDOW), jnp.bfloat16)

    h_chunks, kv_chunks = [], []
    for r0 in range(0, tile, EDGE_ROWS):
        xc = x_ref[r0:r0 + EDGE_ROWS, :]
        ms = jnp.mean(xc * xc, axis=-1, keepdims=True)
        hc = (xc * lax.rsqrt(ms + EPS) * gpre_ref[layer:layer + 1, :]).astype(jnp.bfloat16)
        h_chunks.append(hc)
        kv_chunks.append(jnp.dot(hc, wi_buf[:, OFF_K:OFF_AG], preferred_element_type=jnp.float32))
    h = jnp.concatenate(h_chunks, axis=0)
    kvf = jnp.concatenate(kv_chunks, axis=0)

    def proj(lo, hi):
        return jnp.dot(h, wi_buf[:, lo:hi], preferred_element_type=jnp.float32)

    q_buf[...] = (proj(OFF_Q, OFF_K) * (HEAD_DIM ** -0.5)).astype(jnp.bfloat16)
    kf, vf = kvf[:, :LANES], kvf[:, LANES:]
    lane = lax.broadcasted_iota(jnp.int32, (1, LANES), 1)
    kz_buf[0, WINDOW:WINDOW + tile, :] = jnp.where(lane < HEAD_DIM, kf, 0.0).astype(jnp.bfloat16)
    kz_buf[1, WINDOW:WINDOW + tile, :] = jnp.where(lane >= HEAD_DIM, kf, 0.0).astype(jnp.bfloat16)
    vt_buf[:, WINDOW:WINDOW + tile] = vf.T.astype(jnp.bfloat16)

    def attn_gate_item():
        ag_buf[...] = _silu(proj(OFF_AG, D_IN))

    def pool_u_item():
        u_buf[POOL_HALO:POOL_HALO + tile, :] = proj(OFF_U, OFF_PG)

    pos1 = (lax.broadcasted_iota(jnp.int32, (tile, POOL_GROUP), 0) + (t * tile + 1)
            ).astype(jnp.float32)

    def pool_window_item():
        for g, w in enumerate(POOL_WINDOWS):
            cols = slice(g * POOL_GROUP, (g + 1) * POOL_GROUP)
            ext = u_buf[:, cols]
            acc = ext
            shift = 1
            while shift < w:
                acc = acc + pltpu.roll(acc, shift, axis=0)
                shift *= 2
            cur = ext[POOL_HALO:]
            pooled = acc[POOL_HALO:] / jnp.minimum(pos1, float(w)) - cur
            pooled_buf[:, cols] = pooled.astype(jnp.bfloat16)

    def pool_mix_item():
        for g in range(len(POOL_WINDOWS)):
            cols = slice(g * POOL_GROUP, (g + 1) * POOL_GROUP)
            mixed = jnp.dot(pooled_buf[:, cols], poolw_ref[g].astype(jnp.bfloat16),
                            preferred_element_type=jnp.float32)
            scale = pscale_ref[layer:layer + 1, cols]
            mixp_buf[:, cols] = (mixed * scale * pg_buf[:, cols]).astype(jnp.bfloat16)

    def pool_gate_item():
        pg_buf[...] = _silu(proj(OFF_PG, OFF_Q))

    col = lax.broadcasted_iota(jnp.int32, (1, GQA_GROUP * Q_SUB), 1)
    sink_rows = []
    for kv in range(N_KV_HEADS):
        row = jnp.zeros((1, GQA_GROUP * Q_SUB), jnp.float32)
        for g in range(GQA_GROUP):
            row = jnp.where(col >= g * Q_SUB, sinks_ref[layer, kv * GQA_GROUP + g], row)
        sink_rows.append(row)
    first = (t == 0)
    nt_dims = (((1,), (1,)), ((), ()))
    pad = jnp.zeros((Q_SUB, GQA_GROUP * Q_SUB), jnp.bfloat16)

    def attn_scores(sb):
        r0 = sb * Q_SUB
        qrows = slice(r0, r0 + Q_SUB)
        krows = slice(r0, r0 + KEY_SPAN)
        variant = jnp.where(first, 2 - sb, 0) if sb < 2 else 0
        q4 = jnp.concatenate([q_buf[qrows, g * LANES:(g + 1) * LANES] for g in range(GQA_GROUP)],
                             axis=0)
        kk = jnp.concatenate([kz_buf[0, krows, :], kz_buf[1, krows, :]], axis=0)
        s = lax.dot_general(kk, q4, nt_dims, preferred_element_type=jnp.float32)
        return s + bias_buf[variant]

    def attn_finish(sb, s):
        r0 = sb * Q_SUB
        qrows = slice(r0, r0 + Q_SUB)
        outs = []
        for kv in range(N_KV_HEADS):
            sh = s[kv * KEY_SPAN:(kv + 1) * KEY_SPAN]
            m = jnp.maximum(jnp.max(sh, axis=0, keepdims=True), sink_rows[kv])
            p = jnp.exp(sh - m)
            l = jnp.sum(p, axis=0, keepdims=True) + jnp.exp(sink_rows[kv] - m)
            pb = p.astype(jnp.bfloat16)
            if sb % 2 == 0:
                pb = jnp.concatenate([pb, pad], axis=0)
                c0 = r0
            else:
                pb = jnp.concatenate([pad, pb], axis=0)
                c0 = r0 - Q_SUB
            vt = vt_buf[kv * HEAD_DIM:(kv + 1) * HEAD_DIM, c0:c0 + 2 * LANES]
            o = jnp.dot(vt, pb, preferred_element_type=jnp.float32)
            outs.append(o * (1.0 / l))
        for pr in range(GQA_GROUP // 2):
            both = jnp.concatenate([o[:, pr * LANES:(pr + 1) * LANES] for o in outs], axis=0)
            both_t = both.T
            for half in range(2):
                g = 2 * pr + half
                val = both_t[half * Q_SUB:(half + 1) * Q_SUB] * ag_buf[qrows, g * LANES:(g + 1) * LANES]
                mixa_buf[qrows, g * LANES:(g + 1) * LANES] = val.astype(jnp.bfloat16)

    y_pool = []

    def out_pool_item():
        y_pool.append(jnp.dot(mixp_buf[...], wo_buf[:D_POOL, :],
                              preferred_element_type=jnp.float32))

    dense_items = [[attn_gate_item], [pool_u_item, pool_window_item], [pool_gate_item],
                   [pool_mix_item, out_pool_item]]
    group = n_sub // len(dense_items)
    for p, items in enumerate(dense_items):
        subs = range(p * group, (p + 1) * group)
        scores = [attn_scores(sb) for sb in subs]
        for item in items:
            item()
        for sb, s in zip(subs, scores):
            attn_finish(sb, s)

    u_buf[0:POOL_HALO, :] = u_buf[tile:tile + POOL_HALO, :]
    kz_buf[:, 0:WINDOW, :] = kz_buf[:, tile:tile + WINDOW, :]
    vt_buf[:, 0:WINDOW] = vt_buf[:, tile:tile + WINDOW]

    for r0 in range(0, tile, EDGE_ROWS):
        rows = slice(r0, r0 + EDGE_ROWS)
        y = y_pool[0][rows] + jnp.dot(mixa_buf[rows, :], wo_buf[D_POOL:, :],
                                      preferred_element_type=jnp.float32)
        ms2 = jnp.mean(y * y, axis=-1, keepdims=True)
        o_ref[rows, :] = x_ref[rows, :] + y * lax.rsqrt(ms2 + EPS) * gpost_ref[layer:layer + 1, :]


def _layer(layer, x, w_in, pool_w, pool_scale, sinks, w_out, g_pre, g_post):
    batch, seq, d = x.shape
    tile = SEQ_TILE
    assert seq % tile == 0 and tile % (2 * Q_SUB) == 0 and d == D_MODEL

    def of_layer(*shape):
        return pl.BlockSpec((None,) + shape, lambda b, t: (layer,) + (0,) * len(shape),
                            pipeline_mode=pl.Buffered(1))

    def whole(a):
        return pl.BlockSpec(a.shape, lambda b, t: (0,) * a.ndim)

    return pl.pallas_call(
        functools.partial(_layer_kernel, layer),
        out_shape=jax.ShapeDtypeStruct(x.shape, x.dtype),
        grid=(batch, seq // tile),
        in_specs=[
            pl.BlockSpec((None, tile, d), lambda b, t: (b, t, 0)),
            of_layer(D_MODEL, D_IN),
            of_layer(len(POOL_WINDOWS), POOL_GROUP, POOL_GROUP),
            whole(pool_scale),
            pl.BlockSpec(memory_space=pltpu.SMEM),
            of_layer(D_MODEL, D_MODEL),
            whole(g_pre),
            whole(g_post),
        ],
        out_specs=pl.BlockSpec((None, tile, d), lambda b, t: (b, t, 0)),
        scratch_shapes=[
            pltpu.VMEM((POOL_HALO + tile, D_POOL), jnp.float32),
            pltpu.VMEM((tile, D_ATTN), jnp.bfloat16),
            pltpu.VMEM((N_KV_HEADS, WINDOW + tile, LANES), jnp.bfloat16),
            pltpu.VMEM((LANES, WINDOW + tile), jnp.bfloat16),
            pltpu.VMEM((tile, D_ATTN), jnp.float32),
            pltpu.VMEM((tile, D_POOL), jnp.float32),
            pltpu.VMEM((tile, D_POOL), jnp.bfloat16),
            pltpu.VMEM((tile, D_POOL), jnp.bfloat16),
            pltpu.VMEM((tile, D_ATTN), jnp.bfloat16),
            pltpu.VMEM((3, N_KV_HEADS * KEY_SPAN, GQA_GROUP * Q_SUB), jnp.float32),
            pltpu.VMEM((D_MODEL, D_IN), jnp.bfloat16),
            pltpu.VMEM((D_MODEL, D_MODEL), jnp.bfloat16),
        ],
        compiler_params=pltpu.CompilerParams(
            dimension_semantics=("arbitrary", "arbitrary"),
            vmem_limit_bytes=VMEM_LIMIT_BYTES),
        name="hybrid_layer",
    )(x, w_in, pool_w, pool_scale, sinks, w_out, g_pre, g_post)


@jax.jit
def kernel(x, w_in, pool_w, pool_scale, attn_sinks, w_out, norm_pre, norm_post):
    for layer in range(w_in.shape[0]):
        x = _layer(layer, x, w_in, pool_w, pool_scale, attn_sinks, w_out, norm_pre, norm_post)
    return x
```

```python
import functools

import jax
import jax.numpy as jnp
from jax import lax
from jax.experimental import pallas as pl
from jax.experimental.pallas import tpu as pltpu

D_MODEL = 1024
D_POOL = 512
POOL_WINDOWS = (2, 4, 8, 16)
POOL_GROUP = 128
HEAD_DIM = 64
D_ATTN = 512
N_HEADS = 8
N_KV_HEADS = 2
GQA_GROUP = N_HEADS // N_KV_HEADS
WINDOW = 128
D_IN = 2304
EPS = 1e-6
NEG_INF = -1e30

OFF_U, OFF_PG, OFF_Q, OFF_K, OFF_V, OFF_AG = 0, 512, 1024, 1536, 1664, 1792

LANES = 128
POOL_HALO = 16
SEQ_TILE = 1024
Q_SUB = 64
EDGE_ROWS = 1024
KEY_SPAN = Q_SUB + WINDOW
VMEM_LIMIT_BYTES = 56 * 1024 * 1024


def _silu(x):
    return x * (1.0 / (1.0 + jnp.exp(-x)))


def _layer_kernel(layer, x_ref, win_ref, poolw_ref, pscale_ref, sinks_ref, wout_ref,
                  gpre_ref, gpost_ref, o_ref,
                  u_buf, q_buf, kz_buf, vt_buf, ag_buf, pg_buf, pooled_buf, mixp_buf, mixa_buf, bias_buf, wi_buf,
                  wo_buf):
    tile = x_ref.shape[0]
    n_sub = tile // Q_SUB
    t = pl.program_id(1)

    @pl.when((pl.program_id(0) == 0) & (t == 0))
    def _first_step():
        r = lax.broadcasted_iota(jnp.int32, (KEY_SPAN, GQA_GROUP * Q_SUB), 0)
        col = lax.broadcasted_iota(jnp.int32, (KEY_SPAN, GQA_GROUP * Q_SUB), 1)
        dist = (col & (Q_SUB - 1)) + WINDOW - r
        in_win = (dist >= 0) & (dist < WINDOW)
        distf = dist.astype(jnp.float32)
        for kv in range(N_KV_HEADS):
            slope = jnp.zeros_like(distf)
            for g in range(GQA_GROUP):
                j = kv * GQA_GROUP + g
                slope = jnp.where(col >= g * Q_SUB, 2.0 ** (-8.0 * (j + 1) / N_HEADS), slope)
            b = -slope * distf
            for f in range(3):
                bias_buf[f, kv * KEY_SPAN:(kv + 1) * KEY_SPAN, :] = jnp.where(
                    in_win & (r >= f * Q_SUB), b, NEG_INF)

        a0 = lax.broadcasted_iota(jnp.int32, (D_ATTN, D_ATTN), 0)
        a1 = lax.broadcasted_iota(jnp.int32, (D_ATTN, D_ATTN), 1)

        def source(n):
            return ((n >> 6) & 1) * (GQA_GROUP * HEAD_DIM) + (n >> 7) * HEAD_DIM + (n & (HEAD_DIM - 1))

        perm = jnp.where(a0 == source(a1), 1.0, 0.0).astype(jnp.bfloat16)
        perm_t = jnp.where(a1 == source(a0), 1.0, 0.0).astype(jnp.bfloat16)
        for lo, hi in ((OFF_U, OFF_PG), (OFF_PG, OFF_Q), (OFF_Q, OFF_K), (OFF_K, OFF_AG), (OFF_AG, D_IN)):
            w = win_ref[:, lo:hi].astype(jnp.bfloat16)
            if lo in (OFF_Q, OFF_AG):
                w = jnp.dot(w, perm, preferred_element_type=jnp.float32).astype(jnp.bfloat16)
            wi_buf[:, lo:hi] = w
        wo_buf[:D_POOL, :] = wout_ref[:D_POOL, :].astype(jnp.bfloat16)
        wo_buf[D_POOL:, :] = jnp.dot(perm_t, wout_ref[D_POOL:, :].astype(jnp.bfloat16),
                                     preferred_element_type=jnp.float32).astype(jnp.bfloat16)

    @pl.when(t == 0)
    def _zero_halo():
        u_buf[0:POOL_HALO, :] = jnp.zeros((POOL_HALO, D_POOL), jnp.float32)
        kz_buf[:, 0:WINDOW, :] = jnp.zeros((N_KV_HEADS, WINDOW, LANES), jnp.bfloat16)
        vt_buf[:, 0:WINDOW] = jnp.zeros((LANES, WINDOW), jnp.bfloat16)

    h_chunks, kv_chunks = [], []
    for r0 in range(0, tile, EDGE_ROWS):
        xc = x_ref[r0:r0 + EDGE_ROWS, :]
        ms = jnp.mean(xc * xc, axis=-1, keepdims=True)
        hc = (xc * lax.rsqrt(ms + EPS) * gpre_ref[layer:layer + 1, :]).astype(jnp.bfloat16)
        h_chunks.append(hc)
        kv_chunks.append(jnp.dot(hc, wi_buf[:, OFF_K:OFF_AG], preferred_element_type=jnp.float32))
    h = jnp.concatenate(h_chunks, axis=0)
    kvf = jnp.concatenate(kv_chunks, axis=0)

    def proj(lo, hi):
        return jnp.dot(h, wi_buf[:, lo:hi], preferred_element_type=jnp.float32)

    q_buf[...] = (proj(OFF_Q, OFF_K) * (HEAD_DIM ** -0.5)).astype(jnp.bfloat16)
    kf, vf = kvf[:, :LANES], kvf[:, LANES:]
    lane = lax.broadcasted_iota(jnp.int32, (1, LANES), 1)
    kz_buf[0, WINDOW:WINDOW + tile, :] = jnp.where(lane < HEAD_DIM, kf, 0.0).astype(jnp.bfloat16)
    kz_buf[1, WINDOW:WINDOW + tile, :] = jnp.where(lane >= HEAD_DIM, kf, 0.0).astype(jnp.bfloat16)
    vt_buf[:, WINDOW:WINDOW + tile] = vf.T.astype(jnp.bfloat16)

    def attn_gate_item():
        ag_buf[...] = _silu(proj(OFF_AG, D_IN))

    def pool_u_item():
        u_buf[POOL_HALO:POOL_HALO + tile, :] = proj(OFF_U, OFF_PG)

    pos1 = (lax.broadcasted_iota(jnp.int32, (tile, POOL_GROUP), 0) + (t * tile + 1)
            ).astype(jnp.float32)

    def pool_window_item():
        for g, w in enumerate(POOL_WINDOWS):
            cols = slice(g * POOL_GROUP, (g + 1) * POOL_GROUP)
            ext = u_buf[:, cols]
            acc = ext
            shift = 1
            while shift < w:
                acc = acc + pltpu.roll(acc, shift, axis=0)
                shift *= 2
            cur = ext[POOL_HALO:]
            pooled = acc[POOL_HALO:] / jnp.minimum(pos1, float(w)) - cur
            pooled_buf[:, cols] = pooled.astype(jnp.bfloat16)

    def pool_mix_item():
        for g in range(len(POOL_WINDOWS)):
            cols = slice(g * POOL_GROUP, (g + 1) * POOL_GROUP)
            mixed = jnp.dot(pooled_buf[:, cols], poolw_ref[g].astype(jnp.bfloat16),
                            preferred_element_type=jnp.float32)
            scale = pscale_ref[layer:layer + 1, cols]
            mixp_buf[:, cols] = (mixed * scale * pg_buf[:, cols]).astype(jnp.bfloat16)

    def pool_gate_item():
        pg_buf[...] = _silu(proj(OFF_PG, OFF_Q))

    col = lax.broadcasted_iota(jnp.int32, (1, GQA_GROUP * Q_SUB), 1)
    sink_rows = []
    for kv in range(N_KV_HEADS):
        row = jnp.zeros((1, GQA_GROUP * Q_SUB), jnp.float32)
        for g in range(GQA_GROUP):
            row = jnp.where(col >= g * Q_SUB, sinks_ref[layer, kv * GQA_GROUP + g], row)
        sink_rows.append(row)
    first = (t == 0)
    nt_dims = (((1,), (1,)), ((), ()))
    pad = jnp.zeros((Q_SUB, GQA_GROUP * Q_SUB), jnp.bfloat16)

    def attn_scores(sb):
        r0 = sb * Q_SUB
        qrows = slice(r0, r0 + Q_SUB)
        krows = slice(r0, r0 + KEY_SPAN)
        variant = jnp.where(first, 2 - sb, 0) if sb < 2 else 0
        q4 = jnp.concatenate([q_buf[qrows, g * LANES:(g + 1) * LANES] for g in range(GQA_GROUP)],
                             axis=0)
        kk = jnp.concatenate([kz_buf[0, krows, :], kz_buf[1, krows, :]], axis=0)
        s = lax.dot_general(kk, q4, nt_dims, preferred_element_type=jnp.float32)
        return s + bias_buf[variant]

    def attn_finish(sb, s):
        r0 = sb * Q_SUB
        qrows = slice(r0, r0 + Q_SUB)
        outs = []
        for kv in range(N_KV_HEADS):
            sh = s[kv * KEY_SPAN:(kv + 1) * KEY_SPAN]
            m = jnp.maximum(jnp.max(sh, axis=0, keepdims=True), sink_rows[kv])
            p = jnp.exp(sh - m)
            l = jnp.sum(p, axis=0, keepdims=True) + jnp.exp(sink_rows[kv] - m)
            pb = p.astype(jnp.bfloat16)
            if sb % 2 == 0:
                pb = jnp.concatenate([pb, pad], axis=0)
                c0 = r0
            else:
                pb = jnp.concatenate([pad, pb], axis=0)
                c0 = r0 - Q_SUB
            vt = vt_buf[kv * HEAD_DIM:(kv + 1) * HEAD_DIM, c0:c0 + 2 * LANES]
            o = jnp.dot(vt, pb, preferred_element_type=jnp.float32)
            outs.append(o * (1.0 / l))
        for pr in range(GQA_GROUP // 2):
            both = jnp.concatenate([o[:, pr * LANES:(pr + 1) * LANES] for o in outs], axis=0)
            both_t = both.T
            for half in range(2):
                g = 2 * pr + half
                val = both_t[half * Q_SUB:(half + 1) * Q_SUB] * ag_buf[qrows, g * LANES:(g + 1) * LANES]
                mixa_buf[qrows, g * LANES:(g + 1) * LANES] = val.astype(jnp.bfloat16)

    y_pool = []

    def out_pool_item():
        y_pool.append(jnp.dot(mixp_buf[...], wo_buf[:D_POOL, :],
                              preferred_element_type=jnp.float32))

    dense_items = [[attn_gate_item], [pool_u_item, pool_window_item], [pool_gate_item],
                   [pool_mix_item, out_pool_item]]
    group = n_sub // len(dense_items)
    for p, items in enumerate(dense_items):
        subs = range(p * group, (p + 1) * group)
        scores = [attn_scores(sb) for sb in subs]
        for item in items:
            item()
        for sb, s in zip(subs, scores):
            attn_finish(sb, s)

    u_buf[0:POOL_HALO, :] = u_buf[tile:tile + POOL_HALO, :]
    kz_buf[:, 0:WINDOW, :] = kz_buf[:, tile:tile + WINDOW, :]
    vt_buf[:, 0:WINDOW] = vt_buf[:, tile:tile + WINDOW]

    for r0 in range(0, tile, EDGE_ROWS):
        rows = slice(r0, r0 + EDGE_ROWS)
        y = y_pool[0][rows] + jnp.dot(mixa_buf[rows, :], wo_buf[D_POOL:, :],
                                      preferred_element_type=jnp.float32)
        ms2 = jnp.mean(y * y, axis=-1, keepdims=True)
        o_ref[rows, :] = x_ref[rows, :] + y * lax.rsqrt(ms2 + EPS) * gpost_ref[layer:layer + 1, :]


def _layer(layer, x, w_in, pool_w, pool_scale, sinks, w_out, g_pre, g_post):
    batch, seq, d = x.shape
    tile = SEQ_TILE
    assert seq % tile == 0 and tile % (2 * Q_SUB) == 0 and d == D_MODEL

    def of_layer(*shape):
        return pl.BlockSpec((None,) + shape, lambda b, t: (layer,) + (0,) * len(shape),
                            pipeline_mode=pl.Buffered(1))

    def whole(a):
        return pl.BlockSpec(a.shape, lambda b, t: (0,) * a.ndim)

    return pl.pallas_call(
        functools.partial(_layer_kernel, layer),
        out_shape=jax.ShapeDtypeStruct(x.shape, x.dtype),
        grid=(batch, seq // tile),
        in_specs=[
            pl.BlockSpec((None, tile, d), lambda b, t: (b, t, 0)),
            of_layer(D_MODEL, D_IN),
            of_layer(len(POOL_WINDOWS), POOL_GROUP, POOL_GROUP),
            whole(pool_scale),
            pl.BlockSpec(memory_space=pltpu.SMEM),
            of_layer(D_MODEL, D_MODEL),
            whole(g_pre),
            whole(g_post),
        ],
        out_specs=pl.BlockSpec((None, tile, d), lambda b, t: (b, t, 0)),
        scratch_shapes=[
            pltpu.VMEM((POOL_HALO + tile, D_POOL), jnp.float32),
            pltpu.VMEM((tile, D_ATTN), jnp.bfloat16),
            pltpu.VMEM((N_KV_HEADS, WINDOW + tile, LANES), jnp.bfloat16),
            pltpu.VMEM((LANES, WINDOW + tile), jnp.bfloat16),
            pltpu.VMEM((tile, D_ATTN), jnp.float32),
            pltpu.VMEM((tile, D_POOL), jnp.float32),
            pltpu.VMEM((tile, D_POOL), jnp.bfloat16),
            pltpu.VMEM((tile, D_POOL), jnp.bfloat16),
            pltpu.VMEM((tile, D_ATTN), jnp.bfloat16),
            pltpu.VMEM((3, N_KV_HEADS * KEY_SPAN, GQA_GROUP * Q_SUB), jnp.float32),
            pltpu.VMEM((D_MODEL, D_IN), jnp.bfloat16),
            pltpu.VMEM((D_MODEL, D_MODEL), jnp.bfloat16),
        ],
        compiler_params=pltpu.CompilerParams(
            dimension_semantics=("arbitrary", "arbitrary"),
            vmem_limit_bytes=VMEM_LIMIT_BYTES),
        name="hybrid_layer",
    )(x, w_in, pool_w, pool_scale, sinks, w_out, g_pre, g_post)


@jax.jit
def kernel(x, w_in, pool_w, pool_scale, attn_sinks, w_out, norm_pre, norm_post):
    for layer in range(w_in.shape[0]):
        x = _layer(layer, x, w_in, pool_w, pool_scale, attn_sinks, w_out, norm_pre, norm_post)
    return x
```

```python
import functools

import jax
import jax.numpy as jnp
from jax import lax
from jax.experimental import pallas as pl
from jax.experimental.pallas import tpu as pltpu

D_MODEL = 1024
D_POOL = 512
POOL_WINDOWS = (2, 4, 8, 16)
POOL_GROUP = 128
HEAD_DIM = 64
D_ATTN = 512
N_HEADS = 8
N_KV_HEADS = 2
GQA_GROUP = N_HEADS // N_KV_HEADS
WINDOW = 128
D_IN = 2304
EPS = 1e-6
NEG_INF = -1e30

OFF_U, OFF_PG, OFF_Q, OFF_K, OFF_V, OFF_AG = 0, 512, 1024, 1536, 1664, 1792

LANES = 128
POOL_HALO = 16
SEQ_TILE = 1024
Q_SUB = 64
EDGE_ROWS = 256
KEY_SPAN = Q_SUB + WINDOW
VMEM_LIMIT_BYTES = 56 * 1024 * 1024


def _silu(x):
    return x * (1.0 / (1.0 + jnp.exp(-x)))


def _layer_kernel(layer, x_ref, win_ref, poolw_ref, pscale_ref, sinks_ref, wout_ref,
                  gpre_ref, gpost_ref, o_ref,
                  u_buf, q_buf, kz_buf, vt_buf, ag_buf, pg_buf, pooled_buf, mixp_buf, mixa_buf, bias_buf, wi_buf,
                  wo_buf):
    tile = x_ref.shape[0]
    n_sub = tile // Q_SUB
    t = pl.program_id(1)

    @pl.when((pl.program_id(0) == 0) & (t == 0))
    def _first_step():
        r = lax.broadcasted_iota(jnp.int32, (KEY_SPAN, GQA_GROUP * Q_SUB), 0)
        col = lax.broadcasted_iota(jnp.int32, (KEY_SPAN, GQA_GROUP * Q_SUB), 1)
        dist = (col & (Q_SUB - 1)) + WINDOW - r
        in_win = (dist >= 0) & (dist < WINDOW)
        distf = dist.astype(jnp.float32)
        for kv in range(N_KV_HEADS):
            slope = jnp.zeros_like(distf)
            for g in range(GQA_GROUP):
                j = kv * GQA_GROUP + g
                slope = jnp.where(col >= g * Q_SUB, 2.0 ** (-8.0 * (j + 1) / N_HEADS), slope)
            b = -slope * distf
            for f in range(3):
                bias_buf[f, kv * KEY_SPAN:(kv + 1) * KEY_SPAN, :] = jnp.where(
                    in_win & (r >= f * Q_SUB), b, NEG_INF)

        a0 = lax.broadcasted_iota(jnp.int32, (D_ATTN, D_ATTN), 0)
        a1 = lax.broadcasted_iota(jnp.int32, (D_ATTN, D_ATTN), 1)

        def source(n):
            return ((n >> 6) & 1) * (GQA_GROUP * HEAD_DIM) + (n >> 7) * HEAD_DIM + (n & (HEAD_DIM - 1))

        perm = jnp.where(a0 == source(a1), 1.0, 0.0).astype(jnp.bfloat16)
        perm_t = jnp.where(a1 == source(a0), 1.0, 0.0).astype(jnp.bfloat16)
        for lo, hi in ((OFF_U, OFF_PG), (OFF_PG, OFF_Q), (OFF_Q, OFF_K), (OFF_K, OFF_AG), (OFF_AG, D_IN)):
            w = win_ref[:, lo:hi].astype(jnp.bfloat16)
            if lo in (OFF_Q, OFF_AG):
                w = jnp.dot(w, perm, preferred_element_type=jnp.float32).astype(jnp.bfloat16)
            wi_buf[:, lo:hi] = w
        wo_buf[:D_POOL, :] = wout_ref[:D_POOL, :].astype(jnp.bfloat16)
        wo_buf[D_POOL:, :] = jnp.dot(perm_t, wout_ref[D_POOL:, :].astype(jnp.bfloat16),
                                     preferred_element_type=jnp.float32).astype(jnp.bfloat16)

    @pl.when(t == 0)
    def _zero_halo():
        u_buf[0:POOL_HALO, :] = jnp.zeros((POOL_HALO, D_POOL), jnp.float32)
        kz_buf[:, 0:WINDOW, :] = jnp.zeros((N_KV_HEADS, WINDOW, LANES), jnp.bfloat16)
        vt_buf[:, 0:WINDOW] = jnp.zeros((LANES, WINDOW), jnp.bfloat16)

    h_chunks, kv_chunks = [], []
    for r0 in range(0, tile, EDGE_ROWS):
        xc = x_ref[r0:r0 + EDGE_ROWS, :]
        ms = jnp.mean(xc * xc, axis=-1, keepdims=True)
        hc = (xc * lax.rsqrt(ms + EPS) * gpre_ref[layer:layer + 1, :]).astype(jnp.bfloat16)
        h_chunks.append(hc)
        kv_chunks.append(jnp.dot(hc, wi_buf[:, OFF_K:OFF_AG], preferred_element_type=jnp.float32))
    h = jnp.concatenate(h_chunks, axis=0)
    kvf = jnp.concatenate(kv_chunks, axis=0)

    def proj(lo, hi):
        return jnp.dot(h, wi_buf[:, lo:hi], preferred_element_type=jnp.float32)

    q_buf[...] = (proj(OFF_Q, OFF_K) * (HEAD_DIM ** -0.5)).astype(jnp.bfloat16)
    kf, vf = kvf[:, :LANES], kvf[:, LANES:]
    lane = lax.broadcasted_iota(jnp.int32, (1, LANES), 1)
    kz_buf[0, WINDOW:WINDOW + tile, :] = jnp.where(lane < HEAD_DIM, kf, 0.0).astype(jnp.bfloat16)
    kz_buf[1, WINDOW:WINDOW + tile, :] = jnp.where(lane >= HEAD_DIM, kf, 0.0).astype(jnp.bfloat16)
    vt_buf[:, WINDOW:WINDOW + tile] = vf.T.astype(jnp.bfloat16)

    def attn_gate_item():
        ag_buf[...] = _silu(proj(OFF_AG, D_IN))

    def pool_u_item():
        u_buf[POOL_HALO:POOL_HALO + tile, :] = proj(OFF_U, OFF_PG)

    pos1 = (lax.broadcasted_iota(jnp.int32, (tile, POOL_GROUP), 0) + (t * tile + 1)
            ).astype(jnp.float32)

    def pool_window_item():
        for g, w in enumerate(POOL_WINDOWS):
            cols = slice(g * POOL_GROUP, (g + 1) * POOL_GROUP)
            ext = u_buf[:, cols]
            acc = ext
            shift = 1
            while shift < w:
                acc = acc + pltpu.roll(acc, shift, axis=0)
                shift *= 2
            cur = ext[POOL_HALO:]
            pooled = acc[POOL_HALO:] / jnp.minimum(pos1, float(w)) - cur
            pooled_buf[:, cols] = pooled.astype(jnp.bfloat16)

    def pool_mix_item():
        for g in range(len(POOL_WINDOWS)):
            cols = slice(g * POOL_GROUP, (g + 1) * POOL_GROUP)
            mixed = jnp.dot(pooled_buf[:, cols], poolw_ref[g].astype(jnp.bfloat16),
                            preferred_element_type=jnp.float32)
            scale = pscale_ref[layer:layer + 1, cols]
            mixp_buf[:, cols] = (mixed * scale * pg_buf[:, cols]).astype(jnp.bfloat16)

    def pool_gate_item():
        pg_buf[...] = _silu(proj(OFF_PG, OFF_Q))

    col = lax.broadcasted_iota(jnp.int32, (1, GQA_GROUP * Q_SUB), 1)
    sink_rows = []
    for kv in range(N_KV_HEADS):
        row = jnp.zeros((1, GQA_GROUP * Q_SUB), jnp.float32)
        for g in range(GQA_GROUP):
            row = jnp.where(col >= g * Q_SUB, sinks_ref[layer, kv * GQA_GROUP + g], row)
        sink_rows.append(row)
    first = (t == 0)
    nt_dims = (((1,), (1,)), ((), ()))
    pad = jnp.zeros((Q_SUB, GQA_GROUP * Q_SUB), jnp.bfloat16)

    def attn_scores(sb):
        r0 = sb * Q_SUB
        qrows = slice(r0, r0 + Q_SUB)
        krows = slice(r0, r0 + KEY_SPAN)
        variant = jnp.where(first, 2 - sb, 0) if sb < 2 else 0
        q4 = jnp.concatenate([q_buf[qrows, g * LANES:(g + 1) * LANES] for g in range(GQA_GROUP)],
                             axis=0)
        kk = jnp.concatenate([kz_buf[0, krows, :], kz_buf[1, krows, :]], axis=0)
        s = lax.dot_general(kk, q4, nt_dims, preferred_element_type=jnp.float32)
        return s + bias_buf[variant]

    def attn_finish(sb, s):
        r0 = sb * Q_SUB
        qrows = slice(r0, r0 + Q_SUB)
        outs = []
        for kv in range(N_KV_HEADS):
            sh = s[kv * KEY_SPAN:(kv + 1) * KEY_SPAN]
            m = jnp.maximum(jnp.max(sh, axis=0, keepdims=True), sink_rows[kv])
            p = jnp.exp(sh - m)
            l = jnp.sum(p, axis=0, keepdims=True) + jnp.exp(sink_rows[kv] - m)
            pb = p.astype(jnp.bfloat16)
            if sb % 2 == 0:
                pb = jnp.concatenate([pb, pad], axis=0)
                c0 = r0
            else:
                pb = jnp.concatenate([pad, pb], axis=0)
                c0 = r0 - Q_SUB
            vt = vt_buf[kv * HEAD_DIM:(kv + 1) * HEAD_DIM, c0:c0 + 2 * LANES]
            o = jnp.dot(vt, pb, preferred_element_type=jnp.float32)
            outs.append(o * (1.0 / l))
        for pr in range(GQA_GROUP // 2):
            both = jnp.concatenate([o[:, pr * LANES:(pr + 1) * LANES] for o in outs], axis=0)
            both_t = both.T
            for half in range(2):
                g = 2 * pr + half
                val = both_t[half * Q_SUB:(half + 1) * Q_SUB] * ag_buf[qrows, g * LANES:(g + 1) * LANES]
                mixa_buf[qrows, g * LANES:(g + 1) * LANES] = val.astype(jnp.bfloat16)

    y_pool = []

    def out_pool_item():
        y_pool.append(jnp.dot(mixp_buf[...], wo_buf[:D_POOL, :],
                              preferred_element_type=jnp.float32))

    attn_gate_item()
    dense_items = [[pool_u_item], [pool_gate_item], [pool_window_item, pool_mix_item], [out_pool_item]]
    group = n_sub // len(dense_items)
    for p, items in enumerate(dense_items):
        subs = range(p * group, (p + 1) * group)
        scores = [attn_scores(sb) for sb in subs]
        for item in items:
            item()
        for sb, s in zip(subs, scores):
            attn_finish(sb, s)

    u_buf[0:POOL_HALO, :] = u_buf[tile:tile + POOL_HALO, :]
    kz_buf[:, 0:WINDOW, :] = kz_buf[:, tile:tile + WINDOW, :]
    vt_buf[:, 0:WINDOW] = vt_buf[:, tile:tile + WINDOW]

    for r0 in range(0, tile, EDGE_ROWS):
        rows = slice(r0, r0 + EDGE_ROWS)
        y = y_pool[0][rows] + jnp.dot(mixa_buf[rows, :], wo_buf[D_POOL:, :],
                                      preferred_element_type=jnp.float32)
        ms2 = jnp.mean(y * y, axis=-1, keepdims=True)
        o_ref[rows, :] = x_ref[rows, :] + y * lax.rsqrt(ms2 + EPS) * gpost_ref[layer:layer + 1, :]


def _layer(layer, x, w_in, pool_w, pool_scale, sinks, w_out, g_pre, g_post):
    batch, seq, d = x.shape
    tile = SEQ_TILE
    assert seq % tile == 0 and tile % (2 * Q_SUB) == 0 and d == D_MODEL

    def of_layer(*shape):
        return pl.BlockSpec((None,) + shape, lambda b, t: (layer,) + (0,) * len(shape),
                            pipeline_mode=pl.Buffered(1))

    def whole(a):
        return pl.BlockSpec(a.shape, lambda b, t: (0,) * a.ndim)

    return pl.pallas_call(
        functools.partial(_layer_kernel, layer),
        out_shape=jax.ShapeDtypeStruct(x.shape, x.dtype),
        grid=(batch, seq // tile),
        in_specs=[
            pl.BlockSpec((None, tile, d), lambda b, t: (b, t, 0)),
            of_layer(D_MODEL, D_IN),
            of_layer(len(POOL_WINDOWS), POOL_GROUP, POOL_GROUP),
            whole(pool_scale),
            pl.BlockSpec(memory_space=pltpu.SMEM),
            of_layer(D_MODEL, D_MODEL),
            whole(g_pre),
            whole(g_post),
        ],
        out_specs=pl.BlockSpec((None, tile, d), lambda b, t: (b, t, 0)),
        scratch_shapes=[
            pltpu.VMEM((POOL_HALO + tile, D_POOL), jnp.float32),
            pltpu.VMEM((tile, D_ATTN), jnp.bfloat16),
            pltpu.VMEM((N_KV_HEADS, WINDOW + tile, LANES), jnp.bfloat16),
            pltpu.VMEM((LANES, WINDOW + tile), jnp.bfloat16),
            pltpu.VMEM((tile, D_ATTN), jnp.float32),
            pltpu.VMEM((tile, D_POOL), jnp.float32),
            pltpu.VMEM((tile, D_POOL), jnp.bfloat16),
            pltpu.VMEM((tile, D_POOL), jnp.bfloat16),
            pltpu.VMEM((tile, D_ATTN), jnp.bfloat16),
            pltpu.VMEM((3, N_KV_HEADS * KEY_SPAN, GQA_GROUP * Q_SUB), jnp.float32),
            pltpu.VMEM((D_MODEL, D_IN), jnp.bfloat16),
            pltpu.VMEM((D_MODEL, D_MODEL), jnp.bfloat16),
        ],
        compiler_params=pltpu.CompilerParams(
            dimension_semantics=("arbitrary", "arbitrary"),
            vmem_limit_bytes=VMEM_LIMIT_BYTES),
        name="hybrid_layer",
    )(x, w_in, pool_w, pool_scale, sinks, w_out, g_pre, g_post)


@jax.jit
def kernel(x, w_in, pool_w, pool_scale, attn_sinks, w_out, norm_pre, norm_post):
    for layer in range(w_in.shape[0]):
        x = _layer(layer, x, w_in, pool_w, pool_scale, attn_sinks, w_out, norm_pre, norm_post)
    return x
```

```python
import functools

import jax
import jax.numpy as jnp
from jax import lax
from jax.experimental import pallas as pl
from jax.experimental.pallas import tpu as pltpu

D_MODEL = 1024
D_POOL = 512
POOL_WINDOWS = (2, 4, 8, 16)
POOL_GROUP = 128
HEAD_DIM = 64
D_ATTN = 512
N_HEADS = 8
N_KV_HEADS = 2
GQA_GROUP = N_HEADS // N_KV_HEADS
WINDOW = 128
D_IN = 2304
EPS = 1e-6
NEG_INF = -1e30

OFF_U, OFF_PG, OFF_Q, OFF_K, OFF_V, OFF_AG = 0, 512, 1024, 1536, 1664, 1792

LANES = 128
POOL_HALO = 16
SEQ_TILE = 1024
Q_SUB = 64
EDGE_ROWS = 1024
KEY_SPAN = Q_SUB + WINDOW
VMEM_LIMIT_BYTES = 56 * 1024 * 1024


def _silu(x):
    return x * (1.0 / (1.0 + jnp.exp(-x)))


def _layer_kernel(layer, x_ref, win_ref, poolw_ref, pscale_ref, sinks_ref, wout_ref,
                  gpre_ref, gpost_ref, o_ref,
                  u_buf, q_buf, kz_buf, vt_buf, ag_buf, pg_buf, pooled_buf, mixp_buf, mixa_buf, bias_buf, wi_buf,
                  wo_buf):
    tile = x_ref.shape[0]
    n_sub = tile // Q_SUB
    t = pl.program_id(1)

    @pl.when((pl.program_id(0) == 0) & (t == 0))
    def _first_step():
        r = lax.broadcasted_iota(jnp.int32, (KEY_SPAN, GQA_GROUP * Q_SUB), 0)
        col = lax.broadcasted_iota(jnp.int32, (KEY_SPAN, GQA_GROUP * Q_SUB), 1)
        dist = (col & (Q_SUB - 1)) + WINDOW - r
        in_win = (dist >= 0) & (dist < WINDOW)
        distf = dist.astype(jnp.float32)
        for kv in range(N_KV_HEADS):
            slope = jnp.zeros_like(distf)
            for g in range(GQA_GROUP):
                j = kv * GQA_GROUP + g
                slope = jnp.where(col >= g * Q_SUB, 2.0 ** (-8.0 * (j + 1) / N_HEADS), slope)
            b = -slope * distf
            for f in range(3):
                bias_buf[f, kv * KEY_SPAN:(kv + 1) * KEY_SPAN, :] = jnp.where(
                    in_win & (r >= f * Q_SUB), b, NEG_INF)

        a0 = lax.broadcasted_iota(jnp.int32, (D_ATTN, D_ATTN), 0)
        a1 = lax.broadcasted_iota(jnp.int32, (D_ATTN, D_ATTN), 1)

        def source(n):
            return ((n >> 6) & 1) * (GQA_GROUP * HEAD_DIM) + (n >> 7) * HEAD_DIM + (n & (HEAD_DIM - 1))

        perm = jnp.where(a0 == source(a1), 1.0, 0.0).astype(jnp.bfloat16)
        perm_t = jnp.where(a1 == source(a0), 1.0, 0.0).astype(jnp.bfloat16)
        for lo, hi in ((OFF_U, OFF_PG), (OFF_PG, OFF_Q), (OFF_Q, OFF_K), (OFF_K, OFF_AG), (OFF_AG, D_IN)):
            w = win_ref[:, lo:hi].astype(jnp.bfloat16)
            if lo in (OFF_Q, OFF_AG):
                w = jnp.dot(w, perm, preferred_element_type=jnp.float32).astype(jnp.bfloat16)
            wi_buf[:, lo:hi] = w
        wo_buf[:D_POOL, :] = wout_ref[:D_POOL, :].astype(jnp.bfloat16)
        wo_buf[D_POOL:, :] = jnp.dot(perm_t, wout_ref[D_POOL:, :].astype(jnp.bfloat16),
                                     preferred_element_type=jnp.float32).astype(jnp.bfloat16)

    @pl.when(t == 0)
    def _zero_halo():
        u_buf[0:POOL_HALO, :] = jnp.zeros((POOL_HALO, D_POOL), jnp.float32)
        kz_buf[:, 0:WINDOW, :] = jnp.zeros((N_KV_HEADS, WINDOW, LANES), jnp.bfloat16)
        vt_buf[:, 0:WINDOW] = jnp.zeros((LANES, WINDOW), jnp.bfloat16)

    h_chunks, kv_chunks = [], []
    for r0 in range(0, tile, EDGE_ROWS):
        xc = x_ref[r0:r0 + EDGE_ROWS, :]
        ms = jnp.mean(xc * xc, axis=-1, keepdims=True)
        hc = (xc * lax.rsqrt(ms + EPS) * gpre_ref[layer:layer + 1, :]).astype(jnp.bfloat16)
        h_chunks.append(hc)
        kv_chunks.append(jnp.dot(hc, wi_buf[:, OFF_K:OFF_AG], preferred_element_type=jnp.float32))
    h = jnp.concatenate(h_chunks, axis=0)
    kvf = jnp.concatenate(kv_chunks, axis=0)

    def proj(lo, hi):
        return jnp.dot(h, wi_buf[:, lo:hi], preferred_element_type=jnp.float32)

    q_buf[...] = (proj(OFF_Q, OFF_K) * (HEAD_DIM ** -0.5)).astype(jnp.bfloat16)
    kf, vf = kvf[:, :LANES], kvf[:, LANES:]
    lane = lax.broadcasted_iota(jnp.int32, (1, LANES), 1)
    kz_buf[0, WINDOW:WINDOW + tile, :] = jnp.where(lane < HEAD_DIM, kf, 0.0).astype(jnp.bfloat16)
    kz_buf[1, WINDOW:WINDOW + tile, :] = jnp.where(lane >= HEAD_DIM, kf, 0.0).astype(jnp.bfloat16)
    vt_buf[:, WINDOW:WINDOW + tile] = vf.T.astype(jnp.bfloat16)

    def attn_gate_item():
        ag_buf[...] = _silu(proj(OFF_AG, D_IN))

    def pool_u_item():
        u_buf[POOL_HALO:POOL_HALO + tile, :] = proj(OFF_U, OFF_PG)

    pos1 = (lax.broadcasted_iota(jnp.int32, (tile, POOL_GROUP), 0) + (t * tile + 1)
            ).astype(jnp.float32)

    def pool_window_item():
        for g, w in enumerate(POOL_WINDOWS):
            cols = slice(g * POOL_GROUP, (g + 1) * POOL_GROUP)
            ext = u_buf[:, cols]
            acc = ext
            shift = 1
            while shift < w:
                acc = acc + pltpu.roll(acc, shift, axis=0)
                shift *= 2
            cur = ext[POOL_HALO:]
            pooled = acc[POOL_HALO:] / jnp.minimum(pos1, float(w)) - cur
            pooled_buf[:, cols] = pooled.astype(jnp.bfloat16)

    def pool_mix_item():
        for g in range(len(POOL_WINDOWS)):
            cols = slice(g * POOL_GROUP, (g + 1) * POOL_GROUP)
            mixed = jnp.dot(pooled_buf[:, cols], poolw_ref[g].astype(jnp.bfloat16),
                            preferred_element_type=jnp.float32)
            scale = pscale_ref[layer:layer + 1, cols]
            mixp_buf[:, cols] = (mixed * scale * pg_buf[:, cols]).astype(jnp.bfloat16)

    def pool_gate_item():
        pg_buf[...] = _silu(proj(OFF_PG, OFF_Q))

    col = lax.broadcasted_iota(jnp.int32, (1, GQA_GROUP * Q_SUB), 1)
    sink_rows = []
    for kv in range(N_KV_HEADS):
        row = jnp.zeros((1, GQA_GROUP * Q_SUB), jnp.float32)
        for g in range(GQA_GROUP):
            row = jnp.where(col >= g * Q_SUB, sinks_ref[layer, kv * GQA_GROUP + g], row)
        sink_rows.append(row)
    first = (t == 0)
    nt_dims = (((1,), (1,)), ((), ()))
    pad = jnp.zeros((Q_SUB, GQA_GROUP * Q_SUB), jnp.bfloat16)

    def attn_scores(sb):
        r0 = sb * Q_SUB
        qrows = slice(r0, r0 + Q_SUB)
        krows = slice(r0, r0 + KEY_SPAN)
        variant = jnp.where(first, 2 - sb, 0) if sb < 2 else 0
        q4 = jnp.concatenate([q_buf[qrows, g * LANES:(g + 1) * LANES] for g in range(GQA_GROUP)],
                             axis=0)
        kk = jnp.concatenate([kz_buf[0, krows, :], kz_buf[1, krows, :]], axis=0)
        s = lax.dot_general(kk, q4, nt_dims, preferred_element_type=jnp.float32)
        return s + bias_buf[variant]

    def attn_finish(sb, s):
        r0 = sb * Q_SUB
        qrows = slice(r0, r0 + Q_SUB)
        outs = []
        for kv in range(N_KV_HEADS):
            sh = s[kv * KEY_SPAN:(kv + 1) * KEY_SPAN]
            m = jnp.maximum(jnp.max(sh, axis=0, keepdims=True), sink_rows[kv])
            p = jnp.exp(sh - m)
            l = jnp.sum(p, axis=0, keepdims=True) + jnp.exp(sink_rows[kv] - m)
            pb = p.astype(jnp.bfloat16)
            if sb % 2 == 0:
                pb = jnp.concatenate([pb, pad], axis=0)
                c0 = r0
            else:
                pb = jnp.concatenate([pad, pb], axis=0)
                c0 = r0 - Q_SUB
            vt = vt_buf[kv * HEAD_DIM:(kv + 1) * HEAD_DIM, c0:c0 + 2 * LANES]
            o = jnp.dot(vt, pb, preferred_element_type=jnp.float32)
            outs.append(o * (1.0 / l))
        for pr in range(GQA_GROUP // 2):
            both = jnp.concatenate([o[:, pr * LANES:(pr + 1) * LANES] for o in outs], axis=0)
            both_t = both.T
            for half in range(2):
                g = 2 * pr + half
                val = both_t[half * Q_SUB:(half + 1) * Q_SUB] * ag_buf[qrows, g * LANES:(g + 1) * LANES]
                mixa_buf[qrows, g * LANES:(g + 1) * LANES] = val.astype(jnp.bfloat16)

    y_pool = []

    def out_pool_item():
        y_pool.append(jnp.dot(mixp_buf[...], wo_buf[:D_POOL, :],
                              preferred_element_type=jnp.float32))

    attn_gate_item()
    dense_items = [[pool_u_item], [pool_gate_item], [pool_window_item, pool_mix_item], [out_pool_item]]
    group = n_sub // len(dense_items)
    for p, items in enumerate(dense_items):
        subs = range(p * group, (p + 1) * group)
        scores = [attn_scores(sb) for sb in subs]
        for item in items:
            item()
        for sb, s in zip(subs, scores):
            attn_finish(sb, s)

    u_buf[0:POOL_HALO, :] = u_buf[tile:tile + POOL_HALO, :]
    kz_buf[:, 0:WINDOW, :] = kz_buf[:, tile:tile + WINDOW, :]
    vt_buf[:, 0:WINDOW] = vt_buf[:, tile:tile + WINDOW]

    for r0 in range(0, tile, EDGE_ROWS):
        rows = slice(r0, r0 + EDGE_ROWS)
        y = y_pool[0][rows] + jnp.dot(mixa_buf[rows, :], wo_buf[D_POOL:, :],
                                      preferred_element_type=jnp.float32)
        ms2 = jnp.mean(y * y, axis=-1, keepdims=True)
        o_ref[rows, :] = x_ref[rows, :] + y * lax.rsqrt(ms2 + EPS) * gpost_ref[layer:layer + 1, :]


def _layer(layer, x, w_in, pool_w, pool_scale, sinks, w_out, g_pre, g_post):
    batch, seq, d = x.shape
    tile = SEQ_TILE
    assert seq % tile == 0 and tile % (2 * Q_SUB) == 0 and d == D_MODEL

    def of_layer(*shape):
        return pl.BlockSpec((None,) + shape, lambda b, t: (layer,) + (0,) * len(shape),
                            pipeline_mode=pl.Buffered(1))

    def whole(a):
        return pl.BlockSpec(a.shape, lambda b, t: (0,) * a.ndim)

    return pl.pallas_call(
        functools.partial(_layer_kernel, layer),
        out_shape=jax.ShapeDtypeStruct(x.shape, x.dtype),
        grid=(batch, seq // tile),
        in_specs=[
            pl.BlockSpec((None, tile, d), lambda b, t: (b, t, 0)),
            of_layer(D_MODEL, D_IN),
            of_layer(len(POOL_WINDOWS), POOL_GROUP, POOL_GROUP),
            whole(pool_scale),
            pl.BlockSpec(memory_space=pltpu.SMEM),
            of_layer(D_MODEL, D_MODEL),
            whole(g_pre),
            whole(g_post),
        ],
        out_specs=pl.BlockSpec((None, tile, d), lambda b, t: (b, t, 0)),
        scratch_shapes=[
            pltpu.VMEM((POOL_HALO + tile, D_POOL), jnp.float32),
            pltpu.VMEM((tile, D_ATTN), jnp.bfloat16),
            pltpu.VMEM((N_KV_HEADS, WINDOW + tile, LANES), jnp.bfloat16),
            pltpu.VMEM((LANES, WINDOW + tile), jnp.bfloat16),
            pltpu.VMEM((tile, D_ATTN), jnp.float32),
            pltpu.VMEM((tile, D_POOL), jnp.float32),
            pltpu.VMEM((tile, D_POOL), jnp.bfloat16),
            pltpu.VMEM((tile, D_POOL), jnp.bfloat16),
            pltpu.VMEM((tile, D_ATTN), jnp.bfloat16),
            pltpu.VMEM((3, N_KV_HEADS * KEY_SPAN, GQA_GROUP * Q_SUB), jnp.float32),
            pltpu.VMEM((D_MODEL, D_IN), jnp.bfloat16),
            pltpu.VMEM((D_MODEL, D_MODEL), jnp.bfloat16),
        ],
        compiler_params=pltpu.CompilerParams(
            dimension_semantics=("arbitrary", "arbitrary"),
            vmem_limit_bytes=VMEM_LIMIT_BYTES),
        name="hybrid_layer",
    )(x, w_in, pool_w, pool_scale, sinks, w_out, g_pre, g_post)


@jax.jit
def kernel(x, w_in, pool_w, pool_scale, attn_sinks, w_out, norm_pre, norm_post):
    for layer in range(w_in.shape[0]):
        x = _layer(layer, x, w_in, pool_w, pool_scale, attn_sinks, w_out, norm_pre, norm_post)
    return x
```

```python
import functools

import jax
import jax.numpy as jnp
from jax import lax
from jax.experimental import pallas as pl
from jax.experimental.pallas import tpu as pltpu

D_MODEL = 1024
D_POOL = 512
POOL_WINDOWS = (2, 4, 8, 16)
POOL_GROUP = 128
HEAD_DIM = 64
D_ATTN = 512
N_HEADS = 8
N_KV_HEADS = 2
GQA_GROUP = N_HEADS // N_KV_HEADS
WINDOW = 128
D_IN = 2304
EPS = 1e-6
NEG_INF = -1e30

OFF_U, OFF_PG, OFF_Q, OFF_K, OFF_V, OFF_AG = 0, 512, 1024, 1536, 1664, 1792

LANES = 128
POOL_HALO = 16
SEQ_TILE = 1024
Q_SUB = 64
KEY_SPAN = Q_SUB + WINDOW
VMEM_LIMIT_BYTES = 58 * 1024 * 1024


def _silu(x):
    return x * (1.0 / (1.0 + jnp.exp(-x)))


def _layer_kernel(layer, n_tiles, tiles_per_seq,
                  x_ref, win_ref, poolw_ref, pscale_ref, sinks_ref, wout_ref, gpre_ref, gpost_ref, o_ref,
                  u_buf, q_buf, kz_buf, vt_buf, ag_buf, pg_buf, pooled_buf, mixp_buf, mixa_buf, xprev_buf,
                  bias_buf, wi_buf, wo_buf, pw_buf):
    step = pl.program_id(0)
    t = lax.rem(jnp.minimum(step, n_tiles - 1), tiles_per_seq)

    @pl.when(step == 0)
    def _first_step():
        r = lax.broadcasted_iota(jnp.int32, (KEY_SPAN, GQA_GROUP * Q_SUB), 0)
        col = lax.broadcasted_iota(jnp.int32, (KEY_SPAN, GQA_GROUP * Q_SUB), 1)
        dist = (col & (Q_SUB - 1)) + WINDOW - r
        in_win = (dist >= 0) & (dist < WINDOW)
        distf = dist.astype(jnp.float32)
        for kv in range(N_KV_HEADS):
            slope = jnp.zeros_like(distf)
            for g in range(GQA_GROUP):
                j = kv * GQA_GROUP + g
                slope = jnp.where(col >= g * Q_SUB, 2.0 ** (-8.0 * (j + 1) / N_HEADS), slope)
            b = -slope * distf
            for f in range(3):
                bias_buf[f, kv * KEY_SPAN:(kv + 1) * KEY_SPAN, :] = jnp.where(
                    in_win & (r >= f * Q_SUB), b, NEG_INF)

        a0 = lax.broadcasted_iota(jnp.int32, (D_ATTN, D_ATTN), 0)
        a1 = lax.broadcasted_iota(jnp.int32, (D_ATTN, D_ATTN), 1)

        def source(n):
            return ((n >> 6) & 1) * (GQA_GROUP * HEAD_DIM) + (n >> 7) * HEAD_DIM + (n & (HEAD_DIM - 1))

        perm = jnp.where(a0 == source(a1), 1.0, 0.0).astype(jnp.bfloat16)
        perm_t = jnp.where(a1 == source(a0), 1.0, 0.0).astype(jnp.bfloat16)
        for lo, hi in ((OFF_U, OFF_PG), (OFF_PG, OFF_Q), (OFF_Q, OFF_K), (OFF_K, OFF_AG), (OFF_AG, D_IN)):
            w = win_ref[:, lo:hi].astype(jnp.bfloat16)
            if lo in (OFF_Q, OFF_AG):
                w = jnp.dot(w, perm, preferred_element_type=jnp.float32).astype(jnp.bfloat16)
            wi_buf[:, lo:hi] = w
        wo_buf[:D_POOL, :] = wout_ref[:D_POOL, :].astype(jnp.bfloat16)
        wo_buf[D_POOL:, :] = jnp.dot(perm_t, wout_ref[D_POOL:, :].astype(jnp.bfloat16),
                                     preferred_element_type=jnp.float32).astype(jnp.bfloat16)
        pw_buf[...] = jnp.zeros(pw_buf.shape, jnp.bfloat16)
        for g in range(len(POOL_WINDOWS)):
            d0 = (g % 2) * POOL_GROUP
            pw_buf[g // 2, d0:d0 + POOL_GROUP, d0:d0 + POOL_GROUP] = poolw_ref[g].astype(jnp.bfloat16)

        mixp_buf[...] = jnp.zeros(mixp_buf.shape, jnp.bfloat16)
        mixa_buf[...] = jnp.zeros(mixa_buf.shape, jnp.bfloat16)
        xprev_buf[...] = jnp.zeros(xprev_buf.shape, jnp.float32)

    @pl.when(t == 0)
    def _zero_halo():
        u_buf[0:POOL_HALO, :] = jnp.zeros((POOL_HALO, D_POOL), jnp.float32)
        kz_buf[:, 0:WINDOW, :] = jnp.zeros((N_KV_HEADS, WINDOW, LANES), jnp.bfloat16)
        vt_buf[:, 0:WINDOW] = jnp.zeros((LANES, WINDOW), jnp.bfloat16)

    def back_out():
        return (jnp.dot(mixp_buf[...], wo_buf[:D_POOL, :], preferred_element_type=jnp.float32)
                + jnp.dot(mixa_buf[...], wo_buf[D_POOL:, :], preferred_element_type=jnp.float32))

    def back_finish(y):
        ms2 = jnp.mean(y * y, axis=-1, keepdims=True)
        o_ref[...] = xprev_buf[...] + y * lax.rsqrt(ms2 + EPS) * gpost_ref[layer:layer + 1, :]

    pl.when(step < n_tiles)(functools.partial(
        _front_and_back, layer, t, back_out, back_finish,
        x_ref, pscale_ref, sinks_ref, gpre_ref,
        u_buf, q_buf, kz_buf, vt_buf, ag_buf, pg_buf, pooled_buf, mixp_buf, mixa_buf, xprev_buf,
        bias_buf, wi_buf, pw_buf))

    @pl.when(step == n_tiles)
    def _drain():
        back_finish(back_out())


def _front_and_back(layer, t, back_out, back_finish,
                    x_ref, pscale_ref, sinks_ref, gpre_ref,
                    u_buf, q_buf, kz_buf, vt_buf, ag_buf, pg_buf, pooled_buf, mixp_buf, mixa_buf, xprev_buf,
                    bias_buf, wi_buf, pw_buf):
    tile = x_ref.shape[0]
    n_sub = tile // Q_SUB

    y_prev = back_out()

    x = x_ref[...]
    ms = jnp.mean(x * x, axis=-1, keepdims=True)
    h = (x * lax.rsqrt(ms + EPS) * gpre_ref[layer:layer + 1, :]).astype(jnp.bfloat16)

    def proj(lo, hi):
        return jnp.dot(h, wi_buf[:, lo:hi], preferred_element_type=jnp.float32)

    kvf = proj(OFF_K, OFF_AG)
    q_buf[...] = (proj(OFF_Q, OFF_K) * (HEAD_DIM ** -0.5)).astype(jnp.bfloat16)

    back_finish(y_prev)
    xprev_buf[...] = x_ref[...]

    kf, vf = kvf[:, :LANES], kvf[:, LANES:]
    lane = lax.broadcasted_iota(jnp.int32, (1, LANES), 1)
    kz_buf[0, WINDOW:WINDOW + tile, :] = jnp.where(lane < HEAD_DIM, kf, 0.0).astype(jnp.bfloat16)
    kz_buf[1, WINDOW:WINDOW + tile, :] = jnp.where(lane >= HEAD_DIM, kf, 0.0).astype(jnp.bfloat16)
    vt_buf[:, WINDOW:WINDOW + tile] = vf.T.astype(jnp.bfloat16)

    def attn_gate_item():
        ag_buf[...] = _silu(proj(OFF_AG, D_IN))

    def pool_u_item():
        u_buf[POOL_HALO:POOL_HALO + tile, :] = proj(OFF_U, OFF_PG)

    pos1 = (lax.broadcasted_iota(jnp.int32, (tile, POOL_GROUP), 0) + (t * tile + 1)
            ).astype(jnp.float32)

    def pool_window_item():
        for g, w in enumerate(POOL_WINDOWS):
            cols = slice(g * POOL_GROUP, (g + 1) * POOL_GROUP)
            ext = u_buf[:, cols]
            acc = ext
            shift = 1
            while shift < w:
                acc = acc + pltpu.roll(acc, shift, axis=0)
                shift *= 2
            cur = ext[POOL_HALO:]
            pooled = acc[POOL_HALO:] / jnp.minimum(pos1, float(w)) - cur
            pooled_buf[:, cols] = pooled.astype(jnp.bfloat16)

    def pool_mix_item():
        for pair in range(len(POOL_WINDOWS) // 2):
            cols = slice(pair * 2 * POOL_GROUP, (pair + 1) * 2 * POOL_GROUP)
            mixed = jnp.dot(pooled_buf[:, cols], pw_buf[pair], preferred_element_type=jnp.float32)
            scale = pscale_ref[layer:layer + 1, cols]
            mixp_buf[:, cols] = (mixed * scale * pg_buf[:, cols]).astype(jnp.bfloat16)

    def pool_gate_item():
        pg_buf[...] = _silu(proj(OFF_PG, OFF_Q))

    col = lax.broadcasted_iota(jnp.int32, (1, GQA_GROUP * Q_SUB), 1)
    sink_rows = []
    for kv in range(N_KV_HEADS):
        row = jnp.zeros((1, GQA_GROUP * Q_SUB), jnp.float32)
        for g in range(GQA_GROUP):
            row = jnp.where(col >= g * Q_SUB, sinks_ref[layer, kv * GQA_GROUP + g], row)
        sink_rows.append(row)
    first = (t == 0)
    nt_dims = (((1,), (1,)), ((), ()))
    pad = jnp.zeros((Q_SUB, GQA_GROUP * Q_SUB), jnp.bfloat16)

    def attn_scores(sb):
        r0 = sb * Q_SUB
        qrows = slice(r0, r0 + Q_SUB)
        krows = slice(r0, r0 + KEY_SPAN)
        variant = jnp.where(first, 2 - sb, 0) if sb < 2 else 0
        q4 = jnp.concatenate([q_buf[qrows, g * LANES:(g + 1) * LANES] for g in range(GQA_GROUP)],
                             axis=0)
        kk = jnp.concatenate([kz_buf[0, krows, :], kz_buf[1, krows, :]], axis=0)
        s = lax.dot_general(kk, q4, nt_dims, preferred_element_type=jnp.float32)
        return s + bias_buf[variant]

    def attn_finish(sb, s):
        r0 = sb * Q_SUB
        qrows = slice(r0, r0 + Q_SUB)
        outs = []
        for kv in range(N_KV_HEADS):
            sh = s[kv * KEY_SPAN:(kv + 1) * KEY_SPAN]
            m = jnp.maximum(jnp.max(sh, axis=0, keepdims=True), sink_rows[kv])
            p = jnp.exp(sh - m)
            l = jnp.sum(p, axis=0, keepdims=True) + jnp.exp(sink_rows[kv] - m)
            pb = p.astype(jnp.bfloat16)
            if sb % 2 == 0:
                pb = jnp.concatenate([pb, pad], axis=0)
                c0 = r0
            else:
                pb = jnp.concatenate([pad, pb], axis=0)
                c0 = r0 - Q_SUB
            vt = vt_buf[kv * HEAD_DIM:(kv + 1) * HEAD_DIM, c0:c0 + 2 * LANES]
            o = jnp.dot(vt, pb, preferred_element_type=jnp.float32)
            outs.append(o * (1.0 / l))
        for pr in range(GQA_GROUP // 2):
            both = jnp.concatenate([o[:, pr * LANES:(pr + 1) * LANES] for o in outs], axis=0)
            both_t = both.T
            for half in range(2):
                g = 2 * pr + half
                val = both_t[half * Q_SUB:(half + 1) * Q_SUB] * ag_buf[qrows, g * LANES:(g + 1) * LANES]
                mixa_buf[qrows, g * LANES:(g + 1) * LANES] = val.astype(jnp.bfloat16)

    pool_u_item()
    pool_window_item()
    attn_gate_item()
    halves = (range(0, n_sub // 2), range(n_sub // 2, n_sub))
    for subs, items in zip(halves, ([pool_gate_item], [pool_mix_item])):
        scores = [attn_scores(sb) for sb in subs]
        for item in items:
            item()
        for sb, s in zip(subs, scores):
            attn_finish(sb, s)

    u_buf[0:POOL_HALO, :] = u_buf[tile:tile + POOL_HALO, :]
    kz_buf[:, 0:WINDOW, :] = kz_buf[:, tile:tile + WINDOW, :]
    vt_buf[:, 0:WINDOW] = vt_buf[:, tile:tile + WINDOW]


def _layer(layer, x, w_in, pool_w, pool_scale, sinks, w_out, g_pre, g_post):
    batch, seq, d = x.shape
    tile = SEQ_TILE
    assert seq % tile == 0 and tile % (2 * Q_SUB) == 0 and d == D_MODEL
    tiles_per_seq = seq // tile
    n_tiles = batch * tiles_per_seq

    def front_tile(step):
        f = jnp.minimum(step, n_tiles - 1)
        return (f // tiles_per_seq, f % tiles_per_seq, 0)

    def back_tile(step):
        p = jnp.maximum(step - 1, 0)
        return (p // tiles_per_seq, p % tiles_per_seq, 0)

    def of_layer(*shape):
        return pl.BlockSpec((None,) + shape, lambda step: (layer,) + (0,) * len(shape),
                            pipeline_mode=pl.Buffered(1))

    def whole(a):
        return pl.BlockSpec(a.shape, lambda step: (0,) * a.ndim)

    return pl.pallas_call(
        functools.partial(_layer_kernel, layer, n_tiles, tiles_per_seq),
        out_shape=jax.ShapeDtypeStruct(x.shape, x.dtype),
        grid=(n_tiles + 1,),
        in_specs=[
            pl.BlockSpec((None, tile, d), front_tile),
            of_layer(D_MODEL, D_IN),
            of_layer(len(POOL_WINDOWS), POOL_GROUP, POOL_GROUP),
            whole(pool_scale),
            pl.BlockSpec(memory_space=pltpu.SMEM),
            of_layer(D_MODEL, D_MODEL),
            whole(g_pre),
            whole(g_post),
        ],
        out_specs=pl.BlockSpec((None, tile, d), back_tile),
        scratch_shapes=[
            pltpu.VMEM((POOL_HALO + tile, D_POOL), jnp.float32),
            pltpu.VMEM((tile, D_ATTN), jnp.bfloat16),
            pltpu.VMEM((N_KV_HEADS, WINDOW + tile, LANES), jnp.bfloat16),
            pltpu.VMEM((LANES, WINDOW + tile), jnp.bfloat16),
            pltpu.VMEM((tile, D_ATTN), jnp.float32),
            pltpu.VMEM((tile, D_POOL), jnp.float32),
            pltpu.VMEM((tile, D_POOL), jnp.bfloat16),
            pltpu.VMEM((tile, D_POOL), jnp.bfloat16),
            pltpu.VMEM((tile, D_ATTN), jnp.bfloat16),
            pltpu.VMEM((tile, D_MODEL), jnp.float32),
            pltpu.VMEM((3, N_KV_HEADS * KEY_SPAN, GQA_GROUP * Q_SUB), jnp.float32),
            pltpu.VMEM((D_MODEL, D_IN), jnp.bfloat16),
            pltpu.VMEM((D_MODEL, D_MODEL), jnp.bfloat16),
            pltpu.VMEM((len(POOL_WINDOWS) // 2, 2 * POOL_GROUP, 2 * POOL_GROUP), jnp.bfloat16),
        ],
        compiler_params=pltpu.CompilerParams(
            dimension_semantics=("arbitrary",),
            vmem_limit_bytes=VMEM_LIMIT_BYTES),
        name="hybrid_layer",
    )(x, w_in, pool_w, pool_scale, sinks, w_out, g_pre, g_post)


@jax.jit
def kernel(x, w_in, pool_w, pool_scale, attn_sinks, w_out, norm_pre, norm_post):
    for layer in range(w_in.shape[0]):
        x = _layer(layer, x, w_in, pool_w, pool_scale, attn_sinks, w_out, norm_pre, norm_post)
    return x
```

```python
import functools

import jax
import jax.numpy as jnp
from jax import lax
from jax.experimental import pallas as pl
from jax.experimental.pallas import tpu as pltpu

D_MODEL = 1024
D_POOL = 512
POOL_WINDOWS = (2, 4, 8, 16)
POOL_GROUP = 128
HEAD_DIM = 64
D_ATTN = 512
N_HEADS = 8
N_KV_HEADS = 2
GQA_GROUP = N_HEADS // N_KV_HEADS
WINDOW = 128
D_IN = 2304
EPS = 1e-6
NEG_INF = -1e30

OFF_U, OFF_PG, OFF_Q, OFF_K, OFF_V, OFF_AG = 0, 512, 1024, 1536, 1664, 1792

LANES = 128
POOL_HALO = 16
SEQ_TILE = 1024
Q_SUB = 64
KEY_SPAN = Q_SUB + WINDOW
VMEM_LIMIT_BYTES = 56 * 1024 * 1024


def _silu(x):
    return x * (1.0 / (1.0 + jnp.exp(-x)))


def _layer_kernel(layer, x_ref, win_ref, poolw_ref, pscale_ref, sinks_ref, wout_ref,
                  gpre_ref, gpost_ref, o_ref,
                  u_buf, q_buf, kz_buf, vt_buf, ag_buf, pg_buf, pooled_buf, mixp_buf, mixa_buf, bias_buf, wi_buf,
                  wo_buf, pw_buf):
    tile = x_ref.shape[0]
    n_sub = tile // Q_SUB
    t = pl.program_id(1)

    @pl.when((pl.program_id(0) == 0) & (t == 0))
    def _first_step():
        r = lax.broadcasted_iota(jnp.int32, (KEY_SPAN, GQA_GROUP * Q_SUB), 0)
        col = lax.broadcasted_iota(jnp.int32, (KEY_SPAN, GQA_GROUP * Q_SUB), 1)
        dist = (col & (Q_SUB - 1)) + WINDOW - r
        in_win = (dist >= 0) & (dist < WINDOW)
        distf = dist.astype(jnp.float32)
        for kv in range(N_KV_HEADS):
            slope = jnp.zeros_like(distf)
            for g in range(GQA_GROUP):
                j = kv * GQA_GROUP + g
                slope = jnp.where(col >= g * Q_SUB, 2.0 ** (-8.0 * (j + 1) / N_HEADS), slope)
            b = -slope * distf
            for f in range(3):
                bias_buf[f, kv * KEY_SPAN:(kv + 1) * KEY_SPAN, :] = jnp.where(
                    in_win & (r >= f * Q_SUB), b, NEG_INF)

        a0 = lax.broadcasted_iota(jnp.int32, (D_ATTN, D_ATTN), 0)
        a1 = lax.broadcasted_iota(jnp.int32, (D_ATTN, D_ATTN), 1)

        def source(n):
            return ((n >> 6) & 1) * (GQA_GROUP * HEAD_DIM) + (n >> 7) * HEAD_DIM + (n & (HEAD_DIM - 1))

        perm = jnp.where(a0 == source(a1), 1.0, 0.0).astype(jnp.bfloat16)
        perm_t = jnp.where(a1 == source(a0), 1.0, 0.0).astype(jnp.bfloat16)
        for lo, hi in ((OFF_U, OFF_PG), (OFF_PG, OFF_Q), (OFF_Q, OFF_K), (OFF_K, OFF_AG), (OFF_AG, D_IN)):
            w = win_ref[:, lo:hi].astype(jnp.bfloat16)
            if lo in (OFF_Q, OFF_AG):
                w = jnp.dot(w, perm, preferred_element_type=jnp.float32).astype(jnp.bfloat16)
            wi_buf[:, lo:hi] = w
        wo_buf[:D_POOL, :] = wout_ref[:D_POOL, :].astype(jnp.bfloat16)
        wo_buf[D_POOL:, :] = jnp.dot(perm_t, wout_ref[D_POOL:, :].astype(jnp.bfloat16),
                                     preferred_element_type=jnp.float32).astype(jnp.bfloat16)
        pw_buf[...] = jnp.zeros(pw_buf.shape, jnp.bfloat16)
        for g in range(len(POOL_WINDOWS)):
            d0 = (g % 2) * POOL_GROUP
            pw_buf[g // 2, d0:d0 + POOL_GROUP, d0:d0 + POOL_GROUP] = poolw_ref[g].astype(jnp.bfloat16)

    @pl.when(t == 0)
    def _zero_halo():
        u_buf[0:POOL_HALO, :] = jnp.zeros((POOL_HALO, D_POOL), jnp.float32)
        kz_buf[:, 0:WINDOW, :] = jnp.zeros((N_KV_HEADS, WINDOW, LANES), jnp.bfloat16)
        vt_buf[:, 0:WINDOW] = jnp.zeros((LANES, WINDOW), jnp.bfloat16)

    x = x_ref[...]
    ms = jnp.mean(x * x, axis=-1, keepdims=True)
    h = (x * lax.rsqrt(ms + EPS) * gpre_ref[layer:layer + 1, :]).astype(jnp.bfloat16)

    def proj(lo, hi):
        return jnp.dot(h, wi_buf[:, lo:hi], preferred_element_type=jnp.float32)

    half_rows = tile // 2
    kvf = jnp.concatenate(
        [jnp.dot(h[r0:r0 + half_rows], wi_buf[:, OFF_K:OFF_AG], preferred_element_type=jnp.float32)
         for r0 in (0, half_rows)], axis=0)
    q_buf[...] = (proj(OFF_Q, OFF_K) * (HEAD_DIM ** -0.5)).astype(jnp.bfloat16)
    kf, vf = kvf[:, :LANES], kvf[:, LANES:]
    lane = lax.broadcasted_iota(jnp.int32, (1, LANES), 1)
    kz_buf[0, WINDOW:WINDOW + tile, :] = jnp.where(lane < HEAD_DIM, kf, 0.0).astype(jnp.bfloat16)
    kz_buf[1, WINDOW:WINDOW + tile, :] = jnp.where(lane >= HEAD_DIM, kf, 0.0).astype(jnp.bfloat16)
    vt_buf[:, WINDOW:WINDOW + tile] = vf.T.astype(jnp.bfloat16)

    def attn_gate_item():
        ag_buf[...] = _silu(proj(OFF_AG, D_IN))

    def pool_u_item():
        u_buf[POOL_HALO:POOL_HALO + tile, :] = proj(OFF_U, OFF_PG)

    pos1 = (lax.broadcasted_iota(jnp.int32, (tile, POOL_GROUP), 0) + (t * tile + 1)
            ).astype(jnp.float32)

    def pool_window_item():
        for g, w in enumerate(POOL_WINDOWS):
            cols = slice(g * POOL_GROUP, (g + 1) * POOL_GROUP)
            ext = u_buf[:, cols]
            acc = ext
            shift = 1
            while shift < w:
                acc = acc + pltpu.roll(acc, shift, axis=0)
                shift *= 2
            cur = ext[POOL_HALO:]
            pooled = acc[POOL_HALO:] / jnp.minimum(pos1, float(w)) - cur
            pooled_buf[:, cols] = pooled.astype(jnp.bfloat16)

    def pool_mix_item():
        for pair in range(len(POOL_WINDOWS) // 2):
            cols = slice(pair * 2 * POOL_GROUP, (pair + 1) * 2 * POOL_GROUP)
            mixed = jnp.dot(pooled_buf[:, cols], pw_buf[pair], preferred_element_type=jnp.float32)
            scale = pscale_ref[layer:layer + 1, cols]
            mixp_buf[:, cols] = (mixed * scale * pg_buf[:, cols]).astype(jnp.bfloat16)

    def pool_gate_item():
        pg_buf[...] = _silu(proj(OFF_PG, OFF_Q))

    col = lax.broadcasted_iota(jnp.int32, (1, GQA_GROUP * Q_SUB), 1)
    sink_rows = []
    for kv in range(N_KV_HEADS):
        row = jnp.zeros((1, GQA_GROUP * Q_SUB), jnp.float32)
        for g in range(GQA_GROUP):
            row = jnp.where(col >= g * Q_SUB, sinks_ref[layer, kv * GQA_GROUP + g], row)
        sink_rows.append(row)
    first = (t == 0)
    nt_dims = (((1,), (1,)), ((), ()))
    pad = jnp.zeros((Q_SUB, GQA_GROUP * Q_SUB), jnp.bfloat16)

    def attn_scores(sb):
        r0 = sb * Q_SUB
        qrows = slice(r0, r0 + Q_SUB)
        krows = slice(r0, r0 + KEY_SPAN)
        variant = jnp.where(first, 2 - sb, 0) if sb < 2 else 0
        q4 = jnp.concatenate([q_buf[qrows, g * LANES:(g + 1) * LANES] for g in range(GQA_GROUP)],
                             axis=0)
        kk = jnp.concatenate([kz_buf[0, krows, :], kz_buf[1, krows, :]], axis=0)
        s = lax.dot_general(kk, q4, nt_dims, preferred_element_type=jnp.float32)
        return s + bias_buf[variant]

    def attn_finish(sb, s):
        r0 = sb * Q_SUB
        qrows = slice(r0, r0 + Q_SUB)
        outs = []
        for kv in range(N_KV_HEADS):
            sh = s[kv * KEY_SPAN:(kv + 1) * KEY_SPAN]
            m = jnp.maximum(jnp.max(sh, axis=0, keepdims=True), sink_rows[kv])
            p = jnp.exp(sh - m)
            l = jnp.sum(p, axis=0, keepdims=True) + jnp.exp(sink_rows[kv] - m)
            pb = p.astype(jnp.bfloat16)
            if sb % 2 == 0:
                pb = jnp.concatenate([pb, pad], axis=0)
                c0 = r0
            else:
                pb = jnp.concatenate([pad, pb], axis=0)
                c0 = r0 - Q_SUB
            vt = vt_buf[kv * HEAD_DIM:(kv + 1) * HEAD_DIM, c0:c0 + 2 * LANES]
            o = jnp.dot(vt, pb, preferred_element_type=jnp.float32)
            outs.append(o * (1.0 / l))
        for pr in range(GQA_GROUP // 2):
            both = jnp.concatenate([o[:, pr * LANES:(pr + 1) * LANES] for o in outs], axis=0)
            both_t = both.T
            for half in range(2):
                g = 2 * pr + half
                val = both_t[half * Q_SUB:(half + 1) * Q_SUB] * ag_buf[qrows, g * LANES:(g + 1) * LANES]
                mixa_buf[qrows, g * LANES:(g + 1) * LANES] = val.astype(jnp.bfloat16)

    y_pool = []

    def out_pool_item():
        y_pool.append(jnp.dot(mixp_buf[...], wo_buf[:D_POOL, :],
                              preferred_element_type=jnp.float32))

    attn_gate_item()
    dense_items = [[pool_u_item], [pool_gate_item], [pool_window_item, pool_mix_item], [out_pool_item]]
    group = n_sub // len(dense_items)
    for p, items in enumerate(dense_items):
        subs = range(p * group, (p + 1) * group)
        scores = [attn_scores(sb) for sb in subs]
        for item in items:
            item()
        for sb, s in zip(subs, scores):
            attn_finish(sb, s)

    u_buf[0:POOL_HALO, :] = u_buf[tile:tile + POOL_HALO, :]
    kz_buf[:, 0:WINDOW, :] = kz_buf[:, tile:tile + WINDOW, :]
    vt_buf[:, 0:WINDOW] = vt_buf[:, tile:tile + WINDOW]

    y = y_pool[0] + jnp.dot(mixa_buf[...], wo_buf[D_POOL:, :], preferred_element_type=jnp.float32)
    ms2 = jnp.mean(y * y, axis=-1, keepdims=True)
    o_ref[...] = x_ref[...] + y * lax.rsqrt(ms2 + EPS) * gpost_ref[layer:layer + 1, :]


def _layer(layer, x, w_in, pool_w, pool_scale, sinks, w_out, g_pre, g_post):
    batch, seq, d = x.shape
    tile = SEQ_TILE
    assert seq % tile == 0 and tile % (2 * Q_SUB) == 0 and d == D_MODEL

    def of_layer(*shape):
        return pl.BlockSpec((None,) + shape, lambda b, t: (layer,) + (0,) * len(shape),
                            pipeline_mode=pl.Buffered(1))

    def whole(a):
        return pl.BlockSpec(a.shape, lambda b, t: (0,) * a.ndim)

    return pl.pallas_call(
        functools.partial(_layer_kernel, layer),
        out_shape=jax.ShapeDtypeStruct(x.shape, x.dtype),
        grid=(batch, seq // tile),
        in_specs=[
            pl.BlockSpec((None, tile, d), lambda b, t: (b, t, 0)),
            of_layer(D_MODEL, D_IN),
            of_layer(len(POOL_WINDOWS), POOL_GROUP, POOL_GROUP),
            whole(pool_scale),
            pl.BlockSpec(memory_space=pltpu.SMEM),
            of_layer(D_MODEL, D_MODEL),
            whole(g_pre),
            whole(g_post),
        ],
        out_specs=pl.BlockSpec((None, tile, d), lambda b, t: (b, t, 0)),
        scratch_shapes=[
            pltpu.VMEM((POOL_HALO + tile, D_POOL), jnp.float32),
            pltpu.VMEM((tile, D_ATTN), jnp.bfloat16),
            pltpu.VMEM((N_KV_HEADS, WINDOW + tile, LANES), jnp.bfloat16),
            pltpu.VMEM((LANES, WINDOW + tile), jnp.bfloat16),
            pltpu.VMEM((tile, D_ATTN), jnp.float32),
            pltpu.VMEM((tile, D_POOL), jnp.float32),
            pltpu.VMEM((tile, D_POOL), jnp.bfloat16),
            pltpu.VMEM((tile, D_POOL), jnp.bfloat16),
            pltpu.VMEM((tile, D_ATTN), jnp.bfloat16),
            pltpu.VMEM((3, N_KV_HEADS * KEY_SPAN, GQA_GROUP * Q_SUB), jnp.float32),
            pltpu.VMEM((D_MODEL, D_IN), jnp.bfloat16),
            pltpu.VMEM((D_MODEL, D_MODEL), jnp.bfloat16),
            pltpu.VMEM((len(POOL_WINDOWS) // 2, 2 * POOL_GROUP, 2 * POOL_GROUP), jnp.bfloat16),
        ],
        compiler_params=pltpu.CompilerParams(
            dimension_semantics=("arbitrary", "arbitrary"),
            vmem_limit_bytes=VMEM_LIMIT_BYTES),
        name="hybrid_layer",
    )(x, w_in, pool_w, pool_scale, sinks, w_out, g_pre, g_post)


@jax.jit
def kernel(x, w_in, pool_w, pool_scale, attn_sinks, w_out, norm_pre, norm_post):
    for layer in range(w_in.shape[0]):
        x = _layer(layer, x, w_in, pool_w, pool_scale, attn_sinks, w_out, norm_pre, norm_post)
    return x
```

```python
import functools

import jax
import jax.numpy as jnp
from jax import lax
from jax.experimental import pallas as pl
from jax.experimental.pallas import tpu as pltpu

D_MODEL = 1024
D_POOL = 512
POOL_WINDOWS = (2, 4, 8, 16)
POOL_GROUP = 128
HEAD_DIM = 64
D_ATTN = 512
N_HEADS = 8
N_KV_HEADS = 2
GQA_GROUP = N_HEADS // N_KV_HEADS
WINDOW = 128
D_IN = 2304
EPS = 1e-6
NEG_INF = -1e30
LOG2_E = 1.4426950408889634

OFF_U, OFF_PG, OFF_Q, OFF_K, OFF_V, OFF_AG = 0, 512, 1024, 1536, 1664, 1792

LANES = 128
POOL_HALO = 16
SEQ_TILE = 1024
Q_SUB = 64
KEY_SPAN = Q_SUB + WINDOW
VMEM_LIMIT_BYTES = 56 * 1024 * 1024


def _silu(x):
    return x * (1.0 / (1.0 + jnp.exp(-x)))


def _layer_kernel(layer, x_ref, win_ref, poolw_ref, pscale_ref, sinks_ref, wout_ref,
                  gpre_ref, gpost_ref, o_ref,
                  u_buf, q_buf, kz_buf, vt_buf, ag_buf, pg_buf, pooled_buf, mixp_buf, mixa_buf, bias_buf, wi_buf,
                  wo_buf):
    tile = x_ref.shape[0]
    n_sub = tile // Q_SUB
    t = pl.program_id(1)

    @pl.when((pl.program_id(0) == 0) & (t == 0))
    def _first_step():
        r = lax.broadcasted_iota(jnp.int32, (KEY_SPAN, GQA_GROUP * Q_SUB), 0)
        col = lax.broadcasted_iota(jnp.int32, (KEY_SPAN, GQA_GROUP * Q_SUB), 1)
        dist = (col & (Q_SUB - 1)) + WINDOW - r
        in_win = (dist >= 0) & (dist < WINDOW)
        distf = dist.astype(jnp.float32)
        for kv in range(N_KV_HEADS):
            slope = jnp.zeros_like(distf)
            for g in range(GQA_GROUP):
                j = kv * GQA_GROUP + g
                slope = jnp.where(col >= g * Q_SUB, 2.0 ** (-8.0 * (j + 1) / N_HEADS), slope)
            b = -slope * distf * LOG2_E
            for f in range(3):
                bias_buf[f, kv * KEY_SPAN:(kv + 1) * KEY_SPAN, :] = jnp.where(
                    in_win & (r >= f * Q_SUB), b, NEG_INF)

        a0 = lax.broadcasted_iota(jnp.int32, (D_ATTN, D_ATTN), 0)
        a1 = lax.broadcasted_iota(jnp.int32, (D_ATTN, D_ATTN), 1)

        def source(n):
            return ((n >> 6) & 1) * (GQA_GROUP * HEAD_DIM) + (n >> 7) * HEAD_DIM + (n & (HEAD_DIM - 1))

        perm = jnp.where(a0 == source(a1), 1.0, 0.0).astype(jnp.bfloat16)
        perm_t = jnp.where(a1 == source(a0), 1.0, 0.0).astype(jnp.bfloat16)
        for lo, hi in ((OFF_U, OFF_PG), (OFF_PG, OFF_Q), (OFF_Q, OFF_K), (OFF_K, OFF_AG), (OFF_AG, D_IN)):
            w = win_ref[:, lo:hi].astype(jnp.bfloat16)
            if lo in (OFF_Q, OFF_AG):
                w = jnp.dot(w, perm, preferred_element_type=jnp.float32).astype(jnp.bfloat16)
            wi_buf[:, lo:hi] = w
        wo_buf[:D_POOL, :] = wout_ref[:D_POOL, :].astype(jnp.bfloat16)
        wo_buf[D_POOL:, :] = jnp.dot(perm_t, wout_ref[D_POOL:, :].astype(jnp.bfloat16),
                                     preferred_element_type=jnp.float32).astype(jnp.bfloat16)

    @pl.when(t == 0)
    def _zero_halo():
        u_buf[0:POOL_HALO, :] = jnp.zeros((POOL_HALO, D_POOL), jnp.float32)
        kz_buf[:, 0:WINDOW, :] = jnp.zeros((N_KV_HEADS, WINDOW, LANES), jnp.bfloat16)
        vt_buf[:, 0:WINDOW] = jnp.zeros((LANES, WINDOW), jnp.bfloat16)

    x = x_ref[...]
    ms = jnp.mean(x * x, axis=-1, keepdims=True)
    h = (x * lax.rsqrt(ms + EPS) * gpre_ref[layer:layer + 1, :]).astype(jnp.bfloat16)

    def proj(lo, hi):
        return jnp.dot(h, wi_buf[:, lo:hi], preferred_element_type=jnp.float32)

    kvf = proj(OFF_K, OFF_AG)
    q_buf[...] = (proj(OFF_Q, OFF_K) * (HEAD_DIM ** -0.5 * LOG2_E)).astype(jnp.bfloat16)
    kf, vf = kvf[:, :LANES], kvf[:, LANES:]
    lane = lax.broadcasted_iota(jnp.int32, (1, LANES), 1)
    kz_buf[0, WINDOW:WINDOW + tile, :] = jnp.where(lane < HEAD_DIM, kf, 0.0).astype(jnp.bfloat16)
    kz_buf[1, WINDOW:WINDOW + tile, :] = jnp.where(lane >= HEAD_DIM, kf, 0.0).astype(jnp.bfloat16)
    vt_buf[:, WINDOW:WINDOW + tile] = vf.T.astype(jnp.bfloat16)

    def attn_gate_item():
        ag_buf[...] = _silu(proj(OFF_AG, D_IN))

    def pool_u_item():
        u_buf[POOL_HALO:POOL_HALO + tile, :] = proj(OFF_U, OFF_PG)

    pos1 = (lax.broadcasted_iota(jnp.int32, (tile, POOL_GROUP), 0) + (t * tile + 1)
            ).astype(jnp.float32)

    def pool_window_item():
        for g, w in enumerate(POOL_WINDOWS):
            cols = slice(g * POOL_GROUP, (g + 1) * POOL_GROUP)
            ext = u_buf[:, cols]
            acc = ext
            shift = 1
            while shift < w:
                acc = acc + pltpu.roll(acc, shift, axis=0)
                shift *= 2
            cur = ext[POOL_HALO:]
            pooled = acc[POOL_HALO:] / jnp.minimum(pos1, float(w)) - cur
            pooled_buf[:, cols] = pooled.astype(jnp.bfloat16)

    def pool_mix_item():
        for g in range(len(POOL_WINDOWS)):
            cols = slice(g * POOL_GROUP, (g + 1) * POOL_GROUP)
            mixed = jnp.dot(pooled_buf[:, cols], poolw_ref[g].astype(jnp.bfloat16),
                            preferred_element_type=jnp.float32)
            scale = pscale_ref[layer:layer + 1, cols]
            mixp_buf[:, cols] = (mixed * scale * pg_buf[:, cols]).astype(jnp.bfloat16)

    def pool_gate_item():
        pg_buf[...] = _silu(proj(OFF_PG, OFF_Q))

    col = lax.broadcasted_iota(jnp.int32, (1, GQA_GROUP * Q_SUB), 1)
    sink_rows = []
    for kv in range(N_KV_HEADS):
        row = jnp.zeros((1, GQA_GROUP * Q_SUB), jnp.float32)
        for g in range(GQA_GROUP):
            row = jnp.where(col >= g * Q_SUB, sinks_ref[layer, kv * GQA_GROUP + g], row)
        sink_rows.append(row * LOG2_E)
    first = (t == 0)
    nt_dims = (((1,), (1,)), ((), ()))
    pad = jnp.zeros((Q_SUB, GQA_GROUP * Q_SUB), jnp.bfloat16)

    def attn_scores(sb):
        r0 = sb * Q_SUB
        qrows = slice(r0, r0 + Q_SUB)
        krows = slice(r0, r0 + KEY_SPAN)
        variant = jnp.where(first, 2 - sb, 0) if sb < 2 else 0
        q4 = jnp.concatenate([q_buf[qrows, g * LANES:(g + 1) * LANES] for g in range(GQA_GROUP)],
                             axis=0)
        kk = jnp.concatenate([kz_buf[0, krows, :], kz_buf[1, krows, :]], axis=0)
        s = lax.dot_general(kk, q4, nt_dims, preferred_element_type=jnp.float32)
        return s + bias_buf[variant]

    def attn_finish(sb, s):
        r0 = sb * Q_SUB
        qrows = slice(r0, r0 + Q_SUB)
        outs = []
        for kv in range(N_KV_HEADS):
            sh = s[kv * KEY_SPAN:(kv + 1) * KEY_SPAN]
            m = jnp.maximum(jnp.max(sh, axis=0, keepdims=True), sink_rows[kv])
            p = jnp.exp2(sh - m)
            l = jnp.sum(p, axis=0, keepdims=True) + jnp.exp2(sink_rows[kv] - m)
            pb = p.astype(jnp.bfloat16)
            if sb % 2 == 0:
                pb = jnp.concatenate([pb, pad], axis=0)
                c0 = r0
            else:
                pb = jnp.concatenate([pad, pb], axis=0)
                c0 = r0 - Q_SUB
            vt = vt_buf[kv * HEAD_DIM:(kv + 1) * HEAD_DIM, c0:c0 + 2 * LANES]
            o = jnp.dot(vt, pb, preferred_element_type=jnp.float32)
            outs.append(o * (1.0 / l))
        for pr in range(GQA_GROUP // 2):
            both = jnp.concatenate([o[:, pr * LANES:(pr + 1) * LANES] for o in outs], axis=0)
            both_t = both.T
            for half in range(2):
                g = 2 * pr + half
                val = both_t[half * Q_SUB:(half + 1) * Q_SUB] * ag_buf[qrows, g * LANES:(g + 1) * LANES]
                mixa_buf[qrows, g * LANES:(g + 1) * LANES] = val.astype(jnp.bfloat16)

    y_pool = []

    def out_pool_item():
        y_pool.append(jnp.dot(mixp_buf[...], wo_buf[:D_POOL, :],
                              preferred_element_type=jnp.float32))

    attn_gate_item()
    dense_items = [[pool_u_item], [pool_gate_item], [pool_window_item, pool_mix_item], [out_pool_item]]
    group = n_sub // len(dense_items)
    for p, items in enumerate(dense_items):
        subs = range(p * group, (p + 1) * group)
        scores = [attn_scores(sb) for sb in subs]
        for item in items:
            item()
        for sb, s in zip(subs, scores):
            attn_finish(sb, s)

    u_buf[0:POOL_HALO, :] = u_buf[tile:tile + POOL_HALO, :]
    kz_buf[:, 0:WINDOW, :] = kz_buf[:, tile:tile + WINDOW, :]
    vt_buf[:, 0:WINDOW] = vt_buf[:, tile:tile + WINDOW]

    y = y_pool[0] + jnp.dot(mixa_buf[...], wo_buf[D_POOL:, :], preferred_element_type=jnp.float32)
    ms2 = jnp.mean(y * y, axis=-1, keepdims=True)
    o_ref[...] = x_ref[...] + y * lax.rsqrt(ms2 + EPS) * gpost_ref[layer:layer + 1, :]


def _layer(layer, x, w_in, pool_w, pool_scale, sinks, w_out, g_pre, g_post):
    batch, seq, d = x.shape
    tile = SEQ_TILE
    assert seq % tile == 0 and tile % (2 * Q_SUB) == 0 and d == D_MODEL

    def of_layer(*shape):
        return pl.BlockSpec((None,) + shape, lambda b, t: (layer,) + (0,) * len(shape),
                            pipeline_mode=pl.Buffered(1))

    def whole(a):
        return pl.BlockSpec(a.shape, lambda b, t: (0,) * a.ndim)

    return pl.pallas_call(
        functools.partial(_layer_kernel, layer),
        out_shape=jax.ShapeDtypeStruct(x.shape, x.dtype),
        grid=(batch, seq // tile),
        in_specs=[
            pl.BlockSpec((None, tile, d), lambda b, t: (b, t, 0)),
            of_layer(D_MODEL, D_IN),
            of_layer(len(POOL_WINDOWS), POOL_GROUP, POOL_GROUP),
            whole(pool_scale),
            pl.BlockSpec(memory_space=pltpu.SMEM),
            of_layer(D_MODEL, D_MODEL),
            whole(g_pre),
            whole(g_post),
        ],
        out_specs=pl.BlockSpec((None, tile, d), lambda b, t: (b, t, 0)),
        scratch_shapes=[
            pltpu.VMEM((POOL_HALO + tile, D_POOL), jnp.float32),
            pltpu.VMEM((tile, D_ATTN), jnp.bfloat16),
            pltpu.VMEM((N_KV_HEADS, WINDOW + tile, LANES), jnp.bfloat16),
            pltpu.VMEM((LANES, WINDOW + tile), jnp.bfloat16),
            pltpu.VMEM((tile, D_ATTN), jnp.float32),
            pltpu.VMEM((tile, D_POOL), jnp.float32),
            pltpu.VMEM((tile, D_POOL), jnp.bfloat16),
            pltpu.VMEM((tile, D_POOL), jnp.bfloat16),
            pltpu.VMEM((tile, D_ATTN), jnp.bfloat16),
            pltpu.VMEM((3, N_KV_HEADS * KEY_SPAN, GQA_GROUP * Q_SUB), jnp.float32),
            pltpu.VMEM((D_MODEL, D_IN), jnp.bfloat16),
            pltpu.VMEM((D_MODEL, D_MODEL), jnp.bfloat16),
        ],
        compiler_params=pltpu.CompilerParams(
            dimension_semantics=("arbitrary", "arbitrary"),
            vmem_limit_bytes=VMEM_LIMIT_BYTES),
        name="hybrid_layer",
    )(x, w_in, pool_w, pool_scale, sinks, w_out, g_pre, g_post)


@jax.jit
def kernel(x, w_in, pool_w, pool_scale, attn_sinks, w_out, norm_pre, norm_post):
    for layer in range(w_in.shape[0]):
        x = _layer(layer, x, w_in, pool_w, pool_scale, attn_sinks, w_out, norm_pre, norm_post)
    return x
```

```python
import functools

import jax
import jax.numpy as jnp
from jax import lax
from jax.experimental import pallas as pl
from jax.experimental.pallas import tpu as pltpu

D_MODEL = 1024
D_POOL = 512
POOL_WINDOWS = (2, 4, 8, 16)
POOL_GROUP = 128
HEAD_DIM = 64
D_ATTN = 512
N_HEADS = 8
N_KV_HEADS = 2
GQA_GROUP = N_HEADS // N_KV_HEADS
WINDOW = 128
D_IN = 2304
EPS = 1e-6
NEG_INF = -1e30
LOG2_E = 1.4426950408889634

OFF_U, OFF_PG, OFF_Q, OFF_K, OFF_V, OFF_AG = 0, 512, 1024, 1536, 1664, 1792

LANES = 128
POOL_HALO = 16
SEQ_TILE = 1024
Q_SUB = 64
KEY_SPAN = Q_SUB + WINDOW
VMEM_LIMIT_BYTES = 56 * 1024 * 1024


def _silu(x):
    return x * (1.0 / (1.0 + jnp.exp(-x)))


def _layer_kernel(layer, x_ref, win_ref, poolw_ref, pscale_ref, sinks_ref, wout_ref,
                  gpre_ref, gpost_ref, o_ref,
                  u_buf, q_buf, kz_buf, vt_buf, ag_buf, pg_buf, pooled_buf, mixp_buf, mixa_buf, bias_buf, wi_buf,
                  wo_buf):
    tile = x_ref.shape[0]
    n_sub = tile // Q_SUB
    t = pl.program_id(1)

    @pl.when((pl.program_id(0) == 0) & (t == 0))
    def _first_step():
        r = lax.broadcasted_iota(jnp.int32, (KEY_SPAN, GQA_GROUP * Q_SUB), 0)
        col = lax.broadcasted_iota(jnp.int32, (KEY_SPAN, GQA_GROUP * Q_SUB), 1)
        dist = (col & (Q_SUB - 1)) + WINDOW - r
        in_win = (dist >= 0) & (dist < WINDOW)
        distf = dist.astype(jnp.float32)
        for kv in range(N_KV_HEADS):
            slope = jnp.zeros_like(distf)
            for g in range(GQA_GROUP):
                j = kv * GQA_GROUP + g
                slope = jnp.where(col >= g * Q_SUB, 2.0 ** (-8.0 * (j + 1) / N_HEADS), slope)
            b = -slope * distf * LOG2_E
            for f in range(3):
                bias_buf[f, kv * KEY_SPAN:(kv + 1) * KEY_SPAN, :] = jnp.where(
                    in_win & (r >= f * Q_SUB), b, NEG_INF)

        a0 = lax.broadcasted_iota(jnp.int32, (D_ATTN, D_ATTN), 0)
        a1 = lax.broadcasted_iota(jnp.int32, (D_ATTN, D_ATTN), 1)

        def source(n):
            return ((n >> 6) & 1) * (GQA_GROUP * HEAD_DIM) + (n >> 7) * HEAD_DIM + (n & (HEAD_DIM - 1))

        perm = jnp.where(a0 == source(a1), 1.0, 0.0).astype(jnp.bfloat16)
        perm_t = jnp.where(a1 == source(a0), 1.0, 0.0).astype(jnp.bfloat16)
        eye = (lax.broadcasted_iota(jnp.int32, (LANES, LANES), 0)
               == lax.broadcasted_iota(jnp.int32, (LANES, LANES), 1))
        gain_col = jnp.concatenate(
            [jnp.sum(jnp.where(eye, gpre_ref[layer:layer + 1, j:j + LANES], 0.0), axis=1, keepdims=True)
             for j in range(0, D_MODEL, LANES)], axis=0)
        for lo, hi in ((OFF_U, OFF_PG), (OFF_PG, OFF_Q), (OFF_Q, OFF_K), (OFF_K, OFF_AG), (OFF_AG, D_IN)):
            w = (win_ref[:, lo:hi] * gain_col).astype(jnp.bfloat16)
            if lo in (OFF_Q, OFF_AG):
                w = jnp.dot(w, perm, preferred_element_type=jnp.float32).astype(jnp.bfloat16)
            wi_buf[:, lo:hi] = w
        wo_buf[:D_POOL, :] = wout_ref[:D_POOL, :].astype(jnp.bfloat16)
        wo_buf[D_POOL:, :] = jnp.dot(perm_t, wout_ref[D_POOL:, :].astype(jnp.bfloat16),
                                     preferred_element_type=jnp.float32).astype(jnp.bfloat16)

    @pl.when(t == 0)
    def _zero_halo():
        u_buf[0:POOL_HALO, :] = jnp.zeros((POOL_HALO, D_POOL), jnp.float32)
        kz_buf[:, 0:WINDOW, :] = jnp.zeros((N_KV_HEADS, WINDOW, LANES), jnp.bfloat16)
        vt_buf[:, 0:WINDOW] = jnp.zeros((LANES, WINDOW), jnp.bfloat16)

    x = x_ref[...]
    ms = jnp.mean(x * x, axis=-1, keepdims=True)
    h = (x * lax.rsqrt(ms + EPS)).astype(jnp.bfloat16)

    def proj(lo, hi):
        return jnp.dot(h, wi_buf[:, lo:hi], preferred_element_type=jnp.float32)

    kvf = proj(OFF_K, OFF_AG)
    q_buf[...] = (proj(OFF_Q, OFF_K) * (HEAD_DIM ** -0.5 * LOG2_E)).astype(jnp.bfloat16)
    kf, vf = kvf[:, :LANES], kvf[:, LANES:]
    lane = lax.broadcasted_iota(jnp.int32, (1, LANES), 1)
    kz_buf[0, WINDOW:WINDOW + tile, :] = jnp.where(lane < HEAD_DIM, kf, 0.0).astype(jnp.bfloat16)
    kz_buf[1, WINDOW:WINDOW + tile, :] = jnp.where(lane >= HEAD_DIM, kf, 0.0).astype(jnp.bfloat16)
    vt_buf[:, WINDOW:WINDOW + tile] = vf.T.astype(jnp.bfloat16)

    def attn_gate_item():
        ag_buf[...] = _silu(proj(OFF_AG, D_IN))

    def pool_u_item():
        u_buf[POOL_HALO:POOL_HALO + tile, :] = proj(OFF_U, OFF_PG)

    pos1 = (lax.broadcasted_iota(jnp.int32, (tile, POOL_GROUP), 0) + (t * tile + 1)
            ).astype(jnp.float32)

    def pool_window_item():
        for g, w in enumerate(POOL_WINDOWS):
            cols = slice(g * POOL_GROUP, (g + 1) * POOL_GROUP)
            ext = u_buf[:, cols]
            acc = ext
            shift = 1
            while shift < w:
                acc = acc + pltpu.roll(acc, shift, axis=0)
                shift *= 2
            cur = ext[POOL_HALO:]
            pooled = acc[POOL_HALO:] / jnp.minimum(pos1, float(w)) - cur
            pooled_buf[:, cols] = pooled.astype(jnp.bfloat16)

    def pool_mix_item():
        for g in range(len(POOL_WINDOWS)):
            cols = slice(g * POOL_GROUP, (g + 1) * POOL_GROUP)
            mixed = jnp.dot(pooled_buf[:, cols], poolw_ref[g].astype(jnp.bfloat16),
                            preferred_element_type=jnp.float32)
            scale = pscale_ref[layer:layer + 1, cols]
            mixp_buf[:, cols] = (mixed * scale * pg_buf[:, cols]).astype(jnp.bfloat16)

    def pool_gate_item():
        pg_buf[...] = _silu(proj(OFF_PG, OFF_Q))

    col = lax.broadcasted_iota(jnp.int32, (1, GQA_GROUP * Q_SUB), 1)
    sink_rows = []
    for kv in range(N_KV_HEADS):
        row = jnp.zeros((1, GQA_GROUP * Q_SUB), jnp.float32)
        for g in range(GQA_GROUP):
            row = jnp.where(col >= g * Q_SUB, sinks_ref[layer, kv * GQA_GROUP + g], row)
        sink_rows.append(row * LOG2_E)
    first = (t == 0)
    nt_dims = (((1,), (1,)), ((), ()))
    pad = jnp.zeros((Q_SUB, GQA_GROUP * Q_SUB), jnp.bfloat16)

    def attn_scores(sb):
        r0 = sb * Q_SUB
        qrows = slice(r0, r0 + Q_SUB)
        krows = slice(r0, r0 + KEY_SPAN)
        variant = jnp.where(first, 2 - sb, 0) if sb < 2 else 0
        q4 = jnp.concatenate([q_buf[qrows, g * LANES:(g + 1) * LANES] for g in range(GQA_GROUP)],
                             axis=0)
        kk = jnp.concatenate([kz_buf[0, krows, :], kz_buf[1, krows, :]], axis=0)
        s = lax.dot_general(kk, q4, nt_dims, preferred_element_type=jnp.float32)
        return s + bias_buf[variant]

    def attn_finish(sb, s):
        r0 = sb * Q_SUB
        qrows = slice(r0, r0 + Q_SUB)
        outs = []
        for kv in range(N_KV_HEADS):
            sh = s[kv * KEY_SPAN:(kv + 1) * KEY_SPAN]
            m = jnp.maximum(jnp.max(sh, axis=0, keepdims=True), sink_rows[kv])
            p = jnp.exp2(sh - m)
            l = jnp.sum(p, axis=0, keepdims=True) + jnp.exp2(sink_rows[kv] - m)
            pb = p.astype(jnp.bfloat16)
            if sb % 2 == 0:
                pb = jnp.concatenate([pb, pad], axis=0)
                c0 = r0
            else:
                pb = jnp.concatenate([pad, pb], axis=0)
                c0 = r0 - Q_SUB
            vt = vt_buf[kv * HEAD_DIM:(kv + 1) * HEAD_DIM, c0:c0 + 2 * LANES]
            o = jnp.dot(vt, pb, preferred_element_type=jnp.float32)
            outs.append(o * (1.0 / l))
        for pr in range(GQA_GROUP // 2):
            both = jnp.concatenate([o[:, pr * LANES:(pr + 1) * LANES] for o in outs], axis=0)
            both_t = both.T
            for half in range(2):
                g = 2 * pr + half
                val = both_t[half * Q_SUB:(half + 1) * Q_SUB] * ag_buf[qrows, g * LANES:(g + 1) * LANES]
                mixa_buf[qrows, g * LANES:(g + 1) * LANES] = val.astype(jnp.bfloat16)

    y_pool = []

    def out_pool_item():
        y_pool.append(jnp.dot(mixp_buf[...], wo_buf[:D_POOL, :],
                              preferred_element_type=jnp.float32))

    attn_gate_item()
    dense_items = [[pool_u_item], [pool_gate_item], [pool_window_item, pool_mix_item], [out_pool_item]]
    group = n_sub // len(dense_items)
    for p, items in enumerate(dense_items):
        subs = range(p * group, (p + 1) * group)
        scores = [attn_scores(sb) for sb in subs]
        for item in items:
            item()
        for sb, s in zip(subs, scores):
            attn_finish(sb, s)

    u_buf[0:POOL_HALO, :] = u_buf[tile:tile + POOL_HALO, :]
    kz_buf[:, 0:WINDOW, :] = kz_buf[:, tile:tile + WINDOW, :]
    vt_buf[:, 0:WINDOW] = vt_buf[:, tile:tile + WINDOW]

    y = y_pool[0] + jnp.dot(mixa_buf[...], wo_buf[D_POOL:, :], preferred_element_type=jnp.float32)
    ms2 = jnp.mean(y * y, axis=-1, keepdims=True)
    o_ref[...] = x_ref[...] + y * lax.rsqrt(ms2 + EPS) * gpost_ref[layer:layer + 1, :]


def _layer(layer, x, w_in, pool_w, pool_scale, sinks, w_out, g_pre, g_post):
    batch, seq, d = x.shape
    tile = SEQ_TILE
    assert seq % tile == 0 and tile % (2 * Q_SUB) == 0 and d == D_MODEL

    def of_layer(*shape):
        return pl.BlockSpec((None,) + shape, lambda b, t: (layer,) + (0,) * len(shape),
                            pipeline_mode=pl.Buffered(1))

    def whole(a):
        return pl.BlockSpec(a.shape, lambda b, t: (0,) * a.ndim)

    return pl.pallas_call(
        functools.partial(_layer_kernel, layer),
        out_shape=jax.ShapeDtypeStruct(x.shape, x.dtype),
        grid=(batch, seq // tile),
        in_specs=[
            pl.BlockSpec((None, tile, d), lambda b, t: (b, t, 0)),
            of_layer(D_MODEL, D_IN),
            of_layer(len(POOL_WINDOWS), POOL_GROUP, POOL_GROUP),
            whole(pool_scale),
            pl.BlockSpec(memory_space=pltpu.SMEM),
            of_layer(D_MODEL, D_MODEL),
            whole(g_pre),
            whole(g_post),
        ],
        out_specs=pl.BlockSpec((None, tile, d), lambda b, t: (b, t, 0)),
        scratch_shapes=[
            pltpu.VMEM((POOL_HALO + tile, D_POOL), jnp.float32),
            pltpu.VMEM((tile, D_ATTN), jnp.bfloat16),
            pltpu.VMEM((N_KV_HEADS, WINDOW + tile, LANES), jnp.bfloat16),
            pltpu.VMEM((LANES, WINDOW + tile), jnp.bfloat16),
            pltpu.VMEM((tile, D_ATTN), jnp.float32),
            pltpu.VMEM((tile, D_POOL), jnp.float32),
            pltpu.VMEM((tile, D_POOL), jnp.bfloat16),
            pltpu.VMEM((tile, D_POOL), jnp.bfloat16),
            pltpu.VMEM((tile, D_ATTN), jnp.bfloat16),
            pltpu.VMEM((3, N_KV_HEADS * KEY_SPAN, GQA_GROUP * Q_SUB), jnp.float32),
            pltpu.VMEM((D_MODEL, D_IN), jnp.bfloat16),
            pltpu.VMEM((D_MODEL, D_MODEL), jnp.bfloat16),
        ],
        compiler_params=pltpu.CompilerParams(
            dimension_semantics=("arbitrary", "arbitrary"),
            vmem_limit_bytes=VMEM_LIMIT_BYTES),
        name="hybrid_layer",
    )(x, w_in, pool_w, pool_scale, sinks, w_out, g_pre, g_post)


@jax.jit
def kernel(x, w_in, pool_w, pool_scale, attn_sinks, w_out, norm_pre, norm_post):
    for layer in range(w_in.shape[0]):
        x = _layer(layer, x, w_in, pool_w, pool_scale, attn_sinks, w_out, norm_pre, norm_post)
    return x
```

```python
import functools

import jax
import jax.numpy as jnp
from jax import lax
from jax.experimental import pallas as pl
from jax.experimental.pallas import tpu as pltpu

D_MODEL = 1024
D_POOL = 512
POOL_WINDOWS = (2, 4, 8, 16)
POOL_GROUP = 128
HEAD_DIM = 64
D_ATTN = 512
N_HEADS = 8
N_KV_HEADS = 2
GQA_GROUP = N_HEADS // N_KV_HEADS
WINDOW = 128
D_IN = 2304
EPS = 1e-6
NEG_INF = -1e30
LOG2_E = 1.4426950408889634

OFF_U, OFF_PG, OFF_Q, OFF_K, OFF_V, OFF_AG = 0, 512, 1024, 1536, 1664, 1792

LANES = 128
BF16_ROWS = 16
POOL_HALO = 16
SEQ_TILE = 1024
Q_SUB = 64
KEY_SPAN = Q_SUB + WINDOW
VMEM_LIMIT_BYTES = 56 * 1024 * 1024


def _silu(x):
    return x * (1.0 / (1.0 + jnp.exp(-x)))


def _layer_kernel(layer, x_ref, win_ref, poolw_ref, pscale_ref, sinks_ref, wout_ref,
                  gpre_ref, gpost_ref, o_ref,
                  u_buf, q_buf, kz_buf, vt_buf, ag_buf, pg_buf, pooled_buf, mixp_buf, mixa_buf, bias_buf, wi_buf,
                  wo_buf):
    tile = x_ref.shape[0]
    n_sub = tile // Q_SUB
    t = pl.program_id(1)

    @pl.when((pl.program_id(0) == 0) & (t == 0))
    def _first_step():
        r = lax.broadcasted_iota(jnp.int32, (KEY_SPAN, GQA_GROUP * Q_SUB), 0)
        col = lax.broadcasted_iota(jnp.int32, (KEY_SPAN, GQA_GROUP * Q_SUB), 1)
        dist = (col & (Q_SUB - 1)) + WINDOW - r
        in_win = (dist >= 0) & (dist < WINDOW)
        distf = dist.astype(jnp.float32)
        for kv in range(N_KV_HEADS):
            slope = jnp.zeros_like(distf)
            for g in range(GQA_GROUP):
                j = kv * GQA_GROUP + g
                slope = jnp.where(col >= g * Q_SUB, 2.0 ** (-8.0 * (j + 1) / N_HEADS), slope)
            b = -slope * distf * LOG2_E
            for f in range(3):
                bias_buf[f, kv * KEY_SPAN:(kv + 1) * KEY_SPAN, :] = jnp.where(
                    in_win & (r >= f * Q_SUB), b, NEG_INF)

        a0 = lax.broadcasted_iota(jnp.int32, (D_ATTN, D_ATTN), 0)
        a1 = lax.broadcasted_iota(jnp.int32, (D_ATTN, D_ATTN), 1)

        def source(n):
            return ((n >> 6) & 1) * (GQA_GROUP * HEAD_DIM) + (n >> 7) * HEAD_DIM + (n & (HEAD_DIM - 1))

        perm = jnp.where(a0 == source(a1), 1.0, 0.0).astype(jnp.bfloat16)
        perm_t = jnp.where(a1 == source(a0), 1.0, 0.0).astype(jnp.bfloat16)
        for lo, hi in ((OFF_U, OFF_PG), (OFF_PG, OFF_Q), (OFF_Q, OFF_K), (OFF_K, OFF_AG), (OFF_AG, D_IN)):
            w = win_ref[:, lo:hi].astype(jnp.bfloat16)
            if lo in (OFF_Q, OFF_AG):
                w = jnp.dot(w, perm, preferred_element_type=jnp.float32).astype(jnp.bfloat16)
            wi_buf[:, lo:hi] = w
        wo_buf[:D_POOL, :] = wout_ref[:D_POOL, :].astype(jnp.bfloat16)
        wo_buf[D_POOL:, :] = jnp.dot(perm_t, wout_ref[D_POOL:, :].astype(jnp.bfloat16),
                                     preferred_element_type=jnp.float32).astype(jnp.bfloat16)

    @pl.when(t == 0)
    def _zero_halo():
        u_buf[0:POOL_HALO, :] = jnp.zeros((POOL_HALO, D_POOL), jnp.float32)
        kz_buf[:, 0:WINDOW, :] = jnp.zeros((N_KV_HEADS, WINDOW, LANES), jnp.bfloat16)
        vt_buf[:, 0:WINDOW] = jnp.zeros((LANES, WINDOW), jnp.bfloat16)

    x = x_ref[...]
    ms = jnp.mean(x * x, axis=-1, keepdims=True)
    h = (x * lax.rsqrt(ms + EPS) * gpre_ref[layer:layer + 1, :]).astype(jnp.bfloat16)

    def proj(lo, hi):
        return jnp.dot(h, wi_buf[:, lo:hi], preferred_element_type=jnp.float32)

    kvf = proj(OFF_K, OFF_AG)
    q_buf[...] = (proj(OFF_Q, OFF_K) * (HEAD_DIM ** -0.5 * LOG2_E)).astype(jnp.bfloat16)
    kf, vf = kvf[:, :LANES], kvf[:, LANES:]
    lane = lax.broadcasted_iota(jnp.int32, (1, LANES), 1)
    kz_buf[0, WINDOW:WINDOW + tile, :] = jnp.where(lane < HEAD_DIM, kf, 0.0).astype(jnp.bfloat16)
    kz_buf[1, WINDOW:WINDOW + tile, :] = jnp.where(lane >= HEAD_DIM, kf, 0.0).astype(jnp.bfloat16)
    vt_buf[:, WINDOW:WINDOW + tile] = vf.T.astype(jnp.bfloat16)

    def attn_gate_item():
        ag_buf[...] = _silu(proj(OFF_AG, D_IN))

    def pool_u_item():
        u_buf[POOL_HALO:POOL_HALO + tile, :] = proj(OFF_U, OFF_PG)

    pos1 = (lax.broadcasted_iota(jnp.int32, (tile, POOL_GROUP), 0) + (t * tile + 1)
            ).astype(jnp.float32)

    def pool_window_item():
        for g, w in enumerate(POOL_WINDOWS):
            cols = slice(g * POOL_GROUP, (g + 1) * POOL_GROUP)
            ext = u_buf[:, cols]
            acc = ext
            shift = 1
            while shift < w:
                acc = acc + pltpu.roll(acc, shift, axis=0)
                shift *= 2
            cur = ext[POOL_HALO:]
            pooled = acc[POOL_HALO:] / jnp.minimum(pos1, float(w)) - cur
            pooled_buf[:, cols] = pooled.astype(jnp.bfloat16)

    def pool_mix_item():
        for g in range(len(POOL_WINDOWS)):
            cols = slice(g * POOL_GROUP, (g + 1) * POOL_GROUP)
            mixed = jnp.dot(pooled_buf[:, cols], poolw_ref[g].astype(jnp.bfloat16),
                            preferred_element_type=jnp.float32)
            scale = pscale_ref[layer:layer + 1, cols]
            mixp_buf[:, cols] = (mixed * scale * pg_buf[:, cols]).astype(jnp.bfloat16)

    def pool_gate_item():
        pg_buf[...] = _silu(proj(OFF_PG, OFF_Q))

    col = lax.broadcasted_iota(jnp.int32, (1, GQA_GROUP * Q_SUB), 1)
    sink_rows = []
    for kv in range(N_KV_HEADS):
        row = jnp.zeros((1, GQA_GROUP * Q_SUB), jnp.float32)
        for g in range(GQA_GROUP):
            row = jnp.where(col >= g * Q_SUB, sinks_ref[layer, kv * GQA_GROUP + g], row)
        sink_rows.append(row * LOG2_E)
    first = (t == 0)
    nt_dims = (((1,), (1,)), ((), ()))
    pad = jnp.zeros((Q_SUB, GQA_GROUP * Q_SUB), jnp.bfloat16)
    ones_rows = jnp.ones((BF16_ROWS, 2 * LANES), jnp.bfloat16)

    def attn_scores(sb):
        r0 = sb * Q_SUB
        qrows = slice(r0, r0 + Q_SUB)
        krows = slice(r0, r0 + KEY_SPAN)
        variant = jnp.where(first, 2 - sb, 0) if sb < 2 else 0
        q4 = jnp.concatenate([q_buf[qrows, g * LANES:(g + 1) * LANES] for g in range(GQA_GROUP)],
                             axis=0)
        kk = jnp.concatenate([kz_buf[0, krows, :], kz_buf[1, krows, :]], axis=0)
        s = lax.dot_general(kk, q4, nt_dims, preferred_element_type=jnp.float32)
        return s + bias_buf[variant]

    def attn_finish(sb, s):
        r0 = sb * Q_SUB
        qrows = slice(r0, r0 + Q_SUB)
        outs = []
        for kv in range(N_KV_HEADS):
            sh = s[kv * KEY_SPAN:(kv + 1) * KEY_SPAN]
            m = jnp.maximum(jnp.max(sh, axis=0, keepdims=True), sink_rows[kv])
            pb = jnp.exp2(sh - m).astype(jnp.bfloat16)
            if sb % 2 == 0:
                pb = jnp.concatenate([pb, pad], axis=0)
                c0 = r0
            else:
                pb = jnp.concatenate([pad, pb], axis=0)
                c0 = r0 - Q_SUB
            vt = jnp.concatenate([vt_buf[kv * HEAD_DIM:(kv + 1) * HEAD_DIM, c0:c0 + 2 * LANES], ones_rows],
                                 axis=0)
            o = jnp.dot(vt, pb, preferred_element_type=jnp.float32)
            l = o[HEAD_DIM:HEAD_DIM + 1] + jnp.exp2(sink_rows[kv] - m)
            outs.append(o[:HEAD_DIM] * (1.0 / l))
        for pr in range(GQA_GROUP // 2):
            both = jnp.concatenate([o[:, pr * LANES:(pr + 1) * LANES] for o in outs], axis=0)
            both_t = both.T
            for half in range(2):
                g = 2 * pr + half
                val = both_t[half * Q_SUB:(half + 1) * Q_SUB] * ag_buf[qrows, g * LANES:(g + 1) * LANES]
                mixa_buf[qrows, g * LANES:(g + 1) * LANES] = val.astype(jnp.bfloat16)

    attn_gate_item()
    dense_items = [[pool_u_item], [pool_gate_item], [pool_window_item, pool_mix_item], []]
    group = n_sub // len(dense_items)
    for p, items in enumerate(dense_items):
        subs = range(p * group, (p + 1) * group)
        scores = [attn_scores(sb) for sb in subs]
        for item in items:
            item()
        for sb, s in zip(subs, scores):
            attn_finish(sb, s)

    u_buf[0:POOL_HALO, :] = u_buf[tile:tile + POOL_HALO, :]
    kz_buf[:, 0:WINDOW, :] = kz_buf[:, tile:tile + WINDOW, :]
    vt_buf[:, 0:WINDOW] = vt_buf[:, tile:tile + WINDOW]

    y = (jnp.dot(mixp_buf[...], wo_buf[:D_POOL, :], preferred_element_type=jnp.float32)
         + jnp.dot(mixa_buf[...], wo_buf[D_POOL:, :], preferred_element_type=jnp.float32))
    ms2 = jnp.mean(y * y, axis=-1, keepdims=True)
    o_ref[...] = x_ref[...] + y * lax.rsqrt(ms2 + EPS) * gpost_ref[layer:layer + 1, :]


def _layer(layer, x, w_in, pool_w, pool_scale, sinks, w_out, g_pre, g_post):
    batch, seq, d = x.shape
    tile = SEQ_TILE
    assert seq % tile == 0 and tile % (2 * Q_SUB) == 0 and d == D_MODEL

    def of_layer(*shape):
        return pl.BlockSpec((None,) + shape, lambda b, t: (layer,) + (0,) * len(shape),
                            pipeline_mode=pl.Buffered(1))

    def whole(a):
        return pl.BlockSpec(a.shape, lambda b, t: (0,) * a.ndim)

    return pl.pallas_call(
        functools.partial(_layer_kernel, layer),
        out_shape=jax.ShapeDtypeStruct(x.shape, x.dtype),
        grid=(batch, seq // tile),
        in_specs=[
            pl.BlockSpec((None, tile, d), lambda b, t: (b, t, 0)),
            of_layer(D_MODEL, D_IN),
            of_layer(len(POOL_WINDOWS), POOL_GROUP, POOL_GROUP),
            whole(pool_scale),
            pl.BlockSpec(memory_space=pltpu.SMEM),
            of_layer(D_MODEL, D_MODEL),
            whole(g_pre),
            whole(g_post),
        ],
        out_specs=pl.BlockSpec((None, tile, d), lambda b, t: (b, t, 0)),
        scratch_shapes=[
            pltpu.VMEM((POOL_HALO + tile, D_POOL), jnp.float32),
            pltpu.VMEM((tile, D_ATTN), jnp.bfloat16),
            pltpu.VMEM((N_KV_HEADS, WINDOW + tile, LANES), jnp.bfloat16),
            pltpu.VMEM((LANES, WINDOW + tile), jnp.bfloat16),
            pltpu.VMEM((tile, D_ATTN), jnp.float32),
            pltpu.VMEM((tile, D_POOL), jnp.float32),
            pltpu.VMEM((tile, D_POOL), jnp.bfloat16),
            pltpu.VMEM((tile, D_POOL), jnp.bfloat16),
            pltpu.VMEM((tile, D_ATTN), jnp.bfloat16),
            pltpu.VMEM((3, N_KV_HEADS * KEY_SPAN, GQA_GROUP * Q_SUB), jnp.float32),
            pltpu.VMEM((D_MODEL, D_IN), jnp.bfloat16),
            pltpu.VMEM((D_MODEL, D_MODEL), jnp.bfloat16),
        ],
        compiler_params=pltpu.CompilerParams(
            dimension_semantics=("arbitrary", "arbitrary"),
            vmem_limit_bytes=VMEM_LIMIT_BYTES),
        name="hybrid_layer",
    )(x, w_in, pool_w, pool_scale, sinks, w_out, g_pre, g_post)


@jax.jit
def kernel(x, w_in, pool_w, pool_scale, attn_sinks, w_out, norm_pre, norm_post):
    for layer in range(w_in.shape[0]):
        x = _layer(layer, x, w_in, pool_w, pool_scale, attn_sinks, w_out, norm_pre, norm_post)
    return x
```

```python
import functools

import jax
import jax.numpy as jnp
from jax import lax
from jax.experimental import pallas as pl
from jax.experimental.pallas import tpu as pltpu

D_MODEL = 1024
D_POOL = 512
POOL_WINDOWS = (2, 4, 8, 16)
POOL_GROUP = 128
HEAD_DIM = 64
D_ATTN = 512
N_HEADS = 8
N_KV_HEADS = 2
GQA_GROUP = N_HEADS // N_KV_HEADS
WINDOW = 128
D_IN = 2304
EPS = 1e-6
NEG_INF = -1e30
LOG2_E = 1.4426950408889634

OFF_U, OFF_PG, OFF_Q, OFF_K, OFF_V, OFF_AG = 0, 512, 1024, 1536, 1664, 1792

LANES = 128
BF16_ROWS = 16
POOL_HALO = 16
SEQ_TILE = 1024
Q_SUB = 64
KEY_SPAN = Q_SUB + WINDOW
VMEM_LIMIT_BYTES = 60 * 1024 * 1024


def _silu(x):
    return x * (1.0 / (1.0 + jnp.exp(-x)))


def _layer_kernel(layer, n_tiles, tiles_per_seq,
                  x_ref, xprev_ref, win_ref, poolw_ref, pscale_ref, sinks_ref, wout_ref, gpre_ref, gpost_ref,
                  o_ref,
                  u_buf, q_buf, kz_buf, vt_buf, ag_buf, pg_buf, pooled_buf, mixp_buf, mixa_buf,
                  bias_buf, wi_buf, wo_buf, pw_buf):
    step = pl.program_id(0)
    t = lax.rem(jnp.minimum(step, n_tiles - 1), tiles_per_seq)

    @pl.when(step == 0)
    def _first_step():
        r = lax.broadcasted_iota(jnp.int32, (KEY_SPAN, GQA_GROUP * Q_SUB), 0)
        col = lax.broadcasted_iota(jnp.int32, (KEY_SPAN, GQA_GROUP * Q_SUB), 1)
        dist = (col & (Q_SUB - 1)) + WINDOW - r
        in_win = (dist >= 0) & (dist < WINDOW)
        distf = dist.astype(jnp.float32)
        for kv in range(N_KV_HEADS):
            slope = jnp.zeros_like(distf)
            for g in range(GQA_GROUP):
                j = kv * GQA_GROUP + g
                slope = jnp.where(col >= g * Q_SUB, 2.0 ** (-8.0 * (j + 1) / N_HEADS), slope)
            b = -slope * distf * LOG2_E
            for f in range(3):
                bias_buf[f, kv * KEY_SPAN:(kv + 1) * KEY_SPAN, :] = jnp.where(
                    in_win & (r >= f * Q_SUB), b, NEG_INF)

        a0 = lax.broadcasted_iota(jnp.int32, (D_ATTN, D_ATTN), 0)
        a1 = lax.broadcasted_iota(jnp.int32, (D_ATTN, D_ATTN), 1)

        def source(n):
            return ((n >> 6) & 1) * (GQA_GROUP * HEAD_DIM) + (n >> 7) * HEAD_DIM + (n & (HEAD_DIM - 1))

        perm = jnp.where(a0 == source(a1), 1.0, 0.0).astype(jnp.bfloat16)
        perm_t = jnp.where(a1 == source(a0), 1.0, 0.0).astype(jnp.bfloat16)
        for lo, hi in ((OFF_U, OFF_PG), (OFF_PG, OFF_Q), (OFF_Q, OFF_K), (OFF_K, OFF_AG), (OFF_AG, D_IN)):
            w = win_ref[:, lo:hi].astype(jnp.bfloat16)
            if lo in (OFF_Q, OFF_AG):
                w = jnp.dot(w, perm, preferred_element_type=jnp.float32).astype(jnp.bfloat16)
            wi_buf[:, lo:hi] = w
        wo_buf[:D_POOL, :] = wout_ref[:D_POOL, :].astype(jnp.bfloat16)
        wo_buf[D_POOL:, :] = jnp.dot(perm_t, wout_ref[D_POOL:, :].astype(jnp.bfloat16),
                                     preferred_element_type=jnp.float32).astype(jnp.bfloat16)
        pw_buf[...] = jnp.zeros(pw_buf.shape, jnp.bfloat16)
        for g in range(len(POOL_WINDOWS)):
            d0 = (g % 2) * POOL_GROUP
            pw_buf[g // 2, d0:d0 + POOL_GROUP, d0:d0 + POOL_GROUP] = poolw_ref[g].astype(jnp.bfloat16)

        mixp_buf[...] = jnp.zeros(mixp_buf.shape, jnp.bfloat16)
        mixa_buf[...] = jnp.zeros(mixa_buf.shape, jnp.bfloat16)

    @pl.when(t == 0)
    def _zero_halo():
        u_buf[0:POOL_HALO, :] = jnp.zeros((POOL_HALO, D_POOL), jnp.float32)
        kz_buf[:, 0:WINDOW, :] = jnp.zeros((N_KV_HEADS, WINDOW, LANES), jnp.bfloat16)
        vt_buf[:, 0:WINDOW] = jnp.zeros((LANES, WINDOW), jnp.bfloat16)

    def back_out():
        return (jnp.dot(mixp_buf[...], wo_buf[:D_POOL, :], preferred_element_type=jnp.float32)
                + jnp.dot(mixa_buf[...], wo_buf[D_POOL:, :], preferred_element_type=jnp.float32))

    def back_finish(y):
        ms2 = jnp.mean(y * y, axis=-1, keepdims=True)
        o_ref[...] = xprev_ref[...] + y * lax.rsqrt(ms2 + EPS) * gpost_ref[layer:layer + 1, :]

    pl.when(step < n_tiles)(functools.partial(
        _front_and_back, layer, t, back_out, back_finish,
        x_ref, pscale_ref, sinks_ref, gpre_ref,
        u_buf, q_buf, kz_buf, vt_buf, ag_buf, pg_buf, pooled_buf, mixp_buf, mixa_buf,
        bias_buf, wi_buf, pw_buf))

    @pl.when(step == n_tiles)
    def _drain():
        back_finish(back_out())


def _front_and_back(layer, t, back_out, back_finish,
                    x_ref, pscale_ref, sinks_ref, gpre_ref,
                    u_buf, q_buf, kz_buf, vt_buf, ag_buf, pg_buf, pooled_buf, mixp_buf, mixa_buf,
                    bias_buf, wi_buf, pw_buf):
    tile = x_ref.shape[0]
    n_sub = tile // Q_SUB

    y_prev = back_out()

    x = x_ref[...]
    ms = jnp.mean(x * x, axis=-1, keepdims=True)
    h = (x * lax.rsqrt(ms + EPS) * gpre_ref[layer:layer + 1, :]).astype(jnp.bfloat16)

    def proj(lo, hi):
        return jnp.dot(h, wi_buf[:, lo:hi], preferred_element_type=jnp.float32)

    kvf = proj(OFF_K, OFF_AG)
    q_buf[...] = (proj(OFF_Q, OFF_K) * (HEAD_DIM ** -0.5 * LOG2_E)).astype(jnp.bfloat16)

    back_finish(y_prev)

    kf, vf = kvf[:, :LANES], kvf[:, LANES:]
    lane = lax.broadcasted_iota(jnp.int32, (1, LANES), 1)
    kz_buf[0, WINDOW:WINDOW + tile, :] = jnp.where(lane < HEAD_DIM, kf, 0.0).astype(jnp.bfloat16)
    kz_buf[1, WINDOW:WINDOW + tile, :] = jnp.where(lane >= HEAD_DIM, kf, 0.0).astype(jnp.bfloat16)
    vt_buf[:, WINDOW:WINDOW + tile] = vf.T.astype(jnp.bfloat16)

    def attn_gate_item():
        ag_buf[...] = _silu(proj(OFF_AG, D_IN))

    def pool_u_item():
        u_buf[POOL_HALO:POOL_HALO + tile, :] = proj(OFF_U, OFF_PG)

    pos1 = (lax.broadcasted_iota(jnp.int32, (tile, POOL_GROUP), 0) + (t * tile + 1)
            ).astype(jnp.float32)

    def pool_window_item():
        for g, w in enumerate(POOL_WINDOWS):
            cols = slice(g * POOL_GROUP, (g + 1) * POOL_GROUP)
            ext = u_buf[:, cols]
            acc = ext
            shift = 1
            while shift < w:
                acc = acc + pltpu.roll(acc, shift, axis=0)
                shift *= 2
            cur = ext[POOL_HALO:]
            pooled = acc[POOL_HALO:] / jnp.minimum(pos1, float(w)) - cur
            pooled_buf[:, cols] = pooled.astype(jnp.bfloat16)

    def pool_mix_item():
        for pair in range(len(POOL_WINDOWS) // 2):
            cols = slice(pair * 2 * POOL_GROUP, (pair + 1) * 2 * POOL_GROUP)
            mixed = jnp.dot(pooled_buf[:, cols], pw_buf[pair], preferred_element_type=jnp.float32)
            scale = pscale_ref[layer:layer + 1, cols]
            mixp_buf[:, cols] = (mixed * scale * pg_buf[:, cols]).astype(jnp.bfloat16)

    def pool_gate_item():
        pg_buf[...] = _silu(proj(OFF_PG, OFF_Q))

    col = lax.broadcasted_iota(jnp.int32, (1, GQA_GROUP * Q_SUB), 1)
    sink_rows = []
    for kv in range(N_KV_HEADS):
        row = jnp.zeros((1, GQA_GROUP * Q_SUB), jnp.float32)
        for g in range(GQA_GROUP):
            row = jnp.where(col >= g * Q_SUB, sinks_ref[layer, kv * GQA_GROUP + g], row)
        sink_rows.append(row * LOG2_E)
    first = (t == 0)
    nt_dims = (((1,), (1,)), ((), ()))
    pad = jnp.zeros((Q_SUB, GQA_GROUP * Q_SUB), jnp.bfloat16)
    ones_rows = jnp.ones((BF16_ROWS, 2 * LANES), jnp.bfloat16)

    def attn_scores(sb):
        r0 = sb * Q_SUB
        qrows = slice(r0, r0 + Q_SUB)
        krows = slice(r0, r0 + KEY_SPAN)
        variant = jnp.where(first, 2 - sb, 0) if sb < 2 else 0
        q4 = jnp.concatenate([q_buf[qrows, g * LANES:(g + 1) * LANES] for g in range(GQA_GROUP)],
                             axis=0)
        kk = jnp.concatenate([kz_buf[0, krows, :], kz_buf[1, krows, :]], axis=0)
        s = lax.dot_general(kk, q4, nt_dims, preferred_element_type=jnp.float32)
        return s + bias_buf[variant]

    def attn_finish(sb, s):
        r0 = sb * Q_SUB
        qrows = slice(r0, r0 + Q_SUB)
        outs = []
        for kv in range(N_KV_HEADS):
            sh = s[kv * KEY_SPAN:(kv + 1) * KEY_SPAN]
            m = jnp.maximum(jnp.max(sh, axis=0, keepdims=True), sink_rows[kv])
            pb = jnp.exp2(sh - m).astype(jnp.bfloat16)
            if sb % 2 == 0:
                pb = jnp.concatenate([pb, pad], axis=0)
                c0 = r0
            else:
                pb = jnp.concatenate([pad, pb], axis=0)
                c0 = r0 - Q_SUB
            vt = jnp.concatenate([vt_buf[kv * HEAD_DIM:(kv + 1) * HEAD_DIM, c0:c0 + 2 * LANES], ones_rows],
                                 axis=0)
            o = jnp.dot(vt, pb, preferred_element_type=jnp.float32)
            l = o[HEAD_DIM:HEAD_DIM + 1] + jnp.exp2(sink_rows[kv] - m)
            outs.append(o[:HEAD_DIM] * (1.0 / l))
        for pr in range(GQA_GROUP // 2):
            both = jnp.concatenate([o[:, pr * LANES:(pr + 1) * LANES] for o in outs], axis=0)
            both_t = both.T
            for half in range(2):
                g = 2 * pr + half
                val = both_t[half * Q_SUB:(half + 1) * Q_SUB] * ag_buf[qrows, g * LANES:(g + 1) * LANES]
                mixa_buf[qrows, g * LANES:(g + 1) * LANES] = val.astype(jnp.bfloat16)

    attn_gate_item()
    dense_items = [[pool_u_item], [pool_gate_item], [pool_window_item, pool_mix_item], []]
    group = n_sub // len(dense_items)
    for p, items in enumerate(dense_items):
        subs = range(p * group, (p + 1) * group)
        scores = [attn_scores(sb) for sb in subs]
        for item in items:
            item()
        for sb, s in zip(subs, scores):
            attn_finish(sb, s)

    u_buf[0:POOL_HALO, :] = u_buf[tile:tile + POOL_HALO, :]
    kz_buf[:, 0:WINDOW, :] = kz_buf[:, tile:tile + WINDOW, :]
    vt_buf[:, 0:WINDOW] = vt_buf[:, tile:tile + WINDOW]


def _layer(layer, x, w_in, pool_w, pool_scale, sinks, w_out, g_pre, g_post):
    batch, seq, d = x.shape
    tile = SEQ_TILE
    assert seq % tile == 0 and tile % (2 * Q_SUB) == 0 and d == D_MODEL
    tiles_per_seq = seq // tile
    n_tiles = batch * tiles_per_seq

    def front_tile(step):
        f = jnp.minimum(step, n_tiles - 1)
        return (f // tiles_per_seq, f % tiles_per_seq, 0)

    def back_tile(step):
        p = jnp.maximum(step - 1, 0)
        return (p // tiles_per_seq, p % tiles_per_seq, 0)

    def of_layer(*shape):
        return pl.BlockSpec((None,) + shape, lambda step: (layer,) + (0,) * len(shape),
                            pipeline_mode=pl.Buffered(1))

    def whole(a):
        return pl.BlockSpec(a.shape, lambda step: (0,) * a.ndim)

    return pl.pallas_call(
        functools.partial(_layer_kernel, layer, n_tiles, tiles_per_seq),
        out_shape=jax.ShapeDtypeStruct(x.shape, x.dtype),
        grid=(n_tiles + 1,),
        in_specs=[
            pl.BlockSpec((None, tile, d), front_tile),
            pl.BlockSpec((None, tile, d), back_tile),
            of_layer(D_MODEL, D_IN),
            of_layer(len(POOL_WINDOWS), POOL_GROUP, POOL_GROUP),
            whole(pool_scale),
            pl.BlockSpec(memory_space=pltpu.SMEM),
            of_layer(D_MODEL, D_MODEL),
            whole(g_pre),
            whole(g_post),
        ],
        out_specs=pl.BlockSpec((None, tile, d), back_tile),
        scratch_shapes=[
            pltpu.VMEM((POOL_HALO + tile, D_POOL), jnp.float32),
            pltpu.VMEM((tile, D_ATTN), jnp.bfloat16),
            pltpu.VMEM((N_KV_HEADS, WINDOW + tile, LANES), jnp.bfloat16),
            pltpu.VMEM((LANES, WINDOW + tile), jnp.bfloat16),
            pltpu.VMEM((tile, D_ATTN), jnp.float32),
            pltpu.VMEM((tile, D_POOL), jnp.float32),
            pltpu.VMEM((tile, D_POOL), jnp.bfloat16),
            pltpu.VMEM((tile, D_POOL), jnp.bfloat16),
            pltpu.VMEM((tile, D_ATTN), jnp.bfloat16),
            pltpu.VMEM((3, N_KV_HEADS * KEY_SPAN, GQA_GROUP * Q_SUB), jnp.float32),
            pltpu.VMEM((D_MODEL, D_IN), jnp.bfloat16),
            pltpu.VMEM((D_MODEL, D_MODEL), jnp.bfloat16),
            pltpu.VMEM((len(POOL_WINDOWS) // 2, 2 * POOL_GROUP, 2 * POOL_GROUP), jnp.bfloat16),
        ],
        compiler_params=pltpu.CompilerParams(
            dimension_semantics=("arbitrary",),
            vmem_limit_bytes=VMEM_LIMIT_BYTES),
        name="hybrid_layer",
    )(x, x, w_in, pool_w, pool_scale, sinks, w_out, g_pre, g_post)


@jax.jit
def kernel(x, w_in, pool_w, pool_scale, attn_sinks, w_out, norm_pre, norm_post):
    for layer in range(w_in.shape[0]):
        x = _layer(layer, x, w_in, pool_w, pool_scale, attn_sinks, w_out, norm_pre, norm_post)
    return x
```

```python
import functools

import jax
import jax.numpy as jnp
from jax import lax
from jax.experimental import pallas as pl
from jax.experimental.pallas import tpu as pltpu

D_MODEL = 1024
D_POOL = 512
POOL_WINDOWS = (2, 4, 8, 16)
POOL_GROUP = 128
HEAD_DIM = 64
D_ATTN = 512
N_HEADS = 8
N_KV_HEADS = 2
GQA_GROUP = N_HEADS // N_KV_HEADS
WINDOW = 128
D_IN = 2304
EPS = 1e-6
NEG_INF = -1e30
LOG2_E = 1.4426950408889634

OFF_U, OFF_PG, OFF_Q, OFF_K, OFF_V, OFF_AG = 0, 512, 1024, 1536, 1664, 1792

LANES = 128
BF16_ROWS = 16
POOL_HALO = 16
SEQ_TILE = 1024
Q_SUB = 64
KEY_SPAN = Q_SUB + WINDOW
VMEM_LIMIT_BYTES = 56 * 1024 * 1024


def _silu(x):
    return x * (1.0 / (1.0 + jnp.exp(-x)))


def _layer_kernel(layer, x_ref, win_ref, poolw_ref, pscale_ref, sinks_ref, wout_ref,
                  gpre_ref, gpost_ref, o_ref,
                  u_buf, q_buf, kz_buf, vt_buf, ag_buf, pg_buf, pooled_buf, mixp_buf, mixa_buf, bias_buf, wi_buf,
                  wo_buf, pw_buf):
    tile = x_ref.shape[0]
    n_sub = tile // Q_SUB
    t = pl.program_id(1)

    @pl.when((pl.program_id(0) == 0) & (t == 0))
    def _first_step():
        r = lax.broadcasted_iota(jnp.int32, (KEY_SPAN, GQA_GROUP * Q_SUB), 0)
        col = lax.broadcasted_iota(jnp.int32, (KEY_SPAN, GQA_GROUP * Q_SUB), 1)
        dist = (col & (Q_SUB - 1)) + WINDOW - r
        in_win = (dist >= 0) & (dist < WINDOW)
        distf = dist.astype(jnp.float32)
        for kv in range(N_KV_HEADS):
            slope = jnp.zeros_like(distf)
            for g in range(GQA_GROUP):
                j = kv * GQA_GROUP + g
                slope = jnp.where(col >= g * Q_SUB, 2.0 ** (-8.0 * (j + 1) / N_HEADS), slope)
            b = -slope * distf * LOG2_E
            for f in range(3):
                bias_buf[f, kv * KEY_SPAN:(kv + 1) * KEY_SPAN, :] = jnp.where(
                    in_win & (r >= f * Q_SUB), b, NEG_INF)

        a0 = lax.broadcasted_iota(jnp.int32, (D_ATTN, D_ATTN), 0)
        a1 = lax.broadcasted_iota(jnp.int32, (D_ATTN, D_ATTN), 1)

        def source(n):
            return ((n >> 6) & 1) * (GQA_GROUP * HEAD_DIM) + (n >> 7) * HEAD_DIM + (n & (HEAD_DIM - 1))

        perm = jnp.where(a0 == source(a1), 1.0, 0.0).astype(jnp.bfloat16)
        perm_t = jnp.where(a1 == source(a0), 1.0, 0.0).astype(jnp.bfloat16)
        for lo, hi in ((OFF_U, OFF_PG), (OFF_PG, OFF_Q), (OFF_Q, OFF_K), (OFF_K, OFF_AG), (OFF_AG, D_IN)):
            w = win_ref[:, lo:hi].astype(jnp.bfloat16)
            if lo in (OFF_Q, OFF_AG):
                w = jnp.dot(w, perm, preferred_element_type=jnp.float32).astype(jnp.bfloat16)
            wi_buf[:, lo:hi] = w
        wo_buf[:D_POOL, :] = wout_ref[:D_POOL, :].astype(jnp.bfloat16)
        wo_buf[D_POOL:, :] = jnp.dot(perm_t, wout_ref[D_POOL:, :].astype(jnp.bfloat16),
                                     preferred_element_type=jnp.float32).astype(jnp.bfloat16)
        pw_buf[...] = jnp.zeros(pw_buf.shape, jnp.bfloat16)
        for g in range(len(POOL_WINDOWS)):
            d0 = (g % 2) * POOL_GROUP
            pw_buf[g // 2, d0:d0 + POOL_GROUP, d0:d0 + POOL_GROUP] = poolw_ref[g].astype(jnp.bfloat16)

    @pl.when(t == 0)
    def _zero_halo():
        u_buf[0:POOL_HALO, :] = jnp.zeros((POOL_HALO, D_POOL), jnp.float32)
        kz_buf[:, 0:WINDOW, :] = jnp.zeros((N_KV_HEADS, WINDOW, LANES), jnp.bfloat16)
        vt_buf[:, 0:WINDOW] = jnp.zeros((LANES, WINDOW), jnp.bfloat16)

    x = x_ref[...]
    ms = jnp.mean(x * x, axis=-1, keepdims=True)
    h = (x * lax.rsqrt(ms + EPS) * gpre_ref[layer:layer + 1, :]).astype(jnp.bfloat16)

    def proj(lo, hi):
        return jnp.dot(h, wi_buf[:, lo:hi], preferred_element_type=jnp.float32)

    kvf = proj(OFF_K, OFF_AG)
    q_buf[...] = (proj(OFF_Q, OFF_K) * (HEAD_DIM ** -0.5 * LOG2_E)).astype(jnp.bfloat16)
    kf, vf = kvf[:, :LANES], kvf[:, LANES:]
    lane = lax.broadcasted_iota(jnp.int32, (1, LANES), 1)
    kz_buf[0, WINDOW:WINDOW + tile, :] = jnp.where(lane < HEAD_DIM, kf, 0.0).astype(jnp.bfloat16)
    kz_buf[1, WINDOW:WINDOW + tile, :] = jnp.where(lane >= HEAD_DIM, kf, 0.0).astype(jnp.bfloat16)
    vt_buf[:, WINDOW:WINDOW + tile] = vf.T.astype(jnp.bfloat16)

    def attn_gate_item():
        ag_buf[...] = _silu(proj(OFF_AG, D_IN))

    def pool_u_item():
        u_buf[POOL_HALO:POOL_HALO + tile, :] = proj(OFF_U, OFF_PG)

    pos1 = (lax.broadcasted_iota(jnp.int32, (tile, POOL_GROUP), 0) + (t * tile + 1)
            ).astype(jnp.float32)

    def pool_window_item():
        for g, w in enumerate(POOL_WINDOWS):
            cols = slice(g * POOL_GROUP, (g + 1) * POOL_GROUP)
            ext = u_buf[:, cols]
            acc = ext
            shift = 1
            while shift < w:
                acc = acc + pltpu.roll(acc, shift, axis=0)
                shift *= 2
            cur = ext[POOL_HALO:]
            pooled = acc[POOL_HALO:] / jnp.minimum(pos1, float(w)) - cur
            pooled_buf[:, cols] = pooled.astype(jnp.bfloat16)

    def pool_mix_item():
        for pair in range(len(POOL_WINDOWS) // 2):
            cols = slice(pair * 2 * POOL_GROUP, (pair + 1) * 2 * POOL_GROUP)
            mixed = jnp.dot(pooled_buf[:, cols], pw_buf[pair], preferred_element_type=jnp.float32)
            scale = pscale_ref[layer:layer + 1, cols]
            mixp_buf[:, cols] = (mixed * scale * pg_buf[:, cols]).astype(jnp.bfloat16)

    def pool_gate_item():
        pg_buf[...] = _silu(proj(OFF_PG, OFF_Q))

    col = lax.broadcasted_iota(jnp.int32, (1, GQA_GROUP * Q_SUB), 1)
    sink_rows = []
    for kv in range(N_KV_HEADS):
        row = jnp.zeros((1, GQA_GROUP * Q_SUB), jnp.float32)
        for g in range(GQA_GROUP):
            row = jnp.where(col >= g * Q_SUB, sinks_ref[layer, kv * GQA_GROUP + g], row)
        sink_rows.append(row * LOG2_E)
    first = (t == 0)
    nt_dims = (((1,), (1,)), ((), ()))
    pad = jnp.zeros((Q_SUB, GQA_GROUP * Q_SUB), jnp.bfloat16)
    ones_rows = jnp.ones((BF16_ROWS, 2 * LANES), jnp.bfloat16)

    def attn_scores(sb):
        r0 = sb * Q_SUB
        qrows = slice(r0, r0 + Q_SUB)
        krows = slice(r0, r0 + KEY_SPAN)
        variant = jnp.where(first, 2 - sb, 0) if sb < 2 else 0
        q4 = jnp.concatenate([q_buf[qrows, g * LANES:(g + 1) * LANES] for g in range(GQA_GROUP)],
                             axis=0)
        kk = jnp.concatenate([kz_buf[0, krows, :], kz_buf[1, krows, :]], axis=0)
        s = lax.dot_general(kk, q4, nt_dims, preferred_element_type=jnp.float32)
        return s + bias_buf[variant]

    def attn_finish(sb, s):
        r0 = sb * Q_SUB
        qrows = slice(r0, r0 + Q_SUB)
        outs = []
        for kv in range(N_KV_HEADS):
            sh = s[kv * KEY_SPAN:(kv + 1) * KEY_SPAN]
            m = jnp.maximum(jnp.max(sh, axis=0, keepdims=True), sink_rows[kv])
            pb = jnp.exp2(sh - m).astype(jnp.bfloat16)
            if sb % 2 == 0:
                pb = jnp.concatenate([pb, pad], axis=0)
                c0 = r0
            else:
                pb = jnp.concatenate([pad, pb], axis=0)
                c0 = r0 - Q_SUB
            vt = jnp.concatenate([vt_buf[kv * HEAD_DIM:(kv + 1) * HEAD_DIM, c0:c0 + 2 * LANES], ones_rows],
                                 axis=0)
            o = jnp.dot(vt, pb, preferred_element_type=jnp.float32)
            l = o[HEAD_DIM:HEAD_DIM + 1] + jnp.exp2(sink_rows[kv] - m)
            outs.append(o[:HEAD_DIM] * (1.0 / l))
        for pr in range(GQA_GROUP // 2):
            both = jnp.concatenate([o[:, pr * LANES:(pr + 1) * LANES] for o in outs], axis=0)
            both_t = both.T
            for half in range(2):
                g = 2 * pr + half
                val = both_t[half * Q_SUB:(half + 1) * Q_SUB] * ag_buf[qrows, g * LANES:(g + 1) * LANES]
                mixa_buf[qrows, g * LANES:(g + 1) * LANES] = val.astype(jnp.bfloat16)

    attn_gate_item()
    dense_items = [[pool_u_item], [pool_gate_item], [pool_window_item, pool_mix_item], []]
    group = n_sub // len(dense_items)
    for p, items in enumerate(dense_items):
        subs = range(p * group, (p + 1) * group)
        scores = [attn_scores(sb) for sb in subs]
        for item in items:
            item()
        for sb, s in zip(subs, scores):
            attn_finish(sb, s)

    u_buf[0:POOL_HALO, :] = u_buf[tile:tile + POOL_HALO, :]
    kz_buf[:, 0:WINDOW, :] = kz_buf[:, tile:tile + WINDOW, :]
    vt_buf[:, 0:WINDOW] = vt_buf[:, tile:tile + WINDOW]

    y = (jnp.dot(mixp_buf[...], wo_buf[:D_POOL, :], preferred_element_type=jnp.float32)
         + jnp.dot(mixa_buf[...], wo_buf[D_POOL:, :], preferred_element_type=jnp.float32))
    ms2 = jnp.mean(y * y, axis=-1, keepdims=True)
    o_ref[...] = x_ref[...] + y * lax.rsqrt(ms2 + EPS) * gpost_ref[layer:layer + 1, :]


def _layer(layer, x, w_in, pool_w, pool_scale, sinks, w_out, g_pre, g_post):
    batch, seq, d = x.shape
    tile = SEQ_TILE
    assert seq % tile == 0 and tile % (2 * Q_SUB) == 0 and d == D_MODEL

    def of_layer(*shape):
        return pl.BlockSpec((None,) + shape, lambda b, t: (layer,) + (0,) * len(shape),
                            pipeline_mode=pl.Buffered(1))

    def whole(a):
        return pl.BlockSpec(a.shape, lambda b, t: (0,) * a.ndim)

    return pl.pallas_call(
        functools.partial(_layer_kernel, layer),
        out_shape=jax.ShapeDtypeStruct(x.shape, x.dtype),
        grid=(batch, seq // tile),
        in_specs=[
            pl.BlockSpec((None, tile, d), lambda b, t: (b, t, 0)),
            of_layer(D_MODEL, D_IN),
            of_layer(len(POOL_WINDOWS), POOL_GROUP, POOL_GROUP),
            whole(pool_scale),
            pl.BlockSpec(memory_space=pltpu.SMEM),
            of_layer(D_MODEL, D_MODEL),
            whole(g_pre),
            whole(g_post),
        ],
        out_specs=pl.BlockSpec((None, tile, d), lambda b, t: (b, t, 0)),
        scratch_shapes=[
            pltpu.VMEM((POOL_HALO + tile, D_POOL), jnp.float32),
            pltpu.VMEM((tile, D_ATTN), jnp.bfloat16),
            pltpu.VMEM((N_KV_HEADS, WINDOW + tile, LANES), jnp.bfloat16),
            pltpu.VMEM((LANES, WINDOW + tile), jnp.bfloat16),
            pltpu.VMEM((tile, D_ATTN), jnp.float32),
            pltpu.VMEM((tile, D_POOL), jnp.float32),
            pltpu.VMEM((tile, D_POOL), jnp.bfloat16),
            pltpu.VMEM((tile, D_POOL), jnp.bfloat16),
            pltpu.VMEM((tile, D_ATTN), jnp.bfloat16),
            pltpu.VMEM((3, N_KV_HEADS * KEY_SPAN, GQA_GROUP * Q_SUB), jnp.float32),
            pltpu.VMEM((D_MODEL, D_IN), jnp.bfloat16),
            pltpu.VMEM((D_MODEL, D_MODEL), jnp.bfloat16),
            pltpu.VMEM((len(POOL_WINDOWS) // 2, 2 * POOL_GROUP, 2 * POOL_GROUP), jnp.bfloat16),
        ],
        compiler_params=pltpu.CompilerParams(
            dimension_semantics=("arbitrary", "arbitrary"),
            vmem_limit_bytes=VMEM_LIMIT_BYTES),
        name="hybrid_layer",
    )(x, w_in, pool_w, pool_scale, sinks, w_out, g_pre, g_post)


@jax.jit
def kernel(x, w_in, pool_w, pool_scale, attn_sinks, w_out, norm_pre, norm_post):
    for layer in range(w_in.shape[0]):
        x = _layer(layer, x, w_in, pool_w, pool_scale, attn_sinks, w_out, norm_pre, norm_post)
    return x
```

```python
import functools

import jax
import jax.numpy as jnp
from jax import lax
from jax.experimental import pallas as pl
from jax.experimental.pallas import tpu as pltpu

D_MODEL = 1024
D_POOL = 512
POOL_WINDOWS = (2, 4, 8, 16)
POOL_GROUP = 128
HEAD_DIM = 64
D_ATTN = 512
N_HEADS = 8
N_KV_HEADS = 2
GQA_GROUP = N_HEADS // N_KV_HEADS
WINDOW = 128
D_IN = 2304
EPS = 1e-6
NEG_INF = -1e30
LOG2_E = 1.4426950408889634

OFF_U, OFF_PG, OFF_Q, OFF_K, OFF_V, OFF_AG = 0, 512, 1024, 1536, 1664, 1792

LANES = 128
BF16_ROWS = 16
POOL_HALO = 16
SEQ_TILE = 1024
Q_SUB = 64
OUT_ROWS = 512
KEY_SPAN = Q_SUB + WINDOW
VMEM_LIMIT_BYTES = 56 * 1024 * 1024


def _silu(x):
    return x * (1.0 / (1.0 + jnp.exp(-x)))


def _layer_kernel(layer, x_ref, win_ref, poolw_ref, pscale_ref, sinks_ref, wout_ref,
                  gpre_ref, gpost_ref, o_ref,
                  u_buf, q_buf, kz_buf, vt_buf, ag_buf, pg_buf, pooled_buf, mixp_buf, mixa_buf, bias_buf, wi_buf,
                  wo_buf, pw_buf):
    tile = x_ref.shape[0]
    n_sub = tile // Q_SUB
    t = pl.program_id(1)

    @pl.when((pl.program_id(0) == 0) & (t == 0))
    def _first_step():
        r = lax.broadcasted_iota(jnp.int32, (KEY_SPAN, GQA_GROUP * Q_SUB), 0)
        col = lax.broadcasted_iota(jnp.int32, (KEY_SPAN, GQA_GROUP * Q_SUB), 1)
        dist = (col & (Q_SUB - 1)) + WINDOW - r
        in_win = (dist >= 0) & (dist < WINDOW)
        distf = dist.astype(jnp.float32)
        for kv in range(N_KV_HEADS):
            slope = jnp.zeros_like(distf)
            for g in range(GQA_GROUP):
                j = kv * GQA_GROUP + g
                slope = jnp.where(col >= g * Q_SUB, 2.0 ** (-8.0 * (j + 1) / N_HEADS), slope)
            b = -slope * distf * LOG2_E
            for f in range(3):
                bias_buf[f, kv * KEY_SPAN:(kv + 1) * KEY_SPAN, :] = jnp.where(
                    in_win & (r >= f * Q_SUB), b, NEG_INF)

        a0 = lax.broadcasted_iota(jnp.int32, (D_ATTN, D_ATTN), 0)
        a1 = lax.broadcasted_iota(jnp.int32, (D_ATTN, D_ATTN), 1)

        def source(n):
            return ((n >> 6) & 1) * (GQA_GROUP * HEAD_DIM) + (n >> 7) * HEAD_DIM + (n & (HEAD_DIM - 1))

        perm = jnp.where(a0 == source(a1), 1.0, 0.0).astype(jnp.bfloat16)
        perm_t = jnp.where(a1 == source(a0), 1.0, 0.0).astype(jnp.bfloat16)
        for lo, hi in ((OFF_U, OFF_PG), (OFF_PG, OFF_Q), (OFF_Q, OFF_K), (OFF_K, OFF_AG), (OFF_AG, D_IN)):
            w = win_ref[:, lo:hi].astype(jnp.bfloat16)
            if lo in (OFF_Q, OFF_AG):
                w = jnp.dot(w, perm, preferred_element_type=jnp.float32).astype(jnp.bfloat16)
            wi_buf[:, lo:hi] = w
        wo_buf[:D_POOL, :] = wout_ref[:D_POOL, :].astype(jnp.bfloat16)
        wo_buf[D_POOL:, :] = jnp.dot(perm_t, wout_ref[D_POOL:, :].astype(jnp.bfloat16),
                                     preferred_element_type=jnp.float32).astype(jnp.bfloat16)
        pw_buf[...] = jnp.zeros(pw_buf.shape, jnp.bfloat16)
        for g in range(len(POOL_WINDOWS)):
            d0 = (g % 2) * POOL_GROUP
            pw_buf[g // 2, d0:d0 + POOL_GROUP, d0:d0 + POOL_GROUP] = poolw_ref[g].astype(jnp.bfloat16)

    @pl.when(t == 0)
    def _zero_halo():
        u_buf[0:POOL_HALO, :] = jnp.zeros((POOL_HALO, D_POOL), jnp.float32)
        kz_buf[:, 0:WINDOW, :] = jnp.zeros((N_KV_HEADS, WINDOW, LANES), jnp.bfloat16)
        vt_buf[:, 0:WINDOW] = jnp.zeros((LANES, WINDOW), jnp.bfloat16)

    x = x_ref[...]
    ms = jnp.mean(x * x, axis=-1, keepdims=True)
    h = (x * lax.rsqrt(ms + EPS) * gpre_ref[layer:layer + 1, :]).astype(jnp.bfloat16)

    def proj(lo, hi):
        return jnp.dot(h, wi_buf[:, lo:hi], preferred_element_type=jnp.float32)

    kvf = proj(OFF_K, OFF_AG)
    q_buf[...] = (proj(OFF_Q, OFF_K) * (HEAD_DIM ** -0.5 * LOG2_E)).astype(jnp.bfloat16)
    kf, vf = kvf[:, :LANES], kvf[:, LANES:]
    lane = lax.broadcasted_iota(jnp.int32, (1, LANES), 1)
    kz_buf[0, WINDOW:WINDOW + tile, :] = jnp.where(lane < HEAD_DIM, kf, 0.0).astype(jnp.bfloat16)
    kz_buf[1, WINDOW:WINDOW + tile, :] = jnp.where(lane >= HEAD_DIM, kf, 0.0).astype(jnp.bfloat16)
    vt_buf[:, WINDOW:WINDOW + tile] = vf.T.astype(jnp.bfloat16)

    def attn_gate_item():
        ag_buf[...] = _silu(proj(OFF_AG, D_IN))

    def pool_u_item():
        u_buf[POOL_HALO:POOL_HALO + tile, :] = proj(OFF_U, OFF_PG)

    pos1 = (lax.broadcasted_iota(jnp.int32, (tile, POOL_GROUP), 0) + (t * tile + 1)
            ).astype(jnp.float32)

    def pool_window_item():
        for g, w in enumerate(POOL_WINDOWS):
            cols = slice(g * POOL_GROUP, (g + 1) * POOL_GROUP)
            ext = u_buf[:, cols]
            acc = ext
            shift = 1
            while shift < w:
                acc = acc + pltpu.roll(acc, shift, axis=0)
                shift *= 2
            cur = ext[POOL_HALO:]
            pooled = acc[POOL_HALO:] / jnp.minimum(pos1, float(w)) - cur
            pooled_buf[:, cols] = pooled.astype(jnp.bfloat16)

    def pool_mix_item():
        for pair in range(len(POOL_WINDOWS) // 2):
            cols = slice(pair * 2 * POOL_GROUP, (pair + 1) * 2 * POOL_GROUP)
            mixed = jnp.dot(pooled_buf[:, cols], pw_buf[pair], preferred_element_type=jnp.float32)
            scale = pscale_ref[layer:layer + 1, cols]
            mixp_buf[:, cols] = (mixed * scale * pg_buf[:, cols]).astype(jnp.bfloat16)

    def pool_gate_item():
        pg_buf[...] = _silu(proj(OFF_PG, OFF_Q))

    col = lax.broadcasted_iota(jnp.int32, (1, GQA_GROUP * Q_SUB), 1)
    sink_rows = []
    for kv in range(N_KV_HEADS):
        row = jnp.zeros((1, GQA_GROUP * Q_SUB), jnp.float32)
        for g in range(GQA_GROUP):
            row = jnp.where(col >= g * Q_SUB, sinks_ref[layer, kv * GQA_GROUP + g], row)
        sink_rows.append(row * LOG2_E)
    first = (t == 0)
    nt_dims = (((1,), (1,)), ((), ()))
    pad = jnp.zeros((Q_SUB, GQA_GROUP * Q_SUB), jnp.bfloat16)
    ones_rows = jnp.ones((BF16_ROWS, 2 * LANES), jnp.bfloat16)

    def attn_scores(sb):
        r0 = sb * Q_SUB
        qrows = slice(r0, r0 + Q_SUB)
        krows = slice(r0, r0 + KEY_SPAN)
        variant = jnp.where(first, 2 - sb, 0) if sb < 2 else 0
        q4 = jnp.concatenate([q_buf[qrows, g * LANES:(g + 1) * LANES] for g in range(GQA_GROUP)],
                             axis=0)
        kk = jnp.concatenate([kz_buf[0, krows, :], kz_buf[1, krows, :]], axis=0)
        s = lax.dot_general(kk, q4, nt_dims, preferred_element_type=jnp.float32)
        return s + bias_buf[variant]

    def attn_finish(sb, s):
        r0 = sb * Q_SUB
        qrows = slice(r0, r0 + Q_SUB)
        outs = []
        for kv in range(N_KV_HEADS):
            sh = s[kv * KEY_SPAN:(kv + 1) * KEY_SPAN]
            m = jnp.maximum(jnp.max(sh, axis=0, keepdims=True), sink_rows[kv])
            pb = jnp.exp2(sh - m).astype(jnp.bfloat16)
            if sb % 2 == 0:
                pb = jnp.concatenate([pb, pad], axis=0)
                c0 = r0
            else:
                pb = jnp.concatenate([pad, pb], axis=0)
                c0 = r0 - Q_SUB
            vt = jnp.concatenate([vt_buf[kv * HEAD_DIM:(kv + 1) * HEAD_DIM, c0:c0 + 2 * LANES], ones_rows],
                                 axis=0)
            o = jnp.dot(vt, pb, preferred_element_type=jnp.float32)
            l = o[HEAD_DIM:HEAD_DIM + 1] + jnp.exp2(sink_rows[kv] - m)
            outs.append(o[:HEAD_DIM] * (1.0 / l))
        for pr in range(GQA_GROUP // 2):
            both = jnp.concatenate([o[:, pr * LANES:(pr + 1) * LANES] for o in outs], axis=0)
            both_t = both.T
            for half in range(2):
                g = 2 * pr + half
                val = both_t[half * Q_SUB:(half + 1) * Q_SUB] * ag_buf[qrows, g * LANES:(g + 1) * LANES]
                mixa_buf[qrows, g * LANES:(g + 1) * LANES] = val.astype(jnp.bfloat16)

    attn_gate_item()
    dense_items = [[pool_u_item], [pool_gate_item], [pool_window_item, pool_mix_item], []]
    group = n_sub // len(dense_items)
    for p, items in enumerate(dense_items):
        subs = range(p * group, (p + 1) * group)
        scores = [attn_scores(sb) for sb in subs]
        for item in items:
            item()
        for sb, s in zip(subs, scores):
            attn_finish(sb, s)

    u_buf[0:POOL_HALO, :] = u_buf[tile:tile + POOL_HALO, :]
    kz_buf[:, 0:WINDOW, :] = kz_buf[:, tile:tile + WINDOW, :]
    vt_buf[:, 0:WINDOW] = vt_buf[:, tile:tile + WINDOW]

    for r0 in range(0, tile, OUT_ROWS):
        rows = slice(r0, r0 + OUT_ROWS)
        y = (jnp.dot(mixp_buf[rows, :], wo_buf[:D_POOL, :], preferred_element_type=jnp.float32)
             + jnp.dot(mixa_buf[rows, :], wo_buf[D_POOL:, :], preferred_element_type=jnp.float32))
        ms2 = jnp.mean(y * y, axis=-1, keepdims=True)
        o_ref[rows, :] = x_ref[rows, :] + y * lax.rsqrt(ms2 + EPS) * gpost_ref[layer:layer + 1, :]


def _layer(layer, x, w_in, pool_w, pool_scale, sinks, w_out, g_pre, g_post):
    batch, seq, d = x.shape
    tile = SEQ_TILE
    assert seq % tile == 0 and tile % (2 * Q_SUB) == 0 and d == D_MODEL

    def of_layer(*shape):
        return pl.BlockSpec((None,) + shape, lambda b, t: (layer,) + (0,) * len(shape),
                            pipeline_mode=pl.Buffered(1))

    def whole(a):
        return pl.BlockSpec(a.shape, lambda b, t: (0,) * a.ndim)

    return pl.pallas_call(
        functools.partial(_layer_kernel, layer),
        out_shape=jax.ShapeDtypeStruct(x.shape, x.dtype),
        grid=(batch, seq // tile),
        in_specs=[
            pl.BlockSpec((None, tile, d), lambda b, t: (b, t, 0)),
            of_layer(D_MODEL, D_IN),
            of_layer(len(POOL_WINDOWS), POOL_GROUP, POOL_GROUP),
            whole(pool_scale),
            pl.BlockSpec(memory_space=pltpu.SMEM),
            of_layer(D_MODEL, D_MODEL),
            whole(g_pre),
            whole(g_post),
        ],
        out_specs=pl.BlockSpec((None, tile, d), lambda b, t: (b, t, 0)),
        scratch_shapes=[
            pltpu.VMEM((POOL_HALO + tile, D_POOL), jnp.float32),
            pltpu.VMEM((tile, D_ATTN), jnp.bfloat16),
            pltpu.VMEM((N_KV_HEADS, WINDOW + tile, LANES), jnp.bfloat16),
            pltpu.VMEM((LANES, WINDOW + tile), jnp.bfloat16),
            pltpu.VMEM((tile, D_ATTN), jnp.float32),
            pltpu.VMEM((tile, D_POOL), jnp.float32),
            pltpu.VMEM((tile, D_POOL), jnp.bfloat16),
            pltpu.VMEM((tile, D_POOL), jnp.bfloat16),
            pltpu.VMEM((tile, D_ATTN), jnp.bfloat16),
            pltpu.VMEM((3, N_KV_HEADS * KEY_SPAN, GQA_GROUP * Q_SUB), jnp.float32),
            pltpu.VMEM((D_MODEL, D_IN), jnp.bfloat16),
            pltpu.VMEM((D_MODEL, D_MODEL), jnp.bfloat16),
            pltpu.VMEM((len(POOL_WINDOWS) // 2, 2 * POOL_GROUP, 2 * POOL_GROUP), jnp.bfloat16),
        ],
        compiler_params=pltpu.CompilerParams(
            dimension_semantics=("arbitrary", "arbitrary"),
            vmem_limit_bytes=VMEM_LIMIT_BYTES),
        name="hybrid_layer",
    )(x, w_in, pool_w, pool_scale, sinks, w_out, g_pre, g_post)


@jax.jit
def kernel(x, w_in, pool_w, pool_scale, attn_sinks, w_out, norm_pre, norm_post):
    for layer in range(w_in.shape[0]):
        x = _layer(layer, x, w_in, pool_w, pool_scale, attn_sinks, w_out, norm_pre, norm_post)
    return x
```

```python
import functools

import jax
import jax.numpy as jnp
from jax import lax
from jax.experimental import pallas as pl
from jax.experimental.pallas import tpu as pltpu

D_MODEL = 1024
D_POOL = 512
POOL_WINDOWS = (2, 4, 8, 16)
POOL_GROUP = 128
HEAD_DIM = 64
D_ATTN = 512
N_HEADS = 8
N_KV_HEADS = 2
GQA_GROUP = N_HEADS // N_KV_HEADS
WINDOW = 128
D_IN = 2304
EPS = 1e-6
NEG_INF = -1e30
LOG2_E = 1.4426950408889634

OFF_U, OFF_PG, OFF_Q, OFF_K, OFF_V, OFF_AG = 0, 512, 1024, 1536, 1664, 1792

LANES = 128
BF16_ROWS = 16
POOL_HALO = 16
SEQ_TILE = 1024
Q_SUB = 64
OUT_ROWS = 256
KEY_SPAN = Q_SUB + WINDOW
VMEM_LIMIT_BYTES = 56 * 1024 * 1024


def _silu(x):
    return x * (1.0 / (1.0 + jnp.exp(-x)))


def _layer_kernel(layer, x_ref, win_ref, poolw_ref, pscale_ref, sinks_ref, wout_ref,
                  gpre_ref, gpost_ref, o_ref,
                  u_buf, q_buf, kz_buf, vt_buf, ag_buf, pg_buf, pooled_buf, mixp_buf, mixa_buf, bias_buf, wi_buf,
                  wo_buf, pw_buf):
    tile = x_ref.shape[0]
    n_sub = tile // Q_SUB
    t = pl.program_id(1)

    @pl.when((pl.program_id(0) == 0) & (t == 0))
    def _first_step():
        r = lax.broadcasted_iota(jnp.int32, (KEY_SPAN, GQA_GROUP * Q_SUB), 0)
        col = lax.broadcasted_iota(jnp.int32, (KEY_SPAN, GQA_GROUP * Q_SUB), 1)
        dist = (col & (Q_SUB - 1)) + WINDOW - r
        in_win = (dist >= 0) & (dist < WINDOW)
        distf = dist.astype(jnp.float32)
        for kv in range(N_KV_HEADS):
            slope = jnp.zeros_like(distf)
            for g in range(GQA_GROUP):
                j = kv * GQA_GROUP + g
                slope = jnp.where(col >= g * Q_SUB, 2.0 ** (-8.0 * (j + 1) / N_HEADS), slope)
            b = -slope * distf * LOG2_E
            for f in range(3):
                bias_buf[f, kv * KEY_SPAN:(kv + 1) * KEY_SPAN, :] = jnp.where(
                    in_win & (r >= f * Q_SUB), b, NEG_INF)

        a0 = lax.broadcasted_iota(jnp.int32, (D_ATTN, D_ATTN), 0)
        a1 = lax.broadcasted_iota(jnp.int32, (D_ATTN, D_ATTN), 1)

        def source(n):
            return ((n >> 6) & 1) * (GQA_GROUP * HEAD_DIM) + (n >> 7) * HEAD_DIM + (n & (HEAD_DIM - 1))

        perm = jnp.where(a0 == source(a1), 1.0, 0.0).astype(jnp.bfloat16)
        perm_t = jnp.where(a1 == source(a0), 1.0, 0.0).astype(jnp.bfloat16)
        for lo, hi in ((OFF_U, OFF_PG), (OFF_PG, OFF_Q), (OFF_Q, OFF_K), (OFF_K, OFF_AG), (OFF_AG, D_IN)):
            w = win_ref[:, lo:hi].astype(jnp.bfloat16)
            if lo in (OFF_Q, OFF_AG):
                w = jnp.dot(w, perm, preferred_element_type=jnp.float32).astype(jnp.bfloat16)
            wi_buf[:, lo:hi] = w
        wo_buf[:D_POOL, :] = wout_ref[:D_POOL, :].astype(jnp.bfloat16)
        wo_buf[D_POOL:, :] = jnp.dot(perm_t, wout_ref[D_POOL:, :].astype(jnp.bfloat16),
                                     preferred_element_type=jnp.float32).astype(jnp.bfloat16)
        pw_buf[...] = jnp.zeros(pw_buf.shape, jnp.bfloat16)
        for g in range(len(POOL_WINDOWS)):
            d0 = (g % 2) * POOL_GROUP
            pw_buf[g // 2, d0:d0 + POOL_GROUP, d0:d0 + POOL_GROUP] = poolw_ref[g].astype(jnp.bfloat16)

    @pl.when(t == 0)
    def _zero_halo():
        u_buf[0:POOL_HALO, :] = jnp.zeros((POOL_HALO, D_POOL), jnp.float32)
        kz_buf[:, 0:WINDOW, :] = jnp.zeros((N_KV_HEADS, WINDOW, LANES), jnp.bfloat16)
        vt_buf[:, 0:WINDOW] = jnp.zeros((LANES, WINDOW), jnp.bfloat16)

    x = x_ref[...]
    ms = jnp.mean(x * x, axis=-1, keepdims=True)
    h = (x * lax.rsqrt(ms + EPS) * gpre_ref[layer:layer + 1, :]).astype(jnp.bfloat16)

    def proj(lo, hi):
        return jnp.dot(h, wi_buf[:, lo:hi], preferred_element_type=jnp.float32)

    kvf = proj(OFF_K, OFF_AG)
    q_buf[...] = (proj(OFF_Q, OFF_K) * (HEAD_DIM ** -0.5 * LOG2_E)).astype(jnp.bfloat16)
    kf, vf = kvf[:, :LANES], kvf[:, LANES:]
    lane = lax.broadcasted_iota(jnp.int32, (1, LANES), 1)
    kz_buf[0, WINDOW:WINDOW + tile, :] = jnp.where(lane < HEAD_DIM, kf, 0.0).astype(jnp.bfloat16)
    kz_buf[1, WINDOW:WINDOW + tile, :] = jnp.where(lane >= HEAD_DIM, kf, 0.0).astype(jnp.bfloat16)
    vt_buf[:, WINDOW:WINDOW + tile] = vf.T.astype(jnp.bfloat16)

    def attn_gate_item():
        ag_buf[...] = _silu(proj(OFF_AG, D_IN))

    def pool_u_item():
        u_buf[POOL_HALO:POOL_HALO + tile, :] = proj(OFF_U, OFF_PG)

    pos1 = (lax.broadcasted_iota(jnp.int32, (tile, POOL_GROUP), 0) + (t * tile + 1)
            ).astype(jnp.float32)

    def pool_window_item():
        for g, w in enumerate(POOL_WINDOWS):
            cols = slice(g * POOL_GROUP, (g + 1) * POOL_GROUP)
            ext = u_buf[:, cols]
            acc = ext
            shift = 1
            while shift < w:
                acc = acc + pltpu.roll(acc, shift, axis=0)
                shift *= 2
            cur = ext[POOL_HALO:]
            pooled = acc[POOL_HALO:] / jnp.minimum(pos1, float(w)) - cur
            pooled_buf[:, cols] = pooled.astype(jnp.bfloat16)

    def pool_mix_item():
        for pair in range(len(POOL_WINDOWS) // 2):
            cols = slice(pair * 2 * POOL_GROUP, (pair + 1) * 2 * POOL_GROUP)
            mixed = jnp.dot(pooled_buf[:, cols], pw_buf[pair], preferred_element_type=jnp.float32)
            scale = pscale_ref[layer:layer + 1, cols]
            mixp_buf[:, cols] = (mixed * scale * pg_buf[:, cols]).astype(jnp.bfloat16)

    def pool_gate_item():
        pg_buf[...] = _silu(proj(OFF_PG, OFF_Q))

    col = lax.broadcasted_iota(jnp.int32, (1, GQA_GROUP * Q_SUB), 1)
    sink_rows = []
    for kv in range(N_KV_HEADS):
        row = jnp.zeros((1, GQA_GROUP * Q_SUB), jnp.float32)
        for g in range(GQA_GROUP):
            row = jnp.where(col >= g * Q_SUB, sinks_ref[layer, kv * GQA_GROUP + g], row)
        sink_rows.append(row * LOG2_E)
    first = (t == 0)
    nt_dims = (((1,), (1,)), ((), ()))
    pad = jnp.zeros((Q_SUB, GQA_GROUP * Q_SUB), jnp.bfloat16)
    ones_rows = jnp.ones((BF16_ROWS, 2 * LANES), jnp.bfloat16)

    def attn_scores(sb):
        r0 = sb * Q_SUB
        qrows = slice(r0, r0 + Q_SUB)
        krows = slice(r0, r0 + KEY_SPAN)
        variant = jnp.where(first, 2 - sb, 0) if sb < 2 else 0
        q4 = jnp.concatenate([q_buf[qrows, g * LANES:(g + 1) * LANES] for g in range(GQA_GROUP)],
                             axis=0)
        kk = jnp.concatenate([kz_buf[0, krows, :], kz_buf[1, krows, :]], axis=0)
        s = lax.dot_general(kk, q4, nt_dims, preferred_element_type=jnp.float32)
        return s + bias_buf[variant]

    def attn_finish(sb, s):
        r0 = sb * Q_SUB
        qrows = slice(r0, r0 + Q_SUB)
        outs = []
        for kv in range(N_KV_HEADS):
            sh = s[kv * KEY_SPAN:(kv + 1) * KEY_SPAN]
            m = jnp.maximum(jnp.max(sh, axis=0, keepdims=True), sink_rows[kv])
            pb = jnp.exp2(sh - m).astype(jnp.bfloat16)
            if sb % 2 == 0:
                pb = jnp.concatenate([pb, pad], axis=0)
                c0 = r0
            else:
                pb = jnp.concatenate([pad, pb], axis=0)
                c0 = r0 - Q_SUB
            vt = jnp.concatenate([vt_buf[kv * HEAD_DIM:(kv + 1) * HEAD_DIM, c0:c0 + 2 * LANES], ones_rows],
                                 axis=0)
            o = jnp.dot(vt, pb, preferred_element_type=jnp.float32)
            l = o[HEAD_DIM:HEAD_DIM + 1] + jnp.exp2(sink_rows[kv] - m)
            outs.append(o[:HEAD_DIM] * (1.0 / l))
        for pr in range(GQA_GROUP // 2):
            both = jnp.concatenate([o[:, pr * LANES:(pr + 1) * LANES] for o in outs], axis=0)
            both_t = both.T
            for half in range(2):
                g = 2 * pr + half
                val = both_t[half * Q_SUB:(half + 1) * Q_SUB] * ag_buf[qrows, g * LANES:(g + 1) * LANES]
                mixa_buf[qrows, g * LANES:(g + 1) * LANES] = val.astype(jnp.bfloat16)

    attn_gate_item()
    dense_items = [[pool_u_item], [pool_gate_item], [pool_window_item, pool_mix_item], []]
    group = n_sub // len(dense_items)
    for p, items in enumerate(dense_items):
        subs = range(p * group, (p + 1) * group)
        scores = [attn_scores(sb) for sb in subs]
        for item in items:
            item()
        for sb, s in zip(subs, scores):
            attn_finish(sb, s)

    u_buf[0:POOL_HALO, :] = u_buf[tile:tile + POOL_HALO, :]
    kz_buf[:, 0:WINDOW, :] = kz_buf[:, tile:tile + WINDOW, :]
    vt_buf[:, 0:WINDOW] = vt_buf[:, tile:tile + WINDOW]

    for r0 in range(0, tile, OUT_ROWS):
        rows = slice(r0, r0 + OUT_ROWS)
        y = (jnp.dot(mixp_buf[rows, :], wo_buf[:D_POOL, :], preferred_element_type=jnp.float32)
             + jnp.dot(mixa_buf[rows, :], wo_buf[D_POOL:, :], preferred_element_type=jnp.float32))
        ms2 = jnp.mean(y * y, axis=-1, keepdims=True)
        o_ref[rows, :] = x_ref[rows, :] + y * lax.rsqrt(ms2 + EPS) * gpost_ref[layer:layer + 1, :]


def _layer(layer, x, w_in, pool_w, pool_scale, sinks, w_out, g_pre, g_post):
    batch, seq, d = x.shape
    tile = SEQ_TILE
    assert seq % tile == 0 and tile % (2 * Q_SUB) == 0 and d == D_MODEL

    def of_layer(*shape):
        return pl.BlockSpec((None,) + shape, lambda b, t: (layer,) + (0,) * len(shape),
                            pipeline_mode=pl.Buffered(1))

    def whole(a):
        return pl.BlockSpec(a.shape, lambda b, t: (0,) * a.ndim)

    return pl.pallas_call(
        functools.partial(_layer_kernel, layer),
        out_shape=jax.ShapeDtypeStruct(x.shape, x.dtype),
        grid=(batch, seq // tile),
        in_specs=[
            pl.BlockSpec((None, tile, d), lambda b, t: (b, t, 0)),
            of_layer(D_MODEL, D_IN),
            of_layer(len(POOL_WINDOWS), POOL_GROUP, POOL_GROUP),
            whole(pool_scale),
            pl.BlockSpec(memory_space=pltpu.SMEM),
            of_layer(D_MODEL, D_MODEL),
            whole(g_pre),
            whole(g_post),
        ],
        out_specs=pl.BlockSpec((None, tile, d), lambda b, t: (b, t, 0)),
        scratch_shapes=[
            pltpu.VMEM((POOL_HALO + tile, D_POOL), jnp.float32),
            pltpu.VMEM((tile, D_ATTN), jnp.bfloat16),
            pltpu.VMEM((N_KV_HEADS, WINDOW + tile, LANES), jnp.bfloat16),
            pltpu.VMEM((LANES, WINDOW + tile), jnp.bfloat16),
            pltpu.VMEM((tile, D_ATTN), jnp.float32),
            pltpu.VMEM((tile, D_POOL), jnp.float32),
            pltpu.VMEM((tile, D_POOL), jnp.bfloat16),
            pltpu.VMEM((tile, D_POOL), jnp.bfloat16),
            pltpu.VMEM((tile, D_ATTN), jnp.bfloat16),
            pltpu.VMEM((3, N_KV_HEADS * KEY_SPAN, GQA_GROUP * Q_SUB), jnp.float32),
            pltpu.VMEM((D_MODEL, D_IN), jnp.bfloat16),
            pltpu.VMEM((D_MODEL, D_MODEL), jnp.bfloat16),
            pltpu.VMEM((len(POOL_WINDOWS) // 2, 2 * POOL_GROUP, 2 * POOL_GROUP), jnp.bfloat16),
        ],
        compiler_params=pltpu.CompilerParams(
            dimension_semantics=("arbitrary", "arbitrary"),
            vmem_limit_bytes=VMEM_LIMIT_BYTES),
        name="hybrid_layer",
    )(x, w_in, pool_w, pool_scale, sinks, w_out, g_pre, g_post)


@jax.jit
def kernel(x, w_in, pool_w, pool_scale, attn_sinks, w_out, norm_pre, norm_post):
    for layer in range(w_in.shape[0]):
        x = _layer(layer, x, w_in, pool_w, pool_scale, attn_sinks, w_out, norm_pre, norm_post)
    return x
```

```python
import functools

import jax
import jax.numpy as jnp
from jax import lax
from jax.experimental import pallas as pl
from jax.experimental.pallas import tpu as pltpu

D_MODEL = 1024
D_POOL = 512
POOL_WINDOWS = (2, 4, 8, 16)
POOL_GROUP = 128
HEAD_DIM = 64
D_ATTN = 512
N_HEADS = 8
N_KV_HEADS = 2
GQA_GROUP = N_HEADS // N_KV_HEADS
WINDOW = 128
D_IN = 2304
EPS = 1e-6
NEG_INF = -1e30
LOG2_E = 1.4426950408889634

OFF_U, OFF_PG, OFF_Q, OFF_K, OFF_V, OFF_AG = 0, 512, 1024, 1536, 1664, 1792

LANES = 128
BF16_ROWS = 16
POOL_HALO = 16
SEQ_TILE = 1024
Q_SUB = 64
OUT_ROWS = 512
KEY_SPAN = Q_SUB + WINDOW
VMEM_LIMIT_BYTES = 56 * 1024 * 1024


def _silu(x):
    return x * (1.0 / (1.0 + jnp.exp(-x)))


def _layer_kernel(layer, x_ref, win_ref, poolw_ref, pscale_ref, sinks_ref, wout_ref,
                  gpre_ref, gpost_ref, o_ref,
                  u_buf, q_buf, kz_buf, vt_buf, ag_buf, pg_buf, pooled_buf, mixp_buf, mixa_buf, bias_buf, wi_buf,
                  wo_buf, pw_buf):
    tile = x_ref.shape[0]
    n_sub = tile // Q_SUB
    t = pl.program_id(1)

    @pl.when((pl.program_id(0) == 0) & (t == 0))
    def _first_step():
        r = lax.broadcasted_iota(jnp.int32, (KEY_SPAN, GQA_GROUP * Q_SUB), 0)
        col = lax.broadcasted_iota(jnp.int32, (KEY_SPAN, GQA_GROUP * Q_SUB), 1)
        dist = (col & (Q_SUB - 1)) + WINDOW - r
        in_win = (dist >= 0) & (dist < WINDOW)
        distf = dist.astype(jnp.float32)
        for kv in range(N_KV_HEADS):
            slope = jnp.zeros_like(distf)
            for g in range(GQA_GROUP):
                j = kv * GQA_GROUP + g
                slope = jnp.where(col >= g * Q_SUB, 2.0 ** (-8.0 * (j + 1) / N_HEADS), slope)
            b = -slope * distf * LOG2_E
            for f in range(3):
                bias_buf[f, kv * KEY_SPAN:(kv + 1) * KEY_SPAN, :] = jnp.where(
                    in_win & (r >= f * Q_SUB), b, NEG_INF)

        a0 = lax.broadcasted_iota(jnp.int32, (D_ATTN, D_ATTN), 0)
        a1 = lax.broadcasted_iota(jnp.int32, (D_ATTN, D_ATTN), 1)

        def source(n):
            return ((n >> 6) & 1) * (GQA_GROUP * HEAD_DIM) + (n >> 7) * HEAD_DIM + (n & (HEAD_DIM - 1))

        perm = jnp.where(a0 == source(a1), 1.0, 0.0).astype(jnp.bfloat16)
        perm_t = jnp.where(a1 == source(a0), 1.0, 0.0).astype(jnp.bfloat16)
        for lo, hi in ((OFF_U, OFF_PG), (OFF_PG, OFF_Q), (OFF_Q, OFF_K), (OFF_K, OFF_AG), (OFF_AG, D_IN)):
            w = win_ref[:, lo:hi].astype(jnp.bfloat16)
            if lo in (OFF_Q, OFF_AG):
                w = jnp.dot(w, perm, preferred_element_type=jnp.float32).astype(jnp.bfloat16)
            wi_buf[:, lo:hi] = w
        wo_buf[:D_POOL, :] = wout_ref[:D_POOL, :].astype(jnp.bfloat16)
        wo_buf[D_POOL:, :] = jnp.dot(perm_t, wout_ref[D_POOL:, :].astype(jnp.bfloat16),
                                     preferred_element_type=jnp.float32).astype(jnp.bfloat16)
        pw_buf[...] = jnp.zeros(pw_buf.shape, jnp.bfloat16)
        for g in range(len(POOL_WINDOWS)):
            d0 = (g % 2) * POOL_GROUP
            pw_buf[g // 2, d0:d0 + POOL_GROUP, d0:d0 + POOL_GROUP] = poolw_ref[g].astype(jnp.bfloat16)

    @pl.when(t == 0)
    def _zero_halo():
        u_buf[0:POOL_HALO, :] = jnp.zeros((POOL_HALO, D_POOL), jnp.float32)
        kz_buf[:, 0:WINDOW, :] = jnp.zeros((N_KV_HEADS, WINDOW, LANES), jnp.bfloat16)
        vt_buf[:, 0:WINDOW] = jnp.zeros((LANES, WINDOW), jnp.bfloat16)

    x = x_ref[...]
    ms = jnp.mean(x * x, axis=-1, keepdims=True)
    h = (x * lax.rsqrt(ms + EPS) * gpre_ref[layer:layer + 1, :]).astype(jnp.bfloat16)

    def proj(lo, hi):
        return jnp.dot(h, wi_buf[:, lo:hi], preferred_element_type=jnp.float32)

    qkv = proj(OFF_Q, OFF_AG)
    q_buf[...] = (qkv[:, :D_ATTN] * (HEAD_DIM ** -0.5 * LOG2_E)).astype(jnp.bfloat16)
    kf, vf = qkv[:, D_ATTN:D_ATTN + LANES], qkv[:, D_ATTN + LANES:]
    lane = lax.broadcasted_iota(jnp.int32, (1, LANES), 1)
    kz_buf[0, WINDOW:WINDOW + tile, :] = jnp.where(lane < HEAD_DIM, kf, 0.0).astype(jnp.bfloat16)
    kz_buf[1, WINDOW:WINDOW + tile, :] = jnp.where(lane >= HEAD_DIM, kf, 0.0).astype(jnp.bfloat16)
    vt_buf[:, WINDOW:WINDOW + tile] = vf.T.astype(jnp.bfloat16)

    def attn_gate_item():
        ag_buf[...] = _silu(proj(OFF_AG, D_IN))

    def pool_u_item():
        u_buf[POOL_HALO:POOL_HALO + tile, :] = proj(OFF_U, OFF_PG)

    pos1 = (lax.broadcasted_iota(jnp.int32, (tile, POOL_GROUP), 0) + (t * tile + 1)
            ).astype(jnp.float32)

    def pool_window_item():
        for g, w in enumerate(POOL_WINDOWS):
            cols = slice(g * POOL_GROUP, (g + 1) * POOL_GROUP)
            ext = u_buf[:, cols]
            acc = ext
            shift = 1
            while shift < w:
                acc = acc + pltpu.roll(acc, shift, axis=0)
                shift *= 2
            cur = ext[POOL_HALO:]
            pooled = acc[POOL_HALO:] / jnp.minimum(pos1, float(w)) - cur
            pooled_buf[:, cols] = pooled.astype(jnp.bfloat16)

    def pool_mix_item():
        for pair in range(len(POOL_WINDOWS) // 2):
            cols = slice(pair * 2 * POOL_GROUP, (pair + 1) * 2 * POOL_GROUP)
            mixed = jnp.dot(pooled_buf[:, cols], pw_buf[pair], preferred_element_type=jnp.float32)
            scale = pscale_ref[layer:layer + 1, cols]
            mixp_buf[:, cols] = (mixed * scale * pg_buf[:, cols]).astype(jnp.bfloat16)

    def pool_gate_item():
        pg_buf[...] = _silu(proj(OFF_PG, OFF_Q))

    col = lax.broadcasted_iota(jnp.int32, (1, GQA_GROUP * Q_SUB), 1)
    sink_rows = []
    for kv in range(N_KV_HEADS):
        row = jnp.zeros((1, GQA_GROUP * Q_SUB), jnp.float32)
        for g in range(GQA_GROUP):
            row = jnp.where(col >= g * Q_SUB, sinks_ref[layer, kv * GQA_GROUP + g], row)
        sink_rows.append(row * LOG2_E)
    first = (t == 0)
    nt_dims = (((1,), (1,)), ((), ()))
    pad = jnp.zeros((Q_SUB, GQA_GROUP * Q_SUB), jnp.bfloat16)
    ones_rows = jnp.ones((BF16_ROWS, 2 * LANES), jnp.bfloat16)

    def attn_scores(sb):
        r0 = sb * Q_SUB
        qrows = slice(r0, r0 + Q_SUB)
        krows = slice(r0, r0 + KEY_SPAN)
        variant = jnp.where(first, 2 - sb, 0) if sb < 2 else 0
        q4 = jnp.concatenate([q_buf[qrows, g * LANES:(g + 1) * LANES] for g in range(GQA_GROUP)],
                             axis=0)
        kk = jnp.concatenate([kz_buf[0, krows, :], kz_buf[1, krows, :]], axis=0)
        s = lax.dot_general(kk, q4, nt_dims, preferred_element_type=jnp.float32)
        return s + bias_buf[variant]

    def attn_finish(sb, s):
        r0 = sb * Q_SUB
        qrows = slice(r0, r0 + Q_SUB)
        outs = []
        for kv in range(N_KV_HEADS):
            sh = s[kv * KEY_SPAN:(kv + 1) * KEY_SPAN]
            m = jnp.maximum(jnp.max(sh, axis=0, keepdims=True), sink_rows[kv])
            pb = jnp.exp2(sh - m).astype(jnp.bfloat16)
            if sb % 2 == 0:
                pb = jnp.concatenate([pb, pad], axis=0)
                c0 = r0
            else:
                pb = jnp.concatenate([pad, pb], axis=0)
                c0 = r0 - Q_SUB
            vt = jnp.concatenate([vt_buf[kv * HEAD_DIM:(kv + 1) * HEAD_DIM, c0:c0 + 2 * LANES], ones_rows],
                                 axis=0)
            o = jnp.dot(vt, pb, preferred_element_type=jnp.float32)
            l = o[HEAD_DIM:HEAD_DIM + 1] + jnp.exp2(sink_rows[kv] - m)
            outs.append(o[:HEAD_DIM] * (1.0 / l))
        for pr in range(GQA_GROUP // 2):
            both = jnp.concatenate([o[:, pr * LANES:(pr + 1) * LANES] for o in outs], axis=0)
            both_t = both.T
            for half in range(2):
                g = 2 * pr + half
                val = both_t[half * Q_SUB:(half + 1) * Q_SUB] * ag_buf[qrows, g * LANES:(g + 1) * LANES]
                mixa_buf[qrows, g * LANES:(g + 1) * LANES] = val.astype(jnp.bfloat16)

    attn_gate_item()
    dense_items = [[pool_u_item], [pool_gate_item], [pool_window_item, pool_mix_item], []]
    group = n_sub // len(dense_items)
    for p, items in enumerate(dense_items):
        subs = range(p * group, (p + 1) * group)
        scores = [attn_scores(sb) for sb in subs]
        for item in items:
            item()
        for sb, s in zip(subs, scores):
            attn_finish(sb, s)

    u_buf[0:POOL_HALO, :] = u_buf[tile:tile + POOL_HALO, :]
    kz_buf[:, 0:WINDOW, :] = kz_buf[:, tile:tile + WINDOW, :]
    vt_buf[:, 0:WINDOW] = vt_buf[:, tile:tile + WINDOW]

    for r0 in range(0, tile, OUT_ROWS):
        rows = slice(r0, r0 + OUT_ROWS)
        y = (jnp.dot(mixp_buf[rows, :], wo_buf[:D_POOL, :], preferred_element_type=jnp.float32)
             + jnp.dot(mixa_buf[rows, :], wo_buf[D_POOL:, :], preferred_element_type=jnp.float32))
        ms2 = jnp.mean(y * y, axis=-1, keepdims=True)
        o_ref[rows, :] = x_ref[rows, :] + y * lax.rsqrt(ms2 + EPS) * gpost_ref[layer:layer + 1, :]


def _layer(layer, x, w_in, pool_w, pool_scale, sinks, w_out, g_pre, g_post):
    batch, seq, d = x.shape
    tile = SEQ_TILE
    assert seq % tile == 0 and tile % (2 * Q_SUB) == 0 and d == D_MODEL

    def of_layer(*shape):
        return pl.BlockSpec((None,) + shape, lambda b, t: (layer,) + (0,) * len(shape),
                            pipeline_mode=pl.Buffered(1))

    def whole(a):
        return pl.BlockSpec(a.shape, lambda b, t: (0,) * a.ndim)

    return pl.pallas_call(
        functools.partial(_layer_kernel, layer),
        out_shape=jax.ShapeDtypeStruct(x.shape, x.dtype),
        grid=(batch, seq // tile),
        in_specs=[
            pl.BlockSpec((None, tile, d), lambda b, t: (b, t, 0)),
            of_layer(D_MODEL, D_IN),
            of_layer(len(POOL_WINDOWS), POOL_GROUP, POOL_GROUP),
            whole(pool_scale),
            pl.BlockSpec(memory_space=pltpu.SMEM),
            of_layer(D_MODEL, D_MODEL),
            whole(g_pre),
            whole(g_post),
        ],
        out_specs=pl.BlockSpec((None, tile, d), lambda b, t: (b, t, 0)),
        scratch_shapes=[
            pltpu.VMEM((POOL_HALO + tile, D_POOL), jnp.float32),
            pltpu.VMEM((tile, D_ATTN), jnp.bfloat16),
            pltpu.VMEM((N_KV_HEADS, WINDOW + tile, LANES), jnp.bfloat16),
            pltpu.VMEM((LANES, WINDOW + tile), jnp.bfloat16),
            pltpu.VMEM((tile, D_ATTN), jnp.float32),
            pltpu.VMEM((tile, D_POOL), jnp.float32),
            pltpu.VMEM((tile, D_POOL), jnp.bfloat16),
            pltpu.VMEM((tile, D_POOL), jnp.bfloat16),
            pltpu.VMEM((tile, D_ATTN), jnp.bfloat16),
            pltpu.VMEM((3, N_KV_HEADS * KEY_SPAN, GQA_GROUP * Q_SUB), jnp.float32),
            pltpu.VMEM((D_MODEL, D_IN), jnp.bfloat16),
            pltpu.VMEM((D_MODEL, D_MODEL), jnp.bfloat16),
            pltpu.VMEM((len(POOL_WINDOWS) // 2, 2 * POOL_GROUP, 2 * POOL_GROUP), jnp.bfloat16),
        ],
        compiler_params=pltpu.CompilerParams(
            dimension_semantics=("arbitrary", "arbitrary"),
            vmem_limit_bytes=VMEM_LIMIT_BYTES),
        name="hybrid_layer",
    )(x, w_in, pool_w, pool_scale, sinks, w_out, g_pre, g_post)


@jax.jit
def kernel(x, w_in, pool_w, pool_scale, attn_sinks, w_out, norm_pre, norm_post):
    for layer in range(w_in.shape[0]):
        x = _layer(layer, x, w_in, pool_w, pool_scale, attn_sinks, w_out, norm_pre, norm_post)
    return x
```

```python
import functools

import jax
import jax.numpy as jnp
from jax import lax
from jax.experimental import pallas as pl
from jax.experimental.pallas import tpu as pltpu

D_MODEL = 1024
D_POOL = 512
POOL_WINDOWS = (2, 4, 8, 16)
POOL_GROUP = 128
HEAD_DIM = 64
D_ATTN = 512
N_HEADS = 8
N_KV_HEADS = 2
GQA_GROUP = N_HEADS // N_KV_HEADS
WINDOW = 128
D_IN = 2304
EPS = 1e-6
NEG_INF = -1e30
LOG2_E = 1.4426950408889634

OFF_U, OFF_PG, OFF_Q, OFF_K, OFF_V, OFF_AG = 0, 512, 1024, 1536, 1664, 1792

LANES = 128
BF16_ROWS = 16
POOL_HALO = 16
SEQ_TILE = 1024
Q_SUB = 64
OUT_ROWS = 512
KEY_SPAN = Q_SUB + WINDOW
VMEM_LIMIT_BYTES = 56 * 1024 * 1024


def _silu(x):
    return x * (1.0 / (1.0 + jnp.exp(-x)))


def _layer_kernel(layer, x_ref, win_ref, poolw_ref, pscale_ref, sinks_ref, wout_ref,
                  gpre_ref, gpost_ref, o_ref,
                  u_buf, q_buf, kz_buf, vt_buf, ag_buf, pg_buf, pooled_buf, mixp_buf, mixa_buf, bias_buf, wi_buf,
                  wo_buf, pw_buf):
    tile = x_ref.shape[0]
    n_sub = tile // Q_SUB
    t = pl.program_id(1)

    @pl.when((pl.program_id(0) == 0) & (t == 0))
    def _first_step():
        r = lax.broadcasted_iota(jnp.int32, (KEY_SPAN, GQA_GROUP * Q_SUB), 0)
        col = lax.broadcasted_iota(jnp.int32, (KEY_SPAN, GQA_GROUP * Q_SUB), 1)
        dist = (col & (Q_SUB - 1)) + WINDOW - r
        in_win = (dist >= 0) & (dist < WINDOW)
        distf = dist.astype(jnp.float32)
        for kv in range(N_KV_HEADS):
            slope = jnp.zeros_like(distf)
            for g in range(GQA_GROUP):
                j = kv * GQA_GROUP + g
                slope = jnp.where(col >= g * Q_SUB, 2.0 ** (-8.0 * (j + 1) / N_HEADS), slope)
            b = -slope * distf * LOG2_E
            for f in range(3):
                bias_buf[f, kv * KEY_SPAN:(kv + 1) * KEY_SPAN, :] = jnp.where(
                    in_win & (r >= f * Q_SUB), b, NEG_INF)

        a0 = lax.broadcasted_iota(jnp.int32, (D_ATTN, D_ATTN), 0)
        a1 = lax.broadcasted_iota(jnp.int32, (D_ATTN, D_ATTN), 1)

        def source(n):
            return ((n >> 6) & 1) * (GQA_GROUP * HEAD_DIM) + (n >> 7) * HEAD_DIM + (n & (HEAD_DIM - 1))

        perm = jnp.where(a0 == source(a1), 1.0, 0.0).astype(jnp.bfloat16)
        perm_t = jnp.where(a1 == source(a0), 1.0, 0.0).astype(jnp.bfloat16)
        for lo, hi in ((OFF_U, OFF_PG), (OFF_PG, OFF_Q), (OFF_Q, OFF_K), (OFF_K, OFF_AG), (OFF_AG, D_IN)):
            w = win_ref[:, lo:hi].astype(jnp.bfloat16)
            if lo in (OFF_Q, OFF_AG):
                w = jnp.dot(w, perm, preferred_element_type=jnp.float32).astype(jnp.bfloat16)
            wi_buf[:, lo:hi] = w
        wo_buf[:D_POOL, :] = wout_ref[:D_POOL, :].astype(jnp.bfloat16)
        wo_buf[D_POOL:, :] = jnp.dot(perm_t, wout_ref[D_POOL:, :].astype(jnp.bfloat16),
                                     preferred_element_type=jnp.float32).astype(jnp.bfloat16)
        pw_buf[...] = jnp.zeros(pw_buf.shape, jnp.bfloat16)
        for g in range(len(POOL_WINDOWS)):
            d0 = (g % 2) * POOL_GROUP
            pw_buf[g // 2, d0:d0 + POOL_GROUP, d0:d0 + POOL_GROUP] = poolw_ref[g].astype(jnp.bfloat16)

    @pl.when(t == 0)
    def _zero_halo():
        u_buf[0:POOL_HALO, :] = jnp.zeros((POOL_HALO, D_POOL), jnp.float32)
        kz_buf[:, 0:WINDOW, :] = jnp.zeros((N_KV_HEADS, WINDOW, LANES), jnp.bfloat16)
        vt_buf[:, 0:WINDOW] = jnp.zeros((LANES, WINDOW), jnp.bfloat16)

    x = x_ref[...]
    ms = jnp.mean(x * x, axis=-1, keepdims=True)
    h = (x * lax.rsqrt(ms + EPS) * gpre_ref[layer:layer + 1, :]).astype(jnp.bfloat16)

    def proj(lo, hi):
        return jnp.dot(h, wi_buf[:, lo:hi], preferred_element_type=jnp.float32)

    qkvg = proj(OFF_Q, D_IN)
    q_buf[...] = (qkvg[:, :D_ATTN] * (HEAD_DIM ** -0.5 * LOG2_E)).astype(jnp.bfloat16)
    kf, vf = qkvg[:, D_ATTN:D_ATTN + LANES], qkvg[:, D_ATTN + LANES:D_ATTN + 2 * LANES]
    ag_buf[...] = _silu(qkvg[:, OFF_AG - OFF_Q:])
    lane = lax.broadcasted_iota(jnp.int32, (1, LANES), 1)
    kz_buf[0, WINDOW:WINDOW + tile, :] = jnp.where(lane < HEAD_DIM, kf, 0.0).astype(jnp.bfloat16)
    kz_buf[1, WINDOW:WINDOW + tile, :] = jnp.where(lane >= HEAD_DIM, kf, 0.0).astype(jnp.bfloat16)
    vt_buf[:, WINDOW:WINDOW + tile] = vf.T.astype(jnp.bfloat16)

    def pool_u_item():
        u_buf[POOL_HALO:POOL_HALO + tile, :] = proj(OFF_U, OFF_PG)

    pos1 = (lax.broadcasted_iota(jnp.int32, (tile, POOL_GROUP), 0) + (t * tile + 1)
            ).astype(jnp.float32)

    def pool_window_item():
        for g, w in enumerate(POOL_WINDOWS):
            cols = slice(g * POOL_GROUP, (g + 1) * POOL_GROUP)
            ext = u_buf[:, cols]
            acc = ext
            shift = 1
            while shift < w:
                acc = acc + pltpu.roll(acc, shift, axis=0)
                shift *= 2
            cur = ext[POOL_HALO:]
            pooled = acc[POOL_HALO:] / jnp.minimum(pos1, float(w)) - cur
            pooled_buf[:, cols] = pooled.astype(jnp.bfloat16)

    def pool_mix_item():
        for pair in range(len(POOL_WINDOWS) // 2):
            cols = slice(pair * 2 * POOL_GROUP, (pair + 1) * 2 * POOL_GROUP)
            mixed = jnp.dot(pooled_buf[:, cols], pw_buf[pair], preferred_element_type=jnp.float32)
            scale = pscale_ref[layer:layer + 1, cols]
            mixp_buf[:, cols] = (mixed * scale * pg_buf[:, cols]).astype(jnp.bfloat16)

    def pool_gate_item():
        pg_buf[...] = _silu(proj(OFF_PG, OFF_Q))

    def pool_u_gate_item():
        ug = proj(OFF_U, OFF_Q)
        u_buf[POOL_HALO:POOL_HALO + tile, :] = ug[:, :D_POOL]
        pg_buf[...] = _silu(ug[:, D_POOL:])

    col = lax.broadcasted_iota(jnp.int32, (1, GQA_GROUP * Q_SUB), 1)
    sink_rows = []
    for kv in range(N_KV_HEADS):
        row = jnp.zeros((1, GQA_GROUP * Q_SUB), jnp.float32)
        for g in range(GQA_GROUP):
            row = jnp.where(col >= g * Q_SUB, sinks_ref[layer, kv * GQA_GROUP + g], row)
        sink_rows.append(row * LOG2_E)
    first = (t == 0)
    nt_dims = (((1,), (1,)), ((), ()))
    pad = jnp.zeros((Q_SUB, GQA_GROUP * Q_SUB), jnp.bfloat16)
    ones_rows = jnp.ones((BF16_ROWS, 2 * LANES), jnp.bfloat16)

    def attn_scores(sb):
        r0 = sb * Q_SUB
        qrows = slice(r0, r0 + Q_SUB)
        krows = slice(r0, r0 + KEY_SPAN)
        variant = jnp.where(first, 2 - sb, 0) if sb < 2 else 0
        q4 = jnp.concatenate([q_buf[qrows, g * LANES:(g + 1) * LANES] for g in range(GQA_GROUP)],
                             axis=0)
        kk = jnp.concatenate([kz_buf[0, krows, :], kz_buf[1, krows, :]], axis=0)
        s = lax.dot_general(kk, q4, nt_dims, preferred_element_type=jnp.float32)
        return s + bias_buf[variant]

    def attn_finish(sb, s):
        r0 = sb * Q_SUB
        qrows = slice(r0, r0 + Q_SUB)
        outs = []
        for kv in range(N_KV_HEADS):
            sh = s[kv * KEY_SPAN:(kv + 1) * KEY_SPAN]
            m = jnp.maximum(jnp.max(sh, axis=0, keepdims=True), sink_rows[kv])
            pb = jnp.exp2(sh - m).astype(jnp.bfloat16)
            if sb % 2 == 0:
                pb = jnp.concatenate([pb, pad], axis=0)
                c0 = r0
            else:
                pb = jnp.concatenate([pad, pb], axis=0)
                c0 = r0 - Q_SUB
            vt = jnp.concatenate([vt_buf[kv * HEAD_DIM:(kv + 1) * HEAD_DIM, c0:c0 + 2 * LANES], ones_rows],
                                 axis=0)
            o = jnp.dot(vt, pb, preferred_element_type=jnp.float32)
            l = o[HEAD_DIM:HEAD_DIM + 1] + jnp.exp2(sink_rows[kv] - m)
            outs.append(o[:HEAD_DIM] * (1.0 / l))
        for pr in range(GQA_GROUP // 2):
            both = jnp.concatenate([o[:, pr * LANES:(pr + 1) * LANES] for o in outs], axis=0)
            both_t = both.T
            for half in range(2):
                g = 2 * pr + half
                val = both_t[half * Q_SUB:(half + 1) * Q_SUB] * ag_buf[qrows, g * LANES:(g + 1) * LANES]
                mixa_buf[qrows, g * LANES:(g + 1) * LANES] = val.astype(jnp.bfloat16)

    dense_items = [[pool_u_gate_item], [pool_window_item, pool_mix_item]]
    group = n_sub // len(dense_items)
    for p, items in enumerate(dense_items):
        subs = range(p * group, (p + 1) * group)
        scores = [attn_scores(sb) for sb in subs]
        for item in items:
            item()
        for sb, s in zip(subs, scores):
            attn_finish(sb, s)

    u_buf[0:POOL_HALO, :] = u_buf[tile:tile + POOL_HALO, :]
    kz_buf[:, 0:WINDOW, :] = kz_buf[:, tile:tile + WINDOW, :]
    vt_buf[:, 0:WINDOW] = vt_buf[:, tile:tile + WINDOW]

    for r0 in range(0, tile, OUT_ROWS):
        rows = slice(r0, r0 + OUT_ROWS)
        y = (jnp.dot(mixp_buf[rows, :], wo_buf[:D_POOL, :], preferred_element_type=jnp.float32)
             + jnp.dot(mixa_buf[rows, :], wo_buf[D_POOL:, :], preferred_element_type=jnp.float32))
        ms2 = jnp.mean(y * y, axis=-1, keepdims=True)
        o_ref[rows, :] = x_ref[rows, :] + y * lax.rsqrt(ms2 + EPS) * gpost_ref[layer:layer + 1, :]


def _layer(layer, x, w_in, pool_w, pool_scale, sinks, w_out, g_pre, g_post):
    batch, seq, d = x.shape
    tile = SEQ_TILE
    assert seq % tile == 0 and tile % (2 * Q_SUB) == 0 and d == D_MODEL

    def of_layer(*shape):
        return pl.BlockSpec((None,) + shape, lambda b, t: (layer,) + (0,) * len(shape),
                            pipeline_mode=pl.Buffered(1))

    def whole(a):
        return pl.BlockSpec(a.shape, lambda b, t: (0,) * a.ndim)

    return pl.pallas_call(
        functools.partial(_layer_kernel, layer),
        out_shape=jax.ShapeDtypeStruct(x.shape, x.dtype),
        grid=(batch, seq // tile),
        in_specs=[
            pl.BlockSpec((None, tile, d), lambda b, t: (b, t, 0)),
            of_layer(D_MODEL, D_IN),
            of_layer(len(POOL_WINDOWS), POOL_GROUP, POOL_GROUP),
            whole(pool_scale),
            pl.BlockSpec(memory_space=pltpu.SMEM),
            of_layer(D_MODEL, D_MODEL),
            whole(g_pre),
            whole(g_post),
        ],
        out_specs=pl.BlockSpec((None, tile, d), lambda b, t: (b, t, 0)),
        scratch_shapes=[
            pltpu.VMEM((POOL_HALO + tile, D_POOL), jnp.float32),
            pltpu.VMEM((tile, D_ATTN), jnp.bfloat16),
            pltpu.VMEM((N_KV_HEADS, WINDOW + tile, LANES), jnp.bfloat16),
            pltpu.VMEM((LANES, WINDOW + tile), jnp.bfloat16),
            pltpu.VMEM((tile, D_ATTN), jnp.float32),
            pltpu.VMEM((tile, D_POOL), jnp.float32),
            pltpu.VMEM((tile, D_POOL), jnp.bfloat16),
            pltpu.VMEM((tile, D_POOL), jnp.bfloat16),
            pltpu.VMEM((tile, D_ATTN), jnp.bfloat16),
            pltpu.VMEM((3, N_KV_HEADS * KEY_SPAN, GQA_GROUP * Q_SUB), jnp.float32),
            pltpu.VMEM((D_MODEL, D_IN), jnp.bfloat16),
            pltpu.VMEM((D_MODEL, D_MODEL), jnp.bfloat16),
            pltpu.VMEM((len(POOL_WINDOWS) // 2, 2 * POOL_GROUP, 2 * POOL_GROUP), jnp.bfloat16),
        ],
        compiler_params=pltpu.CompilerParams(
            dimension_semantics=("arbitrary", "arbitrary"),
            vmem_limit_bytes=VMEM_LIMIT_BYTES),
        name="hybrid_layer",
    )(x, w_in, pool_w, pool_scale, sinks, w_out, g_pre, g_post)


@jax.jit
def kernel(x, w_in, pool_w, pool_scale, attn_sinks, w_out, norm_pre, norm_post):
    for layer in range(w_in.shape[0]):
        x = _layer(layer, x, w_in, pool_w, pool_scale, attn_sinks, w_out, norm_pre, norm_post)
    return x
```

```python
import functools

import jax
import jax.numpy as jnp
from jax import lax
from jax.experimental import pallas as pl
from jax.experimental.pallas import tpu as pltpu

D_MODEL = 1024
D_POOL = 512
POOL_WINDOWS = (2, 4, 8, 16)
POOL_GROUP = 128
HEAD_DIM = 64
D_ATTN = 512
N_HEADS = 8
N_KV_HEADS = 2
GQA_GROUP = N_HEADS // N_KV_HEADS
WINDOW = 128
D_IN = 2304
EPS = 1e-6
NEG_INF = -1e30
LOG2_E = 1.4426950408889634

OFF_U, OFF_PG, OFF_Q, OFF_K, OFF_V, OFF_AG = 0, 512, 1024, 1536, 1664, 1792

LANES = 128
BF16_ROWS = 16
POOL_HALO = 16
SEQ_TILE = 1024
Q_SUB = 64
OUT_ROWS = 512
KEY_SPAN = Q_SUB + WINDOW
VMEM_LIMIT_BYTES = 56 * 1024 * 1024


def _silu(x):
    return x * (1.0 / (1.0 + jnp.exp(-x)))


def _layer_kernel(layer, x_ref, win_ref, poolw_ref, pscale_ref, sinks_ref, wout_ref,
                  gpre_ref, gpost_ref, o_ref,
                  u_buf, q_buf, kz_buf, vt_buf, ag_buf, pg_buf, pooled_buf, mixp_buf, mixa_buf, bias_buf, wi_buf,
                  wo_buf, pw_buf):
    tile = x_ref.shape[0]
    n_sub = tile // Q_SUB
    t = pl.program_id(1)

    @pl.when((pl.program_id(0) == 0) & (t == 0))
    def _first_step():
        r = lax.broadcasted_iota(jnp.int32, (KEY_SPAN, GQA_GROUP * Q_SUB), 0)
        col = lax.broadcasted_iota(jnp.int32, (KEY_SPAN, GQA_GROUP * Q_SUB), 1)
        dist = (col & (Q_SUB - 1)) + WINDOW - r
        in_win = (dist >= 0) & (dist < WINDOW)
        distf = dist.astype(jnp.float32)
        for kv in range(N_KV_HEADS):
            slope = jnp.zeros_like(distf)
            for g in range(GQA_GROUP):
                j = kv * GQA_GROUP + g
                slope = jnp.where(col >= g * Q_SUB, 2.0 ** (-8.0 * (j + 1) / N_HEADS), slope)
            b = -slope * distf * LOG2_E
            for f in range(3):
                bias_buf[f, kv * KEY_SPAN:(kv + 1) * KEY_SPAN, :] = jnp.where(
                    in_win & (r >= f * Q_SUB), b, NEG_INF)

        a0 = lax.broadcasted_iota(jnp.int32, (D_ATTN, D_ATTN), 0)
        a1 = lax.broadcasted_iota(jnp.int32, (D_ATTN, D_ATTN), 1)

        def source(n):
            return ((n >> 6) & 1) * (GQA_GROUP * HEAD_DIM) + (n >> 7) * HEAD_DIM + (n & (HEAD_DIM - 1))

        perm = jnp.where(a0 == source(a1), 1.0, 0.0).astype(jnp.bfloat16)
        perm_t = jnp.where(a1 == source(a0), 1.0, 0.0).astype(jnp.bfloat16)
        for lo, hi in ((OFF_U, OFF_PG), (OFF_PG, OFF_Q), (OFF_Q, OFF_K), (OFF_K, OFF_AG), (OFF_AG, D_IN)):
            w = win_ref[:, lo:hi].astype(jnp.bfloat16)
            if lo in (OFF_Q, OFF_AG):
                w = jnp.dot(w, perm, preferred_element_type=jnp.float32).astype(jnp.bfloat16)
            wi_buf[:, lo:hi] = w
        wo_buf[:D_POOL, :] = wout_ref[:D_POOL, :].astype(jnp.bfloat16)
        wo_buf[D_POOL:, :] = jnp.dot(perm_t, wout_ref[D_POOL:, :].astype(jnp.bfloat16),
                                     preferred_element_type=jnp.float32).astype(jnp.bfloat16)
        pw_buf[...] = jnp.zeros(pw_buf.shape, jnp.bfloat16)
        for g in range(len(POOL_WINDOWS)):
            d0 = (g % 2) * POOL_GROUP
            pw_buf[g // 2, d0:d0 + POOL_GROUP, d0:d0 + POOL_GROUP] = poolw_ref[g].astype(jnp.bfloat16)

    @pl.when(t == 0)
    def _zero_halo():
        u_buf[0:POOL_HALO, :] = jnp.zeros((POOL_HALO, D_POOL), jnp.float32)
        kz_buf[:, 0:WINDOW, :] = jnp.zeros((N_KV_HEADS, WINDOW, LANES), jnp.bfloat16)
        vt_buf[:, 0:WINDOW] = jnp.zeros((LANES, WINDOW), jnp.bfloat16)

    x = x_ref[...]
    ms = jnp.mean(x * x, axis=-1, keepdims=True)
    h = (x * lax.rsqrt(ms + EPS) * gpre_ref[layer:layer + 1, :]).astype(jnp.bfloat16)

    def proj(lo, hi):
        return jnp.dot(h, wi_buf[:, lo:hi], preferred_element_type=jnp.float32)

    qkvg = proj(OFF_Q, D_IN)
    q_buf[...] = (qkvg[:, :D_ATTN] * (HEAD_DIM ** -0.5 * LOG2_E)).astype(jnp.bfloat16)
    kf, vf = qkvg[:, D_ATTN:D_ATTN + LANES], qkvg[:, D_ATTN + LANES:D_ATTN + 2 * LANES]
    ag_buf[...] = _silu(qkvg[:, OFF_AG - OFF_Q:])
    lane = lax.broadcasted_iota(jnp.int32, (1, LANES), 1)
    kz_buf[0, WINDOW:WINDOW + tile, :] = jnp.where(lane < HEAD_DIM, kf, 0.0).astype(jnp.bfloat16)
    kz_buf[1, WINDOW:WINDOW + tile, :] = jnp.where(lane >= HEAD_DIM, kf, 0.0).astype(jnp.bfloat16)
    vt_buf[:, WINDOW:WINDOW + tile] = vf.T.astype(jnp.bfloat16)

    def pool_u_item():
        u_buf[POOL_HALO:POOL_HALO + tile, :] = proj(OFF_U, OFF_PG)

    pos1 = (lax.broadcasted_iota(jnp.int32, (tile, POOL_GROUP), 0) + (t * tile + 1)
            ).astype(jnp.float32)

    def pool_window_item():
        for g, w in enumerate(POOL_WINDOWS):
            cols = slice(g * POOL_GROUP, (g + 1) * POOL_GROUP)
            ext = u_buf[:, cols]
            acc = ext
            shift = 1
            while shift < w:
                acc = acc + pltpu.roll(acc, shift, axis=0)
                shift *= 2
            cur = ext[POOL_HALO:]
            pooled = acc[POOL_HALO:] / jnp.minimum(pos1, float(w)) - cur
            pooled_buf[:, cols] = pooled.astype(jnp.bfloat16)

    def pool_mix_item():
        for pair in range(len(POOL_WINDOWS) // 2):
            cols = slice(pair * 2 * POOL_GROUP, (pair + 1) * 2 * POOL_GROUP)
            mixed = jnp.dot(pooled_buf[:, cols], pw_buf[pair], preferred_element_type=jnp.float32)
            scale = pscale_ref[layer:layer + 1, cols]
            mixp_buf[:, cols] = (mixed * scale * pg_buf[:, cols]).astype(jnp.bfloat16)

    def pool_gate_item():
        pg_buf[...] = _silu(proj(OFF_PG, OFF_Q))

    def pool_u_gate_item():
        ug = proj(OFF_U, OFF_Q)
        u_buf[POOL_HALO:POOL_HALO + tile, :] = ug[:, :D_POOL]
        pg_buf[...] = _silu(ug[:, D_POOL:])

    col = lax.broadcasted_iota(jnp.int32, (1, GQA_GROUP * Q_SUB), 1)
    sink_rows = []
    for kv in range(N_KV_HEADS):
        row = jnp.zeros((1, GQA_GROUP * Q_SUB), jnp.float32)
        for g in range(GQA_GROUP):
            row = jnp.where(col >= g * Q_SUB, sinks_ref[layer, kv * GQA_GROUP + g], row)
        sink_rows.append(row * LOG2_E)
    first = (t == 0)
    nt_dims = (((1,), (1,)), ((), ()))
    pad = jnp.zeros((Q_SUB, GQA_GROUP * Q_SUB), jnp.bfloat16)
    ones_rows = jnp.ones((BF16_ROWS, 2 * LANES), jnp.bfloat16)

    def attn_scores(sb):
        r0 = sb * Q_SUB
        qrows = slice(r0, r0 + Q_SUB)
        krows = slice(r0, r0 + KEY_SPAN)
        variant = jnp.where(first, 2 - sb, 0) if sb < 2 else 0
        q4 = jnp.concatenate([q_buf[qrows, g * LANES:(g + 1) * LANES] for g in range(GQA_GROUP)],
                             axis=0)
        kk = jnp.concatenate([kz_buf[0, krows, :], kz_buf[1, krows, :]], axis=0)
        s = lax.dot_general(kk, q4, nt_dims, preferred_element_type=jnp.float32)
        return s + bias_buf[variant]

    def attn_finish(sb, s):
        r0 = sb * Q_SUB
        qrows = slice(r0, r0 + Q_SUB)
        outs = []
        for kv in range(N_KV_HEADS):
            sh = s[kv * KEY_SPAN:(kv + 1) * KEY_SPAN]
            m = jnp.maximum(jnp.max(sh, axis=0, keepdims=True), sink_rows[kv])
            pb = jnp.exp2(sh - m).astype(jnp.bfloat16)
            if sb % 2 == 0:
                pb = jnp.concatenate([pb, pad], axis=0)
                c0 = r0
            else:
                pb = jnp.concatenate([pad, pb], axis=0)
                c0 = r0 - Q_SUB
            vt = jnp.concatenate([vt_buf[kv * HEAD_DIM:(kv + 1) * HEAD_DIM, c0:c0 + 2 * LANES], ones_rows],
                                 axis=0)
            o = jnp.dot(vt, pb, preferred_element_type=jnp.float32)
            l = o[HEAD_DIM:HEAD_DIM + 1] + jnp.exp2(sink_rows[kv] - m)
            outs.append(o[:HEAD_DIM] * (1.0 / l))
        for pr in range(GQA_GROUP // 2):
            both = jnp.concatenate([o[:, pr * LANES:(pr + 1) * LANES] for o in outs], axis=0)
            both_t = both.T
            for half in range(2):
                g = 2 * pr + half
                val = both_t[half * Q_SUB:(half + 1) * Q_SUB] * ag_buf[qrows, g * LANES:(g + 1) * LANES]
                mixa_buf[qrows, g * LANES:(g + 1) * LANES] = val.astype(jnp.bfloat16)

    dense_items = [[pool_u_gate_item, pool_window_item, pool_mix_item]]
    group = n_sub // len(dense_items)
    for p, items in enumerate(dense_items):
        subs = range(p * group, (p + 1) * group)
        scores = [attn_scores(sb) for sb in subs]
        for item in items:
            item()
        for sb, s in zip(subs, scores):
            attn_finish(sb, s)

    u_buf[0:POOL_HALO, :] = u_buf[tile:tile + POOL_HALO, :]
    kz_buf[:, 0:WINDOW, :] = kz_buf[:, tile:tile + WINDOW, :]
    vt_buf[:, 0:WINDOW] = vt_buf[:, tile:tile + WINDOW]

    for r0 in range(0, tile, OUT_ROWS):
        rows = slice(r0, r0 + OUT_ROWS)
        y = (jnp.dot(mixp_buf[rows, :], wo_buf[:D_POOL, :], preferred_element_type=jnp.float32)
             + jnp.dot(mixa_buf[rows, :], wo_buf[D_POOL:, :], preferred_element_type=jnp.float32))
        ms2 = jnp.mean(y * y, axis=-1, keepdims=True)
        o_ref[rows, :] = x_ref[rows, :] + y * lax.rsqrt(ms2 + EPS) * gpost_ref[layer:layer + 1, :]


def _layer(layer, x, w_in, pool_w, pool_scale, sinks, w_out, g_pre, g_post):
    batch, seq, d = x.shape
    tile = SEQ_TILE
    assert seq % tile == 0 and tile % (2 * Q_SUB) == 0 and d == D_MODEL

    def of_layer(*shape):
        return pl.BlockSpec((None,) + shape, lambda b, t: (layer,) + (0,) * len(shape),
                            pipeline_mode=pl.Buffered(1))

    def whole(a):
        return pl.BlockSpec(a.shape, lambda b, t: (0,) * a.ndim)

    return pl.pallas_call(
        functools.partial(_layer_kernel, layer),
        out_shape=jax.ShapeDtypeStruct(x.shape, x.dtype),
        grid=(batch, seq // tile),
        in_specs=[
            pl.BlockSpec((None, tile, d), lambda b, t: (b, t, 0)),
            of_layer(D_MODEL, D_IN),
            of_layer(len(POOL_WINDOWS), POOL_GROUP, POOL_GROUP),
            whole(pool_scale),
            pl.BlockSpec(memory_space=pltpu.SMEM),
            of_layer(D_MODEL, D_MODEL),
            whole(g_pre),
            whole(g_post),
        ],
        out_specs=pl.BlockSpec((None, tile, d), lambda b, t: (b, t, 0)),
        scratch_shapes=[
            pltpu.VMEM((POOL_HALO + tile, D_POOL), jnp.float32),
            pltpu.VMEM((tile, D_ATTN), jnp.bfloat16),
            pltpu.VMEM((N_KV_HEADS, WINDOW + tile, LANES), jnp.bfloat16),
            pltpu.VMEM((LANES, WINDOW + tile), jnp.bfloat16),
            pltpu.VMEM((tile, D_ATTN), jnp.float32),
            pltpu.VMEM((tile, D_POOL), jnp.float32),
            pltpu.VMEM((tile, D_POOL), jnp.bfloat16),
            pltpu.VMEM((tile, D_POOL), jnp.bfloat16),
            pltpu.VMEM((tile, D_ATTN), jnp.bfloat16),
            pltpu.VMEM((3, N_KV_HEADS * KEY_SPAN, GQA_GROUP * Q_SUB), jnp.float32),
            pltpu.VMEM((D_MODEL, D_IN), jnp.bfloat16),
            pltpu.VMEM((D_MODEL, D_MODEL), jnp.bfloat16),
            pltpu.VMEM((len(POOL_WINDOWS) // 2, 2 * POOL_GROUP, 2 * POOL_GROUP), jnp.bfloat16),
        ],
        compiler_params=pltpu.CompilerParams(
            dimension_semantics=("arbitrary", "arbitrary"),
            vmem_limit_bytes=VMEM_LIMIT_BYTES),
        name="hybrid_layer",
    )(x, w_in, pool_w, pool_scale, sinks, w_out, g_pre, g_post)


@jax.jit
def kernel(x, w_in, pool_w, pool_scale, attn_sinks, w_out, norm_pre, norm_post):
    for layer in range(w_in.shape[0]):
        x = _layer(layer, x, w_in, pool_w, pool_scale, attn_sinks, w_out, norm_pre, norm_post)
    return x
```

```python
import functools

import jax
import jax.numpy as jnp
from jax import lax
from jax.experimental import pallas as pl
from jax.experimental.pallas import tpu as pltpu

D_MODEL = 1024
D_POOL = 512
POOL_WINDOWS = (2, 4, 8, 16)
POOL_GROUP = 128
HEAD_DIM = 64
D_ATTN = 512
N_HEADS = 8
N_KV_HEADS = 2
GQA_GROUP = N_HEADS // N_KV_HEADS
WINDOW = 128
D_IN = 2304
EPS = 1e-6
NEG_INF = -1e30
LOG2_E = 1.4426950408889634

OFF_U, OFF_PG, OFF_Q, OFF_K, OFF_V, OFF_AG = 0, 512, 1024, 1536, 1664, 1792

LANES = 128
BF16_ROWS = 16
POOL_HALO = 16
SEQ_TILE = 1024
Q_SUB = 64
OUT_ROWS = 512
KEY_SPAN = Q_SUB + WINDOW
VMEM_LIMIT_BYTES = 56 * 1024 * 1024


def _silu(x):
    return x * (1.0 / (1.0 + jnp.exp(-x)))


def _layer_kernel(layer, x_ref, win_ref, poolw_ref, pscale_ref, sinks_ref, wout_ref,
                  gpre_ref, gpost_ref, o_ref,
                  u_buf, q_buf, kz_buf, vt_buf, ag_buf, pg_buf, pooled_buf, mix_buf, bias_buf, wi_buf,
                  wo_buf, pw_buf):
    tile = x_ref.shape[0]
    n_sub = tile // Q_SUB
    t = pl.program_id(1)

    @pl.when((pl.program_id(0) == 0) & (t == 0))
    def _first_step():
        r = lax.broadcasted_iota(jnp.int32, (KEY_SPAN, GQA_GROUP * Q_SUB), 0)
        col = lax.broadcasted_iota(jnp.int32, (KEY_SPAN, GQA_GROUP * Q_SUB), 1)
        dist = (col & (Q_SUB - 1)) + WINDOW - r
        in_win = (dist >= 0) & (dist < WINDOW)
        distf = dist.astype(jnp.float32)
        for kv in range(N_KV_HEADS):
            slope = jnp.zeros_like(distf)
            for g in range(GQA_GROUP):
                j = kv * GQA_GROUP + g
                slope = jnp.where(col >= g * Q_SUB, 2.0 ** (-8.0 * (j + 1) / N_HEADS), slope)
            b = -slope * distf * LOG2_E
            for f in range(3):
                bias_buf[f, kv * KEY_SPAN:(kv + 1) * KEY_SPAN, :] = jnp.where(
                    in_win & (r >= f * Q_SUB), b, NEG_INF)

        a0 = lax.broadcasted_iota(jnp.int32, (D_ATTN, D_ATTN), 0)
        a1 = lax.broadcasted_iota(jnp.int32, (D_ATTN, D_ATTN), 1)

        def source(n):
            return ((n >> 6) & 1) * (GQA_GROUP * HEAD_DIM) + (n >> 7) * HEAD_DIM + (n & (HEAD_DIM - 1))

        perm = jnp.where(a0 == source(a1), 1.0, 0.0).astype(jnp.bfloat16)
        perm_t = jnp.where(a1 == source(a0), 1.0, 0.0).astype(jnp.bfloat16)
        for lo, hi in ((OFF_U, OFF_PG), (OFF_PG, OFF_Q), (OFF_Q, OFF_K), (OFF_K, OFF_AG), (OFF_AG, D_IN)):
            w = win_ref[:, lo:hi].astype(jnp.bfloat16)
            if lo in (OFF_Q, OFF_AG):
                w = jnp.dot(w, perm, preferred_element_type=jnp.float32).astype(jnp.bfloat16)
            wi_buf[:, lo:hi] = w
        wo_buf[:D_POOL, :] = wout_ref[:D_POOL, :].astype(jnp.bfloat16)
        wo_buf[D_POOL:, :] = jnp.dot(perm_t, wout_ref[D_POOL:, :].astype(jnp.bfloat16),
                                     preferred_element_type=jnp.float32).astype(jnp.bfloat16)
        pw_buf[...] = jnp.zeros(pw_buf.shape, jnp.bfloat16)
        for g in range(len(POOL_WINDOWS)):
            d0 = (g % 2) * POOL_GROUP
            pw_buf[g // 2, d0:d0 + POOL_GROUP, d0:d0 + POOL_GROUP] = poolw_ref[g].astype(jnp.bfloat16)

    @pl.when(t == 0)
    def _zero_halo():
        u_buf[0:POOL_HALO, :] = jnp.zeros((POOL_HALO, D_POOL), jnp.float32)
        kz_buf[:, 0:WINDOW, :] = jnp.zeros((N_KV_HEADS, WINDOW, LANES), jnp.bfloat16)
        vt_buf[:, 0:WINDOW] = jnp.zeros((LANES, WINDOW), jnp.bfloat16)

    x = x_ref[...]
    ms = jnp.mean(x * x, axis=-1, keepdims=True)
    h = (x * lax.rsqrt(ms + EPS) * gpre_ref[layer:layer + 1, :]).astype(jnp.bfloat16)

    def proj(lo, hi):
        return jnp.dot(h, wi_buf[:, lo:hi], preferred_element_type=jnp.float32)

    qkvg = proj(OFF_Q, D_IN)
    q_buf[...] = (qkvg[:, :D_ATTN] * (HEAD_DIM ** -0.5 * LOG2_E)).astype(jnp.bfloat16)
    kf, vf = qkvg[:, D_ATTN:D_ATTN + LANES], qkvg[:, D_ATTN + LANES:D_ATTN + 2 * LANES]
    ag_buf[...] = _silu(qkvg[:, OFF_AG - OFF_Q:])
    lane = lax.broadcasted_iota(jnp.int32, (1, LANES), 1)
    kz_buf[0, WINDOW:WINDOW + tile, :] = jnp.where(lane < HEAD_DIM, kf, 0.0).astype(jnp.bfloat16)
    kz_buf[1, WINDOW:WINDOW + tile, :] = jnp.where(lane >= HEAD_DIM, kf, 0.0).astype(jnp.bfloat16)
    vt_buf[:, WINDOW:WINDOW + tile] = vf.T.astype(jnp.bfloat16)

    def pool_u_item():
        u_buf[POOL_HALO:POOL_HALO + tile, :] = proj(OFF_U, OFF_PG)

    pos1 = (lax.broadcasted_iota(jnp.int32, (tile, POOL_GROUP), 0) + (t * tile + 1)
            ).astype(jnp.float32)

    def pool_window_item():
        for g, w in enumerate(POOL_WINDOWS):
            cols = slice(g * POOL_GROUP, (g + 1) * POOL_GROUP)
            ext = u_buf[:, cols]
            acc = ext
            shift = 1
            while shift < w:
                acc = acc + pltpu.roll(acc, shift, axis=0)
                shift *= 2
            cur = ext[POOL_HALO:]
            pooled = acc[POOL_HALO:] / jnp.minimum(pos1, float(w)) - cur
            pooled_buf[:, cols] = pooled.astype(jnp.bfloat16)

    def pool_mix_item():
        for pair in range(len(POOL_WINDOWS) // 2):
            cols = slice(pair * 2 * POOL_GROUP, (pair + 1) * 2 * POOL_GROUP)
            mixed = jnp.dot(pooled_buf[:, cols], pw_buf[pair], preferred_element_type=jnp.float32)
            scale = pscale_ref[layer:layer + 1, cols]
            mix_buf[:, cols] = (mixed * scale * pg_buf[:, cols]).astype(jnp.bfloat16)

    def pool_gate_item():
        pg_buf[...] = _silu(proj(OFF_PG, OFF_Q))

    def pool_u_gate_item():
        ug = proj(OFF_U, OFF_Q)
        u_buf[POOL_HALO:POOL_HALO + tile, :] = ug[:, :D_POOL]
        pg_buf[...] = _silu(ug[:, D_POOL:])

    col = lax.broadcasted_iota(jnp.int32, (1, GQA_GROUP * Q_SUB), 1)
    sink_rows = []
    for kv in range(N_KV_HEADS):
        row = jnp.zeros((1, GQA_GROUP * Q_SUB), jnp.float32)
        for g in range(GQA_GROUP):
            row = jnp.where(col >= g * Q_SUB, sinks_ref[layer, kv * GQA_GROUP + g], row)
        sink_rows.append(row * LOG2_E)
    first = (t == 0)
    nt_dims = (((1,), (1,)), ((), ()))
    pad = jnp.zeros((Q_SUB, GQA_GROUP * Q_SUB), jnp.bfloat16)
    ones_rows = jnp.ones((BF16_ROWS, 2 * LANES), jnp.bfloat16)

    def attn_scores(sb):
        r0 = sb * Q_SUB
        qrows = slice(r0, r0 + Q_SUB)
        krows = slice(r0, r0 + KEY_SPAN)
        variant = jnp.where(first, 2 - sb, 0) if sb < 2 else 0
        q4 = jnp.concatenate([q_buf[qrows, g * LANES:(g + 1) * LANES] for g in range(GQA_GROUP)],
                             axis=0)
        kk = jnp.concatenate([kz_buf[0, krows, :], kz_buf[1, krows, :]], axis=0)
        s = lax.dot_general(kk, q4, nt_dims, preferred_element_type=jnp.float32)
        return s + bias_buf[variant]

    def attn_finish(sb, s):
        r0 = sb * Q_SUB
        qrows = slice(r0, r0 + Q_SUB)
        outs = []
        for kv in range(N_KV_HEADS):
            sh = s[kv * KEY_SPAN:(kv + 1) * KEY_SPAN]
            m = jnp.maximum(jnp.max(sh, axis=0, keepdims=True), sink_rows[kv])
            pb = jnp.exp2(sh - m).astype(jnp.bfloat16)
            if sb % 2 == 0:
                pb = jnp.concatenate([pb, pad], axis=0)
                c0 = r0
            else:
                pb = jnp.concatenate([pad, pb], axis=0)
                c0 = r0 - Q_SUB
            vt = jnp.concatenate([vt_buf[kv * HEAD_DIM:(kv + 1) * HEAD_DIM, c0:c0 + 2 * LANES], ones_rows],
                                 axis=0)
            o = jnp.dot(vt, pb, preferred_element_type=jnp.float32)
            l = o[HEAD_DIM:HEAD_DIM + 1] + jnp.exp2(sink_rows[kv] - m)
            outs.append(o[:HEAD_DIM] * (1.0 / l))
        for pr in range(GQA_GROUP // 2):
            both = jnp.concatenate([o[:, pr * LANES:(pr + 1) * LANES] for o in outs], axis=0)
            both_t = both.T
            for half in range(2):
                g = 2 * pr + half
                val = both_t[half * Q_SUB:(half + 1) * Q_SUB] * ag_buf[qrows, g * LANES:(g + 1) * LANES]
                mix_buf[qrows, D_POOL + g * LANES:D_POOL + (g + 1) * LANES] = val.astype(jnp.bfloat16)

    dense_items = [[pool_u_gate_item], [pool_window_item, pool_mix_item]]
    group = n_sub // len(dense_items)
    for p, items in enumerate(dense_items):
        subs = range(p * group, (p + 1) * group)
        scores = [attn_scores(sb) for sb in subs]
        for item in items:
            item()
        for sb, s in zip(subs, scores):
            attn_finish(sb, s)

    u_buf[0:POOL_HALO, :] = u_buf[tile:tile + POOL_HALO, :]
    kz_buf[:, 0:WINDOW, :] = kz_buf[:, tile:tile + WINDOW, :]
    vt_buf[:, 0:WINDOW] = vt_buf[:, tile:tile + WINDOW]

    for r0 in range(0, tile, OUT_ROWS):
        rows = slice(r0, r0 + OUT_ROWS)
        y = jnp.dot(mix_buf[rows, :], wo_buf[...], preferred_element_type=jnp.float32)
        ms2 = jnp.mean(y * y, axis=-1, keepdims=True)
        o_ref[rows, :] = x_ref[rows, :] + y * lax.rsqrt(ms2 + EPS) * gpost_ref[layer:layer + 1, :]


def _layer(layer, x, w_in, pool_w, pool_scale, sinks, w_out, g_pre, g_post):
    batch, seq, d = x.shape
    tile = SEQ_TILE
    assert seq % tile == 0 and tile % (2 * Q_SUB) == 0 and d == D_MODEL

    def of_layer(*shape):
        return pl.BlockSpec((None,) + shape, lambda b, t: (layer,) + (0,) * len(shape),
                            pipeline_mode=pl.Buffered(1))

    def whole(a):
        return pl.BlockSpec(a.shape, lambda b, t: (0,) * a.ndim)

    return pl.pallas_call(
        functools.partial(_layer_kernel, layer),
        out_shape=jax.ShapeDtypeStruct(x.shape, x.dtype),
        grid=(batch, seq // tile),
        in_specs=[
            pl.BlockSpec((None, tile, d), lambda b, t: (b, t, 0)),
            of_layer(D_MODEL, D_IN),
            of_layer(len(POOL_WINDOWS), POOL_GROUP, POOL_GROUP),
            whole(pool_scale),
            pl.BlockSpec(memory_space=pltpu.SMEM),
            of_layer(D_MODEL, D_MODEL),
            whole(g_pre),
            whole(g_post),
        ],
        out_specs=pl.BlockSpec((None, tile, d), lambda b, t: (b, t, 0)),
        scratch_shapes=[
            pltpu.VMEM((POOL_HALO + tile, D_POOL), jnp.float32),
            pltpu.VMEM((tile, D_ATTN), jnp.bfloat16),
            pltpu.VMEM((N_KV_HEADS, WINDOW + tile, LANES), jnp.bfloat16),
            pltpu.VMEM((LANES, WINDOW + tile), jnp.bfloat16),
            pltpu.VMEM((tile, D_ATTN), jnp.float32),
            pltpu.VMEM((tile, D_POOL), jnp.float32),
            pltpu.VMEM((tile, D_POOL), jnp.bfloat16),
            pltpu.VMEM((tile, D_MODEL), jnp.bfloat16),
            pltpu.VMEM((3, N_KV_HEADS * KEY_SPAN, GQA_GROUP * Q_SUB), jnp.float32),
            pltpu.VMEM((D_MODEL, D_IN), jnp.bfloat16),
            pltpu.VMEM((D_MODEL, D_MODEL), jnp.bfloat16),
            pltpu.VMEM((len(POOL_WINDOWS) // 2, 2 * POOL_GROUP, 2 * POOL_GROUP), jnp.bfloat16),
        ],
        compiler_params=pltpu.CompilerParams(
            dimension_semantics=("arbitrary", "arbitrary"),
            vmem_limit_bytes=VMEM_LIMIT_BYTES),
        name="hybrid_layer",
    )(x, w_in, pool_w, pool_scale, sinks, w_out, g_pre, g_post)


@jax.jit
def kernel(x, w_in, pool_w, pool_scale, attn_sinks, w_out, norm_pre, norm_post):
    for layer in range(w_in.shape[0]):
        x = _layer(layer, x, w_in, pool_w, pool_scale, attn_sinks, w_out, norm_pre, norm_post)
    return x
```

```python
import functools

import jax
import jax.numpy as jnp
from jax import lax
from jax.experimental import pallas as pl
from jax.experimental.pallas import tpu as pltpu

D_MODEL = 1024
D_POOL = 512
POOL_WINDOWS = (2, 4, 8, 16)
POOL_GROUP = 128
HEAD_DIM = 64
D_ATTN = 512
N_HEADS = 8
N_KV_HEADS = 2
GQA_GROUP = N_HEADS // N_KV_HEADS
WINDOW = 128
D_IN = 2304
EPS = 1e-6
NEG_INF = -1e30
LOG2_E = 1.4426950408889634

OFF_U, OFF_PG, OFF_Q, OFF_K, OFF_V, OFF_AG = 0, 512, 1024, 1536, 1664, 1792

LANES = 128
BF16_ROWS = 16
POOL_HALO = 16
SEQ_TILE = 1024
Q_SUB = 64
OUT_ROWS = 512
KEY_SPAN = Q_SUB + WINDOW
VMEM_LIMIT_BYTES = 56 * 1024 * 1024


def _silu(x):
    return x * (1.0 / (1.0 + jnp.exp(-x)))


def _layer_kernel(layer, x_ref, win_ref, poolw_ref, pscale_ref, sinks_ref, wout_ref,
                  gpre_ref, gpost_ref, o_ref,
                  u_buf, q_buf, kz_buf, vt_buf, ag_buf, pg_buf, pooled_buf, mixp_buf, mixa_buf, bias_buf, wi_buf,
                  wo_buf, pw_buf):
    tile = x_ref.shape[0]
    n_sub = tile // Q_SUB
    t = pl.program_id(1)

    @pl.when((pl.program_id(0) == 0) & (t == 0))
    def _first_step():
        r = lax.broadcasted_iota(jnp.int32, (KEY_SPAN, GQA_GROUP * Q_SUB), 0)
        col = lax.broadcasted_iota(jnp.int32, (KEY_SPAN, GQA_GROUP * Q_SUB), 1)
        dist = (col & (Q_SUB - 1)) + WINDOW - r
        in_win = (dist >= 0) & (dist < WINDOW)
        distf = dist.astype(jnp.float32)
        for kv in range(N_KV_HEADS):
            slope = jnp.zeros_like(distf)
            for g in range(GQA_GROUP):
                j = kv * GQA_GROUP + g
                slope = jnp.where(col >= g * Q_SUB, 2.0 ** (-8.0 * (j + 1) / N_HEADS), slope)
            b = -slope * distf * LOG2_E
            for f in range(3):
                bias_buf[f, kv * KEY_SPAN:(kv + 1) * KEY_SPAN, :] = jnp.where(
                    in_win & (r >= f * Q_SUB), b, NEG_INF)

        a0 = lax.broadcasted_iota(jnp.int32, (D_ATTN, D_ATTN), 0)
        a1 = lax.broadcasted_iota(jnp.int32, (D_ATTN, D_ATTN), 1)

        def source(n):
            return ((n >> 6) & 1) * (GQA_GROUP * HEAD_DIM) + (n >> 7) * HEAD_DIM + (n & (HEAD_DIM - 1))

        perm = jnp.where(a0 == source(a1), 1.0, 0.0).astype(jnp.bfloat16)
        perm_t = jnp.where(a1 == source(a0), 1.0, 0.0).astype(jnp.bfloat16)
        for lo, hi in ((OFF_U, OFF_PG), (OFF_PG, OFF_Q), (OFF_Q, OFF_K), (OFF_K, OFF_AG), (OFF_AG, D_IN)):
            w = win_ref[:, lo:hi].astype(jnp.bfloat16)
            if lo in (OFF_Q, OFF_AG):
                w = jnp.dot(w, perm, preferred_element_type=jnp.float32).astype(jnp.bfloat16)
            wi_buf[:, lo:hi] = w
        wo_buf[:D_POOL, :] = wout_ref[:D_POOL, :].astype(jnp.bfloat16)
        wo_buf[D_POOL:, :] = jnp.dot(perm_t, wout_ref[D_POOL:, :].astype(jnp.bfloat16),
                                     preferred_element_type=jnp.float32).astype(jnp.bfloat16)
        pw_buf[...] = jnp.zeros(pw_buf.shape, jnp.bfloat16)
        for g in range(len(POOL_WINDOWS)):
            d0 = (g % 2) * POOL_GROUP
            pw_buf[g // 2, d0:d0 + POOL_GROUP, d0:d0 + POOL_GROUP] = poolw_ref[g].astype(jnp.bfloat16)

    @pl.when(t == 0)
    def _zero_halo():
        u_buf[0:POOL_HALO, :] = jnp.zeros((POOL_HALO, D_POOL), jnp.float32)
        kz_buf[:, 0:WINDOW, :] = jnp.zeros((N_KV_HEADS, WINDOW, LANES), jnp.bfloat16)
        vt_buf[:, 0:WINDOW] = jnp.zeros((LANES, WINDOW), jnp.bfloat16)

    x = x_ref[...]
    ms = jnp.mean(x * x, axis=-1, keepdims=True)
    h = (x * lax.rsqrt(ms + EPS) * gpre_ref[layer:layer + 1, :]).astype(jnp.bfloat16)

    def proj(lo, hi):
        return jnp.dot(h, wi_buf[:, lo:hi], preferred_element_type=jnp.float32)

    qkvg = proj(OFF_Q, D_IN)
    q_buf[...] = (qkvg[:, :D_ATTN] * (HEAD_DIM ** -0.5 * LOG2_E)).astype(jnp.bfloat16)
    kf, vf = qkvg[:, D_ATTN:D_ATTN + LANES], qkvg[:, D_ATTN + LANES:D_ATTN + 2 * LANES]
    ag_buf[...] = _silu(qkvg[:, OFF_AG - OFF_Q:])
    lane = lax.broadcasted_iota(jnp.int32, (1, LANES), 1)
    kz_buf[0, WINDOW:WINDOW + tile, :] = jnp.where(lane < HEAD_DIM, kf, 0.0).astype(jnp.bfloat16)
    kz_buf[1, WINDOW:WINDOW + tile, :] = jnp.where(lane >= HEAD_DIM, kf, 0.0).astype(jnp.bfloat16)
    vt_buf[:, WINDOW:WINDOW + tile] = vf.T.astype(jnp.bfloat16)

    def pool_u_item():
        u_buf[POOL_HALO:POOL_HALO + tile, :] = proj(OFF_U, OFF_PG)

    pos1 = (lax.broadcasted_iota(jnp.int32, (tile, POOL_GROUP), 0) + (t * tile + 1)
            ).astype(jnp.float32)

    def pool_window_item():
        for g, w in enumerate(POOL_WINDOWS):
            cols = slice(g * POOL_GROUP, (g + 1) * POOL_GROUP)
            ext = u_buf[:, cols]
            acc = ext
            shift = 1
            while shift < w:
                acc = acc + pltpu.roll(acc, shift, axis=0)
                shift *= 2
            cur = ext[POOL_HALO:]
            pooled = acc[POOL_HALO:] / jnp.minimum(pos1, float(w)) - cur
            pooled_buf[:, cols] = pooled.astype(jnp.bfloat16)

    def pool_mix_item():
        for pair in range(len(POOL_WINDOWS) // 2):
            cols = slice(pair * 2 * POOL_GROUP, (pair + 1) * 2 * POOL_GROUP)
            mixed = jnp.dot(pooled_buf[:, cols], pw_buf[pair], preferred_element_type=jnp.float32)
            scale = pscale_ref[layer:layer + 1, cols]
            mixp_buf[:, cols] = (mixed * scale * pg_buf[:, cols]).astype(jnp.bfloat16)

    def pool_gate_item():
        pg_buf[...] = _silu(proj(OFF_PG, OFF_Q))

    def pool_u_gate_item():
        ug = proj(OFF_U, OFF_Q)
        u_buf[POOL_HALO:POOL_HALO + tile, :] = ug[:, :D_POOL]
        pg_buf[...] = _silu(ug[:, D_POOL:])

    col = lax.broadcasted_iota(jnp.int32, (1, GQA_GROUP * Q_SUB), 1)
    sink_rows = []
    for kv in range(N_KV_HEADS):
        row = jnp.zeros((1, GQA_GROUP * Q_SUB), jnp.float32)
        for g in range(GQA_GROUP):
            row = jnp.where(col >= g * Q_SUB, sinks_ref[layer, kv * GQA_GROUP + g], row)
        sink_rows.append(row * LOG2_E)
    first = (t == 0)
    nt_dims = (((1,), (1,)), ((), ()))
    pad = jnp.zeros((Q_SUB, GQA_GROUP * Q_SUB), jnp.bfloat16)
    ones_rows = jnp.ones((BF16_ROWS, 2 * LANES), jnp.bfloat16)

    def attn_scores(sb):
        r0 = sb * Q_SUB
        qrows = slice(r0, r0 + Q_SUB)
        krows = slice(r0, r0 + KEY_SPAN)
        variant = jnp.where(first, 2 - sb, 0) if sb < 2 else 0
        q4 = jnp.concatenate([q_buf[qrows, g * LANES:(g + 1) * LANES] for g in range(GQA_GROUP)],
                             axis=0)
        kk = jnp.concatenate([kz_buf[0, krows, :], kz_buf[1, krows, :]], axis=0)
        s = lax.dot_general(kk, q4, nt_dims, preferred_element_type=jnp.float32)
        return s + bias_buf[variant]

    def attn_finish(sb, s):
        r0 = sb * Q_SUB
        qrows = slice(r0, r0 + Q_SUB)
        lrows = slice(r0 % OUT_ROWS, r0 % OUT_ROWS + Q_SUB)
        outs = []
        for kv in range(N_KV_HEADS):
            sh = s[kv * KEY_SPAN:(kv + 1) * KEY_SPAN]
            m = jnp.maximum(jnp.max(sh, axis=0, keepdims=True), sink_rows[kv])
            pb = jnp.exp2(sh - m).astype(jnp.bfloat16)
            if sb % 2 == 0:
                pb = jnp.concatenate([pb, pad], axis=0)
                c0 = r0
            else:
                pb = jnp.concatenate([pad, pb], axis=0)
                c0 = r0 - Q_SUB
            vt = jnp.concatenate([vt_buf[kv * HEAD_DIM:(kv + 1) * HEAD_DIM, c0:c0 + 2 * LANES], ones_rows],
                                 axis=0)
            o = jnp.dot(vt, pb, preferred_element_type=jnp.float32)
            l = o[HEAD_DIM:HEAD_DIM + 1] + jnp.exp2(sink_rows[kv] - m)
            outs.append(o[:HEAD_DIM] * (1.0 / l))
        for pr in range(GQA_GROUP // 2):
            both = jnp.concatenate([o[:, pr * LANES:(pr + 1) * LANES] for o in outs], axis=0)
            both_t = both.T
            for half in range(2):
                g = 2 * pr + half
                val = both_t[half * Q_SUB:(half + 1) * Q_SUB] * ag_buf[qrows, g * LANES:(g + 1) * LANES]
                mixa_buf[r0 // OUT_ROWS, lrows, g * LANES:(g + 1) * LANES] = val.astype(jnp.bfloat16)

    def out_half_item(half):
        rows = slice(half * OUT_ROWS, (half + 1) * OUT_ROWS)
        y = (jnp.dot(mixp_buf[rows, :], wo_buf[:D_POOL, :], preferred_element_type=jnp.float32)
             + jnp.dot(mixa_buf[half], wo_buf[D_POOL:, :], preferred_element_type=jnp.float32))
        ms2 = jnp.mean(y * y, axis=-1, keepdims=True)
        o_ref[rows, :] = x_ref[rows, :] + y * lax.rsqrt(ms2 + EPS) * gpost_ref[layer:layer + 1, :]

    dense_items = [[pool_u_gate_item, pool_window_item, pool_mix_item],
                   [functools.partial(out_half_item, 0)]]
    group = n_sub // len(dense_items)
    for p, items in enumerate(dense_items):
        subs = range(p * group, (p + 1) * group)
        scores = [attn_scores(sb) for sb in subs]
        for item in items:
            item()
        for sb, s in zip(subs, scores):
            attn_finish(sb, s)
    out_half_item(1)

    u_buf[0:POOL_HALO, :] = u_buf[tile:tile + POOL_HALO, :]
    kz_buf[:, 0:WINDOW, :] = kz_buf[:, tile:tile + WINDOW, :]
    vt_buf[:, 0:WINDOW] = vt_buf[:, tile:tile + WINDOW]


def _layer(layer, x, w_in, pool_w, pool_scale, sinks, w_out, g_pre, g_post):
    batch, seq, d = x.shape
    tile = SEQ_TILE
    assert seq % tile == 0 and tile % (2 * Q_SUB) == 0 and d == D_MODEL

    def of_layer(*shape):
        return pl.BlockSpec((None,) + shape, lambda b, t: (layer,) + (0,) * len(shape),
                            pipeline_mode=pl.Buffered(1))

    def whole(a):
        return pl.BlockSpec(a.shape, lambda b, t: (0,) * a.ndim)

    return pl.pallas_call(
        functools.partial(_layer_kernel, layer),
        out_shape=jax.ShapeDtypeStruct(x.shape, x.dtype),
        grid=(batch, seq // tile),
        in_specs=[
            pl.BlockSpec((None, tile, d), lambda b, t: (b, t, 0)),
            of_layer(D_MODEL, D_IN),
            of_layer(len(POOL_WINDOWS), POOL_GROUP, POOL_GROUP),
            whole(pool_scale),
            pl.BlockSpec(memory_space=pltpu.SMEM),
            of_layer(D_MODEL, D_MODEL),
            whole(g_pre),
            whole(g_post),
        ],
        out_specs=pl.BlockSpec((None, tile, d), lambda b, t: (b, t, 0)),
        scratch_shapes=[
            pltpu.VMEM((POOL_HALO + tile, D_POOL), jnp.float32),
            pltpu.VMEM((tile, D_ATTN), jnp.bfloat16),
            pltpu.VMEM((N_KV_HEADS, WINDOW + tile, LANES), jnp.bfloat16),
            pltpu.VMEM((LANES, WINDOW + tile), jnp.bfloat16),
            pltpu.VMEM((tile, D_ATTN), jnp.float32),
            pltpu.VMEM((tile, D_POOL), jnp.float32),
            pltpu.VMEM((tile, D_POOL), jnp.bfloat16),
            pltpu.VMEM((tile, D_POOL), jnp.bfloat16),
            pltpu.VMEM((tile // OUT_ROWS, OUT_ROWS, D_ATTN), jnp.bfloat16),
            pltpu.VMEM((3, N_KV_HEADS * KEY_SPAN, GQA_GROUP * Q_SUB), jnp.float32),
            pltpu.VMEM((D_MODEL, D_IN), jnp.bfloat16),
            pltpu.VMEM((D_MODEL, D_MODEL), jnp.bfloat16),
            pltpu.VMEM((len(POOL_WINDOWS) // 2, 2 * POOL_GROUP, 2 * POOL_GROUP), jnp.bfloat16),
        ],
        compiler_params=pltpu.CompilerParams(
            dimension_semantics=("arbitrary", "arbitrary"),
            vmem_limit_bytes=VMEM_LIMIT_BYTES),
        name="hybrid_layer",
    )(x, w_in, pool_w, pool_scale, sinks, w_out, g_pre, g_post)


@jax.jit
def kernel(x, w_in, pool_w, pool_scale, attn_sinks, w_out, norm_pre, norm_post):
    for layer in range(w_in.shape[0]):
        x = _layer(layer, x, w_in, pool_w, pool_scale, attn_sinks, w_out, norm_pre, norm_post)
    return x
```

```python
import functools

import jax
import jax.numpy as jnp
from jax import lax
from jax.experimental import pallas as pl
from jax.experimental.pallas import tpu as pltpu

D_MODEL = 1024
D_POOL = 512
POOL_WINDOWS = (2, 4, 8, 16)
POOL_GROUP = 128
HEAD_DIM = 64
D_ATTN = 512
N_HEADS = 8
N_KV_HEADS = 2
GQA_GROUP = N_HEADS // N_KV_HEADS
WINDOW = 128
D_IN = 2304
EPS = 1e-6
NEG_INF = -1e30
LOG2_E = 1.4426950408889634

OFF_U, OFF_PG, OFF_Q, OFF_K, OFF_V, OFF_AG = 0, 512, 1024, 1536, 1664, 1792

LANES = 128
BF16_ROWS = 16
PV_ROWS = HEAD_DIM + BF16_ROWS
POOL_HALO = 16
SEQ_TILE = 1024
Q_SUB = 64
OUT_ROWS = 512
KEY_SPAN = Q_SUB + WINDOW
VMEM_LIMIT_BYTES = 56 * 1024 * 1024


def _silu(x):
    return x * (1.0 / (1.0 + jnp.exp(-x)))


def _layer_kernel(layer, x_ref, win_ref, poolw_ref, pscale_ref, sinks_ref, wout_ref,
                  gpre_ref, gpost_ref, o_ref,
                  u_buf, q_buf, kz_buf, vt_buf, ag_buf, pg_buf, pooled_buf, mixp_buf, mixa_buf, bias_buf, wi_buf,
                  wo_buf, pw_buf):
    tile = x_ref.shape[0]
    n_sub = tile // Q_SUB
    t = pl.program_id(1)

    @pl.when((pl.program_id(0) == 0) & (t == 0))
    def _first_step():
        r = lax.broadcasted_iota(jnp.int32, (KEY_SPAN, GQA_GROUP * Q_SUB), 0)
        col = lax.broadcasted_iota(jnp.int32, (KEY_SPAN, GQA_GROUP * Q_SUB), 1)
        dist = (col & (Q_SUB - 1)) + WINDOW - r
        in_win = (dist >= 0) & (dist < WINDOW)
        distf = dist.astype(jnp.float32)
        for kv in range(N_KV_HEADS):
            slope = jnp.zeros_like(distf)
            for g in range(GQA_GROUP):
                j = kv * GQA_GROUP + g
                slope = jnp.where(col >= g * Q_SUB, 2.0 ** (-8.0 * (j + 1) / N_HEADS), slope)
            b = -slope * distf * LOG2_E
            for f in range(3):
                bias_buf[f, kv * KEY_SPAN:(kv + 1) * KEY_SPAN, :] = jnp.where(
                    in_win & (r >= f * Q_SUB), b, NEG_INF)

        a0 = lax.broadcasted_iota(jnp.int32, (D_ATTN, D_ATTN), 0)
        a1 = lax.broadcasted_iota(jnp.int32, (D_ATTN, D_ATTN), 1)

        def source(n):
            return ((n >> 6) & 1) * (GQA_GROUP * HEAD_DIM) + (n >> 7) * HEAD_DIM + (n & (HEAD_DIM - 1))

        perm = jnp.where(a0 == source(a1), 1.0, 0.0).astype(jnp.bfloat16)
        perm_t = jnp.where(a1 == source(a0), 1.0, 0.0).astype(jnp.bfloat16)
        for lo, hi in ((OFF_U, OFF_PG), (OFF_PG, OFF_Q), (OFF_Q, OFF_K), (OFF_K, OFF_AG), (OFF_AG, D_IN)):
            w = win_ref[:, lo:hi].astype(jnp.bfloat16)
            if lo in (OFF_Q, OFF_AG):
                w = jnp.dot(w, perm, preferred_element_type=jnp.float32).astype(jnp.bfloat16)
            wi_buf[:, lo:hi] = w
        wo_buf[:D_POOL, :] = wout_ref[:D_POOL, :].astype(jnp.bfloat16)
        wo_buf[D_POOL:, :] = jnp.dot(perm_t, wout_ref[D_POOL:, :].astype(jnp.bfloat16),
                                     preferred_element_type=jnp.float32).astype(jnp.bfloat16)
        pw_buf[...] = jnp.zeros(pw_buf.shape, jnp.bfloat16)
        for g in range(len(POOL_WINDOWS)):
            d0 = (g % 2) * POOL_GROUP
            pw_buf[g // 2, d0:d0 + POOL_GROUP, d0:d0 + POOL_GROUP] = poolw_ref[g].astype(jnp.bfloat16)

    @pl.when(t == 0)
    def _zero_halo():
        u_buf[0:POOL_HALO, :] = jnp.zeros((POOL_HALO, D_POOL), jnp.float32)
        kz_buf[:, 0:WINDOW, :] = jnp.zeros((N_KV_HEADS, WINDOW, LANES), jnp.bfloat16)
        vt_buf[:, 0:WINDOW] = jnp.zeros((LANES, WINDOW), jnp.bfloat16)

    x = x_ref[...]
    ms = jnp.mean(x * x, axis=-1, keepdims=True)
    h = (x * lax.rsqrt(ms + EPS) * gpre_ref[layer:layer + 1, :]).astype(jnp.bfloat16)

    def proj(lo, hi):
        return jnp.dot(h, wi_buf[:, lo:hi], preferred_element_type=jnp.float32)

    qkvg = proj(OFF_Q, D_IN)
    q_buf[...] = (qkvg[:, :D_ATTN] * (HEAD_DIM ** -0.5 * LOG2_E)).astype(jnp.bfloat16)
    kf, vf = qkvg[:, D_ATTN:D_ATTN + LANES], qkvg[:, D_ATTN + LANES:D_ATTN + 2 * LANES]
    ag_buf[...] = _silu(qkvg[:, OFF_AG - OFF_Q:])
    lane = lax.broadcasted_iota(jnp.int32, (1, LANES), 1)
    kz_buf[0, WINDOW:WINDOW + tile, :] = jnp.where(lane < HEAD_DIM, kf, 0.0).astype(jnp.bfloat16)
    kz_buf[1, WINDOW:WINDOW + tile, :] = jnp.where(lane >= HEAD_DIM, kf, 0.0).astype(jnp.bfloat16)
    vt_buf[:, WINDOW:WINDOW + tile] = vf.T.astype(jnp.bfloat16)

    def pool_u_item():
        u_buf[POOL_HALO:POOL_HALO + tile, :] = proj(OFF_U, OFF_PG)

    pos1 = (lax.broadcasted_iota(jnp.int32, (tile, POOL_GROUP), 0) + (t * tile + 1)
            ).astype(jnp.float32)

    def pool_window_item():
        for g, w in enumerate(POOL_WINDOWS):
            cols = slice(g * POOL_GROUP, (g + 1) * POOL_GROUP)
            ext = u_buf[:, cols]
            acc = ext
            shift = 1
            while shift < w:
                acc = acc + pltpu.roll(acc, shift, axis=0)
                shift *= 2
            cur = ext[POOL_HALO:]
            pooled = acc[POOL_HALO:] / jnp.minimum(pos1, float(w)) - cur
            pooled_buf[:, cols] = pooled.astype(jnp.bfloat16)

    def pool_mix_item():
        for pair in range(len(POOL_WINDOWS) // 2):
            cols = slice(pair * 2 * POOL_GROUP, (pair + 1) * 2 * POOL_GROUP)
            mixed = jnp.dot(pooled_buf[:, cols], pw_buf[pair], preferred_element_type=jnp.float32)
            scale = pscale_ref[layer:layer + 1, cols]
            mixp_buf[:, cols] = (mixed * scale * pg_buf[:, cols]).astype(jnp.bfloat16)

    def pool_gate_item():
        pg_buf[...] = _silu(proj(OFF_PG, OFF_Q))

    def pool_u_gate_item():
        ug = proj(OFF_U, OFF_Q)
        u_buf[POOL_HALO:POOL_HALO + tile, :] = ug[:, :D_POOL]
        pg_buf[...] = _silu(ug[:, D_POOL:])

    col = lax.broadcasted_iota(jnp.int32, (1, GQA_GROUP * Q_SUB), 1)
    sink_rows = []
    for kv in range(N_KV_HEADS):
        row = jnp.zeros((1, GQA_GROUP * Q_SUB), jnp.float32)
        for g in range(GQA_GROUP):
            row = jnp.where(col >= g * Q_SUB, sinks_ref[layer, kv * GQA_GROUP + g], row)
        sink_rows.append(row * LOG2_E)
    first = (t == 0)
    nt_dims = (((1,), (1,)), ((), ()))
    pad = jnp.zeros((Q_SUB, GQA_GROUP * Q_SUB), jnp.bfloat16)
    ones_rows = jnp.ones((BF16_ROWS, 2 * LANES), jnp.bfloat16)

    def attn_scores(sb):
        r0 = sb * Q_SUB
        qrows = slice(r0, r0 + Q_SUB)
        krows = slice(r0, r0 + KEY_SPAN)
        variant = jnp.where(first, 2 - sb, 0) if sb < 2 else 0
        q4 = jnp.concatenate([q_buf[qrows, g * LANES:(g + 1) * LANES] for g in range(GQA_GROUP)],
                             axis=0)
        kk = jnp.concatenate([kz_buf[0, krows, :], kz_buf[1, krows, :]], axis=0)
        s = lax.dot_general(kk, q4, nt_dims, preferred_element_type=jnp.float32)
        return s + bias_buf[variant]

    def attn_finish(sb, s):
        r0 = sb * Q_SUB
        qrows = slice(r0, r0 + Q_SUB)
        lrows = slice(r0 % OUT_ROWS, r0 % OUT_ROWS + Q_SUB)
        c0 = r0 if sb % 2 == 0 else r0 - Q_SUB
        maxes, probs, vts = [], [], []
        for kv in range(N_KV_HEADS):
            sh = s[kv * KEY_SPAN:(kv + 1) * KEY_SPAN]
            m = jnp.maximum(jnp.max(sh, axis=0, keepdims=True), sink_rows[kv])
            pb = jnp.exp2(sh - m).astype(jnp.bfloat16)
            probs.append(jnp.concatenate([pb, pad] if sb % 2 == 0 else [pad, pb], axis=0))
            maxes.append(m)
            vts += [vt_buf[kv * HEAD_DIM:(kv + 1) * HEAD_DIM, c0:c0 + 2 * LANES], ones_rows]
        o_all = jnp.dot(jnp.concatenate(vts, axis=0), jnp.concatenate(probs, axis=1),
                        preferred_element_type=jnp.float32)
        outs = []
        for kv in range(N_KV_HEADS):
            o = o_all[kv * PV_ROWS:(kv + 1) * PV_ROWS, kv * 2 * LANES:(kv + 1) * 2 * LANES]
            l = o[HEAD_DIM:HEAD_DIM + 1] + jnp.exp2(sink_rows[kv] - maxes[kv])
            outs.append(o[:HEAD_DIM] * (1.0 / l))
        for pr in range(GQA_GROUP // 2):
            both = jnp.concatenate([o[:, pr * LANES:(pr + 1) * LANES] for o in outs], axis=0)
            both_t = both.T
            for half in range(2):
                g = 2 * pr + half
                val = both_t[half * Q_SUB:(half + 1) * Q_SUB] * ag_buf[qrows, g * LANES:(g + 1) * LANES]
                mixa_buf[r0 // OUT_ROWS, lrows, g * LANES:(g + 1) * LANES] = val.astype(jnp.bfloat16)

    def out_half_item(half):
        rows = slice(half * OUT_ROWS, (half + 1) * OUT_ROWS)
        y = (jnp.dot(mixp_buf[rows, :], wo_buf[:D_POOL, :], preferred_element_type=jnp.float32)
             + jnp.dot(mixa_buf[half], wo_buf[D_POOL:, :], preferred_element_type=jnp.float32))
        ms2 = jnp.mean(y * y, axis=-1, keepdims=True)
        o_ref[rows, :] = x_ref[rows, :] + y * lax.rsqrt(ms2 + EPS) * gpost_ref[layer:layer + 1, :]

    dense_items = [[pool_u_gate_item, pool_window_item, pool_mix_item],
                   [functools.partial(out_half_item, 0)]]
    group = n_sub // len(dense_items)
    for p, items in enumerate(dense_items):
        subs = range(p * group, (p + 1) * group)
        scores = [attn_scores(sb) for sb in subs]
        for item in items:
            item()
        for sb, s in zip(subs, scores):
            attn_finish(sb, s)
    out_half_item(1)

    u_buf[0:POOL_HALO, :] = u_buf[tile:tile + POOL_HALO, :]
    kz_buf[:, 0:WINDOW, :] = kz_buf[:, tile:tile + WINDOW, :]
    vt_buf[:, 0:WINDOW] = vt_buf[:, tile:tile + WINDOW]


def _layer(layer, x, w_in, pool_w, pool_scale, sinks, w_out, g_pre, g_post):
    batch, seq, d = x.shape
    tile = SEQ_TILE
    assert seq % tile == 0 and tile % (2 * Q_SUB) == 0 and d == D_MODEL

    def of_layer(*shape):
        return pl.BlockSpec((None,) + shape, lambda b, t: (layer,) + (0,) * len(shape),
                            pipeline_mode=pl.Buffered(1))

    def whole(a):
        return pl.BlockSpec(a.shape, lambda b, t: (0,) * a.ndim)

    return pl.pallas_call(
        functools.partial(_layer_kernel, layer),
        out_shape=jax.ShapeDtypeStruct(x.shape, x.dtype),
        grid=(batch, seq // tile),
        in_specs=[
            pl.BlockSpec((None, tile, d), lambda b, t: (b, t, 0)),
            of_layer(D_MODEL, D_IN),
            of_layer(len(POOL_WINDOWS), POOL_GROUP, POOL_GROUP),
            whole(pool_scale),
            pl.BlockSpec(memory_space=pltpu.SMEM),
            of_layer(D_MODEL, D_MODEL),
            whole(g_pre),
            whole(g_post),
        ],
        out_specs=pl.BlockSpec((None, tile, d), lambda b, t: (b, t, 0)),
        scratch_shapes=[
            pltpu.VMEM((POOL_HALO + tile, D_POOL), jnp.float32),
            pltpu.VMEM((tile, D_ATTN), jnp.bfloat16),
            pltpu.VMEM((N_KV_HEADS, WINDOW + tile, LANES), jnp.bfloat16),
            pltpu.VMEM((LANES, WINDOW + tile), jnp.bfloat16),
            pltpu.VMEM((tile, D_ATTN), jnp.float32),
            pltpu.VMEM((tile, D_POOL), jnp.float32),
            pltpu.VMEM((tile, D_POOL), jnp.bfloat16),
            pltpu.VMEM((tile, D_POOL), jnp.bfloat16),
            pltpu.VMEM((tile // OUT_ROWS, OUT_ROWS, D_ATTN), jnp.bfloat16),
            pltpu.VMEM((3, N_KV_HEADS * KEY_SPAN, GQA_GROUP * Q_SUB), jnp.float32),
            pltpu.VMEM((D_MODEL, D_IN), jnp.bfloat16),
            pltpu.VMEM((D_MODEL, D_MODEL), jnp.bfloat16),
            pltpu.VMEM((len(POOL_WINDOWS) // 2, 2 * POOL_GROUP, 2 * POOL_GROUP), jnp.bfloat16),
        ],
        compiler_params=pltpu.CompilerParams(
            dimension_semantics=("arbitrary", "arbitrary"),
            vmem_limit_bytes=VMEM_LIMIT_BYTES),
        name="hybrid_layer",
    )(x, w_in, pool_w, pool_scale, sinks, w_out, g_pre, g_post)


@jax.jit
def kernel(x, w_in, pool_w, pool_scale, attn_sinks, w_out, norm_pre, norm_post):
    for layer in range(w_in.shape[0]):
        x = _layer(layer, x, w_in, pool_w, pool_scale, attn_sinks, w_out, norm_pre, norm_post)
    return x
```

```python
import functools

import jax
import jax.numpy as jnp
from jax import lax
from jax.experimental import pallas as pl
from jax.experimental.pallas import tpu as pltpu

D_MODEL = 1024
D_POOL = 512
POOL_WINDOWS = (2, 4, 8, 16)
POOL_GROUP = 128
HEAD_DIM = 64
D_ATTN = 512
N_HEADS = 8
N_KV_HEADS = 2
GQA_GROUP = N_HEADS // N_KV_HEADS
WINDOW = 128
D_IN = 2304
EPS = 1e-6
NEG_INF = -1e30
LOG2_E = 1.4426950408889634

OFF_U, OFF_PG, OFF_Q, OFF_K, OFF_V, OFF_AG = 0, 512, 1024, 1536, 1664, 1792

LANES = 128
BF16_ROWS = 16
PV_ROWS = HEAD_DIM + BF16_ROWS
POOL_HALO = 16
SEQ_TILE = 1024
Q_SUB = 64
OUT_ROWS = 512
KEY_SPAN = Q_SUB + WINDOW
VMEM_LIMIT_BYTES = 56 * 1024 * 1024


def _silu(x):
    return x * (1.0 / (1.0 + jnp.exp(-x)))


def _layer_kernel(layer, x_ref, win_ref, poolw_ref, pscale_ref, sinks_ref, wout_ref,
                  gpre_ref, gpost_ref, o_ref,
                  u_buf, q_buf, kz_buf, vt_buf, ag_buf, pg_buf, pooled_buf, mixp_buf, mixa_buf, bias_buf, wi_buf,
                  wo_buf, pw_buf):
    tile = x_ref.shape[0]
    n_sub = tile // Q_SUB
    t = pl.program_id(1)

    @pl.when((pl.program_id(0) == 0) & (t == 0))
    def _first_step():
        r = lax.broadcasted_iota(jnp.int32, (KEY_SPAN, GQA_GROUP * Q_SUB), 0)
        col = lax.broadcasted_iota(jnp.int32, (KEY_SPAN, GQA_GROUP * Q_SUB), 1)
        dist = (col & (Q_SUB - 1)) + WINDOW - r
        in_win = (dist >= 0) & (dist < WINDOW)
        distf = dist.astype(jnp.float32)
        for kv in range(N_KV_HEADS):
            slope = jnp.zeros_like(distf)
            for g in range(GQA_GROUP):
                j = kv * GQA_GROUP + g
                slope = jnp.where(col >= g * Q_SUB, 2.0 ** (-8.0 * (j + 1) / N_HEADS), slope)
            b = -slope * distf * LOG2_E
            for f in range(3):
                bias_buf[f, kv * KEY_SPAN:(kv + 1) * KEY_SPAN, :] = jnp.where(
                    in_win & (r >= f * Q_SUB), b, NEG_INF)

        a0 = lax.broadcasted_iota(jnp.int32, (D_ATTN, D_ATTN), 0)
        a1 = lax.broadcasted_iota(jnp.int32, (D_ATTN, D_ATTN), 1)

        def source(n):
            return ((n >> 6) & 1) * (GQA_GROUP * HEAD_DIM) + (n >> 7) * HEAD_DIM + (n & (HEAD_DIM - 1))

        perm = jnp.where(a0 == source(a1), 1.0, 0.0).astype(jnp.bfloat16)
        perm_t = jnp.where(a1 == source(a0), 1.0, 0.0).astype(jnp.bfloat16)
        for lo, hi in ((OFF_U, OFF_PG), (OFF_PG, OFF_Q), (OFF_Q, OFF_K), (OFF_K, OFF_AG), (OFF_AG, D_IN)):
            w = win_ref[:, lo:hi].astype(jnp.bfloat16)
            if lo in (OFF_Q, OFF_AG):
                w = jnp.dot(w, perm, preferred_element_type=jnp.float32).astype(jnp.bfloat16)
            wi_buf[:, lo:hi] = w
        wo_buf[:D_POOL, :] = wout_ref[:D_POOL, :].astype(jnp.bfloat16)
        wo_buf[D_POOL:, :] = jnp.dot(perm_t, wout_ref[D_POOL:, :].astype(jnp.bfloat16),
                                     preferred_element_type=jnp.float32).astype(jnp.bfloat16)
        pw_buf[...] = jnp.zeros(pw_buf.shape, jnp.bfloat16)
        for g in range(len(POOL_WINDOWS)):
            d0 = (g % 2) * POOL_GROUP
            pw_buf[g // 2, d0:d0 + POOL_GROUP, d0:d0 + POOL_GROUP] = poolw_ref[g].astype(jnp.bfloat16)

    @pl.when(t == 0)
    def _zero_halo():
        u_buf[0:POOL_HALO, :] = jnp.zeros((POOL_HALO, D_POOL), jnp.float32)
        kz_buf[:, 0:WINDOW, :] = jnp.zeros((N_KV_HEADS, WINDOW, LANES), jnp.bfloat16)
        vt_buf[:, 0:WINDOW] = jnp.zeros((LANES, WINDOW), jnp.bfloat16)

    x = x_ref[...]
    ms = jnp.mean(x * x, axis=-1, keepdims=True)
    h = (x * lax.rsqrt(ms + EPS) * gpre_ref[layer:layer + 1, :]).astype(jnp.bfloat16)

    def proj(lo, hi):
        return jnp.dot(h, wi_buf[:, lo:hi], preferred_element_type=jnp.float32)

    qkvg = proj(OFF_Q, D_IN)
    q_buf[...] = (qkvg[:, :D_ATTN] * (HEAD_DIM ** -0.5 * LOG2_E)).astype(jnp.bfloat16)
    kf, vf = qkvg[:, D_ATTN:D_ATTN + LANES], qkvg[:, D_ATTN + LANES:D_ATTN + 2 * LANES]
    ag_buf[...] = _silu(qkvg[:, OFF_AG - OFF_Q:])
    lane = lax.broadcasted_iota(jnp.int32, (1, LANES), 1)
    kz_buf[0, WINDOW:WINDOW + tile, :] = jnp.where(lane < HEAD_DIM, kf, 0.0).astype(jnp.bfloat16)
    kz_buf[1, WINDOW:WINDOW + tile, :] = jnp.where(lane >= HEAD_DIM, kf, 0.0).astype(jnp.bfloat16)
    vt_buf[:, WINDOW:WINDOW + tile] = vf.T.astype(jnp.bfloat16)

    def pool_u_item():
        u_buf[POOL_HALO:POOL_HALO + tile, :] = proj(OFF_U, OFF_PG)

    pos1 = (lax.broadcasted_iota(jnp.int32, (tile, POOL_GROUP), 0) + (t * tile + 1)
            ).astype(jnp.float32)

    def pool_window_item():
        for g, w in enumerate(POOL_WINDOWS):
            cols = slice(g * POOL_GROUP, (g + 1) * POOL_GROUP)
            ext = u_buf[:, cols]
            acc = ext
            shift = 1
            while shift < w:
                acc = acc + pltpu.roll(acc, shift, axis=0)
                shift *= 2
            cur = ext[POOL_HALO:]
            pooled = acc[POOL_HALO:] / jnp.minimum(pos1, float(w)) - cur
            pooled_buf[:, cols] = pooled.astype(jnp.bfloat16)

    def pool_mix_item():
        for pair in range(len(POOL_WINDOWS) // 2):
            cols = slice(pair * 2 * POOL_GROUP, (pair + 1) * 2 * POOL_GROUP)
            mixed = jnp.dot(pooled_buf[:, cols], pw_buf[pair], preferred_element_type=jnp.float32)
            scale = pscale_ref[layer:layer + 1, cols]
            mixp_buf[:, cols] = (mixed * scale * pg_buf[:, cols]).astype(jnp.bfloat16)

    def pool_gate_item():
        pg_buf[...] = _silu(proj(OFF_PG, OFF_Q))

    def pool_u_gate_item():
        ug = proj(OFF_U, OFF_Q)
        u_buf[POOL_HALO:POOL_HALO + tile, :] = ug[:, :D_POOL]
        pg_buf[...] = _silu(ug[:, D_POOL:])

    col = lax.broadcasted_iota(jnp.int32, (1, GQA_GROUP * Q_SUB), 1)
    sink_rows = []
    for kv in range(N_KV_HEADS):
        row = jnp.zeros((1, GQA_GROUP * Q_SUB), jnp.float32)
        for g in range(GQA_GROUP):
            row = jnp.where(col >= g * Q_SUB, sinks_ref[layer, kv * GQA_GROUP + g], row)
        sink_rows.append(row * LOG2_E)
    first = (t == 0)
    nt_dims = (((1,), (1,)), ((), ()))
    pad = jnp.zeros((Q_SUB, GQA_GROUP * Q_SUB), jnp.bfloat16)
    ones_rows = jnp.ones((BF16_ROWS, 2 * LANES), jnp.bfloat16)

    def attn_scores(sb):
        r0 = sb * Q_SUB
        qrows = slice(r0, r0 + Q_SUB)
        krows = slice(r0, r0 + KEY_SPAN)
        variant = jnp.where(first, 2 - sb, 0) if sb < 2 else 0
        q4 = jnp.concatenate([q_buf[qrows, g * LANES:(g + 1) * LANES] for g in range(GQA_GROUP)],
                             axis=0)
        kk = jnp.concatenate([kz_buf[0, krows, :], kz_buf[1, krows, :]], axis=0)
        s = lax.dot_general(kk, q4, nt_dims, preferred_element_type=jnp.float32)
        return s + bias_buf[variant]

    def attn_finish_pair(sb_even, scores_pair):
        c0 = sb_even * Q_SUB
        vts = []
        for kv in range(N_KV_HEADS):
            vts += [vt_buf[kv * HEAD_DIM:(kv + 1) * HEAD_DIM, c0:c0 + 2 * LANES], ones_rows]
        maxes, probs = [], []
        for j, s in enumerate(scores_pair):
            for kv in range(N_KV_HEADS):
                sh = s[kv * KEY_SPAN:(kv + 1) * KEY_SPAN]
                m = jnp.maximum(jnp.max(sh, axis=0, keepdims=True), sink_rows[kv])
                pb = jnp.exp2(sh - m).astype(jnp.bfloat16)
                probs.append(jnp.concatenate([pb, pad] if j == 0 else [pad, pb], axis=0))
                maxes.append(m)
        o_all = jnp.dot(jnp.concatenate(vts, axis=0), jnp.concatenate(probs, axis=1),
                        preferred_element_type=jnp.float32)
        for j in range(2):
            r0 = (sb_even + j) * Q_SUB
            qrows = slice(r0, r0 + Q_SUB)
            lrows = slice(r0 % OUT_ROWS, r0 % OUT_ROWS + Q_SUB)
            outs = []
            for kv in range(N_KV_HEADS):
                blk = j * N_KV_HEADS + kv
                o = o_all[kv * PV_ROWS:(kv + 1) * PV_ROWS, blk * 2 * LANES:(blk + 1) * 2 * LANES]
                l = o[HEAD_DIM:HEAD_DIM + 1] + jnp.exp2(sink_rows[kv] - maxes[blk])
                outs.append(o[:HEAD_DIM] * (1.0 / l))
            for pr in range(GQA_GROUP // 2):
                both = jnp.concatenate([o[:, pr * LANES:(pr + 1) * LANES] for o in outs], axis=0)
                both_t = both.T
                for half in range(2):
                    g = 2 * pr + half
                    val = both_t[half * Q_SUB:(half + 1) * Q_SUB] * ag_buf[qrows, g * LANES:(g + 1) * LANES]
                    mixa_buf[r0 // OUT_ROWS, lrows, g * LANES:(g + 1) * LANES] = val.astype(jnp.bfloat16)

    def out_half_item(half):
        rows = slice(half * OUT_ROWS, (half + 1) * OUT_ROWS)
        y = (jnp.dot(mixp_buf[rows, :], wo_buf[:D_POOL, :], preferred_element_type=jnp.float32)
             + jnp.dot(mixa_buf[half], wo_buf[D_POOL:, :], preferred_element_type=jnp.float32))
        ms2 = jnp.mean(y * y, axis=-1, keepdims=True)
        o_ref[rows, :] = x_ref[rows, :] + y * lax.rsqrt(ms2 + EPS) * gpost_ref[layer:layer + 1, :]

    dense_items = [[pool_u_gate_item, pool_window_item, pool_mix_item],
                   [functools.partial(out_half_item, 0)]]
    group = n_sub // len(dense_items)
    for p, items in enumerate(dense_items):
        subs = range(p * group, (p + 1) * group)
        scores = [attn_scores(sb) for sb in subs]
        for item in items:
            item()
        for j in range(0, group, 2):
            attn_finish_pair(subs[j], scores[j:j + 2])
    out_half_item(1)

    u_buf[0:POOL_HALO, :] = u_buf[tile:tile + POOL_HALO, :]
    kz_buf[:, 0:WINDOW, :] = kz_buf[:, tile:tile + WINDOW, :]
    vt_buf[:, 0:WINDOW] = vt_buf[:, tile:tile + WINDOW]


def _layer(layer, x, w_in, pool_w, pool_scale, sinks, w_out, g_pre, g_post):
    batch, seq, d = x.shape
    tile = SEQ_TILE
    assert seq % tile == 0 and tile % (2 * Q_SUB) == 0 and d == D_MODEL

    def of_layer(*shape):
        return pl.BlockSpec((None,) + shape, lambda b, t: (layer,) + (0,) * len(shape),
                            pipeline_mode=pl.Buffered(1))

    def whole(a):
        return pl.BlockSpec(a.shape, lambda b, t: (0,) * a.ndim)

    return pl.pallas_call(
        functools.partial(_layer_kernel, layer),
        out_shape=jax.ShapeDtypeStruct(x.shape, x.dtype),
        grid=(batch, seq // tile),
        in_specs=[
            pl.BlockSpec((None, tile, d), lambda b, t: (b, t, 0)),
            of_layer(D_MODEL, D_IN),
            of_layer(len(POOL_WINDOWS), POOL_GROUP, POOL_GROUP),
            whole(pool_scale),
            pl.BlockSpec(memory_space=pltpu.SMEM),
            of_layer(D_MODEL, D_MODEL),
            whole(g_pre),
            whole(g_post),
        ],
        out_specs=pl.BlockSpec((None, tile, d), lambda b, t: (b, t, 0)),
        scratch_shapes=[
            pltpu.VMEM((POOL_HALO + tile, D_POOL), jnp.float32),
            pltpu.VMEM((tile, D_ATTN), jnp.bfloat16),
            pltpu.VMEM((N_KV_HEADS, WINDOW + tile, LANES), jnp.bfloat16),
            pltpu.VMEM((LANES, WINDOW + tile), jnp.bfloat16),
            pltpu.VMEM((tile, D_ATTN), jnp.float32),
            pltpu.VMEM((tile, D_POOL), jnp.float32),
            pltpu.VMEM((tile, D_POOL), jnp.bfloat16),
            pltpu.VMEM((tile, D_POOL), jnp.bfloat16),
            pltpu.VMEM((tile // OUT_ROWS, OUT_ROWS, D_ATTN), jnp.bfloat16),
            pltpu.VMEM((3, N_KV_HEADS * KEY_SPAN, GQA_GROUP * Q_SUB), jnp.float32),
            pltpu.VMEM((D_MODEL, D_IN), jnp.bfloat16),
            pltpu.VMEM((D_MODEL, D_MODEL), jnp.bfloat16),
            pltpu.VMEM((len(POOL_WINDOWS) // 2, 2 * POOL_GROUP, 2 * POOL_GROUP), jnp.bfloat16),
        ],
        compiler_params=pltpu.CompilerParams(
            dimension_semantics=("arbitrary", "arbitrary"),
            vmem_limit_bytes=VMEM_LIMIT_BYTES),
        name="hybrid_layer",
    )(x, w_in, pool_w, pool_scale, sinks, w_out, g_pre, g_post)


@jax.jit
def kernel(x, w_in, pool_w, pool_scale, attn_sinks, w_out, norm_pre, norm_post):
    for layer in range(w_in.shape[0]):
        x = _layer(layer, x, w_in, pool_w, pool_scale, attn_sinks, w_out, norm_pre, norm_post)
    return x
```

```python
import functools

import jax
import jax.numpy as jnp
from jax import lax
from jax.experimental import pallas as pl
from jax.experimental.pallas import tpu as pltpu

D_MODEL = 1024
D_POOL = 512
POOL_WINDOWS = (2, 4, 8, 16)
POOL_GROUP = 128
HEAD_DIM = 64
D_ATTN = 512
N_HEADS = 8
N_KV_HEADS = 2
GQA_GROUP = N_HEADS // N_KV_HEADS
WINDOW = 128
D_IN = 2304
EPS = 1e-6
NEG_INF = -1e30
LOG2_E = 1.4426950408889634

OFF_U, OFF_PG, OFF_Q, OFF_K, OFF_V, OFF_AG = 0, 512, 1024, 1536, 1664, 1792

LANES = 128
BF16_ROWS = 16
PV_ROWS = HEAD_DIM + BF16_ROWS
POOL_HALO = 16
SEQ_TILE = 1024
Q_SUB = 64
OUT_ROWS = 512
KEY_SPAN = Q_SUB + WINDOW
VMEM_LIMIT_BYTES = 56 * 1024 * 1024


def _silu(x):
    return x * (1.0 / (1.0 + jnp.exp(-x)))


def _layer_kernel(layer, x_ref, win_ref, poolw_ref, pscale_ref, sinks_ref, wout_ref,
                  gpre_ref, gpost_ref, o_ref,
                  u_buf, q_buf, kz_buf, vt_buf, ag_buf, pg_buf, pooled_buf, mixp_buf, mixa_buf, bias_buf, wi_buf,
                  wo_buf, pw_buf):
    tile = x_ref.shape[0]
    n_sub = tile // Q_SUB
    t = pl.program_id(1)

    @pl.when((pl.program_id(0) == 0) & (t == 0))
    def _first_step():
        r = lax.broadcasted_iota(jnp.int32, (KEY_SPAN, GQA_GROUP * Q_SUB), 0)
        col = lax.broadcasted_iota(jnp.int32, (KEY_SPAN, GQA_GROUP * Q_SUB), 1)
        dist = (col & (Q_SUB - 1)) + WINDOW - r
        in_win = (dist >= 0) & (dist < WINDOW)
        distf = dist.astype(jnp.float32)
        for kv in range(N_KV_HEADS):
            slope = jnp.zeros_like(distf)
            for g in range(GQA_GROUP):
                j = kv * GQA_GROUP + g
                slope = jnp.where(col >= g * Q_SUB, 2.0 ** (-8.0 * (j + 1) / N_HEADS), slope)
            b = -slope * distf * LOG2_E
            for f in range(3):
                bias_buf[f, kv * KEY_SPAN:(kv + 1) * KEY_SPAN, :] = jnp.where(
                    in_win & (r >= f * Q_SUB), b, NEG_INF)

        a0 = lax.broadcasted_iota(jnp.int32, (D_ATTN, D_ATTN), 0)
        a1 = lax.broadcasted_iota(jnp.int32, (D_ATTN, D_ATTN), 1)
        source = ((a1 >> 6) & 1) * (GQA_GROUP * HEAD_DIM) + (a1 >> 7) * HEAD_DIM + (a1 & (HEAD_DIM - 1))
        perm = jnp.where(a0 == source, 1.0, 0.0).astype(jnp.bfloat16)
        for lo, hi in ((OFF_U, OFF_PG), (OFF_PG, OFF_Q), (OFF_Q, OFF_K), (OFF_K, OFF_AG), (OFF_AG, D_IN)):
            w = win_ref[:, lo:hi].astype(jnp.bfloat16)
            if lo in (OFF_Q, OFF_AG):
                w = jnp.dot(w, perm, preferred_element_type=jnp.float32).astype(jnp.bfloat16)
            wi_buf[:, lo:hi] = w
        wo_buf[:D_POOL, :] = wout_ref[:D_POOL, :].astype(jnp.bfloat16)
        for g in range(GQA_GROUP):
            for kv in range(N_KV_HEADS):
                dst = D_POOL + (g * N_KV_HEADS + kv) * HEAD_DIM
                src = D_POOL + (kv * GQA_GROUP + g) * HEAD_DIM
                wo_buf[dst:dst + HEAD_DIM, :] = wout_ref[src:src + HEAD_DIM, :].astype(jnp.bfloat16)
        pw_buf[...] = jnp.zeros(pw_buf.shape, jnp.bfloat16)
        for g in range(len(POOL_WINDOWS)):
            d0 = (g % 2) * POOL_GROUP
            pw_buf[g // 2, d0:d0 + POOL_GROUP, d0:d0 + POOL_GROUP] = poolw_ref[g].astype(jnp.bfloat16)

    @pl.when(t == 0)
    def _zero_halo():
        u_buf[0:POOL_HALO, :] = jnp.zeros((POOL_HALO, D_POOL), jnp.float32)
        kz_buf[:, 0:WINDOW, :] = jnp.zeros((N_KV_HEADS, WINDOW, LANES), jnp.bfloat16)
        vt_buf[:, 0:WINDOW] = jnp.zeros((LANES, WINDOW), jnp.bfloat16)

    x = x_ref[...]
    ms = jnp.mean(x * x, axis=-1, keepdims=True)
    h = (x * lax.rsqrt(ms + EPS) * gpre_ref[layer:layer + 1, :]).astype(jnp.bfloat16)

    def proj(lo, hi):
        return jnp.dot(h, wi_buf[:, lo:hi], preferred_element_type=jnp.float32)

    qkvg = proj(OFF_Q, D_IN)
    q_buf[...] = (qkvg[:, :D_ATTN] * (HEAD_DIM ** -0.5 * LOG2_E)).astype(jnp.bfloat16)
    kf, vf = qkvg[:, D_ATTN:D_ATTN + LANES], qkvg[:, D_ATTN + LANES:D_ATTN + 2 * LANES]
    ag_buf[...] = _silu(qkvg[:, OFF_AG - OFF_Q:])
    lane = lax.broadcasted_iota(jnp.int32, (1, LANES), 1)
    kz_buf[0, WINDOW:WINDOW + tile, :] = jnp.where(lane < HEAD_DIM, kf, 0.0).astype(jnp.bfloat16)
    kz_buf[1, WINDOW:WINDOW + tile, :] = jnp.where(lane >= HEAD_DIM, kf, 0.0).astype(jnp.bfloat16)
    vt_buf[:, WINDOW:WINDOW + tile] = vf.T.astype(jnp.bfloat16)

    def pool_u_item():
        u_buf[POOL_HALO:POOL_HALO + tile, :] = proj(OFF_U, OFF_PG)

    pos1 = (lax.broadcasted_iota(jnp.int32, (tile, POOL_GROUP), 0) + (t * tile + 1)
            ).astype(jnp.float32)

    def pool_window_item():
        for g, w in enumerate(POOL_WINDOWS):
            cols = slice(g * POOL_GROUP, (g + 1) * POOL_GROUP)
            ext = u_buf[:, cols]
            acc = ext
            shift = 1
            while shift < w:
                acc = acc + pltpu.roll(acc, shift, axis=0)
                shift *= 2
            cur = ext[POOL_HALO:]
            pooled = acc[POOL_HALO:] / jnp.minimum(pos1, float(w)) - cur
            pooled_buf[:, cols] = pooled.astype(jnp.bfloat16)

    def pool_mix_item():
        for pair in range(len(POOL_WINDOWS) // 2):
            cols = slice(pair * 2 * POOL_GROUP, (pair + 1) * 2 * POOL_GROUP)
            mixed = jnp.dot(pooled_buf[:, cols], pw_buf[pair], preferred_element_type=jnp.float32)
            scale = pscale_ref[layer:layer + 1, cols]
            mixp_buf[:, cols] = (mixed * scale * pg_buf[:, cols]).astype(jnp.bfloat16)

    def pool_gate_item():
        pg_buf[...] = _silu(proj(OFF_PG, OFF_Q))

    def pool_u_gate_item():
        ug = proj(OFF_U, OFF_Q)
        u_buf[POOL_HALO:POOL_HALO + tile, :] = ug[:, :D_POOL]
        pg_buf[...] = _silu(ug[:, D_POOL:])

    col = lax.broadcasted_iota(jnp.int32, (1, GQA_GROUP * Q_SUB), 1)
    sink_rows = []
    for kv in range(N_KV_HEADS):
        row = jnp.zeros((1, GQA_GROUP * Q_SUB), jnp.float32)
        for g in range(GQA_GROUP):
            row = jnp.where(col >= g * Q_SUB, sinks_ref[layer, kv * GQA_GROUP + g], row)
        sink_rows.append(row * LOG2_E)
    first = (t == 0)
    nt_dims = (((1,), (1,)), ((), ()))
    pad = jnp.zeros((Q_SUB, GQA_GROUP * Q_SUB), jnp.bfloat16)
    ones_rows = jnp.ones((BF16_ROWS, 2 * LANES), jnp.bfloat16)

    def attn_scores(sb):
        r0 = sb * Q_SUB
        qrows = slice(r0, r0 + Q_SUB)
        krows = slice(r0, r0 + KEY_SPAN)
        variant = jnp.where(first, 2 - sb, 0) if sb < 2 else 0
        q4 = jnp.concatenate([q_buf[qrows, g * LANES:(g + 1) * LANES] for g in range(GQA_GROUP)],
                             axis=0)
        kk = jnp.concatenate([kz_buf[0, krows, :], kz_buf[1, krows, :]], axis=0)
        s = lax.dot_general(kk, q4, nt_dims, preferred_element_type=jnp.float32)
        return s + bias_buf[variant]

    def attn_finish_pair(sb_even, scores_pair):
        c0 = sb_even * Q_SUB
        vts = []
        for kv in range(N_KV_HEADS):
            vts += [vt_buf[kv * HEAD_DIM:(kv + 1) * HEAD_DIM, c0:c0 + 2 * LANES], ones_rows]
        maxes, probs = [], []
        for j, s in enumerate(scores_pair):
            for kv in range(N_KV_HEADS):
                sh = s[kv * KEY_SPAN:(kv + 1) * KEY_SPAN]
                m = jnp.maximum(jnp.max(sh, axis=0, keepdims=True), sink_rows[kv])
                pb = jnp.exp2(sh - m).astype(jnp.bfloat16)
                probs.append(jnp.concatenate([pb, pad] if j == 0 else [pad, pb], axis=0))
                maxes.append(m)
        o_all = jnp.dot(jnp.concatenate(vts, axis=0), jnp.concatenate(probs, axis=1),
                        preferred_element_type=jnp.float32)
        for j in range(2):
            r0 = (sb_even + j) * Q_SUB
            qrows = slice(r0, r0 + Q_SUB)
            lrows = slice(r0 % OUT_ROWS, r0 % OUT_ROWS + Q_SUB)
            outs = []
            for kv in range(N_KV_HEADS):
                blk = j * N_KV_HEADS + kv
                o = o_all[kv * PV_ROWS:(kv + 1) * PV_ROWS, blk * 2 * LANES:(blk + 1) * 2 * LANES]
                l = o[HEAD_DIM:HEAD_DIM + 1] + jnp.exp2(sink_rows[kv] - maxes[blk])
                outs.append(o[:HEAD_DIM] * (1.0 / l))
            for pr in range(GQA_GROUP // 2):
                both = jnp.concatenate([o[:, pr * LANES:(pr + 1) * LANES] for o in outs], axis=0)
                both_t = both.T
                for half in range(2):
                    g = 2 * pr + half
                    val = both_t[half * Q_SUB:(half + 1) * Q_SUB] * ag_buf[qrows, g * LANES:(g + 1) * LANES]
                    mixa_buf[r0 // OUT_ROWS, lrows, g * LANES:(g + 1) * LANES] = val.astype(jnp.bfloat16)

    def out_half_item(half):
        rows = slice(half * OUT_ROWS, (half + 1) * OUT_ROWS)
        y = (jnp.dot(mixp_buf[rows, :], wo_buf[:D_POOL, :], preferred_element_type=jnp.float32)
             + jnp.dot(mixa_buf[half], wo_buf[D_POOL:, :], preferred_element_type=jnp.float32))
        ms2 = jnp.mean(y * y, axis=-1, keepdims=True)
        o_ref[rows, :] = x_ref[rows, :] + y * lax.rsqrt(ms2 + EPS) * gpost_ref[layer:layer + 1, :]

    dense_items = [[pool_u_gate_item, pool_window_item, pool_mix_item],
                   [functools.partial(out_half_item, 0)]]
    group = n_sub // len(dense_items)
    for p, items in enumerate(dense_items):
        subs = range(p * group, (p + 1) * group)
        scores = [attn_scores(sb) for sb in subs]
        for item in items:
            item()
        for j in range(0, group, 2):
            attn_finish_pair(subs[j], scores[j:j + 2])
    out_half_item(1)

    u_buf[0:POOL_HALO, :] = u_buf[tile:tile + POOL_HALO, :]
    kz_buf[:, 0:WINDOW, :] = kz_buf[:, tile:tile + WINDOW, :]
    vt_buf[:, 0:WINDOW] = vt_buf[:, tile:tile + WINDOW]


def _layer(layer, x, w_in, pool_w, pool_scale, sinks, w_out, g_pre, g_post):
    batch, seq, d = x.shape
    tile = SEQ_TILE
    assert seq % tile == 0 and tile % (2 * Q_SUB) == 0 and d == D_MODEL

    def of_layer(*shape):
        return pl.BlockSpec((None,) + shape, lambda b, t: (layer,) + (0,) * len(shape),
                            pipeline_mode=pl.Buffered(1))

    def whole(a):
        return pl.BlockSpec(a.shape, lambda b, t: (0,) * a.ndim)

    return pl.pallas_call(
        functools.partial(_layer_kernel, layer),
        out_shape=jax.ShapeDtypeStruct(x.shape, x.dtype),
        grid=(batch, seq // tile),
        in_specs=[
            pl.BlockSpec((None, tile, d), lambda b, t: (b, t, 0)),
            of_layer(D_MODEL, D_IN),
            of_layer(len(POOL_WINDOWS), POOL_GROUP, POOL_GROUP),
            whole(pool_scale),
            pl.BlockSpec(memory_space=pltpu.SMEM),
            of_layer(D_MODEL, D_MODEL),
            whole(g_pre),
            whole(g_post),
        ],
        out_specs=pl.BlockSpec((None, tile, d), lambda b, t: (b, t, 0)),
        scratch_shapes=[
            pltpu.VMEM((POOL_HALO + tile, D_POOL), jnp.float32),
            pltpu.VMEM((tile, D_ATTN), jnp.bfloat16),
            pltpu.VMEM((N_KV_HEADS, WINDOW + tile, LANES), jnp.bfloat16),
            pltpu.VMEM((LANES, WINDOW + tile), jnp.bfloat16),
            pltpu.VMEM((tile, D_ATTN), jnp.float32),
            pltpu.VMEM((tile, D_POOL), jnp.float32),
            pltpu.VMEM((tile, D_POOL), jnp.bfloat16),
            pltpu.VMEM((tile, D_POOL), jnp.bfloat16),
            pltpu.VMEM((tile // OUT_ROWS, OUT_ROWS, D_ATTN), jnp.bfloat16),
            pltpu.VMEM((3, N_KV_HEADS * KEY_SPAN, GQA_GROUP * Q_SUB), jnp.float32),
            pltpu.VMEM((D_MODEL, D_IN), jnp.bfloat16),
            pltpu.VMEM((D_MODEL, D_MODEL), jnp.bfloat16),
            pltpu.VMEM((len(POOL_WINDOWS) // 2, 2 * POOL_GROUP, 2 * POOL_GROUP), jnp.bfloat16),
        ],
        compiler_params=pltpu.CompilerParams(
            dimension_semantics=("arbitrary", "arbitrary"),
            vmem_limit_bytes=VMEM_LIMIT_BYTES),
        name="hybrid_layer",
    )(x, w_in, pool_w, pool_scale, sinks, w_out, g_pre, g_post)


@jax.jit
def kernel(x, w_in, pool_w, pool_scale, attn_sinks, w_out, norm_pre, norm_post):
    for layer in range(w_in.shape[0]):
        x = _layer(layer, x, w_in, pool_w, pool_scale, attn_sinks, w_out, norm_pre, norm_post)
    return x
```

```python
import functools

import jax
import jax.numpy as jnp
from jax import lax
from jax.experimental import pallas as pl
from jax.experimental.pallas import tpu as pltpu

D_MODEL = 1024
D_POOL = 512
POOL_WINDOWS = (2, 4, 8, 16)
POOL_GROUP = 128
HEAD_DIM = 64
D_ATTN = 512
N_HEADS = 8
N_KV_HEADS = 2
GQA_GROUP = N_HEADS // N_KV_HEADS
WINDOW = 128
D_IN = 2304
EPS = 1e-6
NEG_INF = -1e30
LOG2_E = 1.4426950408889634

OFF_U, OFF_PG, OFF_Q, OFF_K, OFF_V, OFF_AG = 0, 512, 1024, 1536, 1664, 1792

LANES = 128
BF16_ROWS = 16
PV_ROWS = HEAD_DIM + BF16_ROWS
POOL_HALO = 16
SEQ_TILE = 1024
Q_SUB = 64
OUT_ROWS = 512
KEY_SPAN = Q_SUB + WINDOW
VMEM_LIMIT_BYTES = 56 * 1024 * 1024


def _silu(x):
    return x * (1.0 / (1.0 + jnp.exp(-x)))


def _layer_kernel(layer, x_ref, win_ref, poolw_ref, pscale_ref, sinks_ref, wout_ref,
                  gpre_ref, gpost_ref, o_ref,
                  u_buf, q_buf, kz_buf, vt_buf, ag_buf, pg_buf, pooled_buf, mixp_buf, mixa_buf, bias_buf, wi_buf,
                  wo_buf, pw_buf):
    tile = x_ref.shape[0]
    n_sub = tile // Q_SUB
    t = pl.program_id(1)

    @pl.when((pl.program_id(0) == 0) & (t == 0))
    def _first_step():
        r = lax.broadcasted_iota(jnp.int32, (KEY_SPAN, GQA_GROUP * Q_SUB), 0)
        col = lax.broadcasted_iota(jnp.int32, (KEY_SPAN, GQA_GROUP * Q_SUB), 1)
        dist = (col & (Q_SUB - 1)) + WINDOW - r
        in_win = (dist >= 0) & (dist < WINDOW)
        distf = dist.astype(jnp.float32)
        for kv in range(N_KV_HEADS):
            slope = jnp.zeros_like(distf)
            for g in range(GQA_GROUP):
                j = kv * GQA_GROUP + g
                slope = jnp.where(col >= g * Q_SUB, 2.0 ** (-8.0 * (j + 1) / N_HEADS), slope)
            b = -slope * distf * LOG2_E
            for f in range(3):
                bias_buf[f, kv * KEY_SPAN:(kv + 1) * KEY_SPAN, :] = jnp.where(
                    in_win & (r >= f * Q_SUB), b, NEG_INF)

        a0 = lax.broadcasted_iota(jnp.int32, (D_ATTN, D_ATTN), 0)
        a1 = lax.broadcasted_iota(jnp.int32, (D_ATTN, D_ATTN), 1)
        source = ((a1 >> 6) & 1) * (GQA_GROUP * HEAD_DIM) + (a1 >> 7) * HEAD_DIM + (a1 & (HEAD_DIM - 1))
        perm = jnp.where(a0 == source, 1.0, 0.0).astype(jnp.bfloat16)
        for lo, hi in ((OFF_U, OFF_PG), (OFF_PG, OFF_Q), (OFF_Q, OFF_K), (OFF_K, OFF_AG), (OFF_AG, D_IN)):
            w = win_ref[:, lo:hi].astype(jnp.bfloat16)
            if lo in (OFF_Q, OFF_AG):
                w = jnp.dot(w, perm, preferred_element_type=jnp.float32).astype(jnp.bfloat16)
            wi_buf[:, lo:hi] = w
        wo_buf[:D_POOL, :] = wout_ref[:D_POOL, :].astype(jnp.bfloat16)
        for g in range(GQA_GROUP):
            for kv in range(N_KV_HEADS):
                dst = D_POOL + (g * N_KV_HEADS + kv) * HEAD_DIM
                src = D_POOL + (kv * GQA_GROUP + g) * HEAD_DIM
                wo_buf[dst:dst + HEAD_DIM, :] = wout_ref[src:src + HEAD_DIM, :].astype(jnp.bfloat16)
        pw_buf[...] = jnp.zeros(pw_buf.shape, jnp.bfloat16)
        for g in range(len(POOL_WINDOWS)):
            d0 = (g % 2) * POOL_GROUP
            pw_buf[g // 2, d0:d0 + POOL_GROUP, d0:d0 + POOL_GROUP] = poolw_ref[g].astype(jnp.bfloat16)

    @pl.when(t == 0)
    def _zero_halo():
        u_buf[0:POOL_HALO, :] = jnp.zeros((POOL_HALO, D_POOL), jnp.float32)
        kz_buf[:, 0:WINDOW, :] = jnp.zeros((N_KV_HEADS, WINDOW, LANES), jnp.bfloat16)
        vt_buf[:, 0:WINDOW] = jnp.zeros((LANES, WINDOW), jnp.bfloat16)

    x = x_ref[...]
    ms = jnp.mean(x * x, axis=-1, keepdims=True)
    h = (x * lax.rsqrt(ms + EPS) * gpre_ref[layer:layer + 1, :]).astype(jnp.bfloat16)

    def proj(lo, hi):
        return jnp.dot(h, wi_buf[:, lo:hi], preferred_element_type=jnp.float32)

    proj_all = proj(0, D_IN)
    u_buf[POOL_HALO:POOL_HALO + tile, :] = proj_all[:, OFF_U:OFF_PG]
    pg_buf[...] = _silu(proj_all[:, OFF_PG:OFF_Q])
    qkvg = proj_all[:, OFF_Q:]
    q_buf[...] = (qkvg[:, :D_ATTN] * (HEAD_DIM ** -0.5 * LOG2_E)).astype(jnp.bfloat16)
    kf, vf = qkvg[:, D_ATTN:D_ATTN + LANES], qkvg[:, D_ATTN + LANES:D_ATTN + 2 * LANES]
    ag_buf[...] = _silu(qkvg[:, OFF_AG - OFF_Q:])
    lane = lax.broadcasted_iota(jnp.int32, (1, LANES), 1)
    kz_buf[0, WINDOW:WINDOW + tile, :] = jnp.where(lane < HEAD_DIM, kf, 0.0).astype(jnp.bfloat16)
    kz_buf[1, WINDOW:WINDOW + tile, :] = jnp.where(lane >= HEAD_DIM, kf, 0.0).astype(jnp.bfloat16)
    vt_buf[:, WINDOW:WINDOW + tile] = vf.T.astype(jnp.bfloat16)

    pos1 = (lax.broadcasted_iota(jnp.int32, (tile, POOL_GROUP), 0) + (t * tile + 1)
            ).astype(jnp.float32)

    def pool_window_item():
        for g, w in enumerate(POOL_WINDOWS):
            cols = slice(g * POOL_GROUP, (g + 1) * POOL_GROUP)
            ext = u_buf[:, cols]
            acc = ext
            shift = 1
            while shift < w:
                acc = acc + pltpu.roll(acc, shift, axis=0)
                shift *= 2
            cur = ext[POOL_HALO:]
            pooled = acc[POOL_HALO:] / jnp.minimum(pos1, float(w)) - cur
            pooled_buf[:, cols] = pooled.astype(jnp.bfloat16)

    def pool_mix_item():
        for pair in range(len(POOL_WINDOWS) // 2):
            cols = slice(pair * 2 * POOL_GROUP, (pair + 1) * 2 * POOL_GROUP)
            mixed = jnp.dot(pooled_buf[:, cols], pw_buf[pair], preferred_element_type=jnp.float32)
            scale = pscale_ref[layer:layer + 1, cols]
            mixp_buf[:, cols] = (mixed * scale * pg_buf[:, cols]).astype(jnp.bfloat16)

    col = lax.broadcasted_iota(jnp.int32, (1, GQA_GROUP * Q_SUB), 1)
    sink_rows = []
    for kv in range(N_KV_HEADS):
        row = jnp.zeros((1, GQA_GROUP * Q_SUB), jnp.float32)
        for g in range(GQA_GROUP):
            row = jnp.where(col >= g * Q_SUB, sinks_ref[layer, kv * GQA_GROUP + g], row)
        sink_rows.append(row * LOG2_E)
    first = (t == 0)
    nt_dims = (((1,), (1,)), ((), ()))
    pad = jnp.zeros((Q_SUB, GQA_GROUP * Q_SUB), jnp.bfloat16)
    ones_rows = jnp.ones((BF16_ROWS, 2 * LANES), jnp.bfloat16)

    def attn_scores(sb):
        r0 = sb * Q_SUB
        qrows = slice(r0, r0 + Q_SUB)
        krows = slice(r0, r0 + KEY_SPAN)
        variant = jnp.where(first, 2 - sb, 0) if sb < 2 else 0
        q4 = jnp.concatenate([q_buf[qrows, g * LANES:(g + 1) * LANES] for g in range(GQA_GROUP)],
                             axis=0)
        kk = jnp.concatenate([kz_buf[0, krows, :], kz_buf[1, krows, :]], axis=0)
        s = lax.dot_general(kk, q4, nt_dims, preferred_element_type=jnp.float32)
        return s + bias_buf[variant]

    def attn_finish_pair(sb_even, scores_pair):
        c0 = sb_even * Q_SUB
        vts = []
        for kv in range(N_KV_HEADS):
            vts += [vt_buf[kv * HEAD_DIM:(kv + 1) * HEAD_DIM, c0:c0 + 2 * LANES], ones_rows]
        maxes, probs = [], []
        for j, s in enumerate(scores_pair):
            for kv in range(N_KV_HEADS):
                sh = s[kv * KEY_SPAN:(kv + 1) * KEY_SPAN]
                m = jnp.maximum(jnp.max(sh, axis=0, keepdims=True), sink_rows[kv])
                pb = jnp.exp2(sh - m).astype(jnp.bfloat16)
                probs.append(jnp.concatenate([pb, pad] if j == 0 else [pad, pb], axis=0))
                maxes.append(m)
        o_all = jnp.dot(jnp.concatenate(vts, axis=0), jnp.concatenate(probs, axis=1),
                        preferred_element_type=jnp.float32)
        for j in range(2):
            r0 = (sb_even + j) * Q_SUB
            qrows = slice(r0, r0 + Q_SUB)
            lrows = slice(r0 % OUT_ROWS, r0 % OUT_ROWS + Q_SUB)
            outs = []
            for kv in range(N_KV_HEADS):
                blk = j * N_KV_HEADS + kv
                o = o_all[kv * PV_ROWS:(kv + 1) * PV_ROWS, blk * 2 * LANES:(blk + 1) * 2 * LANES]
                l = o[HEAD_DIM:HEAD_DIM + 1] + jnp.exp2(sink_rows[kv] - maxes[blk])
                outs.append(o[:HEAD_DIM] * (1.0 / l))
            for pr in range(GQA_GROUP // 2):
                both = jnp.concatenate([o[:, pr * LANES:(pr + 1) * LANES] for o in outs], axis=0)
                both_t = both.T
                for half in range(2):
                    g = 2 * pr + half
                    val = both_t[half * Q_SUB:(half + 1) * Q_SUB] * ag_buf[qrows, g * LANES:(g + 1) * LANES]
                    mixa_buf[r0 // OUT_ROWS, lrows, g * LANES:(g + 1) * LANES] = val.astype(jnp.bfloat16)

    def out_half_item(half):
        rows = slice(half * OUT_ROWS, (half + 1) * OUT_ROWS)
        y = (jnp.dot(mixp_buf[rows, :], wo_buf[:D_POOL, :], preferred_element_type=jnp.float32)
             + jnp.dot(mixa_buf[half], wo_buf[D_POOL:, :], preferred_element_type=jnp.float32))
        ms2 = jnp.mean(y * y, axis=-1, keepdims=True)
        o_ref[rows, :] = x_ref[rows, :] + y * lax.rsqrt(ms2 + EPS) * gpost_ref[layer:layer + 1, :]

    dense_items = [[pool_window_item, pool_mix_item],
                   [functools.partial(out_half_item, 0)]]
    group = n_sub // len(dense_items)
    for p, items in enumerate(dense_items):
        subs = range(p * group, (p + 1) * group)
        scores = [attn_scores(sb) for sb in subs]
        for item in items:
            item()
        for j in range(0, group, 2):
            attn_finish_pair(subs[j], scores[j:j + 2])
    out_half_item(1)

    u_buf[0:POOL_HALO, :] = u_buf[tile:tile + POOL_HALO, :]
    kz_buf[:, 0:WINDOW, :] = kz_buf[:, tile:tile + WINDOW, :]
    vt_buf[:, 0:WINDOW] = vt_buf[:, tile:tile + WINDOW]


def _layer(layer, x, w_in, pool_w, pool_scale, sinks, w_out, g_pre, g_post):
    batch, seq, d = x.shape
    tile = SEQ_TILE
    assert seq % tile == 0 and tile % (2 * Q_SUB) == 0 and d == D_MODEL

    def of_layer(*shape):
        return pl.BlockSpec((None,) + shape, lambda b, t: (layer,) + (0,) * len(shape),
                            pipeline_mode=pl.Buffered(1))

    def whole(a):
        return pl.BlockSpec(a.shape, lambda b, t: (0,) * a.ndim)

    return pl.pallas_call(
        functools.partial(_layer_kernel, layer),
        out_shape=jax.ShapeDtypeStruct(x.shape, x.dtype),
        grid=(batch, seq // tile),
        in_specs=[
            pl.BlockSpec((None, tile, d), lambda b, t: (b, t, 0)),
            of_layer(D_MODEL, D_IN),
            of_layer(len(POOL_WINDOWS), POOL_GROUP, POOL_GROUP),
            whole(pool_scale),
            pl.BlockSpec(memory_space=pltpu.SMEM),
            of_layer(D_MODEL, D_MODEL),
            whole(g_pre),
            whole(g_post),
        ],
        out_specs=pl.BlockSpec((None, tile, d), lambda b, t: (b, t, 0)),
        scratch_shapes=[
            pltpu.VMEM((POOL_HALO + tile, D_POOL), jnp.float32),
            pltpu.VMEM((tile, D_ATTN), jnp.bfloat16),
            pltpu.VMEM((N_KV_HEADS, WINDOW + tile, LANES), jnp.bfloat16),
            pltpu.VMEM((LANES, WINDOW + tile), jnp.bfloat16),
            pltpu.VMEM((tile, D_ATTN), jnp.float32),
            pltpu.VMEM((tile, D_POOL), jnp.float32),
            pltpu.VMEM((tile, D_POOL), jnp.bfloat16),
            pltpu.VMEM((tile, D_POOL), jnp.bfloat16),
            pltpu.VMEM((tile // OUT_ROWS, OUT_ROWS, D_ATTN), jnp.bfloat16),
            pltpu.VMEM((3, N_KV_HEADS * KEY_SPAN, GQA_GROUP * Q_SUB), jnp.float32),
            pltpu.VMEM((D_MODEL, D_IN), jnp.bfloat16),
            pltpu.VMEM((D_MODEL, D_MODEL), jnp.bfloat16),
            pltpu.VMEM((len(POOL_WINDOWS) // 2, 2 * POOL_GROUP, 2 * POOL_GROUP), jnp.bfloat16),
        ],
        compiler_params=pltpu.CompilerParams(
            dimension_semantics=("arbitrary", "arbitrary"),
            vmem_limit_bytes=VMEM_LIMIT_BYTES),
        name="hybrid_layer",
    )(x, w_in, pool_w, pool_scale, sinks, w_out, g_pre, g_post)


@jax.jit
def kernel(x, w_in, pool_w, pool_scale, attn_sinks, w_out, norm_pre, norm_post):
    for layer in range(w_in.shape[0]):
        x = _layer(layer, x, w_in, pool_w, pool_scale, attn_sinks, w_out, norm_pre, norm_post)
    return x
```

```python
import functools

import jax
import jax.numpy as jnp
from jax import lax
from jax.experimental import pallas as pl
from jax.experimental.pallas import tpu as pltpu

D_MODEL = 1024
D_POOL = 512
POOL_WINDOWS = (2, 4, 8, 16)
POOL_GROUP = 128
HEAD_DIM = 64
D_ATTN = 512
N_HEADS = 8
N_KV_HEADS = 2
GQA_GROUP = N_HEADS // N_KV_HEADS
WINDOW = 128
D_IN = 2304
EPS = 1e-6
NEG_INF = -1e30
LOG2_E = 1.4426950408889634

OFF_U, OFF_PG, OFF_Q, OFF_K, OFF_V, OFF_AG = 0, 512, 1024, 1536, 1664, 1792

LANES = 128
BF16_ROWS = 16
PV_ROWS = HEAD_DIM + BF16_ROWS
POOL_HALO = 16
SEQ_TILE = 1024
Q_SUB = 64
OUT_ROWS = 512
KEY_SPAN = Q_SUB + WINDOW
VMEM_LIMIT_BYTES = 56 * 1024 * 1024


def _silu(x):
    return x * (1.0 / (1.0 + jnp.exp(-x)))


def _layer_kernel(layer, x_ref, win_ref, poolw_ref, pscale_ref, sinks_ref, wout_ref,
                  gpre_ref, gpost_ref, o_ref,
                  u_buf, q_buf, kz_buf, vt_buf, ag_buf, pg_buf, pooled_buf, mixp_buf, mixa_buf, bias_buf, wi_buf,
                  wo_buf, pw_buf):
    tile = x_ref.shape[0]
    n_sub = tile // Q_SUB
    t = pl.program_id(1)

    @pl.when((pl.program_id(0) == 0) & (t == 0))
    def _first_step():
        r = lax.broadcasted_iota(jnp.int32, (KEY_SPAN, GQA_GROUP * Q_SUB), 0)
        col = lax.broadcasted_iota(jnp.int32, (KEY_SPAN, GQA_GROUP * Q_SUB), 1)
        dist = (col & (Q_SUB - 1)) + WINDOW - r
        in_win = (dist >= 0) & (dist < WINDOW)
        distf = dist.astype(jnp.float32)
        for kv in range(N_KV_HEADS):
            slope = jnp.zeros_like(distf)
            for g in range(GQA_GROUP):
                j = kv * GQA_GROUP + g
                slope = jnp.where(col >= g * Q_SUB, 2.0 ** (-8.0 * (j + 1) / N_HEADS), slope)
            b = -slope * distf * LOG2_E
            for f in range(3):
                bias_buf[f, kv * KEY_SPAN:(kv + 1) * KEY_SPAN, :] = jnp.where(
                    in_win & (r >= f * Q_SUB), b, NEG_INF)

        a0 = lax.broadcasted_iota(jnp.int32, (D_ATTN, D_ATTN), 0)
        a1 = lax.broadcasted_iota(jnp.int32, (D_ATTN, D_ATTN), 1)
        source = ((a1 >> 6) & 1) * (GQA_GROUP * HEAD_DIM) + (a1 >> 7) * HEAD_DIM + (a1 & (HEAD_DIM - 1))
        perm = jnp.where(a0 == source, 1.0, 0.0).astype(jnp.bfloat16)
        for lo, hi in ((OFF_U, OFF_PG), (OFF_PG, OFF_Q), (OFF_Q, OFF_K), (OFF_K, OFF_AG), (OFF_AG, D_IN)):
            w = win_ref[:, lo:hi].astype(jnp.bfloat16)
            if lo in (OFF_Q, OFF_AG):
                w = jnp.dot(w, perm, preferred_element_type=jnp.float32).astype(jnp.bfloat16)
            wi_buf[:, lo:hi] = w
        wo_buf[:D_POOL, :] = wout_ref[:D_POOL, :].astype(jnp.bfloat16)
        for g in range(GQA_GROUP):
            for kv in range(N_KV_HEADS):
                dst = D_POOL + (g * N_KV_HEADS + kv) * HEAD_DIM
                src = D_POOL + (kv * GQA_GROUP + g) * HEAD_DIM
                wo_buf[dst:dst + HEAD_DIM, :] = wout_ref[src:src + HEAD_DIM, :].astype(jnp.bfloat16)
        pw_buf[...] = jnp.zeros(pw_buf.shape, jnp.bfloat16)
        for g in range(len(POOL_WINDOWS)):
            d0 = (g % 2) * POOL_GROUP
            pw_buf[g // 2, d0:d0 + POOL_GROUP, d0:d0 + POOL_GROUP] = poolw_ref[g].astype(jnp.bfloat16)

    @pl.when(t == 0)
    def _zero_halo():
        u_buf[0:POOL_HALO, :] = jnp.zeros((POOL_HALO, D_POOL), jnp.float32)
        kz_buf[:, 0:WINDOW, :] = jnp.zeros((N_KV_HEADS, WINDOW, LANES), jnp.bfloat16)
        vt_buf[:, 0:WINDOW] = jnp.zeros((LANES, WINDOW), jnp.bfloat16)

    x = x_ref[...]
    ms = jnp.mean(x * x, axis=-1, keepdims=True)
    h = (x * lax.rsqrt(ms + EPS) * gpre_ref[layer:layer + 1, :]).astype(jnp.bfloat16)

    def proj(lo, hi):
        return jnp.dot(h, wi_buf[:, lo:hi], preferred_element_type=jnp.float32)

    qkvg = proj(OFF_Q, D_IN)
    q_buf[...] = (qkvg[:, :D_ATTN] * (HEAD_DIM ** -0.5 * LOG2_E)).astype(jnp.bfloat16)
    kf, vf = qkvg[:, D_ATTN:D_ATTN + LANES], qkvg[:, D_ATTN + LANES:D_ATTN + 2 * LANES]
    ag_buf[...] = _silu(qkvg[:, OFF_AG - OFF_Q:])
    lane = lax.broadcasted_iota(jnp.int32, (1, LANES), 1)
    kz_buf[0, WINDOW:WINDOW + tile, :] = jnp.where(lane < HEAD_DIM, kf, 0.0).astype(jnp.bfloat16)
    kz_buf[1, WINDOW:WINDOW + tile, :] = jnp.where(lane >= HEAD_DIM, kf, 0.0).astype(jnp.bfloat16)
    vt_buf[:, WINDOW:WINDOW + tile] = vf.T.astype(jnp.bfloat16)

    pos1 = (lax.broadcasted_iota(jnp.int32, (tile, POOL_GROUP), 0) + (t * tile + 1)
            ).astype(jnp.float32)

    def pool_window_item():
        for g, w in enumerate(POOL_WINDOWS):
            cols = slice(g * POOL_GROUP, (g + 1) * POOL_GROUP)
            ext = u_buf[:, cols]
            acc = ext
            shift = 1
            while shift < w:
                acc = acc + pltpu.roll(acc, shift, axis=0)
                shift *= 2
            cur = ext[POOL_HALO:]
            pooled = acc[POOL_HALO:] / jnp.minimum(pos1, float(w)) - cur
            pooled_buf[:, cols] = pooled.astype(jnp.bfloat16)

    def pool_mix_item():
        for pair in range(len(POOL_WINDOWS) // 2):
            cols = slice(pair * 2 * POOL_GROUP, (pair + 1) * 2 * POOL_GROUP)
            mixed = jnp.dot(pooled_buf[:, cols], pw_buf[pair], preferred_element_type=jnp.float32)
            scale = pscale_ref[layer:layer + 1, cols]
            mixp_buf[:, cols] = (mixed * scale * pg_buf[:, cols]).astype(jnp.bfloat16)

    def pool_u_gate_item():
        ug = proj(OFF_U, OFF_Q)
        u_buf[POOL_HALO:POOL_HALO + tile, :] = ug[:, :D_POOL]
        pg_buf[...] = _silu(ug[:, D_POOL:])

    col = lax.broadcasted_iota(jnp.int32, (1, GQA_GROUP * Q_SUB), 1)
    sink_rows = []
    for kv in range(N_KV_HEADS):
        row = jnp.zeros((1, GQA_GROUP * Q_SUB), jnp.float32)
        for g in range(GQA_GROUP):
            row = jnp.where(col >= g * Q_SUB, sinks_ref[layer, kv * GQA_GROUP + g], row)
        sink_rows.append(row * LOG2_E)
    first = (t == 0)
    nt_dims = (((1,), (1,)), ((), ()))
    pad = jnp.zeros((Q_SUB, GQA_GROUP * Q_SUB), jnp.bfloat16)
    ones_rows = jnp.ones((BF16_ROWS, 2 * LANES), jnp.bfloat16)

    def attn_scores(sb):
        r0 = sb * Q_SUB
        qrows = slice(r0, r0 + Q_SUB)
        krows = slice(r0, r0 + KEY_SPAN)
        variant = jnp.where(first, 2 - sb, 0) if sb < 2 else 0
        q4 = jnp.concatenate([q_buf[qrows, g * LANES:(g + 1) * LANES] for g in range(GQA_GROUP)],
                             axis=0)
        kk = jnp.concatenate([kz_buf[0, krows, :], kz_buf[1, krows, :]], axis=0)
        s = lax.dot_general(kk, q4, nt_dims, preferred_element_type=jnp.float32)
        return s + bias_buf[variant]

    def attn_finish_pair(sb_even, scores_pair):
        c0 = sb_even * Q_SUB
        vts = []
        for kv in range(N_KV_HEADS):
            vts += [vt_buf[kv * HEAD_DIM:(kv + 1) * HEAD_DIM, c0:c0 + 2 * LANES], ones_rows]
        maxes, probs = [], []
        for j, s in enumerate(scores_pair):
            for kv in range(N_KV_HEADS):
                sh = s[kv * KEY_SPAN:(kv + 1) * KEY_SPAN]
                m = jnp.maximum(jnp.max(sh, axis=0, keepdims=True), sink_rows[kv])
                pb = jnp.exp2(sh - m).astype(jnp.bfloat16)
                probs.append(jnp.concatenate([pb, pad] if j == 0 else [pad, pb], axis=0))
                maxes.append(m)
        o_all = jnp.dot(jnp.concatenate(vts, axis=0), jnp.concatenate(probs, axis=1),
                        preferred_element_type=jnp.float32)
        for j in range(2):
            r0 = (sb_even + j) * Q_SUB
            qrows = slice(r0, r0 + Q_SUB)
            lrows = slice(r0 % OUT_ROWS, r0 % OUT_ROWS + Q_SUB)
            outs = []
            for kv in range(N_KV_HEADS):
                blk = j * N_KV_HEADS + kv
                o = o_all[kv * PV_ROWS:(kv + 1) * PV_ROWS, blk * 2 * LANES:(blk + 1) * 2 * LANES]
                l = o[HEAD_DIM:HEAD_DIM + 1] + jnp.exp2(sink_rows[kv] - maxes[blk])
                outs.append(o[:HEAD_DIM] * (1.0 / l))
            for pr in range(GQA_GROUP // 2):
                both = jnp.concatenate([o[:, pr * LANES:(pr + 1) * LANES] for o in outs], axis=0)
                both_t = both.T
                for half in range(2):
                    g = 2 * pr + half
                    val = both_t[half * Q_SUB:(half + 1) * Q_SUB] * ag_buf[qrows, g * LANES:(g + 1) * LANES]
                    mixa_buf[r0 // OUT_ROWS, lrows, g * LANES:(g + 1) * LANES] = val.astype(jnp.bfloat16)

    def out_half_item(half):
        rows = slice(half * OUT_ROWS, (half + 1) * OUT_ROWS)
        y = (jnp.dot(mixp_buf[rows, :], wo_buf[:D_POOL, :], preferred_element_type=jnp.float32)
             + jnp.dot(mixa_buf[half], wo_buf[D_POOL:, :], preferred_element_type=jnp.float32))
        ms2 = jnp.mean(y * y, axis=-1, keepdims=True)
        o_ref[rows, :] = x_ref[rows, :] + y * lax.rsqrt(ms2 + EPS) * gpost_ref[layer:layer + 1, :]

    dense_items = [[pool_u_gate_item, pool_window_item, pool_mix_item],
                   [functools.partial(out_half_item, 0)]]
    group = n_sub // len(dense_items)
    for p, items in enumerate(dense_items):
        subs = range(p * group, (p + 1) * group)
        scores = [attn_scores(sb) for sb in subs]
        for item in items:
            item()
        for j in range(0, group, 2):
            attn_finish_pair(subs[j], scores[j:j + 2])
    out_half_item(1)

    u_buf[0:POOL_HALO, :] = u_buf[tile:tile + POOL_HALO, :]
    kz_buf[:, 0:WINDOW, :] = kz_buf[:, tile:tile + WINDOW, :]
    vt_buf[:, 0:WINDOW] = vt_buf[:, tile:tile + WINDOW]


def _layer(layer, x, w_in, pool_w, pool_scale, sinks, w_out, g_pre, g_post):
    batch, seq, d = x.shape
    tile = SEQ_TILE
    assert seq % tile == 0 and tile == 2 * OUT_ROWS and OUT_ROWS % (2 * Q_SUB) == 0 and d == D_MODEL

    def of_layer(*shape):
        return pl.BlockSpec((None,) + shape, lambda b, t: (layer,) + (0,) * len(shape),
                            pipeline_mode=pl.Buffered(1))

    def whole(a):
        return pl.BlockSpec(a.shape, lambda b, t: (0,) * a.ndim)

    return pl.pallas_call(
        functools.partial(_layer_kernel, layer),
        out_shape=jax.ShapeDtypeStruct(x.shape, x.dtype),
        grid=(batch, seq // tile),
        in_specs=[
            pl.BlockSpec((None, tile, d), lambda b, t: (b, t, 0)),
            of_layer(D_MODEL, D_IN),
            of_layer(len(POOL_WINDOWS), POOL_GROUP, POOL_GROUP),
            whole(pool_scale),
            pl.BlockSpec(memory_space=pltpu.SMEM),
            of_layer(D_MODEL, D_MODEL),
            whole(g_pre),
            whole(g_post),
        ],
        out_specs=pl.BlockSpec((None, tile, d), lambda b, t: (b, t, 0)),
        scratch_shapes=[
            pltpu.VMEM((POOL_HALO + tile, D_POOL), jnp.float32),
            pltpu.VMEM((tile, D_ATTN), jnp.bfloat16),
            pltpu.VMEM((N_KV_HEADS, WINDOW + tile, LANES), jnp.bfloat16),
            pltpu.VMEM((LANES, WINDOW + tile), jnp.bfloat16),
            pltpu.VMEM((tile, D_ATTN), jnp.float32),
            pltpu.VMEM((tile, D_POOL), jnp.float32),
            pltpu.VMEM((tile, D_POOL), jnp.bfloat16),
            pltpu.VMEM((tile, D_POOL), jnp.bfloat16),
            pltpu.VMEM((tile // OUT_ROWS, OUT_ROWS, D_ATTN), jnp.bfloat16),
            pltpu.VMEM((3, N_KV_HEADS * KEY_SPAN, GQA_GROUP * Q_SUB), jnp.float32),
            pltpu.VMEM((D_MODEL, D_IN), jnp.bfloat16),
            pltpu.VMEM((D_MODEL, D_MODEL), jnp.bfloat16),
            pltpu.VMEM((len(POOL_WINDOWS) // 2, 2 * POOL_GROUP, 2 * POOL_GROUP), jnp.bfloat16),
        ],
        compiler_params=pltpu.CompilerParams(
            dimension_semantics=("arbitrary", "arbitrary"),
            vmem_limit_bytes=VMEM_LIMIT_BYTES),
        name="hybrid_layer",
    )(x, w_in, pool_w, pool_scale, sinks, w_out, g_pre, g_post)


@jax.jit
def kernel(x, w_in, pool_w, pool_scale, attn_sinks, w_out, norm_pre, norm_post):
    for layer in range(w_in.shape[0]):
        x = _layer(layer, x, w_in, pool_w, pool_scale, attn_sinks, w_out, norm_pre, norm_post)
    return x
```

```python
import functools

import jax
import jax.numpy as jnp
from jax import lax
from jax.experimental import pallas as pl
from jax.experimental.pallas import tpu as pltpu

D_MODEL = 1024
D_POOL = 512
POOL_WINDOWS = (2, 4, 8, 16)
POOL_GROUP = 128
HEAD_DIM = 64
D_ATTN = 512
N_HEADS = 8
N_KV_HEADS = 2
GQA_GROUP = N_HEADS // N_KV_HEADS
WINDOW = 128
D_IN = 2304
EPS = 1e-6
NEG_INF = -1e30
LOG2_E = 1.4426950408889634

OFF_U, OFF_PG, OFF_Q, OFF_K, OFF_V, OFF_AG = 0, 512, 1024, 1536, 1664, 1792

LANES = 128
BF16_ROWS = 16
PV_ROWS = HEAD_DIM + BF16_ROWS
POOL_HALO = 16
SEQ_TILE = 1024
Q_SUB = 64
OUT_ROWS = 512
KEY_SPAN = Q_SUB + WINDOW
VMEM_LIMIT_BYTES = 56 * 1024 * 1024


def _silu(x):
    return x * (1.0 / (1.0 + jnp.exp(-x)))


def _layer_kernel(layer, x_ref, win_ref, poolw_ref, pscale_ref, sinks_ref, wout_ref,
                  gpre_ref, gpost_ref, o_ref,
                  u_buf, q_buf, kz_buf, vt_buf, ag_buf, pg_buf, pooled_buf, mixp_buf, mixa_buf, bias_buf, wi_buf,
                  wo_buf, pw_buf):
    tile = x_ref.shape[0]
    n_sub = tile // Q_SUB
    t = pl.program_id(1)

    @pl.when((pl.program_id(0) == 0) & (t == 0))
    def _first_step():
        r = lax.broadcasted_iota(jnp.int32, (KEY_SPAN, GQA_GROUP * Q_SUB), 0)
        col = lax.broadcasted_iota(jnp.int32, (KEY_SPAN, GQA_GROUP * Q_SUB), 1)
        dist = (col & (Q_SUB - 1)) + WINDOW - r
        in_win = (dist >= 0) & (dist < WINDOW)
        distf = dist.astype(jnp.float32)
        for kv in range(N_KV_HEADS):
            slope = jnp.zeros_like(distf)
            for g in range(GQA_GROUP):
                j = kv * GQA_GROUP + g
                slope = jnp.where(col >= g * Q_SUB, 2.0 ** (-8.0 * (j + 1) / N_HEADS), slope)
            b = -slope * distf * LOG2_E
            for f in range(3):
                bias_buf[f, kv * KEY_SPAN:(kv + 1) * KEY_SPAN, :] = jnp.where(
                    in_win & (r >= f * Q_SUB), b, NEG_INF)

        a0 = lax.broadcasted_iota(jnp.int32, (D_ATTN, D_ATTN), 0)
        a1 = lax.broadcasted_iota(jnp.int32, (D_ATTN, D_ATTN), 1)
        source = ((a1 >> 6) & 1) * (GQA_GROUP * HEAD_DIM) + (a1 >> 7) * HEAD_DIM + (a1 & (HEAD_DIM - 1))
        perm = jnp.where(a0 == source, 1.0, 0.0).astype(jnp.bfloat16)
        for lo, hi in ((OFF_U, OFF_PG), (OFF_PG, OFF_Q), (OFF_Q, OFF_K), (OFF_K, OFF_AG), (OFF_AG, D_IN)):
            w = win_ref[:, lo:hi].astype(jnp.bfloat16)
            if lo in (OFF_Q, OFF_AG):
                w = jnp.dot(w, perm, preferred_element_type=jnp.float32).astype(jnp.bfloat16)
            wi_buf[:, lo:hi] = w
        wo_buf[:D_POOL, :] = wout_ref[:D_POOL, :].astype(jnp.bfloat16)
        for g in range(GQA_GROUP):
            for kv in range(N_KV_HEADS):
                dst = D_POOL + (g * N_KV_HEADS + kv) * HEAD_DIM
                src = D_POOL + (kv * GQA_GROUP + g) * HEAD_DIM
                wo_buf[dst:dst + HEAD_DIM, :] = wout_ref[src:src + HEAD_DIM, :].astype(jnp.bfloat16)
        pw_buf[...] = jnp.zeros(pw_buf.shape, jnp.bfloat16)
        for g in range(len(POOL_WINDOWS)):
            d0 = (g % 2) * POOL_GROUP
            pw_buf[g // 2, d0:d0 + POOL_GROUP, d0:d0 + POOL_GROUP] = poolw_ref[g].astype(jnp.bfloat16)

    @pl.when(t == 0)
    def _zero_halo():
        u_buf[0:POOL_HALO, :] = jnp.zeros((POOL_HALO, D_POOL), jnp.float32)
        kz_buf[:, 0:WINDOW, :] = jnp.zeros((N_KV_HEADS, WINDOW, LANES), jnp.bfloat16)
        vt_buf[:, 0:WINDOW] = jnp.zeros((LANES, WINDOW), jnp.bfloat16)

    x = x_ref[...]
    ms = jnp.mean(x * x, axis=-1, keepdims=True)
    h = (x * lax.rsqrt(ms + EPS) * gpre_ref[layer:layer + 1, :]).astype(jnp.bfloat16)

    def proj(lo, hi):
        return jnp.dot(h, wi_buf[:, lo:hi], preferred_element_type=jnp.float32)

    qkvg = proj(OFF_Q, D_IN)
    q_buf[...] = (qkvg[:, :D_ATTN] * (HEAD_DIM ** -0.5 * LOG2_E)).astype(jnp.bfloat16)
    kf, vf = qkvg[:, D_ATTN:D_ATTN + LANES], qkvg[:, D_ATTN + LANES:D_ATTN + 2 * LANES]
    ag_buf[...] = _silu(qkvg[:, OFF_AG - OFF_Q:])
    lane = lax.broadcasted_iota(jnp.int32, (1, LANES), 1)
    kz_buf[0, WINDOW:WINDOW + tile, :] = jnp.where(lane < HEAD_DIM, kf, 0.0).astype(jnp.bfloat16)
    kz_buf[1, WINDOW:WINDOW + tile, :] = jnp.where(lane >= HEAD_DIM, kf, 0.0).astype(jnp.bfloat16)
    vt_buf[:, WINDOW:WINDOW + tile] = vf.T.astype(jnp.bfloat16)

    pos1 = (lax.broadcasted_iota(jnp.int32, (tile, POOL_GROUP), 0) + (t * tile + 1)
            ).astype(jnp.float32)

    def pool_window_item():
        for g, w in enumerate(POOL_WINDOWS):
            cols = slice(g * POOL_GROUP, (g + 1) * POOL_GROUP)
            ext = u_buf[:, cols]
            acc = ext
            shift = 1
            while shift < w:
                acc = acc + pltpu.roll(acc, shift, axis=0)
                shift *= 2
            cur = ext[POOL_HALO:]
            pooled = acc[POOL_HALO:] / jnp.minimum(pos1, float(w)) - cur
            pooled_buf[:, cols] = pooled.astype(jnp.bfloat16)

    def pool_mix_item():
        for pair in range(len(POOL_WINDOWS) // 2):
            cols = slice(pair * 2 * POOL_GROUP, (pair + 1) * 2 * POOL_GROUP)
            mixed = jnp.dot(pooled_buf[:, cols], pw_buf[pair], preferred_element_type=jnp.float32)
            scale = pscale_ref[layer:layer + 1, cols]
            mixp_buf[:, cols] = (mixed * scale * pg_buf[:, cols]).astype(jnp.bfloat16)

    def pool_u_gate_item():
        ug = proj(OFF_U, OFF_Q)
        u_buf[POOL_HALO:POOL_HALO + tile, :] = ug[:, :D_POOL]
        pg_buf[...] = _silu(ug[:, D_POOL:])

    col = lax.broadcasted_iota(jnp.int32, (1, GQA_GROUP * Q_SUB), 1)
    sink_rows = []
    for kv in range(N_KV_HEADS):
        row = jnp.zeros((1, GQA_GROUP * Q_SUB), jnp.float32)
        for g in range(GQA_GROUP):
            row = jnp.where(col >= g * Q_SUB, sinks_ref[layer, kv * GQA_GROUP + g], row)
        sink_rows.append(row * LOG2_E)
    first = (t == 0)
    nt_dims = (((1,), (1,)), ((), ()))
    pad = jnp.zeros((Q_SUB, GQA_GROUP * Q_SUB), jnp.bfloat16)
    ones_rows = jnp.ones((BF16_ROWS, 2 * LANES), jnp.bfloat16)

    def attn_scores(sb):
        r0 = sb * Q_SUB
        qrows = slice(r0, r0 + Q_SUB)
        krows = slice(r0, r0 + KEY_SPAN)
        variant = jnp.where(first, 2 - sb, 0) if sb < 2 else 0
        q4 = jnp.concatenate([q_buf[qrows, g * LANES:(g + 1) * LANES] for g in range(GQA_GROUP)],
                             axis=0)
        kk = jnp.concatenate([kz_buf[0, krows, :], kz_buf[1, krows, :]], axis=0)
        s = lax.dot_general(kk, q4, nt_dims, preferred_element_type=jnp.float32)
        return s + bias_buf[variant]

    def attn_finish_pair(sb_even, scores_pair):
        c0 = sb_even * Q_SUB
        vts = []
        for kv in range(N_KV_HEADS):
            vts += [vt_buf[kv * HEAD_DIM:(kv + 1) * HEAD_DIM, c0:c0 + 2 * LANES], ones_rows]
        maxes, probs = [], []
        for j, s in enumerate(scores_pair):
            for kv in range(N_KV_HEADS):
                sh = s[kv * KEY_SPAN:(kv + 1) * KEY_SPAN]
                m = jnp.maximum(jnp.max(sh, axis=0, keepdims=True), sink_rows[kv])
                pb = jnp.exp2(sh - m).astype(jnp.bfloat16)
                probs.append(jnp.concatenate([pb, pad] if j == 0 else [pad, pb], axis=0))
                maxes.append(m)
        o_all = jnp.dot(jnp.concatenate(vts, axis=0), jnp.concatenate(probs, axis=1),
                        preferred_element_type=jnp.float32)
        for j in range(2):
            r0 = (sb_even + j) * Q_SUB
            qrows = slice(r0, r0 + Q_SUB)
            lrows = slice(r0 % OUT_ROWS, r0 % OUT_ROWS + Q_SUB)
            outs = []
            for kv in range(N_KV_HEADS):
                blk = j * N_KV_HEADS + kv
                o = o_all[kv * PV_ROWS:(kv + 1) * PV_ROWS, blk * 2 * LANES:(blk + 1) * 2 * LANES]
                l = o[HEAD_DIM:HEAD_DIM + 1] + jnp.exp2(sink_rows[kv] - maxes[blk])
                outs.append(o[:HEAD_DIM] * (1.0 / l))
            for pr in range(GQA_GROUP // 2):
                both = jnp.concatenate([o[:, pr * LANES:(pr + 1) * LANES] for o in outs], axis=0)
                both_t = both.T
                for half in range(2):
                    g = 2 * pr + half
                    val = both_t[half * Q_SUB:(half + 1) * Q_SUB] * ag_buf[qrows, g * LANES:(g + 1) * LANES]
                    mixa_buf[r0 // OUT_ROWS, lrows, g * LANES:(g + 1) * LANES] = val.astype(jnp.bfloat16)

    def out_half_item(half):
        rows = slice(half * OUT_ROWS, (half + 1) * OUT_ROWS)
        y = (jnp.dot(mixp_buf[rows, :], wo_buf[:D_POOL, :], preferred_element_type=jnp.float32)
             + jnp.dot(mixa_buf[half], wo_buf[D_POOL:, :], preferred_element_type=jnp.float32))
        ms2 = jnp.mean(y * y, axis=-1, keepdims=True)
        o_ref[rows, :] = x_ref[rows, :] + y * lax.rsqrt(ms2 + EPS) * gpost_ref[layer:layer + 1, :]

    dense_items = [[pool_u_gate_item, pool_window_item, pool_mix_item],
                   [functools.partial(out_half_item, 0)]]
    group = n_sub // len(dense_items)
    scores = [attn_scores(sb) for sb in range(group)]
    for p, items in enumerate(dense_items):
        subs = range(p * group, (p + 1) * group)
        for item in items:
            item()
        next_scores = ([attn_scores(sb) for sb in range((p + 1) * group, (p + 2) * group)]
                       if p + 1 < len(dense_items) else [])
        for j in range(0, group, 2):
            attn_finish_pair(subs[j], scores[j:j + 2])
        scores = next_scores
    out_half_item(1)

    u_buf[0:POOL_HALO, :] = u_buf[tile:tile + POOL_HALO, :]
    kz_buf[:, 0:WINDOW, :] = kz_buf[:, tile:tile + WINDOW, :]
    vt_buf[:, 0:WINDOW] = vt_buf[:, tile:tile + WINDOW]


def _layer(layer, x, w_in, pool_w, pool_scale, sinks, w_out, g_pre, g_post):
    batch, seq, d = x.shape
    tile = SEQ_TILE
    assert seq % tile == 0 and tile == 2 * OUT_ROWS and OUT_ROWS % (2 * Q_SUB) == 0 and d == D_MODEL

    def of_layer(*shape):
        return pl.BlockSpec((None,) + shape, lambda b, t: (layer,) + (0,) * len(shape),
                            pipeline_mode=pl.Buffered(1))

    def whole(a):
        return pl.BlockSpec(a.shape, lambda b, t: (0,) * a.ndim)

    return pl.pallas_call(
        functools.partial(_layer_kernel, layer),
        out_shape=jax.ShapeDtypeStruct(x.shape, x.dtype),
        grid=(batch, seq // tile),
        in_specs=[
            pl.BlockSpec((None, tile, d), lambda b, t: (b, t, 0)),
            of_layer(D_MODEL, D_IN),
            of_layer(len(POOL_WINDOWS), POOL_GROUP, POOL_GROUP),
            whole(pool_scale),
            pl.BlockSpec(memory_space=pltpu.SMEM),
            of_layer(D_MODEL, D_MODEL),
            whole(g_pre),
            whole(g_post),
        ],
        out_specs=pl.BlockSpec((None, tile, d), lambda b, t: (b, t, 0)),
        scratch_shapes=[
            pltpu.VMEM((POOL_HALO + tile, D_POOL), jnp.float32),
            pltpu.VMEM((tile, D_ATTN), jnp.bfloat16),
            pltpu.VMEM((N_KV_HEADS, WINDOW + tile, LANES), jnp.bfloat16),
            pltpu.VMEM((LANES, WINDOW + tile), jnp.bfloat16),
            pltpu.VMEM((tile, D_ATTN), jnp.float32),
            pltpu.VMEM((tile, D_POOL), jnp.float32),
            pltpu.VMEM((tile, D_POOL), jnp.bfloat16),
            pltpu.VMEM((tile, D_POOL), jnp.bfloat16),
            pltpu.VMEM((tile // OUT_ROWS, OUT_ROWS, D_ATTN), jnp.bfloat16),
            pltpu.VMEM((3, N_KV_HEADS * KEY_SPAN, GQA_GROUP * Q_SUB), jnp.float32),
            pltpu.VMEM((D_MODEL, D_IN), jnp.bfloat16),
            pltpu.VMEM((D_MODEL, D_MODEL), jnp.bfloat16),
            pltpu.VMEM((len(POOL_WINDOWS) // 2, 2 * POOL_GROUP, 2 * POOL_GROUP), jnp.bfloat16),
        ],
        compiler_params=pltpu.CompilerParams(
            dimension_semantics=("arbitrary", "arbitrary"),
            vmem_limit_bytes=VMEM_LIMIT_BYTES),
        name="hybrid_layer",
    )(x, w_in, pool_w, pool_scale, sinks, w_out, g_pre, g_post)


@jax.jit
def kernel(x, w_in, pool_w, pool_scale, attn_sinks, w_out, norm_pre, norm_post):
    for layer in range(w_in.shape[0]):
        x = _layer(layer, x, w_in, pool_w, pool_scale, attn_sinks, w_out, norm_pre, norm_post)
    return x
```

```python
import functools

import jax
import jax.numpy as jnp
from jax import lax
from jax.experimental import pallas as pl
from jax.experimental.pallas import tpu as pltpu

D_MODEL = 1024
D_POOL = 512
POOL_WINDOWS = (2, 4, 8, 16)
POOL_GROUP = 128
HEAD_DIM = 64
D_ATTN = 512
N_HEADS = 8
N_KV_HEADS = 2
GQA_GROUP = N_HEADS // N_KV_HEADS
WINDOW = 128
D_IN = 2304
EPS = 1e-6
NEG_INF = -1e30
LOG2_E = 1.4426950408889634

OFF_U, OFF_PG, OFF_Q, OFF_K, OFF_V, OFF_AG = 0, 512, 1024, 1536, 1664, 1792

LANES = 128
BF16_ROWS = 16
D_KV = N_KV_HEADS * HEAD_DIM
POOL_HALO = 16
SEQ_TILE = 1024
Q_SUB = 64
OUT_ROWS = 512
KEY_SPAN = Q_SUB + WINDOW
VMEM_LIMIT_BYTES = 56 * 1024 * 1024


def _silu(x):
    return x * (1.0 / (1.0 + jnp.exp(-x)))


def _layer_kernel(layer, x_ref, win_ref, poolw_ref, pscale_ref, sinks_ref, wout_ref,
                  gpre_ref, gpost_ref, o_ref,
                  u_buf, q_buf, kz_buf, vt_buf, ag_buf, pg_buf, pooled_buf, mixp_buf, mixa_buf, bias_buf, wi_buf,
                  wo_buf, pw_buf):
    tile = x_ref.shape[0]
    n_sub = tile // Q_SUB
    t = pl.program_id(1)

    @pl.when((pl.program_id(0) == 0) & (t == 0))
    def _first_step():
        r = lax.broadcasted_iota(jnp.int32, (KEY_SPAN, GQA_GROUP * Q_SUB), 0)
        col = lax.broadcasted_iota(jnp.int32, (KEY_SPAN, GQA_GROUP * Q_SUB), 1)
        dist = (col & (Q_SUB - 1)) + WINDOW - r
        in_win = (dist >= 0) & (dist < WINDOW)
        distf = dist.astype(jnp.float32)
        for kv in range(N_KV_HEADS):
            slope = jnp.zeros_like(distf)
            for g in range(GQA_GROUP):
                j = kv * GQA_GROUP + g
                slope = jnp.where(col >= g * Q_SUB, 2.0 ** (-8.0 * (j + 1) / N_HEADS), slope)
            b = -slope * distf * LOG2_E
            for f in range(3):
                bias_buf[f, kv * KEY_SPAN:(kv + 1) * KEY_SPAN, :] = jnp.where(
                    in_win & (r >= f * Q_SUB), b, NEG_INF)

        a0 = lax.broadcasted_iota(jnp.int32, (D_ATTN, D_ATTN), 0)
        a1 = lax.broadcasted_iota(jnp.int32, (D_ATTN, D_ATTN), 1)
        source = ((a1 >> 6) & 1) * (GQA_GROUP * HEAD_DIM) + (a1 >> 7) * HEAD_DIM + (a1 & (HEAD_DIM - 1))
        perm = jnp.where(a0 == source, 1.0, 0.0).astype(jnp.bfloat16)
        for lo, hi in ((OFF_U, OFF_PG), (OFF_PG, OFF_Q), (OFF_Q, OFF_K), (OFF_K, OFF_AG), (OFF_AG, D_IN)):
            w = win_ref[:, lo:hi].astype(jnp.bfloat16)
            if lo in (OFF_Q, OFF_AG):
                w = jnp.dot(w, perm, preferred_element_type=jnp.float32).astype(jnp.bfloat16)
            wi_buf[:, lo:hi] = w
        wo_buf[:D_POOL, :] = wout_ref[:D_POOL, :].astype(jnp.bfloat16)
        for g in range(GQA_GROUP):
            for kv in range(N_KV_HEADS):
                dst = D_POOL + (g * N_KV_HEADS + kv) * HEAD_DIM
                src = D_POOL + (kv * GQA_GROUP + g) * HEAD_DIM
                wo_buf[dst:dst + HEAD_DIM, :] = wout_ref[src:src + HEAD_DIM, :].astype(jnp.bfloat16)
        pw_buf[...] = jnp.zeros(pw_buf.shape, jnp.bfloat16)
        for g in range(len(POOL_WINDOWS)):
            d0 = (g % 2) * POOL_GROUP
            pw_buf[g // 2, d0:d0 + POOL_GROUP, d0:d0 + POOL_GROUP] = poolw_ref[g].astype(jnp.bfloat16)

    @pl.when(t == 0)
    def _zero_halo():
        u_buf[0:POOL_HALO, :] = jnp.zeros((POOL_HALO, D_POOL), jnp.float32)
        kz_buf[:, 0:WINDOW, :] = jnp.zeros((N_KV_HEADS, WINDOW, LANES), jnp.bfloat16)
        vt_buf[:, 0:WINDOW] = jnp.zeros((LANES, WINDOW), jnp.bfloat16)

    x = x_ref[...]
    ms = jnp.mean(x * x, axis=-1, keepdims=True)
    h = (x * lax.rsqrt(ms + EPS) * gpre_ref[layer:layer + 1, :]).astype(jnp.bfloat16)

    def proj(lo, hi):
        return jnp.dot(h, wi_buf[:, lo:hi], preferred_element_type=jnp.float32)

    qkvg = proj(OFF_Q, D_IN)
    q_buf[...] = (qkvg[:, :D_ATTN] * (HEAD_DIM ** -0.5 * LOG2_E)).astype(jnp.bfloat16)
    kf, vf = qkvg[:, D_ATTN:D_ATTN + LANES], qkvg[:, D_ATTN + LANES:D_ATTN + 2 * LANES]
    ag_buf[...] = _silu(qkvg[:, OFF_AG - OFF_Q:])
    lane = lax.broadcasted_iota(jnp.int32, (1, LANES), 1)
    kz_buf[0, WINDOW:WINDOW + tile, :] = jnp.where(lane < HEAD_DIM, kf, 0.0).astype(jnp.bfloat16)
    kz_buf[1, WINDOW:WINDOW + tile, :] = jnp.where(lane >= HEAD_DIM, kf, 0.0).astype(jnp.bfloat16)
    vt_buf[:, WINDOW:WINDOW + tile] = vf.T.astype(jnp.bfloat16)

    pos1 = (lax.broadcasted_iota(jnp.int32, (tile, POOL_GROUP), 0) + (t * tile + 1)
            ).astype(jnp.float32)

    def pool_window_item():
        for g, w in enumerate(POOL_WINDOWS):
            cols = slice(g * POOL_GROUP, (g + 1) * POOL_GROUP)
            ext = u_buf[:, cols]
            acc = ext
            shift = 1
            while shift < w:
                acc = acc + pltpu.roll(acc, shift, axis=0)
                shift *= 2
            cur = ext[POOL_HALO:]
            pooled = acc[POOL_HALO:] / jnp.minimum(pos1, float(w)) - cur
            pooled_buf[:, cols] = pooled.astype(jnp.bfloat16)

    def pool_mix_item():
        for pair in range(len(POOL_WINDOWS) // 2):
            cols = slice(pair * 2 * POOL_GROUP, (pair + 1) * 2 * POOL_GROUP)
            mixed = jnp.dot(pooled_buf[:, cols], pw_buf[pair], preferred_element_type=jnp.float32)
            scale = pscale_ref[layer:layer + 1, cols]
            mixp_buf[:, cols] = (mixed * scale * pg_buf[:, cols]).astype(jnp.bfloat16)

    def pool_u_gate_item():
        ug = proj(OFF_U, OFF_Q)
        u_buf[POOL_HALO:POOL_HALO + tile, :] = ug[:, :D_POOL]
        pg_buf[...] = _silu(ug[:, D_POOL:])

    col = lax.broadcasted_iota(jnp.int32, (1, GQA_GROUP * Q_SUB), 1)
    sink_rows = []
    for kv in range(N_KV_HEADS):
        row = jnp.zeros((1, GQA_GROUP * Q_SUB), jnp.float32)
        for g in range(GQA_GROUP):
            row = jnp.where(col >= g * Q_SUB, sinks_ref[layer, kv * GQA_GROUP + g], row)
        sink_rows.append(row * LOG2_E)
    first = (t == 0)
    nt_dims = (((1,), (1,)), ((), ()))
    pad = jnp.zeros((Q_SUB, GQA_GROUP * Q_SUB), jnp.bfloat16)
    ones_rows = jnp.ones((BF16_ROWS, 2 * LANES), jnp.bfloat16)

    def attn_scores(sb):
        r0 = sb * Q_SUB
        qrows = slice(r0, r0 + Q_SUB)
        krows = slice(r0, r0 + KEY_SPAN)
        variant = jnp.where(first, 2 - sb, 0) if sb < 2 else 0
        q4 = jnp.concatenate([q_buf[qrows, g * LANES:(g + 1) * LANES] for g in range(GQA_GROUP)],
                             axis=0)
        kk = jnp.concatenate([kz_buf[0, krows, :], kz_buf[1, krows, :]], axis=0)
        s = lax.dot_general(kk, q4, nt_dims, preferred_element_type=jnp.float32)
        return s + bias_buf[variant]

    def attn_finish_pair(sb_even, scores_pair):
        c0 = sb_even * Q_SUB
        vt = jnp.concatenate([vt_buf[:, c0:c0 + 2 * LANES], ones_rows], axis=0)
        maxes, probs = [], []
        for j, s in enumerate(scores_pair):
            for kv in range(N_KV_HEADS):
                sh = s[kv * KEY_SPAN:(kv + 1) * KEY_SPAN]
                m = jnp.maximum(jnp.max(sh, axis=0, keepdims=True), sink_rows[kv])
                pb = jnp.exp2(sh - m).astype(jnp.bfloat16)
                probs.append(jnp.concatenate([pb, pad] if j == 0 else [pad, pb], axis=0))
                maxes.append(m)
        o_all = jnp.dot(vt, jnp.concatenate(probs, axis=1), preferred_element_type=jnp.float32)
        for j in range(2):
            r0 = (sb_even + j) * Q_SUB
            qrows = slice(r0, r0 + Q_SUB)
            lrows = slice(r0 % OUT_ROWS, r0 % OUT_ROWS + Q_SUB)
            outs = []
            for kv in range(N_KV_HEADS):
                blk = j * N_KV_HEADS + kv
                cols = slice(blk * 2 * LANES, (blk + 1) * 2 * LANES)
                l = o_all[D_KV:D_KV + 1, cols] + jnp.exp2(sink_rows[kv] - maxes[blk])
                outs.append(o_all[kv * HEAD_DIM:(kv + 1) * HEAD_DIM, cols] * (1.0 / l))
            for pr in range(GQA_GROUP // 2):
                both = jnp.concatenate([o[:, pr * LANES:(pr + 1) * LANES] for o in outs], axis=0)
                both_t = both.T
                for half in range(2):
                    g = 2 * pr + half
                    val = both_t[half * Q_SUB:(half + 1) * Q_SUB] * ag_buf[qrows, g * LANES:(g + 1) * LANES]
                    mixa_buf[r0 // OUT_ROWS, lrows, g * LANES:(g + 1) * LANES] = val.astype(jnp.bfloat16)

    def out_half_item(half):
        rows = slice(half * OUT_ROWS, (half + 1) * OUT_ROWS)
        y = (jnp.dot(mixp_buf[rows, :], wo_buf[:D_POOL, :], preferred_element_type=jnp.float32)
             + jnp.dot(mixa_buf[half], wo_buf[D_POOL:, :], preferred_element_type=jnp.float32))
        ms2 = jnp.mean(y * y, axis=-1, keepdims=True)
        o_ref[rows, :] = x_ref[rows, :] + y * lax.rsqrt(ms2 + EPS) * gpost_ref[layer:layer + 1, :]

    dense_items = [[pool_u_gate_item, pool_window_item, pool_mix_item],
                   [functools.partial(out_half_item, 0)]]
    group = n_sub // len(dense_items)
    for p, items in enumerate(dense_items):
        subs = range(p * group, (p + 1) * group)
        scores = [attn_scores(sb) for sb in subs]
        for item in items:
            item()
        for j in range(0, group, 2):
            attn_finish_pair(subs[j], scores[j:j + 2])
    out_half_item(1)

    u_buf[0:POOL_HALO, :] = u_buf[tile:tile + POOL_HALO, :]
    kz_buf[:, 0:WINDOW, :] = kz_buf[:, tile:tile + WINDOW, :]
    vt_buf[:, 0:WINDOW] = vt_buf[:, tile:tile + WINDOW]


def _layer(layer, x, w_in, pool_w, pool_scale, sinks, w_out, g_pre, g_post):
    batch, seq, d = x.shape
    tile = SEQ_TILE
    assert seq % tile == 0 and tile == 2 * OUT_ROWS and OUT_ROWS % (2 * Q_SUB) == 0 and d == D_MODEL

    def of_layer(*shape):
        return pl.BlockSpec((None,) + shape, lambda b, t: (layer,) + (0,) * len(shape),
                            pipeline_mode=pl.Buffered(1))

    def whole(a):
        return pl.BlockSpec(a.shape, lambda b, t: (0,) * a.ndim)

    return pl.pallas_call(
        functools.partial(_layer_kernel, layer),
        out_shape=jax.ShapeDtypeStruct(x.shape, x.dtype),
        grid=(batch, seq // tile),
        in_specs=[
            pl.BlockSpec((None, tile, d), lambda b, t: (b, t, 0)),
            of_layer(D_MODEL, D_IN),
            of_layer(len(POOL_WINDOWS), POOL_GROUP, POOL_GROUP),
            whole(pool_scale),
            pl.BlockSpec(memory_space=pltpu.SMEM),
            of_layer(D_MODEL, D_MODEL),
            whole(g_pre),
            whole(g_post),
        ],
        out_specs=pl.BlockSpec((None, tile, d), lambda b, t: (b, t, 0)),
        scratch_shapes=[
            pltpu.VMEM((POOL_HALO + tile, D_POOL), jnp.float32),
            pltpu.VMEM((tile, D_ATTN), jnp.bfloat16),
            pltpu.VMEM((N_KV_HEADS, WINDOW + tile, LANES), jnp.bfloat16),
            pltpu.VMEM((LANES, WINDOW + tile), jnp.bfloat16),
            pltpu.VMEM((tile, D_ATTN), jnp.float32),
            pltpu.VMEM((tile, D_POOL), jnp.float32),
            pltpu.VMEM((tile, D_POOL), jnp.bfloat16),
            pltpu.VMEM((tile, D_POOL), jnp.bfloat16),
            pltpu.VMEM((tile // OUT_ROWS, OUT_ROWS, D_ATTN), jnp.bfloat16),
            pltpu.VMEM((3, N_KV_HEADS * KEY_SPAN, GQA_GROUP * Q_SUB), jnp.float32),
            pltpu.VMEM((D_MODEL, D_IN), jnp.bfloat16),
            pltpu.VMEM((D_MODEL, D_MODEL), jnp.bfloat16),
            pltpu.VMEM((len(POOL_WINDOWS) // 2, 2 * POOL_GROUP, 2 * POOL_GROUP), jnp.bfloat16),
        ],
        compiler_params=pltpu.CompilerParams(
            dimension_semantics=("arbitrary", "arbitrary"),
            vmem_limit_bytes=VMEM_LIMIT_BYTES),
        name="hybrid_layer",
    )(x, w_in, pool_w, pool_scale, sinks, w_out, g_pre, g_post)


@jax.jit
def kernel(x, w_in, pool_w, pool_scale, attn_sinks, w_out, norm_pre, norm_post):
    for layer in range(w_in.shape[0]):
        x = _layer(layer, x, w_in, pool_w, pool_scale, attn_sinks, w_out, norm_pre, norm_post)
    return x
```

```python
import functools

import jax
import jax.numpy as jnp
from jax import lax
from jax.experimental import pallas as pl
from jax.experimental.pallas import tpu as pltpu

D_MODEL = 1024
D_POOL = 512
POOL_WINDOWS = (2, 4, 8, 16)
POOL_GROUP = 128
HEAD_DIM = 64
D_ATTN = 512
N_HEADS = 8
N_KV_HEADS = 2
GQA_GROUP = N_HEADS // N_KV_HEADS
WINDOW = 128
D_IN = 2304
EPS = 1e-6
NEG_INF = -1e30
LOG2_E = 1.4426950408889634

OFF_U, OFF_PG, OFF_Q, OFF_K, OFF_V, OFF_AG = 0, 512, 1024, 1536, 1664, 1792

LANES = 128
BF16_ROWS = 16
POOL_HALO = 16
SEQ_TILE = 1024
Q_SUB = 64
OUT_ROWS = 512
KEY_SPAN = Q_SUB + WINDOW
VMEM_LIMIT_BYTES = 56 * 1024 * 1024


def _silu(x):
    return x * (1.0 / (1.0 + jnp.exp(-x)))


def _layer_kernel(layer, x_ref, win_ref, poolw_ref, pscale_ref, sinks_ref, wout_ref,
                  gpre_ref, gpost_ref, o_ref,
                  u_buf, q_buf, kz_buf, vt_buf, ag_buf, pg_buf, pooled_buf, mixp_buf, mixa_buf, bias_buf, wi_buf,
                  wo_buf, pw_buf):
    tile = x_ref.shape[0]
    n_sub = tile // Q_SUB
    t = pl.program_id(1)

    @pl.when((pl.program_id(0) == 0) & (t == 0))
    def _first_step():
        r = lax.broadcasted_iota(jnp.int32, (KEY_SPAN, GQA_GROUP * Q_SUB), 0)
        col = lax.broadcasted_iota(jnp.int32, (KEY_SPAN, GQA_GROUP * Q_SUB), 1)
        dist = (col & (Q_SUB - 1)) + WINDOW - r
        in_win = (dist >= 0) & (dist < WINDOW)
        distf = dist.astype(jnp.float32)
        for kv in range(N_KV_HEADS):
            slope = jnp.zeros_like(distf)
            for g in range(GQA_GROUP):
                j = kv * GQA_GROUP + g
                slope = jnp.where(col >= g * Q_SUB, 2.0 ** (-8.0 * (j + 1) / N_HEADS), slope)
            b = -slope * distf * LOG2_E
            for f in range(3):
                bias_buf[f, kv * KEY_SPAN:(kv + 1) * KEY_SPAN, :] = jnp.where(
                    in_win & (r >= f * Q_SUB), b, NEG_INF)

        a0 = lax.broadcasted_iota(jnp.int32, (D_ATTN, D_ATTN), 0)
        a1 = lax.broadcasted_iota(jnp.int32, (D_ATTN, D_ATTN), 1)
        source = ((a1 >> 6) & 1) * (GQA_GROUP * HEAD_DIM) + (a1 >> 7) * HEAD_DIM + (a1 & (HEAD_DIM - 1))
        perm = jnp.where(a0 == source, 1.0, 0.0).astype(jnp.bfloat16)
        for lo, hi in ((OFF_U, OFF_PG), (OFF_PG, OFF_Q), (OFF_Q, OFF_K), (OFF_K, OFF_AG), (OFF_AG, D_IN)):
            w = win_ref[:, lo:hi].astype(jnp.bfloat16)
            if lo in (OFF_Q, OFF_AG):
                w = jnp.dot(w, perm, preferred_element_type=jnp.float32).astype(jnp.bfloat16)
            wi_buf[:, lo:hi] = w
        wo_buf[:D_POOL, :] = wout_ref[:D_POOL, :].astype(jnp.bfloat16)
        for g in range(GQA_GROUP):
            for kv in range(N_KV_HEADS):
                dst = D_POOL + (g * N_KV_HEADS + kv) * HEAD_DIM
                src = D_POOL + (kv * GQA_GROUP + g) * HEAD_DIM
                wo_buf[dst:dst + HEAD_DIM, :] = wout_ref[src:src + HEAD_DIM, :].astype(jnp.bfloat16)
        pw_buf[...] = jnp.zeros(pw_buf.shape, jnp.bfloat16)
        for g in range(len(POOL_WINDOWS)):
            d0 = (g % 2) * POOL_GROUP
            pw_buf[g // 2, d0:d0 + POOL_GROUP, d0:d0 + POOL_GROUP] = poolw_ref[g].astype(jnp.bfloat16)

    @pl.when(t == 0)
    def _zero_halo():
        u_buf[0:POOL_HALO, :] = jnp.zeros((POOL_HALO, D_POOL), jnp.float32)
        kz_buf[:, 0:WINDOW, :] = jnp.zeros((N_KV_HEADS, WINDOW, LANES), jnp.bfloat16)
        vt_buf[:, 0:WINDOW] = jnp.zeros((LANES, WINDOW), jnp.bfloat16)

    x = x_ref[...]
    ms = jnp.mean(x * x, axis=-1, keepdims=True)
    h = (x * lax.rsqrt(ms + EPS) * gpre_ref[layer:layer + 1, :]).astype(jnp.bfloat16)

    def proj(lo, hi):
        return jnp.dot(h, wi_buf[:, lo:hi], preferred_element_type=jnp.float32)

    qkvg = proj(OFF_Q, D_IN)
    q_buf[...] = (qkvg[:, :D_ATTN] * (HEAD_DIM ** -0.5 * LOG2_E)).astype(jnp.bfloat16)
    kf, vf = qkvg[:, D_ATTN:D_ATTN + LANES], qkvg[:, D_ATTN + LANES:D_ATTN + 2 * LANES]
    ag_buf[...] = _silu(qkvg[:, OFF_AG - OFF_Q:])
    lane = lax.broadcasted_iota(jnp.int32, (1, LANES), 1)
    kz_buf[0, WINDOW:WINDOW + tile, :] = jnp.where(lane < HEAD_DIM, kf, 0.0).astype(jnp.bfloat16)
    kz_buf[1, WINDOW:WINDOW + tile, :] = jnp.where(lane >= HEAD_DIM, kf, 0.0).astype(jnp.bfloat16)
    vt_buf[:, WINDOW:WINDOW + tile] = vf.T.astype(jnp.bfloat16)

    pos1 = (lax.broadcasted_iota(jnp.int32, (tile, POOL_GROUP), 0) + (t * tile + 1)
            ).astype(jnp.float32)

    def pool_window_item():
        for g, w in enumerate(POOL_WINDOWS):
            cols = slice(g * POOL_GROUP, (g + 1) * POOL_GROUP)
            ext = u_buf[:, cols]
            acc = ext
            shift = 1
            while shift < w:
                acc = acc + pltpu.roll(acc, shift, axis=0)
                shift *= 2
            cur = ext[POOL_HALO:]
            pooled = acc[POOL_HALO:] / jnp.minimum(pos1, float(w)) - cur
            pooled_buf[:, cols] = pooled.astype(jnp.bfloat16)

    def pool_mix_item():
        for pair in range(len(POOL_WINDOWS) // 2):
            cols = slice(pair * 2 * POOL_GROUP, (pair + 1) * 2 * POOL_GROUP)
            mixed = jnp.dot(pooled_buf[:, cols], pw_buf[pair], preferred_element_type=jnp.float32)
            scale = pscale_ref[layer:layer + 1, cols]
            mixp_buf[:, cols] = (mixed * scale * pg_buf[:, cols]).astype(jnp.bfloat16)

    def pool_u_gate_item():
        ug = proj(OFF_U, OFF_Q)
        u_buf[POOL_HALO:POOL_HALO + tile, :] = ug[:, :D_POOL]
        pg_buf[...] = _silu(ug[:, D_POOL:])

    col = lax.broadcasted_iota(jnp.int32, (1, GQA_GROUP * Q_SUB), 1)
    sink_rows = []
    for kv in range(N_KV_HEADS):
        row = jnp.zeros((1, GQA_GROUP * Q_SUB), jnp.float32)
        for g in range(GQA_GROUP):
            row = jnp.where(col >= g * Q_SUB, sinks_ref[layer, kv * GQA_GROUP + g], row)
        sink_rows.append(row * LOG2_E)
    first = (t == 0)
    nt_dims = (((1,), (1,)), ((), ()))
    pad = jnp.zeros((Q_SUB, GQA_GROUP * Q_SUB), jnp.bfloat16)
    ones_rows = jnp.ones((BF16_ROWS, 2 * LANES), jnp.bfloat16)

    def attn_scores(sb):
        r0 = sb * Q_SUB
        qrows = slice(r0, r0 + Q_SUB)
        krows = slice(r0, r0 + KEY_SPAN)
        variant = jnp.where(first, 2 - sb, 0) if sb < 2 else 0
        q4 = jnp.concatenate([q_buf[qrows, g * LANES:(g + 1) * LANES] for g in range(GQA_GROUP)],
                             axis=0)
        kk = jnp.concatenate([kz_buf[0, krows, :], kz_buf[1, krows, :]], axis=0)
        s = lax.dot_general(kk, q4, nt_dims, preferred_element_type=jnp.float32)
        return s + bias_buf[variant]

    def attn_finish_pair(sb_even, scores_pair):
        c0 = sb_even * Q_SUB
        vts = [jnp.concatenate([vt_buf[kv * HEAD_DIM:(kv + 1) * HEAD_DIM, c0:c0 + 2 * LANES], ones_rows],
                               axis=0) for kv in range(N_KV_HEADS)]
        maxes, probs = [], []
        for j, s in enumerate(scores_pair):
            for kv in range(N_KV_HEADS):
                sh = s[kv * KEY_SPAN:(kv + 1) * KEY_SPAN]
                m = jnp.maximum(jnp.max(sh, axis=0, keepdims=True), sink_rows[kv])
                pb = jnp.exp2(sh - m).astype(jnp.bfloat16)
                probs.append(jnp.concatenate([pb, pad] if j == 0 else [pad, pb], axis=0))
                maxes.append(m)
        o_kv = [jnp.dot(vts[kv], jnp.concatenate([probs[kv], probs[N_KV_HEADS + kv]], axis=1),
                        preferred_element_type=jnp.float32) for kv in range(N_KV_HEADS)]
        for j in range(2):
            r0 = (sb_even + j) * Q_SUB
            qrows = slice(r0, r0 + Q_SUB)
            lrows = slice(r0 % OUT_ROWS, r0 % OUT_ROWS + Q_SUB)
            outs = []
            for kv in range(N_KV_HEADS):
                o = o_kv[kv][:, j * 2 * LANES:(j + 1) * 2 * LANES]
                l = o[HEAD_DIM:HEAD_DIM + 1] + jnp.exp2(sink_rows[kv] - maxes[j * N_KV_HEADS + kv])
                outs.append(o[:HEAD_DIM] * (1.0 / l))
            for pr in range(GQA_GROUP // 2):
                both = jnp.concatenate([o[:, pr * LANES:(pr + 1) * LANES] for o in outs], axis=0)
                both_t = both.T
                for half in range(2):
                    g = 2 * pr + half
                    val = both_t[half * Q_SUB:(half + 1) * Q_SUB] * ag_buf[qrows, g * LANES:(g + 1) * LANES]
                    mixa_buf[r0 // OUT_ROWS, lrows, g * LANES:(g + 1) * LANES] = val.astype(jnp.bfloat16)

    def out_half_item(half):
        rows = slice(half * OUT_ROWS, (half + 1) * OUT_ROWS)
        y = (jnp.dot(mixp_buf[rows, :], wo_buf[:D_POOL, :], preferred_element_type=jnp.float32)
             + jnp.dot(mixa_buf[half], wo_buf[D_POOL:, :], preferred_element_type=jnp.float32))
        ms2 = jnp.mean(y * y, axis=-1, keepdims=True)
        o_ref[rows, :] = x_ref[rows, :] + y * lax.rsqrt(ms2 + EPS) * gpost_ref[layer:layer + 1, :]

    dense_items = [[pool_u_gate_item, pool_window_item, pool_mix_item],
                   [functools.partial(out_half_item, 0)]]
    group = n_sub // len(dense_items)
    for p, items in enumerate(dense_items):
        subs = range(p * group, (p + 1) * group)
        scores = [attn_scores(sb) for sb in subs]
        for item in items:
            item()
        for j in range(0, group, 2):
            attn_finish_pair(subs[j], scores[j:j + 2])
    out_half_item(1)

    u_buf[0:POOL_HALO, :] = u_buf[tile:tile + POOL_HALO, :]
    kz_buf[:, 0:WINDOW, :] = kz_buf[:, tile:tile + WINDOW, :]
    vt_buf[:, 0:WINDOW] = vt_buf[:, tile:tile + WINDOW]


def _layer(layer, x, w_in, pool_w, pool_scale, sinks, w_out, g_pre, g_post):
    batch, seq, d = x.shape
    tile = SEQ_TILE
    assert seq % tile == 0 and tile == 2 * OUT_ROWS and OUT_ROWS % (2 * Q_SUB) == 0 and d == D_MODEL

    def of_layer(*shape):
        return pl.BlockSpec((None,) + shape, lambda b, t: (layer,) + (0,) * len(shape),
                            pipeline_mode=pl.Buffered(1))

    def whole(a):
        return pl.BlockSpec(a.shape, lambda b, t: (0,) * a.ndim)

    return pl.pallas_call(
        functools.partial(_layer_kernel, layer),
        out_shape=jax.ShapeDtypeStruct(x.shape, x.dtype),
        grid=(batch, seq // tile),
        in_specs=[
            pl.BlockSpec((None, tile, d), lambda b, t: (b, t, 0)),
            of_layer(D_MODEL, D_IN),
            of_layer(len(POOL_WINDOWS), POOL_GROUP, POOL_GROUP),
            whole(pool_scale),
            pl.BlockSpec(memory_space=pltpu.SMEM),
            of_layer(D_MODEL, D_MODEL),
            whole(g_pre),
            whole(g_post),
        ],
        out_specs=pl.BlockSpec((None, tile, d), lambda b, t: (b, t, 0)),
        scratch_shapes=[
            pltpu.VMEM((POOL_HALO + tile, D_POOL), jnp.float32),
            pltpu.VMEM((tile, D_ATTN), jnp.bfloat16),
            pltpu.VMEM((N_KV_HEADS, WINDOW + tile, LANES), jnp.bfloat16),
            pltpu.VMEM((LANES, WINDOW + tile), jnp.bfloat16),
            pltpu.VMEM((tile, D_ATTN), jnp.float32),
            pltpu.VMEM((tile, D_POOL), jnp.float32),
            pltpu.VMEM((tile, D_POOL), jnp.bfloat16),
            pltpu.VMEM((tile, D_POOL), jnp.bfloat16),
            pltpu.VMEM((tile // OUT_ROWS, OUT_ROWS, D_ATTN), jnp.bfloat16),
            pltpu.VMEM((3, N_KV_HEADS * KEY_SPAN, GQA_GROUP * Q_SUB), jnp.float32),
            pltpu.VMEM((D_MODEL, D_IN), jnp.bfloat16),
            pltpu.VMEM((D_MODEL, D_MODEL), jnp.bfloat16),
            pltpu.VMEM((len(POOL_WINDOWS) // 2, 2 * POOL_GROUP, 2 * POOL_GROUP), jnp.bfloat16),
        ],
        compiler_params=pltpu.CompilerParams(
            dimension_semantics=("arbitrary", "arbitrary"),
            vmem_limit_bytes=VMEM_LIMIT_BYTES),
        name="hybrid_layer",
    )(x, w_in, pool_w, pool_scale, sinks, w_out, g_pre, g_post)


@jax.jit
def kernel(x, w_in, pool_w, pool_scale, attn_sinks, w_out, norm_pre, norm_post):
    for layer in range(w_in.shape[0]):
        x = _layer(layer, x, w_in, pool_w, pool_scale, attn_sinks, w_out, norm_pre, norm_post)
    return x
```

```python
import functools

import jax
import jax.numpy as jnp
from jax import lax
from jax.experimental import pallas as pl
from jax.experimental.pallas import tpu as pltpu

D_MODEL = 1024
D_POOL = 512
POOL_WINDOWS = (2, 4, 8, 16)
POOL_GROUP = 128
HEAD_DIM = 64
D_ATTN = 512
N_HEADS = 8
N_KV_HEADS = 2
GQA_GROUP = N_HEADS // N_KV_HEADS
WINDOW = 128
D_IN = 2304
EPS = 1e-6
NEG_INF = -1e30
LOG2_E = 1.4426950408889634

OFF_U, OFF_PG, OFF_Q, OFF_K, OFF_V, OFF_AG = 0, 512, 1024, 1536, 1664, 1792

LANES = 128
BF16_ROWS = 16
POOL_HALO = 16
SEQ_TILE = 1024
Q_SUB = 64
OUT_ROWS = 512
KEY_SPAN = Q_SUB + WINDOW
VMEM_LIMIT_BYTES = 56 * 1024 * 1024


def _silu(x):
    return x * (1.0 / (1.0 + jnp.exp(-x)))


def _layer_kernel(layer, x_ref, win_ref, poolw_ref, pscale_ref, sinks_ref, wout_ref,
                  gpre_ref, gpost_ref, o_ref,
                  u_buf, q_buf, kz_buf, vt_buf, ag_buf, pg_buf, pooled_buf, mixp_buf, mixa_buf, bias_buf, wi_buf,
                  wo_buf, pw_buf):
    tile = x_ref.shape[0]
    n_sub = tile // Q_SUB
    t = pl.program_id(1)

    @pl.when((pl.program_id(0) == 0) & (t == 0))
    def _first_step():
        r = lax.broadcasted_iota(jnp.int32, (KEY_SPAN, GQA_GROUP * Q_SUB), 0)
        col = lax.broadcasted_iota(jnp.int32, (KEY_SPAN, GQA_GROUP * Q_SUB), 1)
        dist = (col & (Q_SUB - 1)) + WINDOW - r
        in_win = (dist >= 0) & (dist < WINDOW)
        distf = dist.astype(jnp.float32)
        for kv in range(N_KV_HEADS):
            slope = jnp.zeros_like(distf)
            for g in range(GQA_GROUP):
                j = kv * GQA_GROUP + g
                slope = jnp.where(col >= g * Q_SUB, 2.0 ** (-8.0 * (j + 1) / N_HEADS), slope)
            b = -slope * distf * LOG2_E
            for f in range(3):
                bias_buf[f, kv * KEY_SPAN:(kv + 1) * KEY_SPAN, :] = jnp.where(
                    in_win & (r >= f * Q_SUB), b, NEG_INF)

        a0 = lax.broadcasted_iota(jnp.int32, (D_ATTN, D_ATTN), 0)
        a1 = lax.broadcasted_iota(jnp.int32, (D_ATTN, D_ATTN), 1)
        source = ((a1 >> 6) & 1) * (GQA_GROUP * HEAD_DIM) + (a1 >> 7) * HEAD_DIM + (a1 & (HEAD_DIM - 1))
        perm = jnp.where(a0 == source, 1.0, 0.0).astype(jnp.bfloat16)
        for lo, hi in ((OFF_U, OFF_PG), (OFF_PG, OFF_Q), (OFF_Q, OFF_K), (OFF_K, OFF_AG), (OFF_AG, D_IN)):
            w = win_ref[:, lo:hi].astype(jnp.bfloat16)
            if lo in (OFF_Q, OFF_AG):
                w = jnp.dot(w, perm, preferred_element_type=jnp.float32).astype(jnp.bfloat16)
            wi_buf[:, lo:hi] = w
        wo_buf[:D_POOL, :] = wout_ref[:D_POOL, :].astype(jnp.bfloat16)
        for g in range(GQA_GROUP):
            for kv in range(N_KV_HEADS):
                dst = D_POOL + (g * N_KV_HEADS + kv) * HEAD_DIM
                src = D_POOL + (kv * GQA_GROUP + g) * HEAD_DIM
                wo_buf[dst:dst + HEAD_DIM, :] = wout_ref[src:src + HEAD_DIM, :].astype(jnp.bfloat16)
        pw_buf[...] = jnp.zeros(pw_buf.shape, jnp.bfloat16)
        for g in range(len(POOL_WINDOWS)):
            d0 = (g % 2) * POOL_GROUP
            pw_buf[g // 2, d0:d0 + POOL_GROUP, d0:d0 + POOL_GROUP] = poolw_ref[g].astype(jnp.bfloat16)

    @pl.when(t == 0)
    def _zero_halo():
        u_buf[0:POOL_HALO, :] = jnp.zeros((POOL_HALO, D_POOL), jnp.float32)
        kz_buf[:, 0:WINDOW, :] = jnp.zeros((N_KV_HEADS, WINDOW, LANES), jnp.bfloat16)
        vt_buf[:, 0:WINDOW] = jnp.zeros((LANES, WINDOW), jnp.bfloat16)

    x = x_ref[...]
    ms = jnp.mean(x * x, axis=-1, keepdims=True)
    h = (x * lax.rsqrt(ms + EPS) * gpre_ref[layer:layer + 1, :]).astype(jnp.bfloat16)

    def proj(lo, hi):
        return jnp.dot(h, wi_buf[:, lo:hi], preferred_element_type=jnp.float32)

    qkvg = proj(OFF_Q, D_IN)
    q_buf[...] = (qkvg[:, :D_ATTN] * (HEAD_DIM ** -0.5 * LOG2_E)).astype(jnp.bfloat16)
    kf, vf = qkvg[:, D_ATTN:D_ATTN + LANES], qkvg[:, D_ATTN + LANES:D_ATTN + 2 * LANES]
    ag_buf[...] = _silu(qkvg[:, OFF_AG - OFF_Q:])
    lane = lax.broadcasted_iota(jnp.int32, (1, LANES), 1)
    kz_buf[0, WINDOW:WINDOW + tile, :] = jnp.where(lane < HEAD_DIM, kf, 0.0).astype(jnp.bfloat16)
    kz_buf[1, WINDOW:WINDOW + tile, :] = jnp.where(lane >= HEAD_DIM, kf, 0.0).astype(jnp.bfloat16)
    vt_buf[:, WINDOW:WINDOW + tile] = vf.T.astype(jnp.bfloat16)

    pos1 = (lax.broadcasted_iota(jnp.int32, (tile, POOL_GROUP), 0) + (t * tile + 1)
            ).astype(jnp.float32)

    def pool_window_item():
        for g, w in enumerate(POOL_WINDOWS):
            cols = slice(g * POOL_GROUP, (g + 1) * POOL_GROUP)
            ext = u_buf[:, cols]
            acc = ext
            shift = 1
            while shift < w:
                acc = acc + pltpu.roll(acc, shift, axis=0)
                shift *= 2
            cur = ext[POOL_HALO:]
            pooled = acc[POOL_HALO:] / jnp.minimum(pos1, float(w)) - cur
            pooled_buf[:, cols] = pooled.astype(jnp.bfloat16)

    def pool_mix_item():
        for pair in range(len(POOL_WINDOWS) // 2):
            cols = slice(pair * 2 * POOL_GROUP, (pair + 1) * 2 * POOL_GROUP)
            mixed = jnp.dot(pooled_buf[:, cols], pw_buf[pair], preferred_element_type=jnp.float32)
            scale = pscale_ref[layer:layer + 1, cols]
            mixp_buf[:, cols] = (mixed * scale * pg_buf[:, cols]).astype(jnp.bfloat16)

    def pool_u_gate_item():
        ug = proj(OFF_U, OFF_Q)
        u_buf[POOL_HALO:POOL_HALO + tile, :] = ug[:, :D_POOL]
        pg_buf[...] = _silu(ug[:, D_POOL:])

    col = lax.broadcasted_iota(jnp.int32, (1, GQA_GROUP * Q_SUB), 1)
    sink_rows = []
    for kv in range(N_KV_HEADS):
        row = jnp.zeros((1, GQA_GROUP * Q_SUB), jnp.float32)
        for g in range(GQA_GROUP):
            row = jnp.where(col >= g * Q_SUB, sinks_ref[layer, kv * GQA_GROUP + g], row)
        sink_rows.append(row * LOG2_E)
    first = (t == 0)
    nt_dims = (((1,), (1,)), ((), ()))
    pad = jnp.zeros((Q_SUB, GQA_GROUP * Q_SUB), jnp.bfloat16)
    ones_rows = jnp.ones((2 * BF16_ROWS, 2 * LANES), jnp.bfloat16)

    def attn_scores(sb):
        r0 = sb * Q_SUB
        qrows = slice(r0, r0 + Q_SUB)
        krows = slice(r0, r0 + KEY_SPAN)
        variant = jnp.where(first, 2 - sb, 0) if sb < 2 else 0
        q4 = jnp.concatenate([q_buf[qrows, g * LANES:(g + 1) * LANES] for g in range(GQA_GROUP)],
                             axis=0)
        kk = jnp.concatenate([kz_buf[0, krows, :], kz_buf[1, krows, :]], axis=0)
        s = lax.dot_general(kk, q4, nt_dims, preferred_element_type=jnp.float32)
        return s + bias_buf[variant]

    def attn_finish_pair(sb_even, scores_pair):
        c0 = sb_even * Q_SUB
        vts = [jnp.concatenate([vt_buf[kv * HEAD_DIM:(kv + 1) * HEAD_DIM, c0:c0 + 2 * LANES], ones_rows],
                               axis=0) for kv in range(N_KV_HEADS)]
        maxes, probs = [], []
        for j, s in enumerate(scores_pair):
            for kv in range(N_KV_HEADS):
                sh = s[kv * KEY_SPAN:(kv + 1) * KEY_SPAN]
                m = jnp.maximum(jnp.max(sh, axis=0, keepdims=True), sink_rows[kv])
                pb = jnp.exp2(sh - m).astype(jnp.bfloat16)
                probs.append(jnp.concatenate([pb, pad] if j == 0 else [pad, pb], axis=0))
                maxes.append(m)
        o_kv = [jnp.dot(vts[kv], jnp.concatenate([probs[kv], probs[N_KV_HEADS + kv]], axis=1),
                        preferred_element_type=jnp.float32) for kv in range(N_KV_HEADS)]
        for j in range(2):
            r0 = (sb_even + j) * Q_SUB
            qrows = slice(r0, r0 + Q_SUB)
            lrows = slice(r0 % OUT_ROWS, r0 % OUT_ROWS + Q_SUB)
            outs = []
            for kv in range(N_KV_HEADS):
                o = o_kv[kv][:, j * 2 * LANES:(j + 1) * 2 * LANES]
                l = o[HEAD_DIM:HEAD_DIM + 1] + jnp.exp2(sink_rows[kv] - maxes[j * N_KV_HEADS + kv])
                outs.append(o[:HEAD_DIM] * (1.0 / l))
            for pr in range(GQA_GROUP // 2):
                both = jnp.concatenate([o[:, pr * LANES:(pr + 1) * LANES] for o in outs], axis=0)
                both_t = both.T
                for half in range(2):
                    g = 2 * pr + half
                    val = both_t[half * Q_SUB:(half + 1) * Q_SUB] * ag_buf[qrows, g * LANES:(g + 1) * LANES]
                    mixa_buf[r0 // OUT_ROWS, lrows, g * LANES:(g + 1) * LANES] = val.astype(jnp.bfloat16)

    def out_half_item(half):
        rows = slice(half * OUT_ROWS, (half + 1) * OUT_ROWS)
        y = (jnp.dot(mixp_buf[rows, :], wo_buf[:D_POOL, :], preferred_element_type=jnp.float32)
             + jnp.dot(mixa_buf[half], wo_buf[D_POOL:, :], preferred_element_type=jnp.float32))
        ms2 = jnp.mean(y * y, axis=-1, keepdims=True)
        o_ref[rows, :] = x_ref[rows, :] + y * lax.rsqrt(ms2 + EPS) * gpost_ref[layer:layer + 1, :]

    dense_items = [[pool_u_gate_item, pool_window_item, pool_mix_item],
                   [functools.partial(out_half_item, 0)]]
    group = n_sub // len(dense_items)
    for p, items in enumerate(dense_items):
        subs = range(p * group, (p + 1) * group)
        scores = [attn_scores(sb) for sb in subs]
        for item in items:
            item()
        for j in range(0, group, 2):
            attn_finish_pair(subs[j], scores[j:j + 2])
    out_half_item(1)

    u_buf[0:POOL_HALO, :] = u_buf[tile:tile + POOL_HALO, :]
    kz_buf[:, 0:WINDOW, :] = kz_buf[:, tile:tile + WINDOW, :]
    vt_buf[:, 0:WINDOW] = vt_buf[:, tile:tile + WINDOW]


def _layer(layer, x, w_in, pool_w, pool_scale, sinks, w_out, g_pre, g_post):
    batch, seq, d = x.shape
    tile = SEQ_TILE
    assert seq % tile == 0 and tile == 2 * OUT_ROWS and OUT_ROWS % (2 * Q_SUB) == 0 and d == D_MODEL

    def of_layer(*shape):
        return pl.BlockSpec((None,) + shape, lambda b, t: (layer,) + (0,) * len(shape),
                            pipeline_mode=pl.Buffered(1))

    def whole(a):
        return pl.BlockSpec(a.shape, lambda b, t: (0,) * a.ndim)

    return pl.pallas_call(
        functools.partial(_layer_kernel, layer),
        out_shape=jax.ShapeDtypeStruct(x.shape, x.dtype),
        grid=(batch, seq // tile),
        in_specs=[
            pl.BlockSpec((None, tile, d), lambda b, t: (b, t, 0)),
            of_layer(D_MODEL, D_IN),
            of_layer(len(POOL_WINDOWS), POOL_GROUP, POOL_GROUP),
            whole(pool_scale),
            pl.BlockSpec(memory_space=pltpu.SMEM),
            of_layer(D_MODEL, D_MODEL),
            whole(g_pre),
            whole(g_post),
        ],
        out_specs=pl.BlockSpec((None, tile, d), lambda b, t: (b, t, 0)),
        scratch_shapes=[
            pltpu.VMEM((POOL_HALO + tile, D_POOL), jnp.float32),
            pltpu.VMEM((tile, D_ATTN), jnp.bfloat16),
            pltpu.VMEM((N_KV_HEADS, WINDOW + tile, LANES), jnp.bfloat16),
            pltpu.VMEM((LANES, WINDOW + tile), jnp.bfloat16),
            pltpu.VMEM((tile, D_ATTN), jnp.float32),
            pltpu.VMEM((tile, D_POOL), jnp.float32),
            pltpu.VMEM((tile, D_POOL), jnp.bfloat16),
            pltpu.VMEM((tile, D_POOL), jnp.bfloat16),
            pltpu.VMEM((tile // OUT_ROWS, OUT_ROWS, D_ATTN), jnp.bfloat16),
            pltpu.VMEM((3, N_KV_HEADS * KEY_SPAN, GQA_GROUP * Q_SUB), jnp.float32),
            pltpu.VMEM((D_MODEL, D_IN), jnp.bfloat16),
            pltpu.VMEM((D_MODEL, D_MODEL), jnp.bfloat16),
            pltpu.VMEM((len(POOL_WINDOWS) // 2, 2 * POOL_GROUP, 2 * POOL_GROUP), jnp.bfloat16),
        ],
        compiler_params=pltpu.CompilerParams(
            dimension_semantics=("arbitrary", "arbitrary"),
            vmem_limit_bytes=VMEM_LIMIT_BYTES),
        name="hybrid_layer",
    )(x, w_in, pool_w, pool_scale, sinks, w_out, g_pre, g_post)


@jax.jit
def kernel(x, w_in, pool_w, pool_scale, attn_sinks, w_out, norm_pre, norm_post):
    for layer in range(w_in.shape[0]):
        x = _layer(layer, x, w_in, pool_w, pool_scale, attn_sinks, w_out, norm_pre, norm_post)
    return x
```

```python
import functools

import jax
import jax.numpy as jnp
from jax import lax
from jax.experimental import pallas as pl
from jax.experimental.pallas import tpu as pltpu

D_MODEL = 1024
D_POOL = 512
POOL_WINDOWS = (2, 4, 8, 16)
POOL_GROUP = 128
HEAD_DIM = 64
D_ATTN = 512
N_HEADS = 8
N_KV_HEADS = 2
GQA_GROUP = N_HEADS // N_KV_HEADS
WINDOW = 128
D_IN = 2304
EPS = 1e-6
NEG_INF = -1e30
LOG2_E = 1.4426950408889634

OFF_U, OFF_PG, OFF_Q, OFF_K, OFF_V, OFF_AG = 0, 512, 1024, 1536, 1664, 1792

LANES = 128
BF16_ROWS = 16
PV_ROWS = HEAD_DIM + BF16_ROWS
POOL_HALO = 16
SEQ_TILE = 1024
Q_SUB = 64
OUT_ROWS = 512
KEY_SPAN = Q_SUB + WINDOW
VMEM_LIMIT_BYTES = 56 * 1024 * 1024


def _silu(x):
    return x * (1.0 / (1.0 + jnp.exp(-x)))


def _layer_kernel(layer, x_ref, win_ref, poolw_ref, pscale_ref, sinks_ref, wout_ref,
                  gpre_ref, gpost_ref, o_ref,
                  u_buf, q_buf, kz_buf, vt_buf, ag_buf, pg_buf, pooled_buf, mixp_buf, mixa_buf, bias_buf, wi_buf,
                  wo_buf, pw_buf, win_stage, wout_stage, w_sem):
    tile = x_ref.shape[0]
    n_sub = tile // Q_SUB
    t = pl.program_id(1)

    @pl.when((pl.program_id(0) == 0) & (t == 0))
    def _first_step():
        win_chunks = ((OFF_Q, OFF_K), (OFF_K, OFF_AG), (OFF_AG, D_IN), (OFF_U, OFF_PG), (OFF_PG, OFF_Q))
        win_copies = [pltpu.make_async_copy(win_ref.at[layer, :, lo:hi], win_stage.at[:, lo:hi], w_sem.at[i])
                      for i, (lo, hi) in enumerate(win_chunks)]
        wout_copies = [pltpu.make_async_copy(wout_ref.at[layer, h * D_POOL:(h + 1) * D_POOL, :],
                                             wout_stage.at[h * D_POOL:(h + 1) * D_POOL, :],
                                             w_sem.at[len(win_chunks) + h]) for h in range(2)]
        for c in win_copies + wout_copies:
            c.start()

        r = lax.broadcasted_iota(jnp.int32, (KEY_SPAN, GQA_GROUP * Q_SUB), 0)
        col = lax.broadcasted_iota(jnp.int32, (KEY_SPAN, GQA_GROUP * Q_SUB), 1)
        dist = (col & (Q_SUB - 1)) + WINDOW - r
        in_win = (dist >= 0) & (dist < WINDOW)
        distf = dist.astype(jnp.float32)
        for kv in range(N_KV_HEADS):
            slope = jnp.zeros_like(distf)
            for g in range(GQA_GROUP):
                j = kv * GQA_GROUP + g
                slope = jnp.where(col >= g * Q_SUB, 2.0 ** (-8.0 * (j + 1) / N_HEADS), slope)
            b = -slope * distf * LOG2_E
            for f in range(3):
                bias_buf[f, kv * KEY_SPAN:(kv + 1) * KEY_SPAN, :] = jnp.where(
                    in_win & (r >= f * Q_SUB), b, NEG_INF)

        a0 = lax.broadcasted_iota(jnp.int32, (D_ATTN, D_ATTN), 0)
        a1 = lax.broadcasted_iota(jnp.int32, (D_ATTN, D_ATTN), 1)
        source = ((a1 >> 6) & 1) * (GQA_GROUP * HEAD_DIM) + (a1 >> 7) * HEAD_DIM + (a1 & (HEAD_DIM - 1))
        perm = jnp.where(a0 == source, 1.0, 0.0).astype(jnp.bfloat16)
        for c, (lo, hi) in zip(win_copies, win_chunks):
            c.wait()
            w = win_stage[:, lo:hi].astype(jnp.bfloat16)
            if lo in (OFF_Q, OFF_AG):
                w = jnp.dot(w, perm, preferred_element_type=jnp.float32).astype(jnp.bfloat16)
            wi_buf[:, lo:hi] = w
        wout_copies[0].wait()
        wo_buf[:D_POOL, :] = wout_stage[:D_POOL, :].astype(jnp.bfloat16)
        wout_copies[1].wait()
        for g in range(GQA_GROUP):
            for kv in range(N_KV_HEADS):
                dst = D_POOL + (g * N_KV_HEADS + kv) * HEAD_DIM
                src = D_POOL + (kv * GQA_GROUP + g) * HEAD_DIM
                wo_buf[dst:dst + HEAD_DIM, :] = wout_stage[src:src + HEAD_DIM, :].astype(jnp.bfloat16)
        pw_buf[...] = jnp.zeros(pw_buf.shape, jnp.bfloat16)
        for g in range(len(POOL_WINDOWS)):
            d0 = (g % 2) * POOL_GROUP
            pw_buf[g // 2, d0:d0 + POOL_GROUP, d0:d0 + POOL_GROUP] = poolw_ref[g].astype(jnp.bfloat16)

    @pl.when(t == 0)
    def _zero_halo():
        u_buf[0:POOL_HALO, :] = jnp.zeros((POOL_HALO, D_POOL), jnp.float32)
        kz_buf[:, 0:WINDOW, :] = jnp.zeros((N_KV_HEADS, WINDOW, LANES), jnp.bfloat16)
        vt_buf[:, 0:WINDOW] = jnp.zeros((LANES, WINDOW), jnp.bfloat16)

    x = x_ref[...]
    ms = jnp.mean(x * x, axis=-1, keepdims=True)
    h = (x * lax.rsqrt(ms + EPS) * gpre_ref[layer:layer + 1, :]).astype(jnp.bfloat16)

    def proj(lo, hi):
        return jnp.dot(h, wi_buf[:, lo:hi], preferred_element_type=jnp.float32)

    qkvg = proj(OFF_Q, D_IN)
    q_buf[...] = (qkvg[:, :D_ATTN] * (HEAD_DIM ** -0.5 * LOG2_E)).astype(jnp.bfloat16)
    kf, vf = qkvg[:, D_ATTN:D_ATTN + LANES], qkvg[:, D_ATTN + LANES:D_ATTN + 2 * LANES]
    ag_buf[...] = _silu(qkvg[:, OFF_AG - OFF_Q:])
    lane = lax.broadcasted_iota(jnp.int32, (1, LANES), 1)
    kz_buf[0, WINDOW:WINDOW + tile, :] = jnp.where(lane < HEAD_DIM, kf, 0.0).astype(jnp.bfloat16)
    kz_buf[1, WINDOW:WINDOW + tile, :] = jnp.where(lane >= HEAD_DIM, kf, 0.0).astype(jnp.bfloat16)
    vt_buf[:, WINDOW:WINDOW + tile] = vf.T.astype(jnp.bfloat16)

    pos1 = (lax.broadcasted_iota(jnp.int32, (tile, POOL_GROUP), 0) + (t * tile + 1)
            ).astype(jnp.float32)

    def pool_window_item():
        for g, w in enumerate(POOL_WINDOWS):
            cols = slice(g * POOL_GROUP, (g + 1) * POOL_GROUP)
            ext = u_buf[:, cols]
            acc = ext
            shift = 1
            while shift < w:
                acc = acc + pltpu.roll(acc, shift, axis=0)
                shift *= 2
            cur = ext[POOL_HALO:]
            pooled = acc[POOL_HALO:] / jnp.minimum(pos1, float(w)) - cur
            pooled_buf[:, cols] = pooled.astype(jnp.bfloat16)

    def pool_mix_item():
        for pair in range(len(POOL_WINDOWS) // 2):
            cols = slice(pair * 2 * POOL_GROUP, (pair + 1) * 2 * POOL_GROUP)
            mixed = jnp.dot(pooled_buf[:, cols], pw_buf[pair], preferred_element_type=jnp.float32)
            scale = pscale_ref[layer:layer + 1, cols]
            mixp_buf[:, cols] = (mixed * scale * pg_buf[:, cols]).astype(jnp.bfloat16)

    def pool_u_gate_item():
        ug = proj(OFF_U, OFF_Q)
        u_buf[POOL_HALO:POOL_HALO + tile, :] = ug[:, :D_POOL]
        pg_buf[...] = _silu(ug[:, D_POOL:])

    col = lax.broadcasted_iota(jnp.int32, (1, GQA_GROUP * Q_SUB), 1)
    sink_rows = []
    for kv in range(N_KV_HEADS):
        row = jnp.zeros((1, GQA_GROUP * Q_SUB), jnp.float32)
        for g in range(GQA_GROUP):
            row = jnp.where(col >= g * Q_SUB, sinks_ref[layer, kv * GQA_GROUP + g], row)
        sink_rows.append(row * LOG2_E)
    first = (t == 0)
    nt_dims = (((1,), (1,)), ((), ()))
    pad = jnp.zeros((Q_SUB, GQA_GROUP * Q_SUB), jnp.bfloat16)
    ones_rows = jnp.ones((BF16_ROWS, 2 * LANES), jnp.bfloat16)

    def attn_scores(sb):
        r0 = sb * Q_SUB
        qrows = slice(r0, r0 + Q_SUB)
        krows = slice(r0, r0 + KEY_SPAN)
        variant = jnp.where(first, 2 - sb, 0) if sb < 2 else 0
        q4 = jnp.concatenate([q_buf[qrows, g * LANES:(g + 1) * LANES] for g in range(GQA_GROUP)],
                             axis=0)
        kk = jnp.concatenate([kz_buf[0, krows, :], kz_buf[1, krows, :]], axis=0)
        s = lax.dot_general(kk, q4, nt_dims, preferred_element_type=jnp.float32)
        return s + bias_buf[variant]

    def attn_finish_pair(sb_even, scores_pair):
        c0 = sb_even * Q_SUB
        vts = []
        for kv in range(N_KV_HEADS):
            vts += [vt_buf[kv * HEAD_DIM:(kv + 1) * HEAD_DIM, c0:c0 + 2 * LANES], ones_rows]
        maxes, probs = [], []
        for j, s in enumerate(scores_pair):
            for kv in range(N_KV_HEADS):
                sh = s[kv * KEY_SPAN:(kv + 1) * KEY_SPAN]
                m = jnp.maximum(jnp.max(sh, axis=0, keepdims=True), sink_rows[kv])
                pb = jnp.exp2(sh - m).astype(jnp.bfloat16)
                probs.append(jnp.concatenate([pb, pad] if j == 0 else [pad, pb], axis=0))
                maxes.append(m)
        o_all = jnp.dot(jnp.concatenate(vts, axis=0), jnp.concatenate(probs, axis=1),
                        preferred_element_type=jnp.float32)
        for j in range(2):
            r0 = (sb_even + j) * Q_SUB
            qrows = slice(r0, r0 + Q_SUB)
            lrows = slice(r0 % OUT_ROWS, r0 % OUT_ROWS + Q_SUB)
            outs = []
            for kv in range(N_KV_HEADS):
                blk = j * N_KV_HEADS + kv
                o = o_all[kv * PV_ROWS:(kv + 1) * PV_ROWS, blk * 2 * LANES:(blk + 1) * 2 * LANES]
                l = o[HEAD_DIM:HEAD_DIM + 1] + jnp.exp2(sink_rows[kv] - maxes[blk])
                outs.append(o[:HEAD_DIM] * (1.0 / l))
            for pr in range(GQA_GROUP // 2):
                both = jnp.concatenate([o[:, pr * LANES:(pr + 1) * LANES] for o in outs], axis=0)
                both_t = both.T
                for half in range(2):
                    g = 2 * pr + half
                    val = both_t[half * Q_SUB:(half + 1) * Q_SUB] * ag_buf[qrows, g * LANES:(g + 1) * LANES]
                    mixa_buf[r0 // OUT_ROWS, lrows, g * LANES:(g + 1) * LANES] = val.astype(jnp.bfloat16)

    def out_half_item(half):
        rows = slice(half * OUT_ROWS, (half + 1) * OUT_ROWS)
        y = (jnp.dot(mixp_buf[rows, :], wo_buf[:D_POOL, :], preferred_element_type=jnp.float32)
             + jnp.dot(mixa_buf[half], wo_buf[D_POOL:, :], preferred_element_type=jnp.float32))
        ms2 = jnp.mean(y * y, axis=-1, keepdims=True)
        o_ref[rows, :] = x_ref[rows, :] + y * lax.rsqrt(ms2 + EPS) * gpost_ref[layer:layer + 1, :]

    dense_items = [[pool_u_gate_item, pool_window_item, pool_mix_item],
                   [functools.partial(out_half_item, 0)]]
    group = n_sub // len(dense_items)
    for p, items in enumerate(dense_items):
        subs = range(p * group, (p + 1) * group)
        scores = [attn_scores(sb) for sb in subs]
        for item in items:
            item()
        for j in range(0, group, 2):
            attn_finish_pair(subs[j], scores[j:j + 2])
    out_half_item(1)

    u_buf[0:POOL_HALO, :] = u_buf[tile:tile + POOL_HALO, :]
    kz_buf[:, 0:WINDOW, :] = kz_buf[:, tile:tile + WINDOW, :]
    vt_buf[:, 0:WINDOW] = vt_buf[:, tile:tile + WINDOW]


def _layer(layer, x, w_in, pool_w, pool_scale, sinks, w_out, g_pre, g_post):
    batch, seq, d = x.shape
    tile = SEQ_TILE
    assert seq % tile == 0 and tile == 2 * OUT_ROWS and OUT_ROWS % (2 * Q_SUB) == 0 and d == D_MODEL

    def of_layer(*shape):
        return pl.BlockSpec((None,) + shape, lambda b, t: (layer,) + (0,) * len(shape),
                            pipeline_mode=pl.Buffered(1))

    def whole(a):
        return pl.BlockSpec(a.shape, lambda b, t: (0,) * a.ndim)

    return pl.pallas_call(
        functools.partial(_layer_kernel, layer),
        out_shape=jax.ShapeDtypeStruct(x.shape, x.dtype),
        grid=(batch, seq // tile),
        in_specs=[
            pl.BlockSpec((None, tile, d), lambda b, t: (b, t, 0)),
            pl.BlockSpec(memory_space=pl.ANY),
            of_layer(len(POOL_WINDOWS), POOL_GROUP, POOL_GROUP),
            whole(pool_scale),
            pl.BlockSpec(memory_space=pltpu.SMEM),
            pl.BlockSpec(memory_space=pl.ANY),
            whole(g_pre),
            whole(g_post),
        ],
        out_specs=pl.BlockSpec((None, tile, d), lambda b, t: (b, t, 0)),
        scratch_shapes=[
            pltpu.VMEM((POOL_HALO + tile, D_POOL), jnp.float32),
            pltpu.VMEM((tile, D_ATTN), jnp.bfloat16),
            pltpu.VMEM((N_KV_HEADS, WINDOW + tile, LANES), jnp.bfloat16),
            pltpu.VMEM((LANES, WINDOW + tile), jnp.bfloat16),
            pltpu.VMEM((tile, D_ATTN), jnp.float32),
            pltpu.VMEM((tile, D_POOL), jnp.float32),
            pltpu.VMEM((tile, D_POOL), jnp.bfloat16),
            pltpu.VMEM((tile, D_POOL), jnp.bfloat16),
            pltpu.VMEM((tile // OUT_ROWS, OUT_ROWS, D_ATTN), jnp.bfloat16),
            pltpu.VMEM((3, N_KV_HEADS * KEY_SPAN, GQA_GROUP * Q_SUB), jnp.float32),
            pltpu.VMEM((D_MODEL, D_IN), jnp.bfloat16),
            pltpu.VMEM((D_MODEL, D_MODEL), jnp.bfloat16),
            pltpu.VMEM((len(POOL_WINDOWS) // 2, 2 * POOL_GROUP, 2 * POOL_GROUP), jnp.bfloat16),
            pltpu.VMEM((D_MODEL, D_IN), jnp.float32),
            pltpu.VMEM((D_MODEL, D_MODEL), jnp.float32),
            pltpu.SemaphoreType.DMA((7,)),
        ],
        compiler_params=pltpu.CompilerParams(
            dimension_semantics=("arbitrary", "arbitrary"),
            vmem_limit_bytes=VMEM_LIMIT_BYTES),
        name="hybrid_layer",
    )(x, w_in, pool_w, pool_scale, sinks, w_out, g_pre, g_post)


@jax.jit
def kernel(x, w_in, pool_w, pool_scale, attn_sinks, w_out, norm_pre, norm_post):
    for layer in range(w_in.shape[0]):
        x = _layer(layer, x, w_in, pool_w, pool_scale, attn_sinks, w_out, norm_pre, norm_post)
    return x
```

```python
import functools

import jax
import jax.numpy as jnp
from jax import lax
from jax.experimental import pallas as pl
from jax.experimental.pallas import tpu as pltpu

D_MODEL = 1024
D_POOL = 512
POOL_WINDOWS = (2, 4, 8, 16)
POOL_GROUP = 128
HEAD_DIM = 64
D_ATTN = 512
N_HEADS = 8
N_KV_HEADS = 2
GQA_GROUP = N_HEADS // N_KV_HEADS
WINDOW = 128
D_IN = 2304
EPS = 1e-6
NEG_INF = -1e30
LOG2_E = 1.4426950408889634

OFF_U, OFF_PG, OFF_Q, OFF_K, OFF_V, OFF_AG = 0, 512, 1024, 1536, 1664, 1792

LANES = 128
BF16_ROWS = 16
PV_ROWS = HEAD_DIM + BF16_ROWS
POOL_HALO = 16
SEQ_TILE = 1024
Q_SUB = 64
OUT_ROWS = 512
KEY_SPAN = Q_SUB + WINDOW
VMEM_LIMIT_BYTES = 56 * 1024 * 1024


def _silu(x):
    return x * (1.0 / (1.0 + jnp.exp(-x)))


def _layer_kernel(layer, *refs):
    first = (pl.program_id(0) == 0) & (pl.program_id(1) == 0)
    pl.when(first)(functools.partial(_step, True, layer, *refs))
    pl.when(jnp.logical_not(first))(functools.partial(_step, False, layer, *refs))


def _step(prep, layer, x_ref, win_ref, poolw_ref, pscale_ref, sinks_ref, wout_ref,
          gpre_ref, gpost_ref, o_ref,
          u_buf, q_buf, kz_buf, vt_buf, ag_buf, pg_buf, pooled_buf, mixp_buf, mixa_buf, bias_buf, wi_buf,
          wo_buf, pw_buf, win_stage, wout_stage, w_sem):
    tile = x_ref.shape[0]
    n_sub = tile // Q_SUB
    t = pl.program_id(1)

    win_chunks = ((OFF_Q, OFF_K), (OFF_K, OFF_AG), (OFF_AG, D_IN), (OFF_U, OFF_PG), (OFF_PG, OFF_Q))
    win_copies = [pltpu.make_async_copy(win_ref.at[layer, :, lo:hi], win_stage.at[:, lo:hi], w_sem.at[i])
                  for i, (lo, hi) in enumerate(win_chunks)]
    wout_copies = [pltpu.make_async_copy(wout_ref.at[layer, h * D_POOL:(h + 1) * D_POOL, :],
                                         wout_stage.at[h * D_POOL:(h + 1) * D_POOL, :],
                                         w_sem.at[len(win_chunks) + h]) for h in range(2)]

    def cast_win_chunk(i, perm=None):
        lo, hi = win_chunks[i]
        win_copies[i].wait()
        w = win_stage[:, lo:hi].astype(jnp.bfloat16)
        if perm is not None:
            w = jnp.dot(w, perm, preferred_element_type=jnp.float32).astype(jnp.bfloat16)
        wi_buf[:, lo:hi] = w

    def cast_wout():
        wout_copies[0].wait()
        wo_buf[:D_POOL, :] = wout_stage[:D_POOL, :].astype(jnp.bfloat16)
        wout_copies[1].wait()
        for g in range(GQA_GROUP):
            for kv in range(N_KV_HEADS):
                dst = D_POOL + (g * N_KV_HEADS + kv) * HEAD_DIM
                src = D_POOL + (kv * GQA_GROUP + g) * HEAD_DIM
                wo_buf[dst:dst + HEAD_DIM, :] = wout_stage[src:src + HEAD_DIM, :].astype(jnp.bfloat16)

    def _first_step():
        for c in win_copies + wout_copies:
            c.start()

        r = lax.broadcasted_iota(jnp.int32, (KEY_SPAN, GQA_GROUP * Q_SUB), 0)
        col = lax.broadcasted_iota(jnp.int32, (KEY_SPAN, GQA_GROUP * Q_SUB), 1)
        dist = (col & (Q_SUB - 1)) + WINDOW - r
        in_win = (dist >= 0) & (dist < WINDOW)
        distf = dist.astype(jnp.float32)
        for kv in range(N_KV_HEADS):
            slope = jnp.zeros_like(distf)
            for g in range(GQA_GROUP):
                j = kv * GQA_GROUP + g
                slope = jnp.where(col >= g * Q_SUB, 2.0 ** (-8.0 * (j + 1) / N_HEADS), slope)
            b = -slope * distf * LOG2_E
            for f in range(3):
                bias_buf[f, kv * KEY_SPAN:(kv + 1) * KEY_SPAN, :] = jnp.where(
                    in_win & (r >= f * Q_SUB), b, NEG_INF)

        a0 = lax.broadcasted_iota(jnp.int32, (D_ATTN, D_ATTN), 0)
        a1 = lax.broadcasted_iota(jnp.int32, (D_ATTN, D_ATTN), 1)
        source = ((a1 >> 6) & 1) * (GQA_GROUP * HEAD_DIM) + (a1 >> 7) * HEAD_DIM + (a1 & (HEAD_DIM - 1))
        perm = jnp.where(a0 == source, 1.0, 0.0).astype(jnp.bfloat16)
        cast_win_chunk(0, perm)
        cast_win_chunk(1)
        cast_win_chunk(2, perm)
        pw_buf[...] = jnp.zeros(pw_buf.shape, jnp.bfloat16)
        for g in range(len(POOL_WINDOWS)):
            d0 = (g % 2) * POOL_GROUP
            pw_buf[g // 2, d0:d0 + POOL_GROUP, d0:d0 + POOL_GROUP] = poolw_ref[g].astype(jnp.bfloat16)

    if prep:
        _first_step()

    @pl.when(t == 0)
    def _zero_halo():
        u_buf[0:POOL_HALO, :] = jnp.zeros((POOL_HALO, D_POOL), jnp.float32)
        kz_buf[:, 0:WINDOW, :] = jnp.zeros((N_KV_HEADS, WINDOW, LANES), jnp.bfloat16)
        vt_buf[:, 0:WINDOW] = jnp.zeros((LANES, WINDOW), jnp.bfloat16)

    x = x_ref[...]
    ms = jnp.mean(x * x, axis=-1, keepdims=True)
    h = (x * lax.rsqrt(ms + EPS) * gpre_ref[layer:layer + 1, :]).astype(jnp.bfloat16)

    def proj(lo, hi):
        return jnp.dot(h, wi_buf[:, lo:hi], preferred_element_type=jnp.float32)

    qkvg = proj(OFF_Q, D_IN)
    q_buf[...] = (qkvg[:, :D_ATTN] * (HEAD_DIM ** -0.5 * LOG2_E)).astype(jnp.bfloat16)
    kf, vf = qkvg[:, D_ATTN:D_ATTN + LANES], qkvg[:, D_ATTN + LANES:D_ATTN + 2 * LANES]
    ag_buf[...] = _silu(qkvg[:, OFF_AG - OFF_Q:])
    lane = lax.broadcasted_iota(jnp.int32, (1, LANES), 1)
    kz_buf[0, WINDOW:WINDOW + tile, :] = jnp.where(lane < HEAD_DIM, kf, 0.0).astype(jnp.bfloat16)
    kz_buf[1, WINDOW:WINDOW + tile, :] = jnp.where(lane >= HEAD_DIM, kf, 0.0).astype(jnp.bfloat16)
    vt_buf[:, WINDOW:WINDOW + tile] = vf.T.astype(jnp.bfloat16)

    pos1 = (lax.broadcasted_iota(jnp.int32, (tile, POOL_GROUP), 0) + (t * tile + 1)
            ).astype(jnp.float32)

    def pool_window_item():
        for g, w in enumerate(POOL_WINDOWS):
            cols = slice(g * POOL_GROUP, (g + 1) * POOL_GROUP)
            ext = u_buf[:, cols]
            acc = ext
            shift = 1
            while shift < w:
                acc = acc + pltpu.roll(acc, shift, axis=0)
                shift *= 2
            cur = ext[POOL_HALO:]
            pooled = acc[POOL_HALO:] / jnp.minimum(pos1, float(w)) - cur
            pooled_buf[:, cols] = pooled.astype(jnp.bfloat16)

    def pool_mix_item():
        for pair in range(len(POOL_WINDOWS) // 2):
            cols = slice(pair * 2 * POOL_GROUP, (pair + 1) * 2 * POOL_GROUP)
            mixed = jnp.dot(pooled_buf[:, cols], pw_buf[pair], preferred_element_type=jnp.float32)
            scale = pscale_ref[layer:layer + 1, cols]
            mixp_buf[:, cols] = (mixed * scale * pg_buf[:, cols]).astype(jnp.bfloat16)

    def pool_u_gate_item():
        if prep:
            cast_win_chunk(3)
            cast_win_chunk(4)
        ug = proj(OFF_U, OFF_Q)
        u_buf[POOL_HALO:POOL_HALO + tile, :] = ug[:, :D_POOL]
        pg_buf[...] = _silu(ug[:, D_POOL:])

    col = lax.broadcasted_iota(jnp.int32, (1, GQA_GROUP * Q_SUB), 1)
    sink_rows = []
    for kv in range(N_KV_HEADS):
        row = jnp.zeros((1, GQA_GROUP * Q_SUB), jnp.float32)
        for g in range(GQA_GROUP):
            row = jnp.where(col >= g * Q_SUB, sinks_ref[layer, kv * GQA_GROUP + g], row)
        sink_rows.append(row * LOG2_E)
    first = (t == 0)
    nt_dims = (((1,), (1,)), ((), ()))
    pad = jnp.zeros((Q_SUB, GQA_GROUP * Q_SUB), jnp.bfloat16)
    ones_rows = jnp.ones((BF16_ROWS, 2 * LANES), jnp.bfloat16)

    def attn_scores(sb):
        r0 = sb * Q_SUB
        qrows = slice(r0, r0 + Q_SUB)
        krows = slice(r0, r0 + KEY_SPAN)
        variant = jnp.where(first, 2 - sb, 0) if sb < 2 else 0
        q4 = jnp.concatenate([q_buf[qrows, g * LANES:(g + 1) * LANES] for g in range(GQA_GROUP)],
                             axis=0)
        kk = jnp.concatenate([kz_buf[0, krows, :], kz_buf[1, krows, :]], axis=0)
        s = lax.dot_general(kk, q4, nt_dims, preferred_element_type=jnp.float32)
        return s + bias_buf[variant]

    def attn_finish_pair(sb_even, scores_pair):
        c0 = sb_even * Q_SUB
        vts = []
        for kv in range(N_KV_HEADS):
            vts += [vt_buf[kv * HEAD_DIM:(kv + 1) * HEAD_DIM, c0:c0 + 2 * LANES], ones_rows]
        maxes, probs = [], []
        for j, s in enumerate(scores_pair):
            for kv in range(N_KV_HEADS):
                sh = s[kv * KEY_SPAN:(kv + 1) * KEY_SPAN]
                m = jnp.maximum(jnp.max(sh, axis=0, keepdims=True), sink_rows[kv])
                pb = jnp.exp2(sh - m).astype(jnp.bfloat16)
                probs.append(jnp.concatenate([pb, pad] if j == 0 else [pad, pb], axis=0))
                maxes.append(m)
        o_all = jnp.dot(jnp.concatenate(vts, axis=0), jnp.concatenate(probs, axis=1),
                        preferred_element_type=jnp.float32)
        for j in range(2):
            r0 = (sb_even + j) * Q_SUB
            qrows = slice(r0, r0 + Q_SUB)
            lrows = slice(r0 % OUT_ROWS, r0 % OUT_ROWS + Q_SUB)
            outs = []
            for kv in range(N_KV_HEADS):
                blk = j * N_KV_HEADS + kv
                o = o_all[kv * PV_ROWS:(kv + 1) * PV_ROWS, blk * 2 * LANES:(blk + 1) * 2 * LANES]
                l = o[HEAD_DIM:HEAD_DIM + 1] + jnp.exp2(sink_rows[kv] - maxes[blk])
                outs.append(o[:HEAD_DIM] * (1.0 / l))
            for pr in range(GQA_GROUP // 2):
                both = jnp.concatenate([o[:, pr * LANES:(pr + 1) * LANES] for o in outs], axis=0)
                both_t = both.T
                for half in range(2):
                    g = 2 * pr + half
                    val = both_t[half * Q_SUB:(half + 1) * Q_SUB] * ag_buf[qrows, g * LANES:(g + 1) * LANES]
                    mixa_buf[r0 // OUT_ROWS, lrows, g * LANES:(g + 1) * LANES] = val.astype(jnp.bfloat16)

    def out_half_item(half):
        rows = slice(half * OUT_ROWS, (half + 1) * OUT_ROWS)
        if prep and half == 0:
            cast_wout()
        y = (jnp.dot(mixp_buf[rows, :], wo_buf[:D_POOL, :], preferred_element_type=jnp.float32)
             + jnp.dot(mixa_buf[half], wo_buf[D_POOL:, :], preferred_element_type=jnp.float32))
        ms2 = jnp.mean(y * y, axis=-1, keepdims=True)
        o_ref[rows, :] = x_ref[rows, :] + y * lax.rsqrt(ms2 + EPS) * gpost_ref[layer:layer + 1, :]

    dense_items = [[pool_u_gate_item, pool_window_item, pool_mix_item],
                   [functools.partial(out_half_item, 0)]]
    group = n_sub // len(dense_items)
    for p, items in enumerate(dense_items):
        subs = range(p * group, (p + 1) * group)
        scores = [attn_scores(sb) for sb in subs]
        for item in items:
            item()
        for j in range(0, group, 2):
            attn_finish_pair(subs[j], scores[j:j + 2])
    out_half_item(1)

    u_buf[0:POOL_HALO, :] = u_buf[tile:tile + POOL_HALO, :]
    kz_buf[:, 0:WINDOW, :] = kz_buf[:, tile:tile + WINDOW, :]
    vt_buf[:, 0:WINDOW] = vt_buf[:, tile:tile + WINDOW]


def _layer(layer, x, w_in, pool_w, pool_scale, sinks, w_out, g_pre, g_post):
    batch, seq, d = x.shape
    tile = SEQ_TILE
    assert seq % tile == 0 and tile == 2 * OUT_ROWS and OUT_ROWS % (2 * Q_SUB) == 0 and d == D_MODEL

    def of_layer(*shape):
        return pl.BlockSpec((None,) + shape, lambda b, t: (layer,) + (0,) * len(shape),
                            pipeline_mode=pl.Buffered(1))

    def whole(a):
        return pl.BlockSpec(a.shape, lambda b, t: (0,) * a.ndim)

    return pl.pallas_call(
        functools.partial(_layer_kernel, layer),
        out_shape=jax.ShapeDtypeStruct(x.shape, x.dtype),
        grid=(batch, seq // tile),
        in_specs=[
            pl.BlockSpec((None, tile, d), lambda b, t: (b, t, 0)),
            pl.BlockSpec(memory_space=pl.ANY),
            of_layer(len(POOL_WINDOWS), POOL_GROUP, POOL_GROUP),
            whole(pool_scale),
            pl.BlockSpec(memory_space=pltpu.SMEM),
            pl.BlockSpec(memory_space=pl.ANY),
            whole(g_pre),
            whole(g_post),
        ],
        out_specs=pl.BlockSpec((None, tile, d), lambda b, t: (b, t, 0)),
        scratch_shapes=[
            pltpu.VMEM((POOL_HALO + tile, D_POOL), jnp.float32),
            pltpu.VMEM((tile, D_ATTN), jnp.bfloat16),
            pltpu.VMEM((N_KV_HEADS, WINDOW + tile, LANES), jnp.bfloat16),
            pltpu.VMEM((LANES, WINDOW + tile), jnp.bfloat16),
            pltpu.VMEM((tile, D_ATTN), jnp.float32),
            pltpu.VMEM((tile, D_POOL), jnp.float32),
            pltpu.VMEM((tile, D_POOL), jnp.bfloat16),
            pltpu.VMEM((tile, D_POOL), jnp.bfloat16),
            pltpu.VMEM((tile // OUT_ROWS, OUT_ROWS, D_ATTN), jnp.bfloat16),
            pltpu.VMEM((3, N_KV_HEADS * KEY_SPAN, GQA_GROUP * Q_SUB), jnp.float32),
            pltpu.VMEM((D_MODEL, D_IN), jnp.bfloat16),
            pltpu.VMEM((D_MODEL, D_MODEL), jnp.bfloat16),
            pltpu.VMEM((len(POOL_WINDOWS) // 2, 2 * POOL_GROUP, 2 * POOL_GROUP), jnp.bfloat16),
            pltpu.VMEM((D_MODEL, D_IN), jnp.float32),
            pltpu.VMEM((D_MODEL, D_MODEL), jnp.float32),
            pltpu.SemaphoreType.DMA((7,)),
        ],
        compiler_params=pltpu.CompilerParams(
            dimension_semantics=("arbitrary", "arbitrary"),
            vmem_limit_bytes=VMEM_LIMIT_BYTES),
        name="hybrid_layer",
    )(x, w_in, pool_w, pool_scale, sinks, w_out, g_pre, g_post)


@jax.jit
def kernel(x, w_in, pool_w, pool_scale, attn_sinks, w_out, norm_pre, norm_post):
    for layer in range(w_in.shape[0]):
        x = _layer(layer, x, w_in, pool_w, pool_scale, attn_sinks, w_out, norm_pre, norm_post)
    return x
```

```python
import functools

import jax
import jax.numpy as jnp
from jax import lax
from jax.experimental import pallas as pl
from jax.experimental.pallas import tpu as pltpu

D_MODEL = 1024
D_POOL = 512
POOL_WINDOWS = (2, 4, 8, 16)
POOL_GROUP = 128
HEAD_DIM = 64
D_ATTN = 512
N_HEADS = 8
N_KV_HEADS = 2
GQA_GROUP = N_HEADS // N_KV_HEADS
WINDOW = 128
D_IN = 2304
EPS = 1e-6
NEG_INF = -1e30
LOG2_E = 1.4426950408889634

OFF_U, OFF_PG, OFF_Q, OFF_K, OFF_V, OFF_AG = 0, 512, 1024, 1536, 1664, 1792

LANES = 128
BF16_ROWS = 16
PV_ROWS = HEAD_DIM + BF16_ROWS
POOL_HALO = 16
SEQ_TILE = 1024
Q_SUB = 64
OUT_ROWS = 512
KEY_SPAN = Q_SUB + WINDOW
VMEM_LIMIT_BYTES = 56 * 1024 * 1024


def _silu(x):
    return x * (1.0 / (1.0 + jnp.exp(-x)))


def _layer_kernel(layer, x_hbm, win_ref, poolw_ref, pscale_ref, sinks_ref, wout_ref,
                  gpre_ref, gpost_ref, o_ref,
                  u_buf, q_buf, kz_buf, vt_buf, ag_buf, pg_buf, pooled_buf, mixp_buf, mixa_buf, bias_buf, wi_buf,
                  wo_buf, pw_buf, win_stage, wout_stage, w_sem, x_stage, x_sem):
    tile = o_ref.shape[0]
    n_sub = tile // Q_SUB
    b = pl.program_id(0)
    t = pl.program_id(1)
    n_t = pl.num_programs(1)
    step = b * n_t + t
    slot = lax.rem(step, 2)

    def x_copy(bb, tt, sl):
        return pltpu.make_async_copy(x_hbm.at[bb, pl.ds(tt * tile, tile), :], x_stage.at[sl], x_sem.at[sl])

    @pl.when((b == 0) & (t == 0))
    def _first_step():
        x_copy(0, 0, 0).start()
        win_chunks = ((OFF_Q, OFF_K), (OFF_K, OFF_AG), (OFF_AG, D_IN), (OFF_U, OFF_PG), (OFF_PG, OFF_Q))
        win_copies = [pltpu.make_async_copy(win_ref.at[layer, :, lo:hi], win_stage.at[:, lo:hi], w_sem.at[i])
                      for i, (lo, hi) in enumerate(win_chunks)]
        wout_copies = [pltpu.make_async_copy(wout_ref.at[layer, h * D_POOL:(h + 1) * D_POOL, :],
                                             wout_stage.at[h * D_POOL:(h + 1) * D_POOL, :],
                                             w_sem.at[len(win_chunks) + h]) for h in range(2)]
        for c in win_copies + wout_copies:
            c.start()

        r = lax.broadcasted_iota(jnp.int32, (KEY_SPAN, GQA_GROUP * Q_SUB), 0)
        col = lax.broadcasted_iota(jnp.int32, (KEY_SPAN, GQA_GROUP * Q_SUB), 1)
        dist = (col & (Q_SUB - 1)) + WINDOW - r
        in_win = (dist >= 0) & (dist < WINDOW)
        distf = dist.astype(jnp.float32)
        for kv in range(N_KV_HEADS):
            slope = jnp.zeros_like(distf)
            for g in range(GQA_GROUP):
                j = kv * GQA_GROUP + g
                slope = jnp.where(col >= g * Q_SUB, 2.0 ** (-8.0 * (j + 1) / N_HEADS), slope)
            b = -slope * distf * LOG2_E
            for f in range(3):
                bias_buf[f, kv * KEY_SPAN:(kv + 1) * KEY_SPAN, :] = jnp.where(
                    in_win & (r >= f * Q_SUB), b, NEG_INF)

        a0 = lax.broadcasted_iota(jnp.int32, (D_ATTN, D_ATTN), 0)
        a1 = lax.broadcasted_iota(jnp.int32, (D_ATTN, D_ATTN), 1)
        source = ((a1 >> 6) & 1) * (GQA_GROUP * HEAD_DIM) + (a1 >> 7) * HEAD_DIM + (a1 & (HEAD_DIM - 1))
        perm = jnp.where(a0 == source, 1.0, 0.0).astype(jnp.bfloat16)
        for c, (lo, hi) in zip(win_copies, win_chunks):
            c.wait()
            w = win_stage[:, lo:hi].astype(jnp.bfloat16)
            if lo in (OFF_Q, OFF_AG):
                w = jnp.dot(w, perm, preferred_element_type=jnp.float32).astype(jnp.bfloat16)
            wi_buf[:, lo:hi] = w
        wout_copies[0].wait()
        wo_buf[:D_POOL, :] = wout_stage[:D_POOL, :].astype(jnp.bfloat16)
        wout_copies[1].wait()
        for g in range(GQA_GROUP):
            for kv in range(N_KV_HEADS):
                dst = D_POOL + (g * N_KV_HEADS + kv) * HEAD_DIM
                src = D_POOL + (kv * GQA_GROUP + g) * HEAD_DIM
                wo_buf[dst:dst + HEAD_DIM, :] = wout_stage[src:src + HEAD_DIM, :].astype(jnp.bfloat16)
        pw_buf[...] = jnp.zeros(pw_buf.shape, jnp.bfloat16)
        for g in range(len(POOL_WINDOWS)):
            d0 = (g % 2) * POOL_GROUP
            pw_buf[g // 2, d0:d0 + POOL_GROUP, d0:d0 + POOL_GROUP] = poolw_ref[g].astype(jnp.bfloat16)

    @pl.when(t == 0)
    def _zero_halo():
        u_buf[0:POOL_HALO, :] = jnp.zeros((POOL_HALO, D_POOL), jnp.float32)
        kz_buf[:, 0:WINDOW, :] = jnp.zeros((N_KV_HEADS, WINDOW, LANES), jnp.bfloat16)
        vt_buf[:, 0:WINDOW] = jnp.zeros((LANES, WINDOW), jnp.bfloat16)

    x_copy(b, t, slot).wait()

    @pl.when(step + 1 < pl.num_programs(0) * n_t)
    def _prefetch_x():
        wrap = t + 1 == n_t
        x_copy(jnp.where(wrap, b + 1, b), jnp.where(wrap, 0, t + 1), 1 - slot).start()

    x_ref = x_stage.at[slot]

    x = x_ref[...]
    ms = jnp.mean(x * x, axis=-1, keepdims=True)
    h = (x * lax.rsqrt(ms + EPS) * gpre_ref[layer:layer + 1, :]).astype(jnp.bfloat16)

    def proj(lo, hi):
        return jnp.dot(h, wi_buf[:, lo:hi], preferred_element_type=jnp.float32)

    qkvg = proj(OFF_Q, D_IN)
    q_buf[...] = (qkvg[:, :D_ATTN] * (HEAD_DIM ** -0.5 * LOG2_E)).astype(jnp.bfloat16)
    kf, vf = qkvg[:, D_ATTN:D_ATTN + LANES], qkvg[:, D_ATTN + LANES:D_ATTN + 2 * LANES]
    ag_buf[...] = _silu(qkvg[:, OFF_AG - OFF_Q:])
    lane = lax.broadcasted_iota(jnp.int32, (1, LANES), 1)
    kz_buf[0, WINDOW:WINDOW + tile, :] = jnp.where(lane < HEAD_DIM, kf, 0.0).astype(jnp.bfloat16)
    kz_buf[1, WINDOW:WINDOW + tile, :] = jnp.where(lane >= HEAD_DIM, kf, 0.0).astype(jnp.bfloat16)
    vt_buf[:, WINDOW:WINDOW + tile] = vf.T.astype(jnp.bfloat16)

    pos1 = (lax.broadcasted_iota(jnp.int32, (tile, POOL_GROUP), 0) + (t * tile + 1)
            ).astype(jnp.float32)

    def pool_window_item():
        for g, w in enumerate(POOL_WINDOWS):
            cols = slice(g * POOL_GROUP, (g + 1) * POOL_GROUP)
            ext = u_buf[:, cols]
            acc = ext
            shift = 1
            while shift < w:
                acc = acc + pltpu.roll(acc, shift, axis=0)
                shift *= 2
            cur = ext[POOL_HALO:]
            pooled = acc[POOL_HALO:] / jnp.minimum(pos1, float(w)) - cur
            pooled_buf[:, cols] = pooled.astype(jnp.bfloat16)

    def pool_mix_item():
        for pair in range(len(POOL_WINDOWS) // 2):
            cols = slice(pair * 2 * POOL_GROUP, (pair + 1) * 2 * POOL_GROUP)
            mixed = jnp.dot(pooled_buf[:, cols], pw_buf[pair], preferred_element_type=jnp.float32)
            scale = pscale_ref[layer:layer + 1, cols]
            mixp_buf[:, cols] = (mixed * scale * pg_buf[:, cols]).astype(jnp.bfloat16)

    def pool_u_gate_item():
        ug = proj(OFF_U, OFF_Q)
        u_buf[POOL_HALO:POOL_HALO + tile, :] = ug[:, :D_POOL]
        pg_buf[...] = _silu(ug[:, D_POOL:])

    col = lax.broadcasted_iota(jnp.int32, (1, GQA_GROUP * Q_SUB), 1)
    sink_rows = []
    for kv in range(N_KV_HEADS):
        row = jnp.zeros((1, GQA_GROUP * Q_SUB), jnp.float32)
        for g in range(GQA_GROUP):
            row = jnp.where(col >= g * Q_SUB, sinks_ref[layer, kv * GQA_GROUP + g], row)
        sink_rows.append(row * LOG2_E)
    first = (t == 0)
    nt_dims = (((1,), (1,)), ((), ()))
    pad = jnp.zeros((Q_SUB, GQA_GROUP * Q_SUB), jnp.bfloat16)
    ones_rows = jnp.ones((BF16_ROWS, 2 * LANES), jnp.bfloat16)

    def attn_scores(sb):
        r0 = sb * Q_SUB
        qrows = slice(r0, r0 + Q_SUB)
        krows = slice(r0, r0 + KEY_SPAN)
        variant = jnp.where(first, 2 - sb, 0) if sb < 2 else 0
        q4 = jnp.concatenate([q_buf[qrows, g * LANES:(g + 1) * LANES] for g in range(GQA_GROUP)],
                             axis=0)
        kk = jnp.concatenate([kz_buf[0, krows, :], kz_buf[1, krows, :]], axis=0)
        s = lax.dot_general(kk, q4, nt_dims, preferred_element_type=jnp.float32)
        return s + bias_buf[variant]

    def attn_finish_pair(sb_even, scores_pair):
        c0 = sb_even * Q_SUB
        vts = []
        for kv in range(N_KV_HEADS):
            vts += [vt_buf[kv * HEAD_DIM:(kv + 1) * HEAD_DIM, c0:c0 + 2 * LANES], ones_rows]
        maxes, probs = [], []
        for j, s in enumerate(scores_pair):
            for kv in range(N_KV_HEADS):
                sh = s[kv * KEY_SPAN:(kv + 1) * KEY_SPAN]
                m = jnp.maximum(jnp.max(sh, axis=0, keepdims=True), sink_rows[kv])
                pb = jnp.exp2(sh - m).astype(jnp.bfloat16)
                probs.append(jnp.concatenate([pb, pad] if j == 0 else [pad, pb], axis=0))
                maxes.append(m)
        o_all = jnp.dot(jnp.concatenate(vts, axis=0), jnp.concatenate(probs, axis=1),
                        preferred_element_type=jnp.float32)
        for j in range(2):
            r0 = (sb_even + j) * Q_SUB
            qrows = slice(r0, r0 + Q_SUB)
            lrows = slice(r0 % OUT_ROWS, r0 % OUT_ROWS + Q_SUB)
            outs = []
            for kv in range(N_KV_HEADS):
                blk = j * N_KV_HEADS + kv
                o = o_all[kv * PV_ROWS:(kv + 1) * PV_ROWS, blk * 2 * LANES:(blk + 1) * 2 * LANES]
                l = o[HEAD_DIM:HEAD_DIM + 1] + jnp.exp2(sink_rows[kv] - maxes[blk])
                outs.append(o[:HEAD_DIM] * (1.0 / l))
            for pr in range(GQA_GROUP // 2):
                both = jnp.concatenate([o[:, pr * LANES:(pr + 1) * LANES] for o in outs], axis=0)
                both_t = both.T
                for half in range(2):
                    g = 2 * pr + half
                    val = both_t[half * Q_SUB:(half + 1) * Q_SUB] * ag_buf[qrows, g * LANES:(g + 1) * LANES]
                    mixa_buf[r0 // OUT_ROWS, lrows, g * LANES:(g + 1) * LANES] = val.astype(jnp.bfloat16)

    def out_half_item(half):
        rows = slice(half * OUT_ROWS, (half + 1) * OUT_ROWS)
        y = (jnp.dot(mixp_buf[rows, :], wo_buf[:D_POOL, :], preferred_element_type=jnp.float32)
             + jnp.dot(mixa_buf[half], wo_buf[D_POOL:, :], preferred_element_type=jnp.float32))
        ms2 = jnp.mean(y * y, axis=-1, keepdims=True)
        o_ref[rows, :] = x_ref[rows, :] + y * lax.rsqrt(ms2 + EPS) * gpost_ref[layer:layer + 1, :]

    dense_items = [[pool_u_gate_item, pool_window_item, pool_mix_item],
                   [functools.partial(out_half_item, 0)]]
    group = n_sub // len(dense_items)
    for p, items in enumerate(dense_items):
        subs = range(p * group, (p + 1) * group)
        scores = [attn_scores(sb) for sb in subs]
        for item in items:
            item()
        for j in range(0, group, 2):
            attn_finish_pair(subs[j], scores[j:j + 2])
    out_half_item(1)

    u_buf[0:POOL_HALO, :] = u_buf[tile:tile + POOL_HALO, :]
    kz_buf[:, 0:WINDOW, :] = kz_buf[:, tile:tile + WINDOW, :]
    vt_buf[:, 0:WINDOW] = vt_buf[:, tile:tile + WINDOW]


def _layer(layer, x, w_in, pool_w, pool_scale, sinks, w_out, g_pre, g_post):
    batch, seq, d = x.shape
    tile = SEQ_TILE
    assert seq % tile == 0 and tile == 2 * OUT_ROWS and OUT_ROWS % (2 * Q_SUB) == 0 and d == D_MODEL

    def of_layer(*shape):
        return pl.BlockSpec((None,) + shape, lambda b, t: (layer,) + (0,) * len(shape),
                            pipeline_mode=pl.Buffered(1))

    def whole(a):
        return pl.BlockSpec(a.shape, lambda b, t: (0,) * a.ndim)

    return pl.pallas_call(
        functools.partial(_layer_kernel, layer),
        out_shape=jax.ShapeDtypeStruct(x.shape, x.dtype),
        grid=(batch, seq // tile),
        in_specs=[
            pl.BlockSpec(memory_space=pl.ANY),
            pl.BlockSpec(memory_space=pl.ANY),
            of_layer(len(POOL_WINDOWS), POOL_GROUP, POOL_GROUP),
            whole(pool_scale),
            pl.BlockSpec(memory_space=pltpu.SMEM),
            pl.BlockSpec(memory_space=pl.ANY),
            whole(g_pre),
            whole(g_post),
        ],
        out_specs=pl.BlockSpec((None, tile, d), lambda b, t: (b, t, 0)),
        scratch_shapes=[
            pltpu.VMEM((POOL_HALO + tile, D_POOL), jnp.float32),
            pltpu.VMEM((tile, D_ATTN), jnp.bfloat16),
            pltpu.VMEM((N_KV_HEADS, WINDOW + tile, LANES), jnp.bfloat16),
            pltpu.VMEM((LANES, WINDOW + tile), jnp.bfloat16),
            pltpu.VMEM((tile, D_ATTN), jnp.float32),
            pltpu.VMEM((tile, D_POOL), jnp.float32),
            pltpu.VMEM((tile, D_POOL), jnp.bfloat16),
            pltpu.VMEM((tile, D_POOL), jnp.bfloat16),
            pltpu.VMEM((tile // OUT_ROWS, OUT_ROWS, D_ATTN), jnp.bfloat16),
            pltpu.VMEM((3, N_KV_HEADS * KEY_SPAN, GQA_GROUP * Q_SUB), jnp.float32),
            pltpu.VMEM((D_MODEL, D_IN), jnp.bfloat16),
            pltpu.VMEM((D_MODEL, D_MODEL), jnp.bfloat16),
            pltpu.VMEM((len(POOL_WINDOWS) // 2, 2 * POOL_GROUP, 2 * POOL_GROUP), jnp.bfloat16),
            pltpu.VMEM((D_MODEL, D_IN), jnp.float32),
            pltpu.VMEM((D_MODEL, D_MODEL), jnp.float32),
            pltpu.SemaphoreType.DMA((7,)),
            pltpu.VMEM((2, tile, d), jnp.float32),
            pltpu.SemaphoreType.DMA((2,)),
        ],
        compiler_params=pltpu.CompilerParams(
            dimension_semantics=("arbitrary", "arbitrary"),
            vmem_limit_bytes=VMEM_LIMIT_BYTES),
        name="hybrid_layer",
    )(x, w_in, pool_w, pool_scale, sinks, w_out, g_pre, g_post)


@jax.jit
def kernel(x, w_in, pool_w, pool_scale, attn_sinks, w_out, norm_pre, norm_post):
    for layer in range(w_in.shape[0]):
        x = _layer(layer, x, w_in, pool_w, pool_scale, attn_sinks, w_out, norm_pre, norm_post)
    return x
```

```python
import functools

import jax
import jax.numpy as jnp
from jax import lax
from jax.experimental import pallas as pl
from jax.experimental.pallas import tpu as pltpu

D_MODEL = 1024
D_POOL = 512
POOL_WINDOWS = (2, 4, 8, 16)
POOL_GROUP = 128
HEAD_DIM = 64
D_ATTN = 512
N_HEADS = 8
N_KV_HEADS = 2
GQA_GROUP = N_HEADS // N_KV_HEADS
WINDOW = 128
D_IN = 2304
EPS = 1e-6
NEG_INF = -1e30
LOG2_E = 1.4426950408889634

OFF_U, OFF_PG, OFF_Q, OFF_K, OFF_V, OFF_AG = 0, 512, 1024, 1536, 1664, 1792

LANES = 128
BF16_ROWS = 16
PV_ROWS = HEAD_DIM + BF16_ROWS
POOL_HALO = 16
SEQ_TILE = 1024
Q_SUB = 64
OUT_ROWS = 512
KEY_SPAN = Q_SUB + WINDOW
VMEM_LIMIT_BYTES = 56 * 1024 * 1024


def _silu(x):
    return x * (1.0 / (1.0 + jnp.exp(-x)))


def _layer_kernel(layer, x_ref, win_ref, poolw_ref, pscale_ref, sinks_ref, wout_ref,
                  gpre_ref, gpost_ref, o_ref,
                  u_buf, q_buf, kz_buf, vt_buf, ag_buf, pg_buf, pooled_buf, mixp_buf, mixa_buf, bias_buf, wi_buf,
                  wo_buf, pw_buf, win_stage, wout_stage, w_sem):
    tile = x_ref.shape[0]
    n_sub = tile // Q_SUB
    t = pl.program_id(1)

    @pl.when((pl.program_id(0) == 0) & (t == 0))
    def _first_step():
        win_chunks = ((OFF_Q, OFF_K), (OFF_K, OFF_AG), (OFF_AG, D_IN), (OFF_U, OFF_PG), (OFF_PG, OFF_Q))
        win_copies = [pltpu.make_async_copy(win_ref.at[layer, :, lo:hi], win_stage.at[:, lo:hi], w_sem.at[i])
                      for i, (lo, hi) in enumerate(win_chunks)]
        wout_copies = [pltpu.make_async_copy(wout_ref.at[layer, h * D_POOL:(h + 1) * D_POOL, :],
                                             wout_stage.at[h * D_POOL:(h + 1) * D_POOL, :],
                                             w_sem.at[len(win_chunks) + h]) for h in range(2)]
        for i, c in enumerate(win_copies + wout_copies):
            c.start(priority=i % 2)

        r = lax.broadcasted_iota(jnp.int32, (KEY_SPAN, GQA_GROUP * Q_SUB), 0)
        col = lax.broadcasted_iota(jnp.int32, (KEY_SPAN, GQA_GROUP * Q_SUB), 1)
        dist = (col & (Q_SUB - 1)) + WINDOW - r
        in_win = (dist >= 0) & (dist < WINDOW)
        distf = dist.astype(jnp.float32)
        for kv in range(N_KV_HEADS):
            slope = jnp.zeros_like(distf)
            for g in range(GQA_GROUP):
                j = kv * GQA_GROUP + g
                slope = jnp.where(col >= g * Q_SUB, 2.0 ** (-8.0 * (j + 1) / N_HEADS), slope)
            b = -slope * distf * LOG2_E
            for f in range(3):
                bias_buf[f, kv * KEY_SPAN:(kv + 1) * KEY_SPAN, :] = jnp.where(
                    in_win & (r >= f * Q_SUB), b, NEG_INF)

        a0 = lax.broadcasted_iota(jnp.int32, (D_ATTN, D_ATTN), 0)
        a1 = lax.broadcasted_iota(jnp.int32, (D_ATTN, D_ATTN), 1)
        source = ((a1 >> 6) & 1) * (GQA_GROUP * HEAD_DIM) + (a1 >> 7) * HEAD_DIM + (a1 & (HEAD_DIM - 1))
        perm = jnp.where(a0 == source, 1.0, 0.0).astype(jnp.bfloat16)
        for c, (lo, hi) in zip(win_copies, win_chunks):
            c.wait()
            w = win_stage[:, lo:hi].astype(jnp.bfloat16)
            if lo in (OFF_Q, OFF_AG):
                w = jnp.dot(w, perm, preferred_element_type=jnp.float32).astype(jnp.bfloat16)
            wi_buf[:, lo:hi] = w
        wout_copies[0].wait()
        wo_buf[:D_POOL, :] = wout_stage[:D_POOL, :].astype(jnp.bfloat16)
        wout_copies[1].wait()
        for g in range(GQA_GROUP):
            for kv in range(N_KV_HEADS):
                dst = D_POOL + (g * N_KV_HEADS + kv) * HEAD_DIM
                src = D_POOL + (kv * GQA_GROUP + g) * HEAD_DIM
                wo_buf[dst:dst + HEAD_DIM, :] = wout_stage[src:src + HEAD_DIM, :].astype(jnp.bfloat16)
        pw_buf[...] = jnp.zeros(pw_buf.shape, jnp.bfloat16)
        for g in range(len(POOL_WINDOWS)):
            d0 = (g % 2) * POOL_GROUP
            pw_buf[g // 2, d0:d0 + POOL_GROUP, d0:d0 + POOL_GROUP] = poolw_ref[g].astype(jnp.bfloat16)

    @pl.when(t == 0)
    def _zero_halo():
        u_buf[0:POOL_HALO, :] = jnp.zeros((POOL_HALO, D_POOL), jnp.float32)
        kz_buf[:, 0:WINDOW, :] = jnp.zeros((N_KV_HEADS, WINDOW, LANES), jnp.bfloat16)
        vt_buf[:, 0:WINDOW] = jnp.zeros((LANES, WINDOW), jnp.bfloat16)

    x = x_ref[...]
    ms = jnp.mean(x * x, axis=-1, keepdims=True)
    h = (x * lax.rsqrt(ms + EPS) * gpre_ref[layer:layer + 1, :]).astype(jnp.bfloat16)

    def proj(lo, hi):
        return jnp.dot(h, wi_buf[:, lo:hi], preferred_element_type=jnp.float32)

    qkvg = proj(OFF_Q, D_IN)
    q_buf[...] = (qkvg[:, :D_ATTN] * (HEAD_DIM ** -0.5 * LOG2_E)).astype(jnp.bfloat16)
    kf, vf = qkvg[:, D_ATTN:D_ATTN + LANES], qkvg[:, D_ATTN + LANES:D_ATTN + 2 * LANES]
    ag_buf[...] = _silu(qkvg[:, OFF_AG - OFF_Q:])
    lane = lax.broadcasted_iota(jnp.int32, (1, LANES), 1)
    kz_buf[0, WINDOW:WINDOW + tile, :] = jnp.where(lane < HEAD_DIM, kf, 0.0).astype(jnp.bfloat16)
    kz_buf[1, WINDOW:WINDOW + tile, :] = jnp.where(lane >= HEAD_DIM, kf, 0.0).astype(jnp.bfloat16)
    vt_buf[:, WINDOW:WINDOW + tile] = vf.T.astype(jnp.bfloat16)

    pos1 = (lax.broadcasted_iota(jnp.int32, (tile, POOL_GROUP), 0) + (t * tile + 1)
            ).astype(jnp.float32)

    def pool_window_item():
        for g, w in enumerate(POOL_WINDOWS):
            cols = slice(g * POOL_GROUP, (g + 1) * POOL_GROUP)
            ext = u_buf[:, cols]
            acc = ext
            shift = 1
            while shift < w:
                acc = acc + pltpu.roll(acc, shift, axis=0)
                shift *= 2
            cur = ext[POOL_HALO:]
            pooled = acc[POOL_HALO:] / jnp.minimum(pos1, float(w)) - cur
            pooled_buf[:, cols] = pooled.astype(jnp.bfloat16)

    def pool_mix_item():
        for pair in range(len(POOL_WINDOWS) // 2):
            cols = slice(pair * 2 * POOL_GROUP, (pair + 1) * 2 * POOL_GROUP)
            mixed = jnp.dot(pooled_buf[:, cols], pw_buf[pair], preferred_element_type=jnp.float32)
            scale = pscale_ref[layer:layer + 1, cols]
            mixp_buf[:, cols] = (mixed * scale * pg_buf[:, cols]).astype(jnp.bfloat16)

    def pool_u_gate_item():
        ug = proj(OFF_U, OFF_Q)
        u_buf[POOL_HALO:POOL_HALO + tile, :] = ug[:, :D_POOL]
        pg_buf[...] = _silu(ug[:, D_POOL:])

    col = lax.broadcasted_iota(jnp.int32, (1, GQA_GROUP * Q_SUB), 1)
    sink_rows = []
    for kv in range(N_KV_HEADS):
        row = jnp.zeros((1, GQA_GROUP * Q_SUB), jnp.float32)
        for g in range(GQA_GROUP):
            row = jnp.where(col >= g * Q_SUB, sinks_ref[layer, kv * GQA_GROUP + g], row)
        sink_rows.append(row * LOG2_E)
    first = (t == 0)
    nt_dims = (((1,), (1,)), ((), ()))
    pad = jnp.zeros((Q_SUB, GQA_GROUP * Q_SUB), jnp.bfloat16)
    ones_rows = jnp.ones((BF16_ROWS, 2 * LANES), jnp.bfloat16)

    def attn_scores(sb):
        r0 = sb * Q_SUB
        qrows = slice(r0, r0 + Q_SUB)
        krows = slice(r0, r0 + KEY_SPAN)
        variant = jnp.where(first, 2 - sb, 0) if sb < 2 else 0
        q4 = jnp.concatenate([q_buf[qrows, g * LANES:(g + 1) * LANES] for g in range(GQA_GROUP)],
                             axis=0)
        kk = jnp.concatenate([kz_buf[0, krows, :], kz_buf[1, krows, :]], axis=0)
        s = lax.dot_general(kk, q4, nt_dims, preferred_element_type=jnp.float32)
        return s + bias_buf[variant]

    def attn_finish_pair(sb_even, scores_pair):
        c0 = sb_even * Q_SUB
        vts = []
        for kv in range(N_KV_HEADS):
            vts += [vt_buf[kv * HEAD_DIM:(kv + 1) * HEAD_DIM, c0:c0 + 2 * LANES], ones_rows]
        maxes, probs = [], []
        for j, s in enumerate(scores_pair):
            for kv in range(N_KV_HEADS):
                sh = s[kv * KEY_SPAN:(kv + 1) * KEY_SPAN]
                m = jnp.maximum(jnp.max(sh, axis=0, keepdims=True), sink_rows[kv])
                pb = jnp.exp2(sh - m).astype(jnp.bfloat16)
                probs.append(jnp.concatenate([pb, pad] if j == 0 else [pad, pb], axis=0))
                maxes.append(m)
        o_all = jnp.dot(jnp.concatenate(vts, axis=0), jnp.concatenate(probs, axis=1),
                        preferred_element_type=jnp.float32)
        for j in range(2):
            r0 = (sb_even + j) * Q_SUB
            qrows = slice(r0, r0 + Q_SUB)
            lrows = slice(r0 % OUT_ROWS, r0 % OUT_ROWS + Q_SUB)
            outs = []
            for kv in range(N_KV_HEADS):
                blk = j * N_KV_HEADS + kv
                o = o_all[kv * PV_ROWS:(kv + 1) * PV_ROWS, blk * 2 * LANES:(blk + 1) * 2 * LANES]
                l = o[HEAD_DIM:HEAD_DIM + 1] + jnp.exp2(sink_rows[kv] - maxes[blk])
                outs.append(o[:HEAD_DIM] * (1.0 / l))
            for pr in range(GQA_GROUP // 2):
                both = jnp.concatenate([o[:, pr * LANES:(pr + 1) * LANES] for o in outs], axis=0)
                both_t = both.T
                for half in range(2):
                    g = 2 * pr + half
                    val = both_t[half * Q_SUB:(half + 1) * Q_SUB] * ag_buf[qrows, g * LANES:(g + 1) * LANES]
                    mixa_buf[r0 // OUT_ROWS, lrows, g * LANES:(g + 1) * LANES] = val.astype(jnp.bfloat16)

    def out_half_item(half):
        rows = slice(half * OUT_ROWS, (half + 1) * OUT_ROWS)
        y = (jnp.dot(mixp_buf[rows, :], wo_buf[:D_POOL, :], preferred_element_type=jnp.float32)
             + jnp.dot(mixa_buf[half], wo_buf[D_POOL:, :], preferred_element_type=jnp.float32))
        ms2 = jnp.mean(y * y, axis=-1, keepdims=True)
        o_ref[rows, :] = x_ref[rows, :] + y * lax.rsqrt(ms2 + EPS) * gpost_ref[layer:layer + 1, :]

    dense_items = [[pool_u_gate_item, pool_window_item, pool_mix_item],
                   [functools.partial(out_half_item, 0)]]
    group = n_sub // len(dense_items)
    for p, items in enumerate(dense_items):
        subs = range(p * group, (p + 1) * group)
        scores = [attn_scores(sb) for sb in subs]
        for item in items:
            item()
        for j in range(0, group, 2):
            attn_finish_pair(subs[j], scores[j:j + 2])
    out_half_item(1)

    u_buf[0:POOL_HALO, :] = u_buf[tile:tile + POOL_HALO, :]
    kz_buf[:, 0:WINDOW, :] = kz_buf[:, tile:tile + WINDOW, :]
    vt_buf[:, 0:WINDOW] = vt_buf[:, tile:tile + WINDOW]


def _layer(layer, x, w_in, pool_w, pool_scale, sinks, w_out, g_pre, g_post):
    batch, seq, d = x.shape
    tile = SEQ_TILE
    assert seq % tile == 0 and tile == 2 * OUT_ROWS and OUT_ROWS % (2 * Q_SUB) == 0 and d == D_MODEL

    def of_layer(*shape):
        return pl.BlockSpec((None,) + shape, lambda b, t: (layer,) + (0,) * len(shape),
                            pipeline_mode=pl.Buffered(1))

    def whole(a):
        return pl.BlockSpec(a.shape, lambda b, t: (0,) * a.ndim)

    return pl.pallas_call(
        functools.partial(_layer_kernel, layer),
        out_shape=jax.ShapeDtypeStruct(x.shape, x.dtype),
        grid=(batch, seq // tile),
        in_specs=[
            pl.BlockSpec((None, tile, d), lambda b, t: (b, t, 0)),
            pl.BlockSpec(memory_space=pl.ANY),
            of_layer(len(POOL_WINDOWS), POOL_GROUP, POOL_GROUP),
            whole(pool_scale),
            pl.BlockSpec(memory_space=pltpu.SMEM),
            pl.BlockSpec(memory_space=pl.ANY),
            whole(g_pre),
            whole(g_post),
        ],
        out_specs=pl.BlockSpec((None, tile, d), lambda b, t: (b, t, 0)),
        scratch_shapes=[
            pltpu.VMEM((POOL_HALO + tile, D_POOL), jnp.float32),
            pltpu.VMEM((tile, D_ATTN), jnp.bfloat16),
            pltpu.VMEM((N_KV_HEADS, WINDOW + tile, LANES), jnp.bfloat16),
            pltpu.VMEM((LANES, WINDOW + tile), jnp.bfloat16),
            pltpu.VMEM((tile, D_ATTN), jnp.float32),
            pltpu.VMEM((tile, D_POOL), jnp.float32),
            pltpu.VMEM((tile, D_POOL), jnp.bfloat16),
            pltpu.VMEM((tile, D_POOL), jnp.bfloat16),
            pltpu.VMEM((tile // OUT_ROWS, OUT_ROWS, D_ATTN), jnp.bfloat16),
            pltpu.VMEM((3, N_KV_HEADS * KEY_SPAN, GQA_GROUP * Q_SUB), jnp.float32),
            pltpu.VMEM((D_MODEL, D_IN), jnp.bfloat16),
            pltpu.VMEM((D_MODEL, D_MODEL), jnp.bfloat16),
            pltpu.VMEM((len(POOL_WINDOWS) // 2, 2 * POOL_GROUP, 2 * POOL_GROUP), jnp.bfloat16),
            pltpu.VMEM((D_MODEL, D_IN), jnp.float32),
            pltpu.VMEM((D_MODEL, D_MODEL), jnp.float32),
            pltpu.SemaphoreType.DMA((7,)),
        ],
        compiler_params=pltpu.CompilerParams(
            dimension_semantics=("arbitrary", "arbitrary"),
            vmem_limit_bytes=VMEM_LIMIT_BYTES),
        name="hybrid_layer",
    )(x, w_in, pool_w, pool_scale, sinks, w_out, g_pre, g_post)


@jax.jit
def kernel(x, w_in, pool_w, pool_scale, attn_sinks, w_out, norm_pre, norm_post):
    for layer in range(w_in.shape[0]):
        x = _layer(layer, x, w_in, pool_w, pool_scale, attn_sinks, w_out, norm_pre, norm_post)
    return x
```

```python
import functools

import jax
import jax.numpy as jnp
from jax import lax
from jax.experimental import pallas as pl
from jax.experimental.pallas import tpu as pltpu

D_MODEL = 1024
D_POOL = 512
POOL_WINDOWS = (2, 4, 8, 16)
POOL_GROUP = 128
HEAD_DIM = 64
D_ATTN = 512
N_HEADS = 8
N_KV_HEADS = 2
GQA_GROUP = N_HEADS // N_KV_HEADS
WINDOW = 128
D_IN = 2304
EPS = 1e-6
NEG_INF = -1e30
LOG2_E = 1.4426950408889634

OFF_U, OFF_PG, OFF_Q, OFF_K, OFF_V, OFF_AG = 0, 512, 1024, 1536, 1664, 1792

LANES = 128
BF16_ROWS = 16
PV_ROWS = HEAD_DIM + BF16_ROWS
POOL_HALO = 16
SEQ_TILE = 1024
Q_SUB = 64
OUT_ROWS = 512
KEY_SPAN = Q_SUB + WINDOW
VMEM_LIMIT_BYTES = 56 * 1024 * 1024


def _silu(x):
    return x * (1.0 / (1.0 + jnp.exp(-x)))


def _layer_kernel(layer, x_ref, win_ref, poolw_ref, pscale_ref, sinks_ref, wout_ref,
                  gpre_ref, gpost_ref, o_ref,
                  u_buf, q_buf, kz_buf, vt_buf, ag_buf, pg_buf, pooled_buf, mixp_buf, mixa_buf, bias_buf, wi_buf,
                  wo_buf, pw_buf, win_stage, wout_stage, w_sem):
    tile = x_ref.shape[0]
    n_sub = tile // Q_SUB
    t = pl.program_id(1)

    @pl.when((pl.program_id(0) == 0) & (t == 0))
    def _first_step():
        win_chunks = ((OFF_Q, OFF_K), (OFF_K, OFF_AG), (OFF_AG, D_IN), (OFF_U, OFF_PG), (OFF_PG, OFF_Q))
        win_copies = [pltpu.make_async_copy(win_ref.at[layer, :, lo:hi], win_stage.at[:, lo:hi], w_sem.at[i])
                      for i, (lo, hi) in enumerate(win_chunks)]
        wout_copies = [pltpu.make_async_copy(wout_ref.at[layer, h * D_POOL:(h + 1) * D_POOL, :],
                                             wout_stage.at[h * D_POOL:(h + 1) * D_POOL, :],
                                             w_sem.at[len(win_chunks) + h]) for h in range(2)]
        copies = win_copies + wout_copies
        for c in copies[:2]:
            c.start()

        r = lax.broadcasted_iota(jnp.int32, (KEY_SPAN, GQA_GROUP * Q_SUB), 0)
        col = lax.broadcasted_iota(jnp.int32, (KEY_SPAN, GQA_GROUP * Q_SUB), 1)
        dist = (col & (Q_SUB - 1)) + WINDOW - r
        in_win = (dist >= 0) & (dist < WINDOW)
        distf = dist.astype(jnp.float32)
        for kv in range(N_KV_HEADS):
            slope = jnp.zeros_like(distf)
            for g in range(GQA_GROUP):
                j = kv * GQA_GROUP + g
                slope = jnp.where(col >= g * Q_SUB, 2.0 ** (-8.0 * (j + 1) / N_HEADS), slope)
            b = -slope * distf * LOG2_E
            for f in range(3):
                bias_buf[f, kv * KEY_SPAN:(kv + 1) * KEY_SPAN, :] = jnp.where(
                    in_win & (r >= f * Q_SUB), b, NEG_INF)

        a0 = lax.broadcasted_iota(jnp.int32, (D_ATTN, D_ATTN), 0)
        a1 = lax.broadcasted_iota(jnp.int32, (D_ATTN, D_ATTN), 1)
        source = ((a1 >> 6) & 1) * (GQA_GROUP * HEAD_DIM) + (a1 >> 7) * HEAD_DIM + (a1 & (HEAD_DIM - 1))
        perm = jnp.where(a0 == source, 1.0, 0.0).astype(jnp.bfloat16)
        for i, (c, (lo, hi)) in enumerate(zip(win_copies, win_chunks)):
            c.wait()
            copies[i + 2].start()
            w = win_stage[:, lo:hi].astype(jnp.bfloat16)
            if lo in (OFF_Q, OFF_AG):
                w = jnp.dot(w, perm, preferred_element_type=jnp.float32).astype(jnp.bfloat16)
            wi_buf[:, lo:hi] = w
        wout_copies[0].wait()
        wo_buf[:D_POOL, :] = wout_stage[:D_POOL, :].astype(jnp.bfloat16)
        wout_copies[1].wait()
        for g in range(GQA_GROUP):
            for kv in range(N_KV_HEADS):
                dst = D_POOL + (g * N_KV_HEADS + kv) * HEAD_DIM
                src = D_POOL + (kv * GQA_GROUP + g) * HEAD_DIM
                wo_buf[dst:dst + HEAD_DIM, :] = wout_stage[src:src + HEAD_DIM, :].astype(jnp.bfloat16)
        pw_buf[...] = jnp.zeros(pw_buf.shape, jnp.bfloat16)
        for g in range(len(POOL_WINDOWS)):
            d0 = (g % 2) * POOL_GROUP
            pw_buf[g // 2, d0:d0 + POOL_GROUP, d0:d0 + POOL_GROUP] = poolw_ref[g].astype(jnp.bfloat16)

    @pl.when(t == 0)
    def _zero_halo():
        u_buf[0:POOL_HALO, :] = jnp.zeros((POOL_HALO, D_POOL), jnp.float32)
        kz_buf[:, 0:WINDOW, :] = jnp.zeros((N_KV_HEADS, WINDOW, LANES), jnp.bfloat16)
        vt_buf[:, 0:WINDOW] = jnp.zeros((LANES, WINDOW), jnp.bfloat16)

    x = x_ref[...]
    ms = jnp.mean(x * x, axis=-1, keepdims=True)
    h = (x * lax.rsqrt(ms + EPS) * gpre_ref[layer:layer + 1, :]).astype(jnp.bfloat16)

    def proj(lo, hi):
        return jnp.dot(h, wi_buf[:, lo:hi], preferred_element_type=jnp.float32)

    qkvg = proj(OFF_Q, D_IN)
    q_buf[...] = (qkvg[:, :D_ATTN] * (HEAD_DIM ** -0.5 * LOG2_E)).astype(jnp.bfloat16)
    kf, vf = qkvg[:, D_ATTN:D_ATTN + LANES], qkvg[:, D_ATTN + LANES:D_ATTN + 2 * LANES]
    ag_buf[...] = _silu(qkvg[:, OFF_AG - OFF_Q:])
    lane = lax.broadcasted_iota(jnp.int32, (1, LANES), 1)
    kz_buf[0, WINDOW:WINDOW + tile, :] = jnp.where(lane < HEAD_DIM, kf, 0.0).astype(jnp.bfloat16)
    kz_buf[1, WINDOW:WINDOW + tile, :] = jnp.where(lane >= HEAD_DIM, kf, 0.0).astype(jnp.bfloat16)
    vt_buf[:, WINDOW:WINDOW + tile] = vf.T.astype(jnp.bfloat16)

    pos1 = (lax.broadcasted_iota(jnp.int32, (tile, POOL_GROUP), 0) + (t * tile + 1)
            ).astype(jnp.float32)

    def pool_window_item():
        for g, w in enumerate(POOL_WINDOWS):
            cols = slice(g * POOL_GROUP, (g + 1) * POOL_GROUP)
            ext = u_buf[:, cols]
            acc = ext
            shift = 1
            while shift < w:
                acc = acc + pltpu.roll(acc, shift, axis=0)
                shift *= 2
            cur = ext[POOL_HALO:]
            pooled = acc[POOL_HALO:] / jnp.minimum(pos1, float(w)) - cur
            pooled_buf[:, cols] = pooled.astype(jnp.bfloat16)

    def pool_mix_item():
        for pair in range(len(POOL_WINDOWS) // 2):
            cols = slice(pair * 2 * POOL_GROUP, (pair + 1) * 2 * POOL_GROUP)
            mixed = jnp.dot(pooled_buf[:, cols], pw_buf[pair], preferred_element_type=jnp.float32)
            scale = pscale_ref[layer:layer + 1, cols]
            mixp_buf[:, cols] = (mixed * scale * pg_buf[:, cols]).astype(jnp.bfloat16)

    def pool_u_gate_item():
        ug = proj(OFF_U, OFF_Q)
        u_buf[POOL_HALO:POOL_HALO + tile, :] = ug[:, :D_POOL]
        pg_buf[...] = _silu(ug[:, D_POOL:])

    col = lax.broadcasted_iota(jnp.int32, (1, GQA_GROUP * Q_SUB), 1)
    sink_rows = []
    for kv in range(N_KV_HEADS):
        row = jnp.zeros((1, GQA_GROUP * Q_SUB), jnp.float32)
        for g in range(GQA_GROUP):
            row = jnp.where(col >= g * Q_SUB, sinks_ref[layer, kv * GQA_GROUP + g], row)
        sink_rows.append(row * LOG2_E)
    first = (t == 0)
    nt_dims = (((1,), (1,)), ((), ()))
    pad = jnp.zeros((Q_SUB, GQA_GROUP * Q_SUB), jnp.bfloat16)
    ones_rows = jnp.ones((BF16_ROWS, 2 * LANES), jnp.bfloat16)

    def attn_scores(sb):
        r0 = sb * Q_SUB
        qrows = slice(r0, r0 + Q_SUB)
        krows = slice(r0, r0 + KEY_SPAN)
        variant = jnp.where(first, 2 - sb, 0) if sb < 2 else 0
        q4 = jnp.concatenate([q_buf[qrows, g * LANES:(g + 1) * LANES] for g in range(GQA_GROUP)],
                             axis=0)
        kk = jnp.concatenate([kz_buf[0, krows, :], kz_buf[1, krows, :]], axis=0)
        s = lax.dot_general(kk, q4, nt_dims, preferred_element_type=jnp.float32)
        return s + bias_buf[variant]

    def attn_finish_pair(sb_even, scores_pair):
        c0 = sb_even * Q_SUB
        vts = []
        for kv in range(N_KV_HEADS):
            vts += [vt_buf[kv * HEAD_DIM:(kv + 1) * HEAD_DIM, c0:c0 + 2 * LANES], ones_rows]
        maxes, probs = [], []
        for j, s in enumerate(scores_pair):
            for kv in range(N_KV_HEADS):
                sh = s[kv * KEY_SPAN:(kv + 1) * KEY_SPAN]
                m = jnp.maximum(jnp.max(sh, axis=0, keepdims=True), sink_rows[kv])
                pb = jnp.exp2(sh - m).astype(jnp.bfloat16)
                probs.append(jnp.concatenate([pb, pad] if j == 0 else [pad, pb], axis=0))
                maxes.append(m)
        o_all = jnp.dot(jnp.concatenate(vts, axis=0), jnp.concatenate(probs, axis=1),
                        preferred_element_type=jnp.float32)
        for j in range(2):
            r0 = (sb_even + j) * Q_SUB
            qrows = slice(r0, r0 + Q_SUB)
            lrows = slice(r0 % OUT_ROWS, r0 % OUT_ROWS + Q_SUB)
            outs = []
            for kv in range(N_KV_HEADS):
                blk = j * N_KV_HEADS + kv
                o = o_all[kv * PV_ROWS:(kv + 1) * PV_ROWS, blk * 2 * LANES:(blk + 1) * 2 * LANES]
                l = o[HEAD_DIM:HEAD_DIM + 1] + jnp.exp2(sink_rows[kv] - maxes[blk])
                outs.append(o[:HEAD_DIM] * (1.0 / l))
            for pr in range(GQA_GROUP // 2):
                both = jnp.concatenate([o[:, pr * LANES:(pr + 1) * LANES] for o in outs], axis=0)
                both_t = both.T
                for half in range(2):
                    g = 2 * pr + half
                    val = both_t[half * Q_SUB:(half + 1) * Q_SUB] * ag_buf[qrows, g * LANES:(g + 1) * LANES]
                    mixa_buf[r0 // OUT_ROWS, lrows, g * LANES:(g + 1) * LANES] = val.astype(jnp.bfloat16)

    def out_half_item(half):
        rows = slice(half * OUT_ROWS, (half + 1) * OUT_ROWS)
        y = (jnp.dot(mixp_buf[rows, :], wo_buf[:D_POOL, :], preferred_element_type=jnp.float32)
             + jnp.dot(mixa_buf[half], wo_buf[D_POOL:, :], preferred_element_type=jnp.float32))
        ms2 = jnp.mean(y * y, axis=-1, keepdims=True)
        o_ref[rows, :] = x_ref[rows, :] + y * lax.rsqrt(ms2 + EPS) * gpost_ref[layer:layer + 1, :]

    dense_items = [[pool_u_gate_item, pool_window_item, pool_mix_item],
                   [functools.partial(out_half_item, 0)]]
    group = n_sub // len(dense_items)
    for p, items in enumerate(dense_items):
        subs = range(p * group, (p + 1) * group)
        scores = [attn_scores(sb) for sb in subs]
        for item in items:
            item()
        for j in range(0, group, 2):
            attn_finish_pair(subs[j], scores[j:j + 2])
    out_half_item(1)

    u_buf[0:POOL_HALO, :] = u_buf[tile:tile + POOL_HALO, :]
    kz_buf[:, 0:WINDOW, :] = kz_buf[:, tile:tile + WINDOW, :]
    vt_buf[:, 0:WINDOW] = vt_buf[:, tile:tile + WINDOW]


def _layer(layer, x, w_in, pool_w, pool_scale, sinks, w_out, g_pre, g_post):
    batch, seq, d = x.shape
    tile = SEQ_TILE
    assert seq % tile == 0 and tile == 2 * OUT_ROWS and OUT_ROWS % (2 * Q_SUB) == 0 and d == D_MODEL

    def of_layer(*shape):
        return pl.BlockSpec((None,) + shape, lambda b, t: (layer,) + (0,) * len(shape),
                            pipeline_mode=pl.Buffered(1))

    def whole(a):
        return pl.BlockSpec(a.shape, lambda b, t: (0,) * a.ndim)

    return pl.pallas_call(
        functools.partial(_layer_kernel, layer),
        out_shape=jax.ShapeDtypeStruct(x.shape, x.dtype),
        grid=(batch, seq // tile),
        in_specs=[
            pl.BlockSpec((None, tile, d), lambda b, t: (b, t, 0)),
            pl.BlockSpec(memory_space=pl.ANY),
            of_layer(len(POOL_WINDOWS), POOL_GROUP, POOL_GROUP),
            whole(pool_scale),
            pl.BlockSpec(memory_space=pltpu.SMEM),
            pl.BlockSpec(memory_space=pl.ANY),
            whole(g_pre),
            whole(g_post),
        ],
        out_specs=pl.BlockSpec((None, tile, d), lambda b, t: (b, t, 0)),
        scratch_shapes=[
            pltpu.VMEM((POOL_HALO + tile, D_POOL), jnp.float32),
            pltpu.VMEM((tile, D_ATTN), jnp.bfloat16),
            pltpu.VMEM((N_KV_HEADS, WINDOW + tile, LANES), jnp.bfloat16),
            pltpu.VMEM((LANES, WINDOW + tile), jnp.bfloat16),
            pltpu.VMEM((tile, D_ATTN), jnp.float32),
            pltpu.VMEM((tile, D_POOL), jnp.float32),
            pltpu.VMEM((tile, D_POOL), jnp.bfloat16),
            pltpu.VMEM((tile, D_POOL), jnp.bfloat16),
            pltpu.VMEM((tile // OUT_ROWS, OUT_ROWS, D_ATTN), jnp.bfloat16),
            pltpu.VMEM((3, N_KV_HEADS * KEY_SPAN, GQA_GROUP * Q_SUB), jnp.float32),
            pltpu.VMEM((D_MODEL, D_IN), jnp.bfloat16),
            pltpu.VMEM((D_MODEL, D_MODEL), jnp.bfloat16),
            pltpu.VMEM((len(POOL_WINDOWS) // 2, 2 * POOL_GROUP, 2 * POOL_GROUP), jnp.bfloat16),
            pltpu.VMEM((D_MODEL, D_IN), jnp.float32),
            pltpu.VMEM((D_MODEL, D_MODEL), jnp.float32),
            pltpu.SemaphoreType.DMA((7,)),
        ],
        compiler_params=pltpu.CompilerParams(
            dimension_semantics=("arbitrary", "arbitrary"),
            vmem_limit_bytes=VMEM_LIMIT_BYTES),
        name="hybrid_layer",
    )(x, w_in, pool_w, pool_scale, sinks, w_out, g_pre, g_post)


@jax.jit
def kernel(x, w_in, pool_w, pool_scale, attn_sinks, w_out, norm_pre, norm_post):
    for layer in range(w_in.shape[0]):
        x = _layer(layer, x, w_in, pool_w, pool_scale, attn_sinks, w_out, norm_pre, norm_post)
    return x
```

```python
import functools

import jax
import jax.numpy as jnp
from jax import lax
from jax.experimental import pallas as pl
from jax.experimental.pallas import tpu as pltpu

D_MODEL = 1024
D_POOL = 512
POOL_WINDOWS = (2, 4, 8, 16)
POOL_GROUP = 128
HEAD_DIM = 64
D_ATTN = 512
N_HEADS = 8
N_KV_HEADS = 2
GQA_GROUP = N_HEADS // N_KV_HEADS
WINDOW = 128
D_IN = 2304
EPS = 1e-6
NEG_INF = -1e30
LOG2_E = 1.4426950408889634

OFF_U, OFF_PG, OFF_Q, OFF_K, OFF_V, OFF_AG = 0, 512, 1024, 1536, 1664, 1792

LANES = 128
BF16_ROWS = 16
PV_ROWS = HEAD_DIM + BF16_ROWS
POOL_HALO = 16
SEQ_TILE = 1024
Q_SUB = 64
OUT_ROWS = 512
KEY_SPAN = Q_SUB + WINDOW
VMEM_LIMIT_BYTES = 56 * 1024 * 1024


def _silu(x):
    return x * (1.0 / (1.0 + jnp.exp(-x)))


def _layer_kernel(layer, x_ref, win_ref, poolw_ref, pscale_ref, sinks_ref, wout_ref,
                  gpre_ref, gpost_ref, o_ref,
                  u_buf, q_buf, kz_buf, vt_buf, ag_buf, pg_buf, pooled_buf, mixp_buf, mixa_buf, bias_buf, wi_buf,
                  wo_buf, pw_buf, win_stage, wout_stage, w_sem):
    tile = x_ref.shape[0]
    n_sub = tile // Q_SUB
    t = pl.program_id(1)

    @pl.when((pl.program_id(0) == 0) & (t == 0))
    def _first_step():
        win_chunks = ((OFF_Q, OFF_K), (OFF_K, OFF_AG), (OFF_AG, D_IN), (OFF_U, OFF_PG), (OFF_PG, OFF_Q))
        def row_halves(src, dst, r0, r1, c0, c1, s):
            m = (r0 + r1) // 2
            return [pltpu.make_async_copy(src.at[layer, a:b, c0:c1], dst.at[a:b, c0:c1], w_sem.at[s + k])
                    for k, (a, b) in enumerate(((r0, m), (m, r1)))]

        win_copies = [row_halves(win_ref, win_stage, 0, D_MODEL, lo, hi, 2 * i)
                      for i, (lo, hi) in enumerate(win_chunks)]
        wout_copies = [row_halves(wout_ref, wout_stage, h * D_POOL, (h + 1) * D_POOL, 0, D_MODEL,
                                  2 * (len(win_chunks) + h)) for h in range(2)]
        for pair in win_copies + wout_copies:
            for c in pair:
                c.start()

        r = lax.broadcasted_iota(jnp.int32, (KEY_SPAN, GQA_GROUP * Q_SUB), 0)
        col = lax.broadcasted_iota(jnp.int32, (KEY_SPAN, GQA_GROUP * Q_SUB), 1)
        dist = (col & (Q_SUB - 1)) + WINDOW - r
        in_win = (dist >= 0) & (dist < WINDOW)
        distf = dist.astype(jnp.float32)
        for kv in range(N_KV_HEADS):
            slope = jnp.zeros_like(distf)
            for g in range(GQA_GROUP):
                j = kv * GQA_GROUP + g
                slope = jnp.where(col >= g * Q_SUB, 2.0 ** (-8.0 * (j + 1) / N_HEADS), slope)
            b = -slope * distf * LOG2_E
            for f in range(3):
                bias_buf[f, kv * KEY_SPAN:(kv + 1) * KEY_SPAN, :] = jnp.where(
                    in_win & (r >= f * Q_SUB), b, NEG_INF)

        a0 = lax.broadcasted_iota(jnp.int32, (D_ATTN, D_ATTN), 0)
        a1 = lax.broadcasted_iota(jnp.int32, (D_ATTN, D_ATTN), 1)
        source = ((a1 >> 6) & 1) * (GQA_GROUP * HEAD_DIM) + (a1 >> 7) * HEAD_DIM + (a1 & (HEAD_DIM - 1))
        perm = jnp.where(a0 == source, 1.0, 0.0).astype(jnp.bfloat16)
        for pair, (lo, hi) in zip(win_copies, win_chunks):
            for c in pair:
                c.wait()
            w = win_stage[:, lo:hi].astype(jnp.bfloat16)
            if lo in (OFF_Q, OFF_AG):
                w = jnp.dot(w, perm, preferred_element_type=jnp.float32).astype(jnp.bfloat16)
            wi_buf[:, lo:hi] = w
        for c in wout_copies[0]:
            c.wait()
        wo_buf[:D_POOL, :] = wout_stage[:D_POOL, :].astype(jnp.bfloat16)
        for c in wout_copies[1]:
            c.wait()
        for g in range(GQA_GROUP):
            for kv in range(N_KV_HEADS):
                dst = D_POOL + (g * N_KV_HEADS + kv) * HEAD_DIM
                src = D_POOL + (kv * GQA_GROUP + g) * HEAD_DIM
                wo_buf[dst:dst + HEAD_DIM, :] = wout_stage[src:src + HEAD_DIM, :].astype(jnp.bfloat16)
        pw_buf[...] = jnp.zeros(pw_buf.shape, jnp.bfloat16)
        for g in range(len(POOL_WINDOWS)):
            d0 = (g % 2) * POOL_GROUP
            pw_buf[g // 2, d0:d0 + POOL_GROUP, d0:d0 + POOL_GROUP] = poolw_ref[g].astype(jnp.bfloat16)

    @pl.when(t == 0)
    def _zero_halo():
        u_buf[0:POOL_HALO, :] = jnp.zeros((POOL_HALO, D_POOL), jnp.float32)
        kz_buf[:, 0:WINDOW, :] = jnp.zeros((N_KV_HEADS, WINDOW, LANES), jnp.bfloat16)
        vt_buf[:, 0:WINDOW] = jnp.zeros((LANES, WINDOW), jnp.bfloat16)

    x = x_ref[...]
    ms = jnp.mean(x * x, axis=-1, keepdims=True)
    h = (x * lax.rsqrt(ms + EPS) * gpre_ref[layer:layer + 1, :]).astype(jnp.bfloat16)

    def proj(lo, hi):
        return jnp.dot(h, wi_buf[:, lo:hi], preferred_element_type=jnp.float32)

    qkvg = proj(OFF_Q, D_IN)
    q_buf[...] = (qkvg[:, :D_ATTN] * (HEAD_DIM ** -0.5 * LOG2_E)).astype(jnp.bfloat16)
    kf, vf = qkvg[:, D_ATTN:D_ATTN + LANES], qkvg[:, D_ATTN + LANES:D_ATTN + 2 * LANES]
    ag_buf[...] = _silu(qkvg[:, OFF_AG - OFF_Q:])
    lane = lax.broadcasted_iota(jnp.int32, (1, LANES), 1)
    kz_buf[0, WINDOW:WINDOW + tile, :] = jnp.where(lane < HEAD_DIM, kf, 0.0).astype(jnp.bfloat16)
    kz_buf[1, WINDOW:WINDOW + tile, :] = jnp.where(lane >= HEAD_DIM, kf, 0.0).astype(jnp.bfloat16)
    vt_buf[:, WINDOW:WINDOW + tile] = vf.T.astype(jnp.bfloat16)

    pos1 = (lax.broadcasted_iota(jnp.int32, (tile, POOL_GROUP), 0) + (t * tile + 1)
            ).astype(jnp.float32)

    def pool_window_item():
        for g, w in enumerate(POOL_WINDOWS):
            cols = slice(g * POOL_GROUP, (g + 1) * POOL_GROUP)
            ext = u_buf[:, cols]
            acc = ext
            shift = 1
            while shift < w:
                acc = acc + pltpu.roll(acc, shift, axis=0)
                shift *= 2
            cur = ext[POOL_HALO:]
            pooled = acc[POOL_HALO:] / jnp.minimum(pos1, float(w)) - cur
            pooled_buf[:, cols] = pooled.astype(jnp.bfloat16)

    def pool_mix_item():
        for pair in range(len(POOL_WINDOWS) // 2):
            cols = slice(pair * 2 * POOL_GROUP, (pair + 1) * 2 * POOL_GROUP)
            mixed = jnp.dot(pooled_buf[:, cols], pw_buf[pair], preferred_element_type=jnp.float32)
            scale = pscale_ref[layer:layer + 1, cols]
            mixp_buf[:, cols] = (mixed * scale * pg_buf[:, cols]).astype(jnp.bfloat16)

    def pool_u_gate_item():
        ug = proj(OFF_U, OFF_Q)
        u_buf[POOL_HALO:POOL_HALO + tile, :] = ug[:, :D_POOL]
        pg_buf[...] = _silu(ug[:, D_POOL:])

    col = lax.broadcasted_iota(jnp.int32, (1, GQA_GROUP * Q_SUB), 1)
    sink_rows = []
    for kv in range(N_KV_HEADS):
        row = jnp.zeros((1, GQA_GROUP * Q_SUB), jnp.float32)
        for g in range(GQA_GROUP):
            row = jnp.where(col >= g * Q_SUB, sinks_ref[layer, kv * GQA_GROUP + g], row)
        sink_rows.append(row * LOG2_E)
    first = (t == 0)
    nt_dims = (((1,), (1,)), ((), ()))
    pad = jnp.zeros((Q_SUB, GQA_GROUP * Q_SUB), jnp.bfloat16)
    ones_rows = jnp.ones((BF16_ROWS, 2 * LANES), jnp.bfloat16)

    def attn_scores(sb):
        r0 = sb * Q_SUB
        qrows = slice(r0, r0 + Q_SUB)
        krows = slice(r0, r0 + KEY_SPAN)
        variant = jnp.where(first, 2 - sb, 0) if sb < 2 else 0
        q4 = jnp.concatenate([q_buf[qrows, g * LANES:(g + 1) * LANES] for g in range(GQA_GROUP)],
                             axis=0)
        kk = jnp.concatenate([kz_buf[0, krows, :], kz_buf[1, krows, :]], axis=0)
        s = lax.dot_general(kk, q4, nt_dims, preferred_element_type=jnp.float32)
        return s + bias_buf[variant]

    def attn_finish_pair(sb_even, scores_pair):
        c0 = sb_even * Q_SUB
        vts = []
        for kv in range(N_KV_HEADS):
            vts += [vt_buf[kv * HEAD_DIM:(kv + 1) * HEAD_DIM, c0:c0 + 2 * LANES], ones_rows]
        maxes, probs = [], []
        for j, s in enumerate(scores_pair):
            for kv in range(N_KV_HEADS):
                sh = s[kv * KEY_SPAN:(kv + 1) * KEY_SPAN]
                m = jnp.maximum(jnp.max(sh, axis=0, keepdims=True), sink_rows[kv])
                pb = jnp.exp2(sh - m).astype(jnp.bfloat16)
                probs.append(jnp.concatenate([pb, pad] if j == 0 else [pad, pb], axis=0))
                maxes.append(m)
        o_all = jnp.dot(jnp.concatenate(vts, axis=0), jnp.concatenate(probs, axis=1),
                        preferred_element_type=jnp.float32)
        for j in range(2):
            r0 = (sb_even + j) * Q_SUB
            qrows = slice(r0, r0 + Q_SUB)
            lrows = slice(r0 % OUT_ROWS, r0 % OUT_ROWS + Q_SUB)
            outs = []
            for kv in range(N_KV_HEADS):
                blk = j * N_KV_HEADS + kv
                o = o_all[kv * PV_ROWS:(kv + 1) * PV_ROWS, blk * 2 * LANES:(blk + 1) * 2 * LANES]
                l = o[HEAD_DIM:HEAD_DIM + 1] + jnp.exp2(sink_rows[kv] - maxes[blk])
                outs.append(o[:HEAD_DIM] * (1.0 / l))
            for pr in range(GQA_GROUP // 2):
                both = jnp.concatenate([o[:, pr * LANES:(pr + 1) * LANES] for o in outs], axis=0)
                both_t = both.T
                for half in range(2):
                    g = 2 * pr + half
                    val = both_t[half * Q_SUB:(half + 1) * Q_SUB] * ag_buf[qrows, g * LANES:(g + 1) * LANES]
                    mixa_buf[r0 // OUT_ROWS, lrows, g * LANES:(g + 1) * LANES] = val.astype(jnp.bfloat16)

    def out_half_item(half):
        rows = slice(half * OUT_ROWS, (half + 1) * OUT_ROWS)
        y = (jnp.dot(mixp_buf[rows, :], wo_buf[:D_POOL, :], preferred_element_type=jnp.float32)
             + jnp.dot(mixa_buf[half], wo_buf[D_POOL:, :], preferred_element_type=jnp.float32))
        ms2 = jnp.mean(y * y, axis=-1, keepdims=True)
        o_ref[rows, :] = x_ref[rows, :] + y * lax.rsqrt(ms2 + EPS) * gpost_ref[layer:layer + 1, :]

    dense_items = [[pool_u_gate_item, pool_window_item, pool_mix_item],
                   [functools.partial(out_half_item, 0)]]
    group = n_sub // len(dense_items)
    for p, items in enumerate(dense_items):
        subs = range(p * group, (p + 1) * group)
        scores = [attn_scores(sb) for sb in subs]
        for item in items:
            item()
        for j in range(0, group, 2):
            attn_finish_pair(subs[j], scores[j:j + 2])
    out_half_item(1)

    u_buf[0:POOL_HALO, :] = u_buf[tile:tile + POOL_HALO, :]
    kz_buf[:, 0:WINDOW, :] = kz_buf[:, tile:tile + WINDOW, :]
    vt_buf[:, 0:WINDOW] = vt_buf[:, tile:tile + WINDOW]


def _layer(layer, x, w_in, pool_w, pool_scale, sinks, w_out, g_pre, g_post):
    batch, seq, d = x.shape
    tile = SEQ_TILE
    assert seq % tile == 0 and tile == 2 * OUT_ROWS and OUT_ROWS % (2 * Q_SUB) == 0 and d == D_MODEL

    def of_layer(*shape):
        return pl.BlockSpec((None,) + shape, lambda b, t: (layer,) + (0,) * len(shape),
                            pipeline_mode=pl.Buffered(1))

    def whole(a):
        return pl.BlockSpec(a.shape, lambda b, t: (0,) * a.ndim)

    return pl.pallas_call(
        functools.partial(_layer_kernel, layer),
        out_shape=jax.ShapeDtypeStruct(x.shape, x.dtype),
        grid=(batch, seq // tile),
        in_specs=[
            pl.BlockSpec((None, tile, d), lambda b, t: (b, t, 0)),
            pl.BlockSpec(memory_space=pl.ANY),
            of_layer(len(POOL_WINDOWS), POOL_GROUP, POOL_GROUP),
            whole(pool_scale),
            pl.BlockSpec(memory_space=pltpu.SMEM),
            pl.BlockSpec(memory_space=pl.ANY),
            whole(g_pre),
            whole(g_post),
        ],
        out_specs=pl.BlockSpec((None, tile, d), lambda b, t: (b, t, 0)),
        scratch_shapes=[
            pltpu.VMEM((POOL_HALO + tile, D_POOL), jnp.float32),
            pltpu.VMEM((tile, D_ATTN), jnp.bfloat16),
            pltpu.VMEM((N_KV_HEADS, WINDOW + tile, LANES), jnp.bfloat16),
            pltpu.VMEM((LANES, WINDOW + tile), jnp.bfloat16),
            pltpu.VMEM((tile, D_ATTN), jnp.float32),
            pltpu.VMEM((tile, D_POOL), jnp.float32),
            pltpu.VMEM((tile, D_POOL), jnp.bfloat16),
            pltpu.VMEM((tile, D_POOL), jnp.bfloat16),
            pltpu.VMEM((tile // OUT_ROWS, OUT_ROWS, D_ATTN), jnp.bfloat16),
            pltpu.VMEM((3, N_KV_HEADS * KEY_SPAN, GQA_GROUP * Q_SUB), jnp.float32),
            pltpu.VMEM((D_MODEL, D_IN), jnp.bfloat16),
            pltpu.VMEM((D_MODEL, D_MODEL), jnp.bfloat16),
            pltpu.VMEM((len(POOL_WINDOWS) // 2, 2 * POOL_GROUP, 2 * POOL_GROUP), jnp.bfloat16),
            pltpu.VMEM((D_MODEL, D_IN), jnp.float32),
            pltpu.VMEM((D_MODEL, D_MODEL), jnp.float32),
            pltpu.SemaphoreType.DMA((14,)),
        ],
        compiler_params=pltpu.CompilerParams(
            dimension_semantics=("arbitrary", "arbitrary"),
            vmem_limit_bytes=VMEM_LIMIT_BYTES),
        name="hybrid_layer",
    )(x, w_in, pool_w, pool_scale, sinks, w_out, g_pre, g_post)


@jax.jit
def kernel(x, w_in, pool_w, pool_scale, attn_sinks, w_out, norm_pre, norm_post):
    for layer in range(w_in.shape[0]):
        x = _layer(layer, x, w_in, pool_w, pool_scale, attn_sinks, w_out, norm_pre, norm_post)
    return x
```

```python
import functools

import jax
import jax.numpy as jnp
from jax import lax
from jax.experimental import pallas as pl
from jax.experimental.pallas import tpu as pltpu

D_MODEL = 1024
D_POOL = 512
POOL_WINDOWS = (2, 4, 8, 16)
POOL_GROUP = 128
HEAD_DIM = 64
D_ATTN = 512
N_HEADS = 8
N_KV_HEADS = 2
GQA_GROUP = N_HEADS // N_KV_HEADS
WINDOW = 128
D_IN = 2304
EPS = 1e-6
NEG_INF = -1e30
LOG2_E = 1.4426950408889634

OFF_U, OFF_PG, OFF_Q, OFF_K, OFF_V, OFF_AG = 0, 512, 1024, 1536, 1664, 1792

LANES = 128
BF16_ROWS = 16
PV_ROWS = HEAD_DIM + BF16_ROWS
POOL_HALO = 16
SEQ_TILE = 1024
Q_SUB = 64
OUT_ROWS = 512
KEY_SPAN = Q_SUB + WINDOW
VMEM_LIMIT_BYTES = 56 * 1024 * 1024


def _silu(x):
    return x * (1.0 / (1.0 + jnp.exp(-x)))


def _layer_kernel(layer, x_ref, win_ref, poolw_ref, pscale_ref, sinks_ref, wout_ref,
                  gpre_ref, gpost_ref, o_ref,
                  u_buf, q_buf, kz_buf, vt_buf, ag_buf, pg_buf, pooled_buf, mixp_buf, mixa_buf, bias_buf, wi_buf,
                  wo_buf, pw_buf, win_stage, wout_stage, w_sem):
    tile = x_ref.shape[0]
    n_sub = tile // Q_SUB
    t = pl.program_id(1)

    @pl.when((pl.program_id(0) == 0) & (t == 0))
    def _first_step():
        win_chunks = ((OFF_Q, OFF_K), (OFF_K, OFF_AG), (OFF_AG, D_IN), (OFF_U, OFF_PG), (OFF_PG, OFF_Q))
        n_row = 4
        rows = D_MODEL // n_row
        win_copies = [pltpu.make_async_copy(win_ref.at[layer, i * rows:(i + 1) * rows, :],
                                            win_stage.at[i * rows:(i + 1) * rows, :], w_sem.at[i])
                      for i in range(n_row)]
        wout_copies = [pltpu.make_async_copy(wout_ref.at[layer, h * D_POOL:(h + 1) * D_POOL, :],
                                             wout_stage.at[h * D_POOL:(h + 1) * D_POOL, :],
                                             w_sem.at[n_row + h]) for h in range(2)]
        for c in win_copies + wout_copies:
            c.start()

        r = lax.broadcasted_iota(jnp.int32, (KEY_SPAN, GQA_GROUP * Q_SUB), 0)
        col = lax.broadcasted_iota(jnp.int32, (KEY_SPAN, GQA_GROUP * Q_SUB), 1)
        dist = (col & (Q_SUB - 1)) + WINDOW - r
        in_win = (dist >= 0) & (dist < WINDOW)
        distf = dist.astype(jnp.float32)
        for kv in range(N_KV_HEADS):
            slope = jnp.zeros_like(distf)
            for g in range(GQA_GROUP):
                j = kv * GQA_GROUP + g
                slope = jnp.where(col >= g * Q_SUB, 2.0 ** (-8.0 * (j + 1) / N_HEADS), slope)
            b = -slope * distf * LOG2_E
            for f in range(3):
                bias_buf[f, kv * KEY_SPAN:(kv + 1) * KEY_SPAN, :] = jnp.where(
                    in_win & (r >= f * Q_SUB), b, NEG_INF)

        a0 = lax.broadcasted_iota(jnp.int32, (D_ATTN, D_ATTN), 0)
        a1 = lax.broadcasted_iota(jnp.int32, (D_ATTN, D_ATTN), 1)
        source = ((a1 >> 6) & 1) * (GQA_GROUP * HEAD_DIM) + (a1 >> 7) * HEAD_DIM + (a1 & (HEAD_DIM - 1))
        perm = jnp.where(a0 == source, 1.0, 0.0).astype(jnp.bfloat16)
        for i, c in enumerate(win_copies):
            c.wait()
            r = slice(i * rows, (i + 1) * rows)
            for lo, hi in win_chunks:
                w = win_stage[r, lo:hi].astype(jnp.bfloat16)
                if lo in (OFF_Q, OFF_AG):
                    w = jnp.dot(w, perm, preferred_element_type=jnp.float32).astype(jnp.bfloat16)
                wi_buf[r, lo:hi] = w
        wout_copies[0].wait()
        wo_buf[:D_POOL, :] = wout_stage[:D_POOL, :].astype(jnp.bfloat16)
        wout_copies[1].wait()
        for g in range(GQA_GROUP):
            for kv in range(N_KV_HEADS):
                dst = D_POOL + (g * N_KV_HEADS + kv) * HEAD_DIM
                src = D_POOL + (kv * GQA_GROUP + g) * HEAD_DIM
                wo_buf[dst:dst + HEAD_DIM, :] = wout_stage[src:src + HEAD_DIM, :].astype(jnp.bfloat16)
        pw_buf[...] = jnp.zeros(pw_buf.shape, jnp.bfloat16)
        for g in range(len(POOL_WINDOWS)):
            d0 = (g % 2) * POOL_GROUP
            pw_buf[g // 2, d0:d0 + POOL_GROUP, d0:d0 + POOL_GROUP] = poolw_ref[g].astype(jnp.bfloat16)

    @pl.when(t == 0)
    def _zero_halo():
        u_buf[0:POOL_HALO, :] = jnp.zeros((POOL_HALO, D_POOL), jnp.float32)
        kz_buf[:, 0:WINDOW, :] = jnp.zeros((N_KV_HEADS, WINDOW, LANES), jnp.bfloat16)
        vt_buf[:, 0:WINDOW] = jnp.zeros((LANES, WINDOW), jnp.bfloat16)

    x = x_ref[...]
    ms = jnp.mean(x * x, axis=-1, keepdims=True)
    h = (x * lax.rsqrt(ms + EPS) * gpre_ref[layer:layer + 1, :]).astype(jnp.bfloat16)

    def proj(lo, hi):
        return jnp.dot(h, wi_buf[:, lo:hi], preferred_element_type=jnp.float32)

    qkvg = proj(OFF_Q, D_IN)
    q_buf[...] = (qkvg[:, :D_ATTN] * (HEAD_DIM ** -0.5 * LOG2_E)).astype(jnp.bfloat16)
    kf, vf = qkvg[:, D_ATTN:D_ATTN + LANES], qkvg[:, D_ATTN + LANES:D_ATTN + 2 * LANES]
    ag_buf[...] = _silu(qkvg[:, OFF_AG - OFF_Q:])
    lane = lax.broadcasted_iota(jnp.int32, (1, LANES), 1)
    kz_buf[0, WINDOW:WINDOW + tile, :] = jnp.where(lane < HEAD_DIM, kf, 0.0).astype(jnp.bfloat16)
    kz_buf[1, WINDOW:WINDOW + tile, :] = jnp.where(lane >= HEAD_DIM, kf, 0.0).astype(jnp.bfloat16)
    vt_buf[:, WINDOW:WINDOW + tile] = vf.T.astype(jnp.bfloat16)

    pos1 = (lax.broadcasted_iota(jnp.int32, (tile, POOL_GROUP), 0) + (t * tile + 1)
            ).astype(jnp.float32)

    def pool_window_item():
        for g, w in enumerate(POOL_WINDOWS):
            cols = slice(g * POOL_GROUP, (g + 1) * POOL_GROUP)
            ext = u_buf[:, cols]
            acc = ext
            shift = 1
            while shift < w:
                acc = acc + pltpu.roll(acc, shift, axis=0)
                shift *= 2
            cur = ext[POOL_HALO:]
            pooled = acc[POOL_HALO:] / jnp.minimum(pos1, float(w)) - cur
            pooled_buf[:, cols] = pooled.astype(jnp.bfloat16)

    def pool_mix_item():
        for pair in range(len(POOL_WINDOWS) // 2):
            cols = slice(pair * 2 * POOL_GROUP, (pair + 1) * 2 * POOL_GROUP)
            mixed = jnp.dot(pooled_buf[:, cols], pw_buf[pair], preferred_element_type=jnp.float32)
            scale = pscale_ref[layer:layer + 1, cols]
            mixp_buf[:, cols] = (mixed * scale * pg_buf[:, cols]).astype(jnp.bfloat16)

    def pool_u_gate_item():
        ug = proj(OFF_U, OFF_Q)
        u_buf[POOL_HALO:POOL_HALO + tile, :] = ug[:, :D_POOL]
        pg_buf[...] = _silu(ug[:, D_POOL:])

    col = lax.broadcasted_iota(jnp.int32, (1, GQA_GROUP * Q_SUB), 1)
    sink_rows = []
    for kv in range(N_KV_HEADS):
        row = jnp.zeros((1, GQA_GROUP * Q_SUB), jnp.float32)
        for g in range(GQA_GROUP):
            row = jnp.where(col >= g * Q_SUB, sinks_ref[layer, kv * GQA_GROUP + g], row)
        sink_rows.append(row * LOG2_E)
    first = (t == 0)
    nt_dims = (((1,), (1,)), ((), ()))
    pad = jnp.zeros((Q_SUB, GQA_GROUP * Q_SUB), jnp.bfloat16)
    ones_rows = jnp.ones((BF16_ROWS, 2 * LANES), jnp.bfloat16)

    def attn_scores(sb):
        r0 = sb * Q_SUB
        qrows = slice(r0, r0 + Q_SUB)
        krows = slice(r0, r0 + KEY_SPAN)
        variant = jnp.where(first, 2 - sb, 0) if sb < 2 else 0
        q4 = jnp.concatenate([q_buf[qrows, g * LANES:(g + 1) * LANES] for g in range(GQA_GROUP)],
                             axis=0)
        kk = jnp.concatenate([kz_buf[0, krows, :], kz_buf[1, krows, :]], axis=0)
        s = lax.dot_general(kk, q4, nt_dims, preferred_element_type=jnp.float32)
        return s + bias_buf[variant]

    def attn_finish_pair(sb_even, scores_pair):
        c0 = sb_even * Q_SUB
        vts = []
        for kv in range(N_KV_HEADS):
            vts += [vt_buf[kv * HEAD_DIM:(kv + 1) * HEAD_DIM, c0:c0 + 2 * LANES], ones_rows]
        maxes, probs = [], []
        for j, s in enumerate(scores_pair):
            for kv in range(N_KV_HEADS):
                sh = s[kv * KEY_SPAN:(kv + 1) * KEY_SPAN]
                m = jnp.maximum(jnp.max(sh, axis=0, keepdims=True), sink_rows[kv])
                pb = jnp.exp2(sh - m).astype(jnp.bfloat16)
                probs.append(jnp.concatenate([pb, pad] if j == 0 else [pad, pb], axis=0))
                maxes.append(m)
        o_all = jnp.dot(jnp.concatenate(vts, axis=0), jnp.concatenate(probs, axis=1),
                        preferred_element_type=jnp.float32)
        for j in range(2):
            r0 = (sb_even + j) * Q_SUB
            qrows = slice(r0, r0 + Q_SUB)
            lrows = slice(r0 % OUT_ROWS, r0 % OUT_ROWS + Q_SUB)
            outs = []
            for kv in range(N_KV_HEADS):
                blk = j * N_KV_HEADS + kv
                o = o_all[kv * PV_ROWS:(kv + 1) * PV_ROWS, blk * 2 * LANES:(blk + 1) * 2 * LANES]
                l = o[HEAD_DIM:HEAD_DIM + 1] + jnp.exp2(sink_rows[kv] - maxes[blk])
                outs.append(o[:HEAD_DIM] * (1.0 / l))
            for pr in range(GQA_GROUP // 2):
                both = jnp.concatenate([o[:, pr * LANES:(pr + 1) * LANES] for o in outs], axis=0)
                both_t = both.T
                for half in range(2):
                    g = 2 * pr + half
                    val = both_t[half * Q_SUB:(half + 1) * Q_SUB] * ag_buf[qrows, g * LANES:(g + 1) * LANES]
                    mixa_buf[r0 // OUT_ROWS, lrows, g * LANES:(g + 1) * LANES] = val.astype(jnp.bfloat16)

    def out_half_item(half):
        rows = slice(half * OUT_ROWS, (half + 1) * OUT_ROWS)
        y = (jnp.dot(mixp_buf[rows, :], wo_buf[:D_POOL, :], preferred_element_type=jnp.float32)
             + jnp.dot(mixa_buf[half], wo_buf[D_POOL:, :], preferred_element_type=jnp.float32))
        ms2 = jnp.mean(y * y, axis=-1, keepdims=True)
        o_ref[rows, :] = x_ref[rows, :] + y * lax.rsqrt(ms2 + EPS) * gpost_ref[layer:layer + 1, :]

    dense_items = [[pool_u_gate_item, pool_window_item, pool_mix_item],
                   [functools.partial(out_half_item, 0)]]
    group = n_sub // len(dense_items)
    for p, items in enumerate(dense_items):
        subs = range(p * group, (p + 1) * group)
        scores = [attn_scores(sb) for sb in subs]
        for item in items:
            item()
        for j in range(0, group, 2):
            attn_finish_pair(subs[j], scores[j:j + 2])
    out_half_item(1)

    u_buf[0:POOL_HALO, :] = u_buf[tile:tile + POOL_HALO, :]
    kz_buf[:, 0:WINDOW, :] = kz_buf[:, tile:tile + WINDOW, :]
    vt_buf[:, 0:WINDOW] = vt_buf[:, tile:tile + WINDOW]


def _layer(layer, x, w_in, pool_w, pool_scale, sinks, w_out, g_pre, g_post):
    batch, seq, d = x.shape
    tile = SEQ_TILE
    assert seq % tile == 0 and tile == 2 * OUT_ROWS and OUT_ROWS % (2 * Q_SUB) == 0 and d == D_MODEL

    def of_layer(*shape):
        return pl.BlockSpec((None,) + shape, lambda b, t: (layer,) + (0,) * len(shape),
                            pipeline_mode=pl.Buffered(1))

    def whole(a):
        return pl.BlockSpec(a.shape, lambda b, t: (0,) * a.ndim)

    return pl.pallas_call(
        functools.partial(_layer_kernel, layer),
        out_shape=jax.ShapeDtypeStruct(x.shape, x.dtype),
        grid=(batch, seq // tile),
        in_specs=[
            pl.BlockSpec((None, tile, d), lambda b, t: (b, t, 0)),
            pl.BlockSpec(memory_space=pl.ANY),
            of_layer(len(POOL_WINDOWS), POOL_GROUP, POOL_GROUP),
            whole(pool_scale),
            pl.BlockSpec(memory_space=pltpu.SMEM),
            pl.BlockSpec(memory_space=pl.ANY),
            whole(g_pre),
            whole(g_post),
        ],
        out_specs=pl.BlockSpec((None, tile, d), lambda b, t: (b, t, 0)),
        scratch_shapes=[
            pltpu.VMEM((POOL_HALO + tile, D_POOL), jnp.float32),
            pltpu.VMEM((tile, D_ATTN), jnp.bfloat16),
            pltpu.VMEM((N_KV_HEADS, WINDOW + tile, LANES), jnp.bfloat16),
            pltpu.VMEM((LANES, WINDOW + tile), jnp.bfloat16),
            pltpu.VMEM((tile, D_ATTN), jnp.float32),
            pltpu.VMEM((tile, D_POOL), jnp.float32),
            pltpu.VMEM((tile, D_POOL), jnp.bfloat16),
            pltpu.VMEM((tile, D_POOL), jnp.bfloat16),
            pltpu.VMEM((tile // OUT_ROWS, OUT_ROWS, D_ATTN), jnp.bfloat16),
            pltpu.VMEM((3, N_KV_HEADS * KEY_SPAN, GQA_GROUP * Q_SUB), jnp.float32),
            pltpu.VMEM((D_MODEL, D_IN), jnp.bfloat16),
            pltpu.VMEM((D_MODEL, D_MODEL), jnp.bfloat16),
            pltpu.VMEM((len(POOL_WINDOWS) // 2, 2 * POOL_GROUP, 2 * POOL_GROUP), jnp.bfloat16),
            pltpu.VMEM((D_MODEL, D_IN), jnp.float32),
            pltpu.VMEM((D_MODEL, D_MODEL), jnp.float32),
            pltpu.SemaphoreType.DMA((6,)),
        ],
        compiler_params=pltpu.CompilerParams(
            dimension_semantics=("arbitrary", "arbitrary"),
            vmem_limit_bytes=VMEM_LIMIT_BYTES),
        name="hybrid_layer",
    )(x, w_in, pool_w, pool_scale, sinks, w_out, g_pre, g_post)


@jax.jit
def kernel(x, w_in, pool_w, pool_scale, attn_sinks, w_out, norm_pre, norm_post):
    for layer in range(w_in.shape[0]):
        x = _layer(layer, x, w_in, pool_w, pool_scale, attn_sinks, w_out, norm_pre, norm_post)
    return x
```

```python
import functools

import jax
import jax.numpy as jnp
from jax import lax
from jax.experimental import pallas as pl
from jax.experimental.pallas import tpu as pltpu

D_MODEL = 1024
D_POOL = 512
POOL_WINDOWS = (2, 4, 8, 16)
POOL_GROUP = 128
HEAD_DIM = 64
D_ATTN = 512
N_HEADS = 8
N_KV_HEADS = 2
GQA_GROUP = N_HEADS // N_KV_HEADS
WINDOW = 128
D_IN = 2304
EPS = 1e-6
NEG_INF = -1e30
LOG2_E = 1.4426950408889634

OFF_U, OFF_PG, OFF_Q, OFF_K, OFF_V, OFF_AG = 0, 512, 1024, 1536, 1664, 1792

LANES = 128
BF16_ROWS = 16
PV_ROWS = HEAD_DIM + BF16_ROWS
POOL_HALO = 16
SEQ_TILE = 1024
Q_SUB = 64
OUT_ROWS = 512
KEY_SPAN = Q_SUB + WINDOW
VMEM_LIMIT_BYTES = 56 * 1024 * 1024


def _silu(x):
    return x * (1.0 / (1.0 + jnp.exp(-x)))


def _layer_kernel(layer, x_ref, win_ref, poolw_ref, pscale_ref, sinks_ref, wout_ref,
                  gpre_ref, gpost_ref, o_ref,
                  u_buf, q_buf, kz_buf, vt_buf, ag_buf, pg_buf, pooled_buf, mixp_buf, mixa_buf, bias_buf, wi_buf,
                  wo_buf, pw_buf, win_stage, wout_stage, w_sem):
    tile = x_ref.shape[0]
    n_sub = tile // Q_SUB
    t = pl.program_id(1)

    @pl.when((pl.program_id(0) == 0) & (t == 0))
    def _first_step():
        win_chunks = ((OFF_Q, OFF_K), (OFF_K, OFF_AG), (OFF_AG, D_IN), (OFF_U, OFF_PG), (OFF_PG, OFF_Q))
        win_copies = [pltpu.make_async_copy(win_ref.at[layer, :, lo:hi], win_stage.at[:, lo:hi], w_sem.at[i])
                      for i, (lo, hi) in enumerate(win_chunks)]
        wout_copies = [pltpu.make_async_copy(wout_ref.at[layer, h * D_POOL:(h + 1) * D_POOL, :],
                                             wout_stage.at[h * D_POOL:(h + 1) * D_POOL, :],
                                             w_sem.at[len(win_chunks) + h]) for h in range(2)]
        for c in win_copies + wout_copies:
            c.start(priority=1)

        r = lax.broadcasted_iota(jnp.int32, (KEY_SPAN, GQA_GROUP * Q_SUB), 0)
        col = lax.broadcasted_iota(jnp.int32, (KEY_SPAN, GQA_GROUP * Q_SUB), 1)
        dist = (col & (Q_SUB - 1)) + WINDOW - r
        in_win = (dist >= 0) & (dist < WINDOW)
        distf = dist.astype(jnp.float32)
        for kv in range(N_KV_HEADS):
            slope = jnp.zeros_like(distf)
            for g in range(GQA_GROUP):
                j = kv * GQA_GROUP + g
                slope = jnp.where(col >= g * Q_SUB, 2.0 ** (-8.0 * (j + 1) / N_HEADS), slope)
            b = -slope * distf * LOG2_E
            for f in range(3):
                bias_buf[f, kv * KEY_SPAN:(kv + 1) * KEY_SPAN, :] = jnp.where(
                    in_win & (r >= f * Q_SUB), b, NEG_INF)

        a0 = lax.broadcasted_iota(jnp.int32, (D_ATTN, D_ATTN), 0)
        a1 = lax.broadcasted_iota(jnp.int32, (D_ATTN, D_ATTN), 1)
        source = ((a1 >> 6) & 1) * (GQA_GROUP * HEAD_DIM) + (a1 >> 7) * HEAD_DIM + (a1 & (HEAD_DIM - 1))
        perm = jnp.where(a0 == source, 1.0, 0.0).astype(jnp.bfloat16)
        for c, (lo, hi) in zip(win_copies, win_chunks):
            c.wait()
            w = win_stage[:, lo:hi].astype(jnp.bfloat16)
            if lo in (OFF_Q, OFF_AG):
                w = jnp.dot(w, perm, preferred_element_type=jnp.float32).astype(jnp.bfloat16)
            wi_buf[:, lo:hi] = w
        wout_copies[0].wait()
        wo_buf[:D_POOL, :] = wout_stage[:D_POOL, :].astype(jnp.bfloat16)
        wout_copies[1].wait()
        for g in range(GQA_GROUP):
            for kv in range(N_KV_HEADS):
                dst = D_POOL + (g * N_KV_HEADS + kv) * HEAD_DIM
                src = D_POOL + (kv * GQA_GROUP + g) * HEAD_DIM
                wo_buf[dst:dst + HEAD_DIM, :] = wout_stage[src:src + HEAD_DIM, :].astype(jnp.bfloat16)
        pw_buf[...] = jnp.zeros(pw_buf.shape, jnp.bfloat16)
        for g in range(len(POOL_WINDOWS)):
            d0 = (g % 2) * POOL_GROUP
            pw_buf[g // 2, d0:d0 + POOL_GROUP, d0:d0 + POOL_GROUP] = poolw_ref[g].astype(jnp.bfloat16)

    @pl.when(t == 0)
    def _zero_halo():
        u_buf[0:POOL_HALO, :] = jnp.zeros((POOL_HALO, D_POOL), jnp.float32)
        kz_buf[:, 0:WINDOW, :] = jnp.zeros((N_KV_HEADS, WINDOW, LANES), jnp.bfloat16)
        vt_buf[:, 0:WINDOW] = jnp.zeros((LANES, WINDOW), jnp.bfloat16)

    x = x_ref[...]
    ms = jnp.mean(x * x, axis=-1, keepdims=True)
    h = (x * lax.rsqrt(ms + EPS) * gpre_ref[layer:layer + 1, :]).astype(jnp.bfloat16)

    def proj(lo, hi):
        return jnp.dot(h, wi_buf[:, lo:hi], preferred_element_type=jnp.float32)

    qkvg = proj(OFF_Q, D_IN)
    q_buf[...] = (qkvg[:, :D_ATTN] * (HEAD_DIM ** -0.5 * LOG2_E)).astype(jnp.bfloat16)
    kf, vf = qkvg[:, D_ATTN:D_ATTN + LANES], qkvg[:, D_ATTN + LANES:D_ATTN + 2 * LANES]
    ag_buf[...] = _silu(qkvg[:, OFF_AG - OFF_Q:])
    lane = lax.broadcasted_iota(jnp.int32, (1, LANES), 1)
    kz_buf[0, WINDOW:WINDOW + tile, :] = jnp.where(lane < HEAD_DIM, kf, 0.0).astype(jnp.bfloat16)
    kz_buf[1, WINDOW:WINDOW + tile, :] = jnp.where(lane >= HEAD_DIM, kf, 0.0).astype(jnp.bfloat16)
    vt_buf[:, WINDOW:WINDOW + tile] = vf.T.astype(jnp.bfloat16)

    pos1 = (lax.broadcasted_iota(jnp.int32, (tile, POOL_GROUP), 0) + (t * tile + 1)
            ).astype(jnp.float32)

    def pool_window_item():
        for g, w in enumerate(POOL_WINDOWS):
            cols = slice(g * POOL_GROUP, (g + 1) * POOL_GROUP)
            ext = u_buf[:, cols]
            acc = ext
            shift = 1
            while shift < w:
                acc = acc + pltpu.roll(acc, shift, axis=0)
                shift *= 2
            cur = ext[POOL_HALO:]
            pooled = acc[POOL_HALO:] / jnp.minimum(pos1, float(w)) - cur
            pooled_buf[:, cols] = pooled.astype(jnp.bfloat16)

    def pool_mix_item():
        for pair in range(len(POOL_WINDOWS) // 2):
            cols = slice(pair * 2 * POOL_GROUP, (pair + 1) * 2 * POOL_GROUP)
            mixed = jnp.dot(pooled_buf[:, cols], pw_buf[pair], preferred_element_type=jnp.float32)
            scale = pscale_ref[layer:layer + 1, cols]
            mixp_buf[:, cols] = (mixed * scale * pg_buf[:, cols]).astype(jnp.bfloat16)

    def pool_u_gate_item():
        ug = proj(OFF_U, OFF_Q)
        u_buf[POOL_HALO:POOL_HALO + tile, :] = ug[:, :D_POOL]
        pg_buf[...] = _silu(ug[:, D_POOL:])

    col = lax.broadcasted_iota(jnp.int32, (1, GQA_GROUP * Q_SUB), 1)
    sink_rows = []
    for kv in range(N_KV_HEADS):
        row = jnp.zeros((1, GQA_GROUP * Q_SUB), jnp.float32)
        for g in range(GQA_GROUP):
            row = jnp.where(col >= g * Q_SUB, sinks_ref[layer, kv * GQA_GROUP + g], row)
        sink_rows.append(row * LOG2_E)
    first = (t == 0)
    nt_dims = (((1,), (1,)), ((), ()))
    pad = jnp.zeros((Q_SUB, GQA_GROUP * Q_SUB), jnp.bfloat16)
    ones_rows = jnp.ones((BF16_ROWS, 2 * LANES), jnp.bfloat16)

    def attn_scores(sb):
        r0 = sb * Q_SUB
        qrows = slice(r0, r0 + Q_SUB)
        krows = slice(r0, r0 + KEY_SPAN)
        variant = jnp.where(first, 2 - sb, 0) if sb < 2 else 0
        q4 = jnp.concatenate([q_buf[qrows, g * LANES:(g + 1) * LANES] for g in range(GQA_GROUP)],
                             axis=0)
        kk = jnp.concatenate([kz_buf[0, krows, :], kz_buf[1, krows, :]], axis=0)
        s = lax.dot_general(kk, q4, nt_dims, preferred_element_type=jnp.float32)
        return s + bias_buf[variant]

    def attn_finish_pair(sb_even, scores_pair):
        c0 = sb_even * Q_SUB
        vts = []
        for kv in range(N_KV_HEADS):
            vts += [vt_buf[kv * HEAD_DIM:(kv + 1) * HEAD_DIM, c0:c0 + 2 * LANES], ones_rows]
        maxes, probs = [], []
        for j, s in enumerate(scores_pair):
            for kv in range(N_KV_HEADS):
                sh = s[kv * KEY_SPAN:(kv + 1) * KEY_SPAN]
                m = jnp.maximum(jnp.max(sh, axis=0, keepdims=True), sink_rows[kv])
                pb = jnp.exp2(sh - m).astype(jnp.bfloat16)
                probs.append(jnp.concatenate([pb, pad] if j == 0 else [pad, pb], axis=0))
                maxes.append(m)
        o_all = jnp.dot(jnp.concatenate(vts, axis=0), jnp.concatenate(probs, axis=1),
                        preferred_element_type=jnp.float32)
        for j in range(2):
            r0 = (sb_even + j) * Q_SUB
            qrows = slice(r0, r0 + Q_SUB)
            lrows = slice(r0 % OUT_ROWS, r0 % OUT_ROWS + Q_SUB)
            outs = []
            for kv in range(N_KV_HEADS):
                blk = j * N_KV_HEADS + kv
                o = o_all[kv * PV_ROWS:(kv + 1) * PV_ROWS, blk * 2 * LANES:(blk + 1) * 2 * LANES]
                l = o[HEAD_DIM:HEAD_DIM + 1] + jnp.exp2(sink_rows[kv] - maxes[blk])
                outs.append(o[:HEAD_DIM] * (1.0 / l))
            for pr in range(GQA_GROUP // 2):
                both = jnp.concatenate([o[:, pr * LANES:(pr + 1) * LANES] for o in outs], axis=0)
                both_t = both.T
                for half in range(2):
                    g = 2 * pr + half
                    val = both_t[half * Q_SUB:(half + 1) * Q_SUB] * ag_buf[qrows, g * LANES:(g + 1) * LANES]
                    mixa_buf[r0 // OUT_ROWS, lrows, g * LANES:(g + 1) * LANES] = val.astype(jnp.bfloat16)

    def out_half_item(half):
        rows = slice(half * OUT_ROWS, (half + 1) * OUT_ROWS)
        y = (jnp.dot(mixp_buf[rows, :], wo_buf[:D_POOL, :], preferred_element_type=jnp.float32)
             + jnp.dot(mixa_buf[half], wo_buf[D_POOL:, :], preferred_element_type=jnp.float32))
        ms2 = jnp.mean(y * y, axis=-1, keepdims=True)
        o_ref[rows, :] = x_ref[rows, :] + y * lax.rsqrt(ms2 + EPS) * gpost_ref[layer:layer + 1, :]

    dense_items = [[pool_u_gate_item, pool_window_item, pool_mix_item],
                   [functools.partial(out_half_item, 0)]]
    group = n_sub // len(dense_items)
    for p, items in enumerate(dense_items):
        subs = range(p * group, (p + 1) * group)
        scores = [attn_scores(sb) for sb in subs]
        for item in items:
            item()
        for j in range(0, group, 2):
            attn_finish_pair(subs[j], scores[j:j + 2])
    out_half_item(1)

    u_buf[0:POOL_HALO, :] = u_buf[tile:tile + POOL_HALO, :]
    kz_buf[:, 0:WINDOW, :] = kz_buf[:, tile:tile + WINDOW, :]
    vt_buf[:, 0:WINDOW] = vt_buf[:, tile:tile + WINDOW]


def _layer(layer, x, w_in, pool_w, pool_scale, sinks, w_out, g_pre, g_post):
    batch, seq, d = x.shape
    tile = SEQ_TILE
    assert seq % tile == 0 and tile == 2 * OUT_ROWS and OUT_ROWS % (2 * Q_SUB) == 0 and d == D_MODEL

    def of_layer(*shape):
        return pl.BlockSpec((None,) + shape, lambda b, t: (layer,) + (0,) * len(shape),
                            pipeline_mode=pl.Buffered(1))

    def whole(a):
        return pl.BlockSpec(a.shape, lambda b, t: (0,) * a.ndim)

    return pl.pallas_call(
        functools.partial(_layer_kernel, layer),
        out_shape=jax.ShapeDtypeStruct(x.shape, x.dtype),
        grid=(batch, seq // tile),
        in_specs=[
            pl.BlockSpec((None, tile, d), lambda b, t: (b, t, 0)),
            pl.BlockSpec(memory_space=pl.ANY),
            of_layer(len(POOL_WINDOWS), POOL_GROUP, POOL_GROUP),
            whole(pool_scale),
            pl.BlockSpec(memory_space=pltpu.SMEM),
            pl.BlockSpec(memory_space=pl.ANY),
            whole(g_pre),
            whole(g_post),
        ],
        out_specs=pl.BlockSpec((None, tile, d), lambda b, t: (b, t, 0)),
        scratch_shapes=[
            pltpu.VMEM((POOL_HALO + tile, D_POOL), jnp.float32),
            pltpu.VMEM((tile, D_ATTN), jnp.bfloat16),
            pltpu.VMEM((N_KV_HEADS, WINDOW + tile, LANES), jnp.bfloat16),
            pltpu.VMEM((LANES, WINDOW + tile), jnp.bfloat16),
            pltpu.VMEM((tile, D_ATTN), jnp.float32),
            pltpu.VMEM((tile, D_POOL), jnp.float32),
            pltpu.VMEM((tile, D_POOL), jnp.bfloat16),
            pltpu.VMEM((tile, D_POOL), jnp.bfloat16),
            pltpu.VMEM((tile // OUT_ROWS, OUT_ROWS, D_ATTN), jnp.bfloat16),
            pltpu.VMEM((3, N_KV_HEADS * KEY_SPAN, GQA_GROUP * Q_SUB), jnp.float32),
            pltpu.VMEM((D_MODEL, D_IN), jnp.bfloat16),
            pltpu.VMEM((D_MODEL, D_MODEL), jnp.bfloat16),
            pltpu.VMEM((len(POOL_WINDOWS) // 2, 2 * POOL_GROUP, 2 * POOL_GROUP), jnp.bfloat16),
            pltpu.VMEM((D_MODEL, D_IN), jnp.float32),
            pltpu.VMEM((D_MODEL, D_MODEL), jnp.float32),
            pltpu.SemaphoreType.DMA((7,)),
        ],
        compiler_params=pltpu.CompilerParams(
            dimension_semantics=("arbitrary", "arbitrary"),
            vmem_limit_bytes=VMEM_LIMIT_BYTES),
        name="hybrid_layer",
    )(x, w_in, pool_w, pool_scale, sinks, w_out, g_pre, g_post)


@jax.jit
def kernel(x, w_in, pool_w, pool_scale, attn_sinks, w_out, norm_pre, norm_post):
    for layer in range(w_in.shape[0]):
        x = _layer(layer, x, w_in, pool_w, pool_scale, attn_sinks, w_out, norm_pre, norm_post)
    return x
```
